```python
import jax, jax.numpy as jnp
from jax import lax
import numpy as np

D_MODEL = 1024
BATCH = 8
SEQ = 4096
DEPTH = 2

GRID_W = 64
CTX_LEN = 256
N_MIXERS = 2
N_RET_LAYERS = (DEPTH + 1) // 2
N_CONV_LAYERS = DEPTH // 2
RET_HEADS = 4
RET_QK_DIM = D_MODEL // RET_HEADS
RET_V_DIM = 2 * RET_QK_DIM
RET_QK_TOTAL = RET_HEADS * RET_QK_DIM
RET_V_TOTAL = RET_HEADS * RET_V_DIM
RET_IN_DIM = 2 * RET_QK_TOTAL + 2 * RET_V_TOTAL
RET_CHUNK = 128
ROPE_AXES = (64, 96, 96)
ROPE_THETA = 10000.0
CONV_WIDTH = 31
N_EXPERTS = 64
TOP_K = 8
N_GROUPS = 8
TOPK_GROUPS = 4
EXPERT_DIM = 256
SHARED_DIM = 256
ROUTE_SCALE = 2.5
MOE_BLOCK = 128
NORM_EPS = 1e-6
LN_EPS = 1e-5

kernel_name = 'hybrid_retention_conformer_moe_dit'


def rms_norm(x, g):
    xf = x.astype(jnp.float32)
    y = xf * lax.rsqrt(jnp.mean(xf * xf, -1, keepdims=True) + NORM_EPS)
    return (y * g.astype(jnp.float32)).astype(x.dtype)


def modulate(x, shift, scale):
    return x * (1 + scale) + shift


def rope_angles(pos):
    parts = []
    for a, d in enumerate(ROPE_AXES):
        inv = ROPE_THETA ** (-jnp.arange(0, d, 2, dtype=jnp.float32) / d)
        parts.append(pos[:, a:a + 1].astype(jnp.float32) * inv[None, :])
    return jnp.concatenate(parts, -1)


def apply_rope(x, ang):
    cos = jnp.cos(ang)[None, :, None, :].astype(x.dtype)
    sin = jnp.sin(ang)[None, :, None, :].astype(x.dtype)
    xr = x.reshape(x.shape[:-1] + (-1, 2))
    x0, x1 = xr[..., 0], xr[..., 1]
    return jnp.stack([x0 * cos - x1 * sin, x0 * sin + x1 * cos], -1).reshape(x.shape)


def retention_chunkwise(q, k, v, log_g, s0):
    b, h, n, dk = q.shape
    dv = v.shape[-1]
    c = RET_CHUNK
    nc = n // c
    qc = q.reshape(b, h, nc, c, dk)
    kc = k.reshape(b, h, nc, c, dk)
    vc = v.reshape(b, h, nc, c, dv)
    idx = jnp.arange(c, dtype=jnp.float32)
    rel = idx[:, None] - idx[None, :]
    dmask = jnp.where(rel >= 0, jnp.exp(log_g[:, None, None] * jnp.maximum(rel, 0.0)), 0.0).astype(q.dtype)
    scores = jnp.einsum('bhcid,bhcjd->bhcij', qc, kc) * dmask[None, :, None]
    o_intra = jnp.einsum('bhcij,bhcje->bhcie', scores, vc)
    q_dec = jnp.exp(log_g[:, None] * (idx + 1.0)).astype(q.dtype)[None, :, :, None]
    k_dec = jnp.exp(log_g[:, None] * (c - 1.0 - idx)).astype(q.dtype)[None, :, :, None]
    c_dec = jnp.exp(log_g * c).astype(q.dtype)[None, :, None, None]

    def step(s, inp):
        qi, ki, vi = inp
        o = jnp.einsum('bhid,bhde->bhie', qi * q_dec, s)
        s = s * c_dec + jnp.einsum('bhjd,bhje->bhde', ki * k_dec, vi)
        return s, o

    xs = (jnp.moveaxis(qc, 2, 0), jnp.moveaxis(kc, 2, 0), jnp.moveaxis(vc, 2, 0))
    _, o_inter = lax.scan(step, s0, xs)
    o = o_intra + jnp.moveaxis(o_inter, 0, 2)
    return o.reshape(b, h, n, dv)


def retention_final_state(k, v, log_g, reverse):
    n = k.shape[2]
    pos = jnp.arange(n, dtype=jnp.float32)
    dist = pos if reverse else (n - 1.0 - pos)
    w = jnp.exp(log_g[:, None] * dist).astype(k.dtype)
    return jnp.einsum('bhmd,bhme->bhde', k * w[None, :, :, None], v)


def retention_mixer(h_lat, h_ctx, ang_lat, ang_ctx, w_in, w_out, decay_logit, with_ctx_out):
    log_g = jax.nn.log_sigmoid(decay_logit.astype(jnp.float32))

    def heads(z, d):
        return z.reshape(z.shape[0], z.shape[1], RET_HEADS, d)

    def to_bhnd(t):
        return jnp.transpose(t, (0, 2, 1, 3))

    def qkvg(hh, ang):
        z = hh @ w_in
        q, k, v, g = jnp.split(z, [RET_QK_TOTAL, 2 * RET_QK_TOTAL, 2 * RET_QK_TOTAL + RET_V_TOTAL], -1)
        q = apply_rope(heads(q, RET_QK_DIM), ang) * (RET_QK_DIM ** -0.5)
        k = apply_rope(heads(k, RET_QK_DIM), ang)
        return to_bhnd(q), to_bhnd(k), to_bhnd(heads(v, RET_V_DIM)), g

    def bidir(q, k, v, s_f, s_b):
        fwd = retention_chunkwise(q, k, v, log_g[0], s_f)
        bwd = retention_chunkwise(jnp.flip(q, 2), jnp.flip(k, 2), jnp.flip(v, 2), log_g[1], s_b)
        return fwd + jnp.flip(bwd, 2)

    def readout(y, g):
        yf = y.astype(jnp.float32)
        yf = yf * lax.rsqrt(jnp.mean(yf * yf, -1, keepdims=True) + NORM_EPS)
        yb = jnp.transpose(yf.astype(y.dtype), (0, 2, 1, 3)).reshape(y.shape[0], y.shape[2], RET_V_TOTAL)
        return (yb * jax.nn.silu(g)) @ w_out

    if with_ctx_out:
        qc, kc, vc, gc = qkvg(h_ctx, ang_ctx)
    else:
        z = h_ctx @ w_in[:, RET_QK_TOTAL:2 * RET_QK_TOTAL + RET_V_TOTAL]
        kc_, vc_ = jnp.split(z, [RET_QK_TOTAL], -1)
        kc = to_bhnd(apply_rope(heads(kc_, RET_QK_DIM), ang_ctx))
        vc = to_bhnd(heads(vc_, RET_V_DIM))
    s_f = retention_final_state(kc, vc, log_g[0], False)
    s_b = retention_final_state(kc, vc, log_g[1], True)
    q, k, v, g = qkvg(h_lat, ang_lat)
    out_lat = readout(bidir(q, k, v, s_f, s_b), g)
    if with_ctx_out:
        zero = jnp.zeros_like(s_f)
        out_ctx = readout(bidir(qc, kc, vc, zero, zero), gc)
        return out_lat, out_ctx
    return out_lat, None


def conv_module(hh, w1, b1, w_dw, b_dw, ln_g, ln_b, w2, b2):
    z = hh @ w1 + b1
    a, gate = jnp.split(z, 2, -1)
    u = a * jax.nn.sigmoid(gate)
    u = lax.conv_general_dilated(u, w_dw[:, None, :].astype(u.dtype), window_strides=(1,),
                                 padding=[(CONV_WIDTH // 2, CONV_WIDTH // 2)],
                                 dimension_numbers=('NWC', 'WIO', 'NWC'),
                                 feature_group_count=D_MODEL) + b_dw
    uf = u.astype(jnp.float32)
    mu = jnp.mean(uf, -1, keepdims=True)
    var = jnp.mean(jnp.square(uf - mu), -1, keepdims=True)
    uf = (uf - mu) * lax.rsqrt(var + LN_EPS) * ln_g.astype(jnp.float32) + ln_b.astype(jnp.float32)
    u = jax.nn.silu(uf.astype(u.dtype))
    return u @ w2 + b2


def swiglu(t, w1, w3, w2):
    return (jax.nn.silu(t @ w1) * (t @ w3)) @ w2


def moe_ffn(hh, w_router, router_bias, w1, w3, w2, ws1, ws3, ws2):
    shp = hh.shape
    hf = hh.reshape(-1, D_MODEL)
    n = hf.shape[0]
    scores = jax.nn.sigmoid(hf.astype(jnp.float32) @ w_router.astype(jnp.float32))
    biased = scores + router_bias.astype(jnp.float32)
    per = N_EXPERTS // N_GROUPS
    grp_score = lax.top_k(biased.reshape(n, N_GROUPS, per), 2)[0].sum(-1)
    _, gidx = lax.top_k(grp_score, TOPK_GROUPS)
    gmask = jax.nn.one_hot(gidx, N_GROUPS, dtype=jnp.float32).sum(1) > 0
    emask = jnp.repeat(gmask, per, axis=1)
    _, eidx = lax.top_k(jnp.where(emask, biased, -jnp.inf), TOP_K)
    wsel = jnp.take_along_axis(scores, eidx, 1)
    wsel = wsel / wsel.sum(-1, keepdims=True) * ROUTE_SCALE
    a = n * TOP_K
    e_flat = eidx.reshape(a).astype(jnp.int32)
    tok_flat = jnp.arange(a, dtype=jnp.int32) // TOP_K
    w_flat = wsel.reshape(a)
    order = jnp.argsort(e_flat)
    e_s = e_flat[order]
    tok_s = tok_flat[order]
    w_s = w_flat[order]
    counts = jax.ops.segment_sum(jnp.ones(a, jnp.int32), e_flat, num_segments=N_EXPERTS)
    start = jnp.cumsum(counts) - counts
    pcounts = (counts + MOE_BLOCK - 1) // MOE_BLOCK * MOE_BLOCK
    pend = jnp.cumsum(pcounts)
    pstart = pend - pcounts
    dest = pstart[e_s] + (jnp.arange(a, dtype=jnp.int32) - start[e_s])
    n_rows = -(-a // MOE_BLOCK) * MOE_BLOCK + N_EXPERTS * MOE_BLOCK
    n_blk = n_rows // MOE_BLOCK
    row_tok = jnp.zeros(n_rows, jnp.int32).at[dest].set(tok_s)
    row_w = jnp.zeros(n_rows, jnp.float32).at[dest].set(w_s)
    blk_e = jnp.minimum(jnp.searchsorted(pend, jnp.arange(n_blk, dtype=jnp.int32) * MOE_BLOCK, side='right'),
                        N_EXPERTS - 1)

    def block(acc, inp):
        toks, ws, e = inp
        y = swiglu(hf[toks], w1[e], w3[e], w2[e])
        return acc.at[toks].add((y * ws[:, None]).astype(acc.dtype)), None

    routed, _ = lax.scan(block, jnp.zeros_like(hf),
                         (row_tok.reshape(n_blk, MOE_BLOCK), row_w.reshape(n_blk, MOE_BLOCK), blk_e))
    out = routed + swiglu(hf, ws1, ws3, ws2)
    return out.reshape(shp)


def setup_inputs(seed: int = 0) -> dict:
    key = jax.random.key(seed)
    ks = jax.random.split(key, 32)

    def nrm(k, shape, scale):
        return jax.random.normal(k, shape, jnp.float32) * scale

    D = D_MODEL
    p = 1.0 - 2.0 ** (-5.0 - np.arange(RET_HEADS))
    decay_base = jnp.asarray(np.log(p / (1.0 - p)), jnp.float32)
    return {
        'x': nrm(ks[0], (BATCH, SEQ, D), 1.0),
        'c': nrm(ks[1], (BATCH, D), 1.0),
        'ctx': nrm(ks[2], (BATCH, CTX_LEN, D), 1.0),
        'c_ctx': nrm(ks[3], (D,), 1.0),
        'mod_w': nrm(ks[4], (DEPTH, D, 6 * D), 0.5 * D ** -0.5),
        'mod_b': nrm(ks[5], (DEPTH, 6 * D), 0.02),
        'norms': 1.0 + nrm(ks[6], (DEPTH, 4, D), 0.05),
        'ret_w_in': nrm(ks[7], (N_RET_LAYERS, D, RET_IN_DIM), D ** -0.5),
        'ret_w_out': nrm(ks[8], (N_RET_LAYERS, RET_V_TOTAL, D), RET_V_TOTAL ** -0.5),
        'ret_decay': decay_base[None, None, :] + nrm(ks[9], (N_RET_LAYERS, 2, RET_HEADS), 0.1),
        'conv_w1': nrm(ks[10], (N_CONV_LAYERS, D, 2 * D), D ** -0.5),
        'conv_b1': nrm(ks[11], (N_CONV_LAYERS, 2 * D), 0.02),
        'conv_dw': nrm(ks[12], (N_CONV_LAYERS, CONV_WIDTH, D), CONV_WIDTH ** -0.5),
        'conv_b_dw': nrm(ks[13], (N_CONV_LAYERS, D), 0.02),
        'conv_ln_g': 1.0 + nrm(ks[14], (N_CONV_LAYERS, D), 0.05),
        'conv_ln_b': nrm(ks[15], (N_CONV_LAYERS, D), 0.02),
        'conv_w2': nrm(ks[16], (N_CONV_LAYERS, D, D), D ** -0.5),
        'conv_b2': nrm(ks[17], (N_CONV_LAYERS, D), 0.02),
        'moe_router': nrm(ks[18], (DEPTH, D, N_EXPERTS), D ** -0.5),
        'moe_bias': nrm(ks[19], (DEPTH, N_EXPERTS), 0.01),
        'moe_w1': nrm(ks[20], (DEPTH, N_EXPERTS, D, EXPERT_DIM), D ** -0.5),
        'moe_w3': nrm(ks[21], (DEPTH, N_EXPERTS, D, EXPERT_DIM), D ** -0.5),
        'moe_w2': nrm(ks[22], (DEPTH, N_EXPERTS, EXPERT_DIM, D), EXPERT_DIM ** -0.5),
        'shared_w1': nrm(ks[23], (DEPTH, D, SHARED_DIM), D ** -0.5),
        'shared_w3': nrm(ks[24], (DEPTH, D, SHARED_DIM), D ** -0.5),
        'shared_w2': nrm(ks[25], (DEPTH, SHARED_DIM, D), SHARED_DIM ** -0.5),
    }


def reference(x, c, ctx, c_ctx, mod_w, mod_b, norms, ret_w_in, ret_w_out, ret_decay,
              conv_w1, conv_b1, conv_dw, conv_b_dw, conv_ln_g, conv_ln_b, conv_w2, conv_b2,
              moe_router, moe_bias, moe_w1, moe_w3, moe_w2, shared_w1, shared_w3, shared_w2):
    T = x.shape[1]
    L = ctx.shape[1]
    ROWS = T // GRID_W
    rows, cols = jnp.meshgrid(jnp.arange(ROWS, dtype=jnp.int32), jnp.arange(GRID_W, dtype=jnp.int32), indexing='ij')
    pos_lat = jnp.stack([jnp.full((T,), L, jnp.int32), rows.reshape(-1), cols.reshape(-1)], -1)
    zl = jnp.zeros((L,), jnp.int32)
    pos_ctx = jnp.stack([jnp.arange(L, dtype=jnp.int32), zl, zl], -1)
    ang_lat = rope_angles(pos_lat)
    ang_ctx = rope_angles(pos_ctx)
    silu_c = jax.nn.silu(c)
    silu_cc = jax.nn.silu(c_ctx)

    h, hc = x, ctx
    for i in range(DEPTH):
        kind = i % N_MIXERS
        j = i // N_MIXERS
        ctx_after = any((l % N_MIXERS) == 0 for l in range(i + 1, DEPTH))
        mod = (silu_c @ mod_w[i] + mod_b[i])[:, None, :]
        sh_m, sc_m, g_m, sh_f, sc_f, g_f = jnp.split(mod, 6, -1)
        if kind == 0 or ctx_after:
            mod_c = silu_cc @ mod_w[i] + mod_b[i]
            csh_m, csc_m, cg_m, csh_f, csc_f, cg_f = jnp.split(mod_c, 6, -1)
            uc = modulate(rms_norm(hc, norms[i, 0]), csh_m, csc_m)
        u = modulate(rms_norm(h, norms[i, 0]), sh_m, sc_m)
        if kind == 0:
            y, yc = retention_mixer(u, uc, ang_lat, ang_ctx, ret_w_in[j], ret_w_out[j], ret_decay[j], ctx_after)
        else:
            conv_args = (conv_w1[j], conv_b1[j], conv_dw[j], conv_b_dw[j], conv_ln_g[j], conv_ln_b[j],
                         conv_w2[j], conv_b2[j])
            y = conv_module(u, *conv_args)
            yc = conv_module(uc, *conv_args) if ctx_after else None
        h = h + g_m * rms_norm(y, norms[i, 1])
        moe_args = (moe_router[i], moe_bias[i], moe_w1[i], moe_w3[i], moe_w2[i],
                    shared_w1[i], shared_w3[i], shared_w2[i])
        u = modulate(rms_norm(h, norms[i, 2]), sh_f, sc_f)
        h = h + g_f * rms_norm(moe_ffn(u, *moe_args), norms[i, 3])
        if ctx_after:
            hc = hc + cg_m * rms_norm(yc, norms[i, 1])
            uc = modulate(rms_norm(hc, norms[i, 2]), csh_f, csc_f)
            hc = hc + cg_f * rms_norm(moe_ffn(uc, *moe_args), norms[i, 3])
    return h
```

```python
import functools

import jax
import jax.numpy as jnp
import numpy as np
from jax import lax
from jax.experimental import pallas as pl
from jax.experimental.pallas import tpu as pltpu

F32 = jnp.float32
BF16 = jnp.bfloat16

D = 1024
HEADS = 4
DK = D // HEADS
DV = 2 * DK
QK_TOT = HEADS * DK
V_TOT = HEADS * DV
IN_DIM = 2 * QK_TOT + 2 * V_TOT
ROPE_AXES = (64, 96, 96)
ROPE_THETA = 10000.0
GRID_W = 64
CONV_W = 31
HALO = 16
N_EXP = 64
N_GRP = 8
PER_GRP = N_EXP // N_GRP
TOP_K = 8
TOP_G = 4
F_EXP = 256
ROUTE_SCALE = 2.5
GATE_LANES = 128
EXP_STEP = 4
NORM_EPS = 1e-6
LN_EPS = 1e-5
RET_CHUNK = 256
VMEM_LIMIT = 56 * 1024 * 1024

_HI = lax.Precision.HIGHEST


def _cparams(*sem):
    return pltpu.CompilerParams(dimension_semantics=sem, vmem_limit_bytes=VMEM_LIMIT)


def _dot(a, b):
    return jnp.dot(a, b, preferred_element_type=F32)


def _rms(xf, g):
    return xf * lax.rsqrt(jnp.mean(xf * xf, axis=-1, keepdims=True) + NORM_EPS) * g


def _silu(x):
    return x * jax.nn.sigmoid(x)


def _mod_kernel(cs_ref, w_ref, b_ref, o_ref):
    s = _silu(cs_ref[...])
    o_ref[...] = jnp.dot(s, w_ref[...], preferred_element_type=F32, precision=_HI) + b_ref[...]


def _mod_call(cs, mod_w, mod_b):
    depth, _, n6 = mod_w.shape
    tn = 1536
    rows = cs.shape[0]
    return pl.pallas_call(
        _mod_kernel,
        grid=(depth, n6 // tn),
        in_specs=[
            pl.BlockSpec((rows, D), lambda i, j: (0, 0)),
            pl.BlockSpec((None, D, tn), lambda i, j: (i, 0, j)),
            pl.BlockSpec((None, 1, tn), lambda i, j: (i, 0, j)),
        ],
        out_specs=pl.BlockSpec((None, rows, tn), lambda i, j: (i, 0, j)),
        out_shape=jax.ShapeDtypeStruct((depth, rows, n6), F32),
        compiler_params=_cparams("parallel", "parallel"),
        name="mod",
    )(cs, mod_w, mod_b.reshape(depth, 1, n6))


def _rope_angles(pos):
    parts = []
    for a, d in enumerate(ROPE_AXES):
        inv = ROPE_THETA ** (-jnp.arange(0, d, 2, dtype=F32) / d)
        parts.append(pos[:, a:a + 1].astype(F32) * inv[None, :])
    return jnp.concatenate(parts, -1)


def _rope_store(z, cos, sin, scale, o_ref, col):
    half = DK // 2
    x0, x1 = z[:, :half], z[:, half:]
    o_ref[:, col:col + half] = ((x0 * cos - x1 * sin) * scale).astype(o_ref.dtype)
    o_ref[:, col + half:col + DK] = ((x0 * sin + x1 * cos) * scale).astype(o_ref.dtype)


def _inproj_kernel(x_ref, g_ref, mod_ref, w_ref, cos_ref, sin_ref, q_ref, k_ref, v_ref, gt_ref):
    u = (_rms(x_ref[...], g_ref[...]) * (1.0 + mod_ref[1:2, :]) + mod_ref[0:1, :]).astype(BF16)
    cos, sin = cos_ref[...], sin_ref[...]
    for h in range(HEADS):
        _rope_store(_dot(u, w_ref[:, h * DK:(h + 1) * DK]), cos, sin, DK ** -0.5, q_ref, h * DK)
        _rope_store(_dot(u, w_ref[:, QK_TOT + h * DK:QK_TOT + (h + 1) * DK]), cos, sin, 1.0, k_ref, h * DK)
    step = 512
    for j in range(V_TOT // step):
        c0 = 2 * QK_TOT + j * step
        v_ref[:, j * step:(j + 1) * step] = _dot(u, w_ref[:, c0:c0 + step]).astype(BF16)
        c1 = 2 * QK_TOT + V_TOT + j * step
        gt_ref[:, j * step:(j + 1) * step] = _dot(u, w_ref[:, c1:c1 + step]).astype(BF16)


def _inproj_call(x, g, mod_l, w_in, cos, sin, tm):
    b, t, _ = x.shape
    grid = (b, t // tm)
    tok = lambda n: pl.BlockSpec((None, tm, n), lambda i, j: (i, j, 0))
    return pl.pallas_call(
        _inproj_kernel,
        grid=grid,
        in_specs=[
            tok(D),
            pl.BlockSpec((1, D), lambda i, j: (0, 0)),
            pl.BlockSpec((None, 6, D), lambda i, j: (i, 0, 0)),
            pl.BlockSpec((D, IN_DIM), lambda i, j: (0, 0), pipeline_mode=pl.Buffered(1)),
            pl.BlockSpec((tm, DK // 2), lambda i, j: (j, 0)),
            pl.BlockSpec((tm, DK // 2), lambda i, j: (j, 0)),
        ],
        out_specs=[tok(QK_TOT), tok(QK_TOT), tok(V_TOT), tok(V_TOT)],
        out_shape=[
            jax.ShapeDtypeStruct((b, t, QK_TOT), BF16),
            jax.ShapeDtypeStruct((b, t, QK_TOT), BF16),
            jax.ShapeDtypeStruct((b, t, V_TOT), BF16),
            jax.ShapeDtypeStruct((b, t, V_TOT), BF16),
        ],
        compiler_params=_cparams("parallel", "parallel"),
        name="inproj",
    )(x, g, mod_l, w_in, cos, sin)


def _row_pow(lg, expo_fn, rows, cols):
    i = lax.broadcasted_iota(jnp.int32, (rows, cols), 0).astype(F32)
    return jnp.exp(lg * expo_fn(i))


def _ctxstate_kernel(lg_ref, c_ref, g_ref, mod_ref, w_ref, cos_ref, sin_ref, sf_ref, sb_ref, k_scr):
    l = c_ref.shape[0]
    u = (_rms(c_ref[...], g_ref[...]) * (1.0 + mod_ref[1:2, :]) + mod_ref[0:1, :]).astype(BF16)
    cos, sin = cos_ref[...], sin_ref[...]
    for h in range(HEADS):
        _rope_store(_dot(u, w_ref[:, h * DK:(h + 1) * DK]), cos, sin, 1.0, k_scr, h * DK)
    for h in range(HEADS):
        kh = k_scr[:, h * DK:(h + 1) * DK]
        vh = _dot(u, w_ref[:, QK_TOT + h * DV:QK_TOT + (h + 1) * DV]).astype(BF16)
        wf = _row_pow(lg_ref[0, h], lambda i: (l - 1.0) - i, l, DK)
        wb = _row_pow(lg_ref[1, h], lambda i: i, l, DK)
        sf_ref[h] = _dot((kh * wf).T.astype(BF16), vh)
        sb_ref[h] = _dot((kh * wb).T.astype(BF16), vh)


def _ctxstate_call(log_g, ctx, g, mod_row, w_kv, cos, sin):
    b, l, _ = ctx.shape
    st = jax.ShapeDtypeStruct((b, HEADS, DK, DV), F32)
    st_spec = pl.BlockSpec((None, HEADS, DK, DV), lambda i: (i, 0, 0, 0))
    return pl.pallas_call(
        _ctxstate_kernel,
        grid=(b,),
        in_specs=[
            pl.BlockSpec(memory_space=pltpu.SMEM),
            pl.BlockSpec((None, l, D), lambda i: (i, 0, 0)),
            pl.BlockSpec((1, D), lambda i: (0, 0)),
            pl.BlockSpec((6, D), lambda i: (0, 0)),
            pl.BlockSpec((D, QK_TOT + V_TOT), lambda i: (0, 0)),
            pl.BlockSpec((l, DK // 2), lambda i: (0, 0)),
            pl.BlockSpec((l, DK // 2), lambda i: (0, 0)),
        ],
        out_specs=[st_spec, st_spec],
        out_shape=[st, st],
        scratch_shapes=[pltpu.VMEM((l, QK_TOT), F32)],
        compiler_params=_cparams("parallel"),
        name="ctxstate",
    )(log_g, ctx, g, mod_row, w_kv, cos, sin)


def _retention_kernel(lg_ref, q_ref, k_ref, v_ref, sf0_ref, sb0_ref, o_ref, acc, sf, sb):
    h = pl.program_id(1)
    t = q_ref.shape[0]
    c = RET_CHUNK
    nc = t // c
    lgf, lgb = lg_ref[0, h], lg_ref[1, h]
    ri = lax.broadcasted_iota(jnp.int32, (c, c), 0).astype(F32)
    ci = lax.broadcasted_iota(jnp.int32, (c, c), 1).astype(F32)
    rel = ri - ci
    mask = (jnp.where(rel >= 0, jnp.exp(lgf * jnp.maximum(rel, 0.0)), 0.0)
            + jnp.where(rel <= 0, jnp.exp(lgb * jnp.maximum(-rel, 0.0)), 0.0))
    qdf = _row_pow(lgf, lambda i: i + 1.0, c, DV)
    qdb = _row_pow(lgb, lambda i: c - i, c, DV)
    kdf = _row_pow(lgf, lambda i: (c - 1.0) - i, c, DK)
    kdb = _row_pow(lgb, lambda i: i, c, DK)
    cdf = jnp.exp(jnp.full((1, DV), lgf * c, F32))
    cdb = jnp.exp(jnp.full((1, DV), lgb * c, F32))
    sf[...] = sf0_ref[...]
    sb[...] = sb0_ref[...]

    def fwd(ic, carry):
        r0 = pl.multiple_of(ic * c, c)
        q = q_ref[pl.ds(r0, c), :]
        k = k_ref[pl.ds(r0, c), :]
        v = v_ref[pl.ds(r0, c), :]
        s = lax.dot_general(q, k, (((1,), (1,)), ((), ())), preferred_element_type=F32)
        o = _dot((s * mask).astype(BF16), v) + qdf * _dot(q, sf[...].astype(BF16))
        acc[pl.ds(r0, c), :] = o
        sf[...] = sf[...] * cdf + _dot((k.astype(F32) * kdf).T.astype(BF16), v)
        return carry

    lax.fori_loop(0, nc, fwd, 0)

    def bwd(jc, carry):
        ic = nc - 1 - jc
        r0 = pl.multiple_of(ic * c, c)
        q = q_ref[pl.ds(r0, c), :]
        k = k_ref[pl.ds(r0, c), :]
        v = v_ref[pl.ds(r0, c), :]
        o = acc[pl.ds(r0, c), :] + qdb * _dot(q, sb[...].astype(BF16))
        o_ref[pl.ds(r0, c), :] = (o * lax.rsqrt(jnp.mean(o * o, axis=-1, keepdims=True) + NORM_EPS)).astype(BF16)
        sb[...] = sb[...] * cdb + _dot((k.astype(F32) * kdb).T.astype(BF16), v)
        return carry

    lax.fori_loop(0, nc, bwd, 0)


def _retention_call(log_g, q, k, v, sf0, sb0):
    b, t, _ = q.shape
    qk_spec = pl.BlockSpec((None, t, DK), lambda i, h: (i, 0, h))
    v_spec = pl.BlockSpec((None, t, DV), lambda i, h: (i, 0, h))
    s_spec = pl.BlockSpec((None, None, DK, DV), lambda i, h: (i, h, 0, 0))
    return pl.pallas_call(
        _retention_kernel,
        grid=(b, HEADS),
        in_specs=[pl.BlockSpec(memory_space=pltpu.SMEM), qk_spec, qk_spec, v_spec, s_spec, s_spec],
        out_specs=v_spec,
        out_shape=jax.ShapeDtypeStruct((b, t, V_TOT), BF16),
        scratch_shapes=[pltpu.VMEM((t, DV), F32), pltpu.VMEM((DK, DV), F32), pltpu.VMEM((DK, DV), F32)],
        compiler_params=_cparams("parallel", "parallel"),
        name="retention",
    )(log_g, q, k, v, sf0, sb0)


def _mixer_epilogue(y, h_ref, n_ref, mod_ref, wr_ref, h_out, u_out, lg_out):
    h1 = h_ref[...] + mod_ref[2:3, :] * _rms(y, n_ref[1:2, :])
    h_out[...] = h1
    u2 = _rms(h1, n_ref[2:3, :]) * (1.0 + mod_ref[4:5, :]) + mod_ref[3:4, :]
    u_out[...] = u2.astype(BF16)
    lg_out[...] = lax.dot_general(wr_ref[...], u2, (((1,), (1,)), ((), ())),
                                  preferred_element_type=F32, precision=_HI)


def _readout_kernel(y_ref, gt_ref, h_ref, n_ref, mod_ref, w_ref, wr_ref, h_out, u_out, lg_out):
    a = (y_ref[...].astype(F32) * _silu(gt_ref[...].astype(F32))).astype(BF16)
    _mixer_epilogue(_dot(a, w_ref[...]), h_ref, n_ref, mod_ref, wr_ref, h_out, u_out, lg_out)


def _epilogue_specs(b, t, tm):
    nt = t // tm
    tok = lambda n: pl.BlockSpec((None, tm, n), lambda i, j: (i, j, 0))
    in_specs = [
        tok(D),
        pl.BlockSpec((4, D), lambda i, j: (0, 0)),
        pl.BlockSpec((None, 6, D), lambda i, j: (i, 0, 0)),
    ]
    wr_spec = pl.BlockSpec((N_EXP, D), lambda i, j: (0, 0))
    out_specs = [tok(D), tok(D), pl.BlockSpec((N_EXP, tm), lambda i, j: (0, i * nt + j))]
    out_shape = [
        jax.ShapeDtypeStruct((b, t, D), F32),
        jax.ShapeDtypeStruct((b, t, D), BF16),
        jax.ShapeDtypeStruct((N_EXP, b * t), F32),
    ]
    return tok, in_specs, wr_spec, out_specs, out_shape


def _readout_call(yn, gt, h, norms_l, mod_l, w_out, wr_t, tm):
    b, t, _ = h.shape
    tok, ep_in, wr_spec, out_specs, out_shape = _epilogue_specs(b, t, tm)
    return pl.pallas_call(
        _readout_kernel,
        grid=(b, t // tm),
        in_specs=[tok(V_TOT), tok(V_TOT)] + ep_in
        + [pl.BlockSpec((V_TOT, D), lambda i, j: (0, 0)), wr_spec],
        out_specs=out_specs,
        out_shape=out_shape,
        compiler_params=_cparams("parallel", "parallel"),
        name="readout",
    )(yn, gt, h, norms_l, mod_l, w_out, wr_t)


def _route_kernel(lg_ref, b_ref, gate_ref):
    tr = lg_ref.shape[1]
    s = jax.nn.sigmoid(lg_ref[...])
    biased = s + b_ref[...]
    neg = -jnp.inf
    b3 = biased.reshape(N_GRP, PER_GRP, tr)
    io3 = lax.broadcasted_iota(jnp.int32, b3.shape, 1).astype(F32)
    m1 = jnp.max(b3, axis=1, keepdims=True)
    i1 = jnp.min(jnp.where(b3 == m1, io3, float(PER_GRP)), axis=1, keepdims=True)
    m2 = jnp.max(jnp.where(io3 == i1, neg, b3), axis=1, keepdims=True)
    gs = (m1 + m2).reshape(N_GRP, tr)
    iog = lax.broadcasted_iota(jnp.int32, gs.shape, 0).astype(F32)
    gsel = jnp.zeros(gs.shape, F32)
    work = gs
    for _ in range(TOP_G):
        m = jnp.max(work, axis=0, keepdims=True)
        gi = jnp.min(jnp.where(work == m, iog, float(N_GRP)), axis=0, keepdims=True)
        hit = iog == gi
        gsel = jnp.where(hit, 1.0, gsel)
        work = jnp.where(hit, neg, work)
    emask = jnp.broadcast_to(gsel.reshape(N_GRP, 1, tr), b3.shape).reshape(N_EXP, tr)
    work = jnp.where(emask > 0.0, biased, neg)
    ioe = lax.broadcasted_iota(jnp.int32, work.shape, 0).astype(F32)
    esel = jnp.zeros(work.shape, F32)
    for _ in range(TOP_K):
        m = jnp.max(work, axis=0, keepdims=True)
        ei = jnp.min(jnp.where(work == m, ioe, float(N_EXP)), axis=0, keepdims=True)
        hit = ioe == ei
        esel = jnp.where(hit, 1.0, esel)
        work = jnp.where(hit, neg, work)
    wsel = esel * s
    gate = wsel / jnp.sum(wsel, axis=0, keepdims=True) * ROUTE_SCALE
    gate_ref[...] = jnp.concatenate([gate, jnp.zeros((GATE_LANES - N_EXP, tr), F32)], 0).T


def _route_call(logits_t, bias, tr):
    n = logits_t.shape[1]
    return pl.pallas_call(
        _route_kernel,
        grid=(n // tr,),
        in_specs=[pl.BlockSpec((N_EXP, tr), lambda i: (0, i)), pl.BlockSpec((N_EXP, 1), lambda i: (0, 0))],
        out_specs=pl.BlockSpec((tr, GATE_LANES), lambda i: (i, 0)),
        out_shape=jax.ShapeDtypeStruct((n, GATE_LANES), F32),
        compiler_params=_cparams("parallel"),
        name="route",
    )(logits_t, bias.reshape(N_EXP, 1))


def _swiglu(u, w1, w3, w2):
    return _dot((_silu(_dot(u, w1)) * _dot(u, w3)).astype(BF16), w2)


def _moe_kernel(u_ref, gate_ref, w1_ref, w3_ref, w2_ref, s1_ref, s3_ref, s2_ref, h_ref, n_ref, mod_ref,
                o_ref, acc):
    eg = pl.program_id(2)
    u = u_ref[...]

    @pl.when(eg == 0)
    def _():
        acc[...] = _swiglu(u, s1_ref[...], s3_ref[...], s2_ref[...])

    g_all = gate_ref[...]
    lane = lax.broadcasted_iota(jnp.int32, g_all.shape, 1)
    tot = jnp.zeros(acc.shape, F32)
    for j in range(EXP_STEP):
        g = jnp.sum(jnp.where(lane == eg * EXP_STEP + j, g_all, 0.0), axis=1, keepdims=True)
        hid = _silu(_dot(u, w1_ref[j])) * _dot(u, w3_ref[j]) * g
        tot = tot + _dot(hid.astype(BF16), w2_ref[j])
    acc[...] += tot

    @pl.when(eg == pl.num_programs(2) - 1)
    def _():
        o_ref[...] = h_ref[...] + mod_ref[5:6, :] * _rms(acc[...], n_ref[3:4, :])


def _moe_call(u2, gate_t, w1, w3, w2, s1, s3, s2, h, norms_l, mod_l, tm):
    b, t, _ = h.shape
    nt = t // tm
    tok = lambda n: pl.BlockSpec((None, tm, n), lambda i, j, e: (i, j, 0))
    const = lambda shp: pl.BlockSpec(shp, lambda i, j, e: (0,) * len(shp))
    return pl.pallas_call(
        _moe_kernel,
        grid=(b, nt, N_EXP // EXP_STEP),
        in_specs=[
            tok(D),
            pl.BlockSpec((tm, GATE_LANES), lambda i, j, e: (i * nt + j, 0)),
            pl.BlockSpec((EXP_STEP, D, F_EXP), lambda i, j, e: (e, 0, 0)),
            pl.BlockSpec((EXP_STEP, D, F_EXP), lambda i, j, e: (e, 0, 0)),
            pl.BlockSpec((EXP_STEP, F_EXP, D), lambda i, j, e: (e, 0, 0)),
            const((D, F_EXP)), const((D, F_EXP)), const((F_EXP, D)),
            tok(D),
            const((4, D)),
            pl.BlockSpec((None, 6, D), lambda i, j, e: (i, 0, 0)),
        ],
        out_specs=tok(D),
        out_shape=jax.ShapeDtypeStruct((b, t, D), F32),
        scratch_shapes=[pltpu.VMEM((tm, D), F32)],
        compiler_params=_cparams("parallel", "parallel", "arbitrary"),
        name="moe",
    )(u2, gate_t, w1, w3, w2, s1, s3, s2, h, norms_l, mod_l)


def _glu_kernel(h_ref, g_ref, mod_ref, w_ref, b_ref, o_ref):
    u = (_rms(h_ref[...], g_ref[...]) * (1.0 + mod_ref[1:2, :]) + mod_ref[0:1, :]).astype(BF16)
    a = _dot(u, w_ref[:, :D]) + b_ref[:, :D]
    gate = _dot(u, w_ref[:, D:]) + b_ref[:, D:]
    o_ref[...] = (a * jax.nn.sigmoid(gate)).astype(BF16)


def _glu_call(h, g, mod_l, w1, b1, tm):
    b, t, _ = h.shape
    tok = lambda n: pl.BlockSpec((None, tm, n), lambda i, j: (i, j, 0))
    return pl.pallas_call(
        _glu_kernel,
        grid=(b, t // tm),
        in_specs=[
            tok(D),
            pl.BlockSpec((1, D), lambda i, j: (0, 0)),
            pl.BlockSpec((None, 6, D), lambda i, j: (i, 0, 0)),
            pl.BlockSpec((D, 2 * D), lambda i, j: (0, 0)),
            pl.BlockSpec((1, 2 * D), lambda i, j: (0, 0)),
        ],
        out_specs=tok(D),
        out_shape=jax.ShapeDtypeStruct((b, t, D), BF16),
        compiler_params=_cparams("parallel", "parallel"),
        name="glu",
    )(h, g, mod_l, w1, b1)


def _conv_kernel(cur_ref, prev_ref, next_ref, dw_ref, cv_ref, w2_ref, h_ref, n_ref, mod_ref, wr_ref,
                 h_out, u_out, lg_out, win):
    j = pl.program_id(1)
    tm = cur_ref.shape[0]
    pad = CONV_W // 2
    win[0:HALO, :] = jnp.where(j > 0, prev_ref[...].astype(F32), 0.0)
    win[HALO:HALO + tm, :] = cur_ref[...].astype(F32)
    win[HALO + tm:, :] = jnp.where(j < pl.num_programs(1) - 1, next_ref[...].astype(F32), 0.0)
    acc = jnp.zeros((tm, D), F32)
    for tap in range(CONV_W):
        off = HALO - pad + tap
        acc = acc + win[off:off + tm, :] * dw_ref[tap:tap + 1, :]
    uf = acc + cv_ref[0:1, :]
    mu = jnp.mean(uf, axis=-1, keepdims=True)
    var = jnp.mean(jnp.square(uf - mu), axis=-1, keepdims=True)
    uf = (uf - mu) * lax.rsqrt(var + LN_EPS) * cv_ref[1:2, :] + cv_ref[2:3, :]
    y = _dot(_silu(uf).astype(BF16), w2_ref[...]) + cv_ref[3:4, :]
    _mixer_epilogue(y, h_ref, n_ref, mod_ref, wr_ref, h_out, u_out, lg_out)


def _conv_call(glu, dw, cvec, w2, h, norms_l, mod_l, wr_t, tm):
    b, t, _ = h.shape
    tok, ep_in, wr_spec, out_specs, out_shape = _epilogue_specs(b, t, tm)
    r = tm // HALO
    last = t // HALO - 1
    return pl.pallas_call(
        _conv_kernel,
        grid=(b, t // tm),
        in_specs=[
            tok(D),
            pl.BlockSpec((None, HALO, D), lambda i, j: (i, jnp.maximum(j * r - 1, 0), 0)),
            pl.BlockSpec((None, HALO, D), lambda i, j: (i, jnp.minimum((j + 1) * r, last), 0)),
            pl.BlockSpec((CONV_W, D), lambda i, j: (0, 0)),
            pl.BlockSpec((4, D), lambda i, j: (0, 0)),
            pl.BlockSpec((D, D), lambda i, j: (0, 0)),
        ] + ep_in + [wr_spec],
        out_specs=out_specs,
        out_shape=out_shape,
        scratch_shapes=[pltpu.VMEM((tm + 2 * HALO, D), F32)],
        compiler_params=_cparams("parallel", "parallel"),
        name="conv",
    )(glu, glu, glu, dw, cvec, w2, h, norms_l, mod_l, wr_t)


def _tile(t, pref):
    return pref if t % pref == 0 else t


def kernel(x, c, ctx, c_ctx, mod_w, mod_b, norms, ret_w_in, ret_w_out, ret_decay, conv_w1, conv_b1, conv_dw,
           conv_b_dw, conv_ln_g, conv_ln_b, conv_w2, conv_b2, moe_router, moe_bias, moe_w1, moe_w3, moe_w2,
           shared_w1, shared_w3, shared_w2):
    b, t, _ = x.shape
    l = ctx.shape[1]
    tm = _tile(t, 512)

    rows = -(-(b + 1) // 8) * 8
    cs = jnp.concatenate([c, c_ctx[None, :], jnp.zeros((rows - b - 1, D), F32)], 0)
    mod = _mod_call(cs, mod_w, mod_b).reshape(mod_w.shape[0], rows, 6, D)

    idx = jnp.arange(t, dtype=jnp.int32)
    pos_lat = jnp.stack([jnp.full((t,), l, jnp.int32), idx // GRID_W, idx % GRID_W], -1)
    zl = jnp.zeros((l,), jnp.int32)
    pos_ctx = jnp.stack([jnp.arange(l, dtype=jnp.int32), zl, zl], -1)
    ang_lat, ang_ctx = _rope_angles(pos_lat), _rope_angles(pos_ctx)

    head_perm = np.concatenate([np.arange(0, DK, 2), np.arange(1, DK, 2)])
    perm = np.concatenate([hh * DK + head_perm for hh in range(2 * HEADS)] + [np.arange(2 * QK_TOT, IN_DIM)])
    w_in = ret_w_in[0][:, perm].astype(BF16)
    log_g = jax.nn.log_sigmoid(ret_decay[0].astype(F32))

    def moe_layer(i, h1, u2, logits_t):
        gate_t = _route_call(logits_t, moe_bias[i], _tile(b * t, 1024))
        return _moe_call(u2, gate_t, moe_w1[i].astype(BF16), moe_w3[i].astype(BF16), moe_w2[i].astype(BF16),
                         shared_w1[i].astype(BF16), shared_w3[i].astype(BF16), shared_w2[i].astype(BF16),
                         h1, norms[i], mod[i], _tile(t, 1024))

    q, k, v, gt = _inproj_call(x, norms[0, 0:1], mod[0], w_in, jnp.cos(ang_lat), jnp.sin(ang_lat), tm)
    sf0, sb0 = _ctxstate_call(log_g, ctx, norms[0, 0:1], mod[0, b], w_in[:, QK_TOT:2 * QK_TOT + V_TOT],
                              jnp.cos(ang_ctx), jnp.sin(ang_ctx))
    yn = _retention_call(log_g, q, k, v, sf0, sb0)
    h1, u2, logits_t = _readout_call(yn, gt, x, norms[0], mod[0], ret_w_out[0].astype(BF16),
                                     moe_router[0].T, tm)
    h2 = moe_layer(0, h1, u2, logits_t)

    glu = _glu_call(h2, norms[1, 0:1], mod[1], conv_w1[0].astype(BF16), conv_b1[0][None, :], tm)
    cvec = jnp.stack([conv_b_dw[0], conv_ln_g[0], conv_ln_b[0], conv_b2[0]], 0)
    h3, u2, logits_t = _conv_call(glu, conv_dw[0], cvec, conv_w2[0].astype(BF16), h2, norms[1], mod[1],
                                  moe_router[1].T, tm)
    return moe_layer(1, h3, u2, logits_t)
```

```python
import functools

import jax
import jax.numpy as jnp
import numpy as np
from jax import lax
from jax.experimental import pallas as pl
from jax.experimental.pallas import tpu as pltpu

F32 = jnp.float32
BF16 = jnp.bfloat16

D = 1024
HEADS = 4
DK = D // HEADS
DV = 2 * DK
QK_TOT = HEADS * DK
V_TOT = HEADS * DV
IN_DIM = 2 * QK_TOT + 2 * V_TOT
ROPE_AXES = (64, 96, 96)
ROPE_THETA = 10000.0
GRID_W = 64
CONV_W = 31
HALO = 16
N_EXP = 64
N_GRP = 8
PER_GRP = N_EXP // N_GRP
TOP_K = 8
TOP_G = 4
F_EXP = 256
ROUTE_SCALE = 2.5
GATE_LANES = 128
MXU_DIM = 256
TM_MOE = 256
SEG_ALIGN = 8
BM = 512
PK = D // 2
NORM_EPS = 1e-6
LN_EPS = 1e-5
RET_CHUNK = 256
VMEM_LIMIT = 56 * 1024 * 1024

_HI = lax.Precision.HIGHEST


def _cparams(*sem):
    return pltpu.CompilerParams(dimension_semantics=sem, vmem_limit_bytes=VMEM_LIMIT)


def _dot(a, b):
    return jnp.dot(a, b, preferred_element_type=F32)


def _rms(xf, g):
    return xf * lax.rsqrt(jnp.mean(xf * xf, axis=-1, keepdims=True) + NORM_EPS) * g


def _silu(x):
    return x * jax.nn.sigmoid(x)


def _mod_kernel(cs_ref, w_ref, b_ref, o_ref):
    s = _silu(cs_ref[...])
    o_ref[...] = jnp.dot(s, w_ref[...], preferred_element_type=F32, precision=_HI) + b_ref[...]


def _mod_call(cs, mod_w, mod_b):
    depth, _, n6 = mod_w.shape
    tn = 1536
    rows = cs.shape[0]
    return pl.pallas_call(
        _mod_kernel,
        grid=(depth, n6 // tn),
        in_specs=[
            pl.BlockSpec((rows, D), lambda i, j: (0, 0)),
            pl.BlockSpec((None, D, tn), lambda i, j: (i, 0, j)),
            pl.BlockSpec((None, 1, tn), lambda i, j: (i, 0, j)),
        ],
        out_specs=pl.BlockSpec((None, rows, tn), lambda i, j: (i, 0, j)),
        out_shape=jax.ShapeDtypeStruct((depth, rows, n6), F32),
        compiler_params=_cparams("parallel", "parallel"),
        name="mod",
    )(cs, mod_w, mod_b.reshape(depth, 1, n6))


def _rope_angles(pos):
    parts = []
    for a, d in enumerate(ROPE_AXES):
        inv = ROPE_THETA ** (-jnp.arange(0, d, 2, dtype=F32) / d)
        parts.append(pos[:, a:a + 1].astype(F32) * inv[None, :])
    return jnp.concatenate(parts, -1)


def _rope_store(z, cos, sin, scale, o_ref, col):
    half = DK // 2
    x0, x1 = z[:, :half], z[:, half:]
    o_ref[:, col:col + half] = ((x0 * cos - x1 * sin) * scale).astype(o_ref.dtype)
    o_ref[:, col + half:col + DK] = ((x0 * sin + x1 * cos) * scale).astype(o_ref.dtype)


def _inproj_kernel(x_ref, g_ref, mod_ref, w_ref, cos_ref, sin_ref, q_ref, k_ref, v_ref, gt_ref):
    u = (_rms(x_ref[...], g_ref[...]) * (1.0 + mod_ref[1:2, :]) + mod_ref[0:1, :]).astype(BF16)
    cos, sin = cos_ref[...], sin_ref[...]
    for h in range(HEADS):
        _rope_store(_dot(u, w_ref[:, h * DK:(h + 1) * DK]), cos, sin, DK ** -0.5, q_ref, h * DK)
        _rope_store(_dot(u, w_ref[:, QK_TOT + h * DK:QK_TOT + (h + 1) * DK]), cos, sin, 1.0, k_ref, h * DK)
    step = 512
    for j in range(V_TOT // step):
        c0 = 2 * QK_TOT + j * step
        v_ref[:, j * step:(j + 1) * step] = _dot(u, w_ref[:, c0:c0 + step]).astype(BF16)
        c1 = 2 * QK_TOT + V_TOT + j * step
        gt_ref[:, j * step:(j + 1) * step] = _dot(u, w_ref[:, c1:c1 + step]).astype(BF16)


def _inproj_call(x, g, mod_l, w_in, cos, sin, tm):
    b, t, _ = x.shape
    grid = (b, t // tm)
    tok = lambda n: pl.BlockSpec((None, tm, n), lambda i, j: (i, j, 0))
    return pl.pallas_call(
        _inproj_kernel,
        grid=grid,
        in_specs=[
            tok(D),
            pl.BlockSpec((1, D), lambda i, j: (0, 0)),
            pl.BlockSpec((None, 6, D), lambda i, j: (i, 0, 0)),
            pl.BlockSpec((D, IN_DIM), lambda i, j: (0, 0), pipeline_mode=pl.Buffered(1)),
            pl.BlockSpec((tm, DK // 2), lambda i, j: (j, 0)),
            pl.BlockSpec((tm, DK // 2), lambda i, j: (j, 0)),
        ],
        out_specs=[tok(QK_TOT), tok(QK_TOT), tok(V_TOT), tok(V_TOT)],
        out_shape=[
            jax.ShapeDtypeStruct((b, t, QK_TOT), BF16),
            jax.ShapeDtypeStruct((b, t, QK_TOT), BF16),
            jax.ShapeDtypeStruct((b, t, V_TOT), BF16),
            jax.ShapeDtypeStruct((b, t, V_TOT), BF16),
        ],
        compiler_params=_cparams("parallel", "parallel"),
        name="inproj",
    )(x, g, mod_l, w_in, cos, sin)


def _row_pow(lg, expo_fn, rows, cols):
    i = lax.broadcasted_iota(jnp.int32, (rows, cols), 0).astype(F32)
    return jnp.exp(lg * expo_fn(i))


def _ctxstate_kernel(lg_ref, c_ref, g_ref, mod_ref, w_ref, cos_ref, sin_ref, sf_ref, sb_ref, k_scr):
    l = c_ref.shape[0]
    u = (_rms(c_ref[...], g_ref[...]) * (1.0 + mod_ref[1:2, :]) + mod_ref[0:1, :]).astype(BF16)
    cos, sin = cos_ref[...], sin_ref[...]
    for h in range(HEADS):
        _rope_store(_dot(u, w_ref[:, h * DK:(h + 1) * DK]), cos, sin, 1.0, k_scr, h * DK)
    for h in range(HEADS):
        kh = k_scr[:, h * DK:(h + 1) * DK]
        vh = _dot(u, w_ref[:, QK_TOT + h * DV:QK_TOT + (h + 1) * DV]).astype(BF16)
        wf = _row_pow(lg_ref[0, h], lambda i: (l - 1.0) - i, l, DK)
        wb = _row_pow(lg_ref[1, h], lambda i: i, l, DK)
        sf_ref[h] = _dot((kh * wf).T.astype(BF16), vh)
        sb_ref[h] = _dot((kh * wb).T.astype(BF16), vh)


def _ctxstate_call(log_g, ctx, g, mod_row, w_kv, cos, sin):
    b, l, _ = ctx.shape
    st = jax.ShapeDtypeStruct((b, HEADS, DK, DV), F32)
    st_spec = pl.BlockSpec((None, HEADS, DK, DV), lambda i: (i, 0, 0, 0))
    return pl.pallas_call(
        _ctxstate_kernel,
        grid=(b,),
        in_specs=[
            pl.BlockSpec(memory_space=pltpu.SMEM),
            pl.BlockSpec((None, l, D), lambda i: (i, 0, 0)),
            pl.BlockSpec((1, D), lambda i: (0, 0)),
            pl.BlockSpec((6, D), lambda i: (0, 0)),
            pl.BlockSpec((D, QK_TOT + V_TOT), lambda i: (0, 0)),
            pl.BlockSpec((l, DK // 2), lambda i: (0, 0)),
            pl.BlockSpec((l, DK // 2), lambda i: (0, 0)),
        ],
        out_specs=[st_spec, st_spec],
        out_shape=[st, st],
        scratch_shapes=[pltpu.VMEM((l, QK_TOT), F32)],
        compiler_params=_cparams("parallel"),
        name="ctxstate",
    )(log_g, ctx, g, mod_row, w_kv, cos, sin)


def _retention_kernel(lg_ref, q_ref, k_ref, v_ref, sf0_ref, sb0_ref, o_ref, acc, sf, sb):
    h = pl.program_id(1)
    t = q_ref.shape[0]
    c = RET_CHUNK
    nc = t // c
    lgf, lgb = lg_ref[0, h], lg_ref[1, h]
    ri = lax.broadcasted_iota(jnp.int32, (c, c), 0).astype(F32)
    ci = lax.broadcasted_iota(jnp.int32, (c, c), 1).astype(F32)
    rel = ri - ci
    mask = (jnp.where(rel >= 0, jnp.exp(lgf * jnp.maximum(rel, 0.0)), 0.0)
            + jnp.where(rel <= 0, jnp.exp(lgb * jnp.maximum(-rel, 0.0)), 0.0))
    qdf = _row_pow(lgf, lambda i: i + 1.0, c, DV)
    qdb = _row_pow(lgb, lambda i: c - i, c, DV)
    kdf = _row_pow(lgf, lambda i: (c - 1.0) - i, c, DK)
    kdb = _row_pow(lgb, lambda i: i, c, DK)
    cdf = jnp.exp(jnp.full((1, DV), lgf * c, F32))
    cdb = jnp.exp(jnp.full((1, DV), lgb * c, F32))
    sf[...] = sf0_ref[...]
    sb[...] = sb0_ref[...]

    def fwd(ic, carry):
        r0 = pl.multiple_of(ic * c, c)
        q = q_ref[pl.ds(r0, c), :]
        k = k_ref[pl.ds(r0, c), :]
        v = v_ref[pl.ds(r0, c), :]
        s = lax.dot_general(q, k, (((1,), (1,)), ((), ())), preferred_element_type=F32)
        o = _dot((s * mask).astype(BF16), v) + qdf * _dot(q, sf[...].astype(BF16))
        acc[pl.ds(r0, c), :] = o
        sf[...] = sf[...] * cdf + _dot((k.astype(F32) * kdf).T.astype(BF16), v)
        return carry

    lax.fori_loop(0, nc, fwd, 0)

    def bwd(jc, carry):
        ic = nc - 1 - jc
        r0 = pl.multiple_of(ic * c, c)
        q = q_ref[pl.ds(r0, c), :]
        k = k_ref[pl.ds(r0, c), :]
        v = v_ref[pl.ds(r0, c), :]
        o = acc[pl.ds(r0, c), :] + qdb * _dot(q, sb[...].astype(BF16))
        o_ref[pl.ds(r0, c), :] = (o * lax.rsqrt(jnp.mean(o * o, axis=-1, keepdims=True) + NORM_EPS)).astype(BF16)
        sb[...] = sb[...] * cdb + _dot((k.astype(F32) * kdb).T.astype(BF16), v)
        return carry

    lax.fori_loop(0, nc, bwd, 0)


def _retention_call(log_g, q, k, v, sf0, sb0):
    b, t, _ = q.shape
    qk_spec = pl.BlockSpec((None, t, DK), lambda i, h: (i, 0, h))
    v_spec = pl.BlockSpec((None, t, DV), lambda i, h: (i, 0, h))
    s_spec = pl.BlockSpec((None, None, DK, DV), lambda i, h: (i, h, 0, 0))
    return pl.pallas_call(
        _retention_kernel,
        grid=(b, HEADS),
        in_specs=[pl.BlockSpec(memory_space=pltpu.SMEM), qk_spec, qk_spec, v_spec, s_spec, s_spec],
        out_specs=v_spec,
        out_shape=jax.ShapeDtypeStruct((b, t, V_TOT), BF16),
        scratch_shapes=[pltpu.VMEM((t, DV), F32), pltpu.VMEM((DK, DV), F32), pltpu.VMEM((DK, DV), F32)],
        compiler_params=_cparams("parallel", "parallel"),
        name="retention",
    )(log_g, q, k, v, sf0, sb0)


def _mixer_epilogue(y, h_ref, n_ref, mod_ref, wr_ref, h_out, u_out, lg_out):
    h1 = h_ref[...] + mod_ref[2:3, :] * _rms(y, n_ref[1:2, :])
    h_out[...] = h1
    u2 = _rms(h1, n_ref[2:3, :]) * (1.0 + mod_ref[4:5, :]) + mod_ref[3:4, :]
    u_out[...] = u2.astype(BF16)
    lg_out[...] = lax.dot_general(wr_ref[...], u2, (((1,), (1,)), ((), ())),
                                  preferred_element_type=F32, precision=_HI)


def _readout_kernel(y_ref, gt_ref, h_ref, n_ref, mod_ref, w_ref, wr_ref, h_out, u_out, lg_out):
    a = (y_ref[...].astype(F32) * _silu(gt_ref[...].astype(F32))).astype(BF16)
    _mixer_epilogue(_dot(a, w_ref[...]), h_ref, n_ref, mod_ref, wr_ref, h_out, u_out, lg_out)


def _epilogue_specs(b, t, tm):
    nt = t // tm
    tok = lambda n: pl.BlockSpec((None, tm, n), lambda i, j: (i, j, 0))
    in_specs = [
        tok(D),
        pl.BlockSpec((4, D), lambda i, j: (0, 0)),
        pl.BlockSpec((None, 6, D), lambda i, j: (i, 0, 0)),
    ]
    wr_spec = pl.BlockSpec((N_EXP, D), lambda i, j: (0, 0))
    out_specs = [tok(D), tok(D), pl.BlockSpec((N_EXP, tm), lambda i, j: (0, i * nt + j))]
    out_shape = [
        jax.ShapeDtypeStruct((b, t, D), F32),
        jax.ShapeDtypeStruct((b, t, D), BF16),
        jax.ShapeDtypeStruct((N_EXP, b * t), F32),
    ]
    return tok, in_specs, wr_spec, out_specs, out_shape


def _readout_call(yn, gt, h, norms_l, mod_l, w_out, wr_t, tm):
    b, t, _ = h.shape
    tok, ep_in, wr_spec, out_specs, out_shape = _epilogue_specs(b, t, tm)
    return pl.pallas_call(
        _readout_kernel,
        grid=(b, t // tm),
        in_specs=[tok(V_TOT), tok(V_TOT)] + ep_in
        + [pl.BlockSpec((V_TOT, D), lambda i, j: (0, 0)), wr_spec],
        out_specs=out_specs,
        out_shape=out_shape,
        compiler_params=_cparams("parallel", "parallel"),
        name="readout",
    )(yn, gt, h, norms_l, mod_l, w_out, wr_t)


def _route_kernel(lg_ref, b_ref, pos_ref, posw_ref, pc_ref, off_ref):
    i = pl.program_id(0)
    tr = lg_ref.shape[1]
    s = jax.nn.sigmoid(lg_ref[...])
    biased = s + b_ref[...]
    neg = -jnp.inf
    b3 = biased.reshape(N_GRP, PER_GRP, tr)
    io3 = lax.broadcasted_iota(jnp.int32, b3.shape, 1).astype(F32)
    m1 = jnp.max(b3, axis=1, keepdims=True)
    i1 = jnp.min(jnp.where(b3 == m1, io3, float(PER_GRP)), axis=1, keepdims=True)
    m2 = jnp.max(jnp.where(io3 == i1, neg, b3), axis=1, keepdims=True)
    gs = (m1 + m2).reshape(N_GRP, tr)
    iog = lax.broadcasted_iota(jnp.int32, gs.shape, 0).astype(F32)
    gsel = jnp.zeros(gs.shape, F32)
    work = gs
    for _ in range(TOP_G):
        m = jnp.max(work, axis=0, keepdims=True)
        gi = jnp.min(jnp.where(work == m, iog, float(N_GRP)), axis=0, keepdims=True)
        hit = iog == gi
        gsel = jnp.where(hit, 1.0, gsel)
        work = jnp.where(hit, neg, work)
    emask = jnp.broadcast_to(gsel.reshape(N_GRP, 1, tr), b3.shape).reshape(N_EXP, tr)
    work = jnp.where(emask > 0.0, biased, neg)
    ioe = lax.broadcasted_iota(jnp.int32, work.shape, 0).astype(F32)
    esel = jnp.zeros(work.shape, F32)
    hits = []
    for _ in range(TOP_K):
        m = jnp.max(work, axis=0, keepdims=True)
        ei = jnp.min(jnp.where(work == m, ioe, float(N_EXP)), axis=0, keepdims=True)
        hit = ioe == ei
        hits.append(hit)
        esel = jnp.where(hit, 1.0, esel)
        work = jnp.where(hit, neg, work)
    wsel = esel * s
    gate = wsel / jnp.sum(wsel, axis=0, keepdims=True) * ROUTE_SCALE

    t_r = lax.broadcasted_iota(jnp.int32, (tr, tr), 0)
    t_c = lax.broadcasted_iota(jnp.int32, (tr, tr), 1)
    rank = _dot(esel.astype(BF16), jnp.where(t_r < t_c, 1.0, 0.0).astype(BF16))
    cnt = jnp.sum(esel, axis=1, keepdims=True)
    pc = jnp.floor((cnt + (SEG_ALIGN - 1.0)) * (1.0 / SEG_ALIGN)) * SEG_ALIGN
    e_r = lax.broadcasted_iota(jnp.int32, (N_EXP, N_EXP), 0)
    e_c = lax.broadcasted_iota(jnp.int32, (N_EXP, N_EXP), 1)
    off = jnp.dot(jnp.where(e_c < e_r, 1.0, 0.0), jnp.broadcast_to(pc, (N_EXP, 128)),
                  preferred_element_type=F32, precision=_HI)[:, 0:1]
    p = off + rank
    posm = jnp.concatenate([jnp.sum(jnp.where(h, p, 0.0), axis=0, keepdims=True) for h in hits], 0)
    wm = jnp.concatenate([jnp.sum(jnp.where(h, gate, 0.0), axis=0, keepdims=True) for h in hits], 0)
    pos_ref[...] = posm
    posw_ref[...] = jnp.concatenate([posm, wm, jnp.zeros((GATE_LANES - 2 * TOP_K, tr), F32)], 0).T

    @pl.when(i == 0)
    def _():
        pc_ref[...] = jnp.zeros(pc_ref.shape, F32)
        off_ref[...] = jnp.zeros(off_ref.shape, F32)

    tile_lane = lax.broadcasted_iota(jnp.int32, pc_ref.shape, 1)
    pc_ref[...] = jnp.where(tile_lane == i, pc, pc_ref[...])
    off_ref[...] = jnp.where(tile_lane == i, off, off_ref[...])


def _route_call(logits_t, bias):
    n = logits_t.shape[1]
    nt = n // TM_MOE
    return pl.pallas_call(
        _route_kernel,
        grid=(nt,),
        in_specs=[pl.BlockSpec((N_EXP, TM_MOE), lambda i: (0, i)), pl.BlockSpec((N_EXP, 1), lambda i: (0, 0))],
        out_specs=[
            pl.BlockSpec((TOP_K, TM_MOE), lambda i: (0, i)),
            pl.BlockSpec((TM_MOE, GATE_LANES), lambda i: (i, 0)),
            pl.BlockSpec((N_EXP, nt), lambda i: (0, 0)),
            pl.BlockSpec((N_EXP, nt), lambda i: (0, 0)),
        ],
        out_shape=[
            jax.ShapeDtypeStruct((TOP_K, n), F32),
            jax.ShapeDtypeStruct((n, GATE_LANES), F32),
            jax.ShapeDtypeStruct((N_EXP, nt), F32),
            jax.ShapeDtypeStruct((N_EXP, nt), F32),
        ],
        compiler_params=_cparams("arbitrary"),
        name="route",
    )(logits_t, bias.reshape(N_EXP, 1))


def _dispatch_tables(pc_t, off_t, nb_max):
    pc = pc_t.astype(jnp.int32)
    region = (pc.sum(1) + BM - 1) // BM * BM
    ends = jnp.cumsum(region)
    dst = (ends - region)[:, None] + jnp.cumsum(pc, 1) - pc
    nblk = ends[-1] // BM
    blk = jnp.minimum(jnp.arange(nb_max, dtype=jnp.int32), nblk - 1)
    blk_e = jnp.minimum(jnp.searchsorted(ends, blk * BM, side='right'), N_EXP - 1).astype(jnp.int32)
    flat = lambda a: a.T.reshape(-1).astype(jnp.int32)
    return flat(off_t), flat(dst), flat(pc), pc.sum(0).astype(jnp.int32), blk_e, nblk.reshape(1).astype(jnp.int32)


def _swiglu(u, w1, w3, w2):
    return _dot((_silu(_dot(u, w1)) * _dot(u, w3)).astype(BF16), w2)


def _round_bf16(x):
    return x.astype(BF16).astype(F32)


def _pack_pair(lo, hi):
    lo_b = lax.bitcast_convert_type(lo, jnp.uint32)
    hi_b = lax.bitcast_convert_type(hi, jnp.uint32)
    return (hi_b & jnp.uint32(0xFFFF0000)) | (lo_b >> 16)


def _unpack_pair(w):
    lo = lax.bitcast_convert_type(w << 16, F32).astype(BF16)
    hi = lax.bitcast_convert_type(w & jnp.uint32(0xFFFF0000), F32).astype(BF16)
    return lo, hi


def _segment_copies(so_ref, do_ref, ln_ref, tile, make_copy):
    def seg(e, carry):
        j = tile * N_EXP + e
        ln = pl.multiple_of(ln_ref[j], SEG_ALIGN)

        @pl.when(ln > 0)
        def _():
            make_copy(pl.multiple_of(so_ref[j], SEG_ALIGN), pl.multiple_of(do_ref[j], SEG_ALIGN), ln).start()

        return carry

    lax.fori_loop(0, N_EXP, seg, 0)


def _dispatch_kernel(so_ref, do_ref, ln_ref, tt_ref, u_ref, pos_ref, xs_in, xs_out, buf, sem):
    del xs_in
    i = pl.program_id(0)
    slot = i % 2
    r, tm = buf.shape[1], u_ref.shape[0]

    def strip(slot_, so, do, ln):
        return pltpu.make_async_copy(buf.at[slot_, pl.ds(so, ln)], xs_out.at[pl.ds(do, ln)], sem.at[slot_])

    def wait_tile(slot_, tile):
        strip(slot_, 0, 0, pl.multiple_of(tt_ref[tile], SEG_ALIGN)).wait()

    @pl.when(i >= 2)
    def _():
        wait_tile(slot, i - 2)

    pos = pos_ref[...]
    row = lax.broadcasted_iota(jnp.int32, (r, tm), 0).astype(F32)
    perm = jnp.zeros((r, tm), F32)
    for k in range(TOP_K):
        perm = jnp.where(row == pos[k:k + 1, :], 1.0, perm)
    perm = perm.astype(BF16)
    step = 256
    for c in range(0, PK, step):
        buf[slot, :, c:c + step] = _pack_pair(_dot(perm, u_ref[:, c:c + step]),
                                              _dot(perm, u_ref[:, PK + c:PK + c + step]))
    _segment_copies(so_ref, do_ref, ln_ref, i, functools.partial(strip, slot))

    @pl.when(i == pl.num_programs(0) - 1)
    def _():
        wait_tile(slot, i)

        @pl.when(i >= 1)
        def _():
            wait_tile(1 - slot, i - 1)


def _expert_kernel(be_ref, nb_ref, x_ref, w1_ref, w3_ref, w2_ref, y_ref, w1b, w3b, w2b):
    i = pl.program_id(0)
    live = i < nb_ref[0]

    @pl.when(live & ((i == 0) | (be_ref[i] != be_ref[jnp.maximum(i - 1, 0)])))
    def _():
        w1b[...] = w1_ref[...].astype(BF16)
        w3b[...] = w3_ref[...].astype(BF16)
        w2b[...] = w2_ref[...].astype(BF16)

    @pl.when(live)
    def _():
        lo, hi = _unpack_pair(x_ref[...])
        a = _dot(lo, w1b[:PK, :]) + _dot(hi, w1b[PK:, :])
        g = _dot(lo, w3b[:PK, :]) + _dot(hi, w3b[PK:, :])
        hid = (_silu(a) * g).astype(BF16)
        y_ref[...] = _pack_pair(_round_bf16(_dot(hid, w2b[:, :PK])), _round_bf16(_dot(hid, w2b[:, PK:])))


def _combine_kernel(so_ref, do_ref, ln_ref, tt_ref, posw_ref, u_ref, ys_ref, s1_ref, s3_ref, s2_ref, h_ref, n_ref,
                    mod_ref, o_ref, buf, sem):
    i = pl.program_id(0)
    nt = pl.num_programs(0)
    slot = i % 2
    r, tm = buf.shape[1], u_ref.shape[0]

    def strip(slot_, so, do, ln):
        return pltpu.make_async_copy(ys_ref.at[pl.ds(do, ln)], buf.at[slot_, pl.ds(so, ln)], sem.at[slot_])

    @pl.when(i == 0)
    def _():
        buf[...] = jnp.zeros(buf.shape, jnp.uint32)
        _segment_copies(so_ref, do_ref, ln_ref, 0, functools.partial(strip, 0))

    @pl.when(i + 1 < nt)
    def _():
        _segment_copies(so_ref, do_ref, ln_ref, i + 1, functools.partial(strip, 1 - slot))

    strip(slot, 0, 0, pl.multiple_of(tt_ref[i], SEG_ALIGN)).wait()

    pw = posw_ref[...]
    col = lax.broadcasted_iota(jnp.int32, (tm, r), 1).astype(F32)
    q = jnp.zeros((tm, r), F32)
    for k in range(TOP_K):
        q = jnp.where(col == pw[:, k:k + 1], pw[:, TOP_K + k:TOP_K + k + 1], q)
    q = q.astype(BF16)
    lo, hi = _unpack_pair(buf[slot])
    moe = jnp.concatenate([_dot(q, lo), _dot(q, hi)], axis=1)
    moe = moe + _swiglu(u_ref[...], s1_ref[...], s3_ref[...], s2_ref[...])
    o_ref[...] = h_ref[...] + mod_ref[5:6, :] * _rms(moe, n_ref[3:4, :])


def _sorted_rows(tm):
    return -(-(TOP_K * tm + N_EXP * (SEG_ALIGN - 1)) // MXU_DIM) * MXU_DIM


def _max_blocks(n):
    nt = n // TM_MOE
    rows = TOP_K * n + N_EXP * (SEG_ALIGN - 1) * nt + N_EXP * (BM - SEG_ALIGN)
    return -(-rows // BM)


def _dispatch_call(tabs, u2, pos, nb_max):
    n = u2.shape[0]
    r = _sorted_rows(TM_MOE)
    xs0 = jnp.zeros((nb_max * BM, PK), jnp.uint32)
    return pl.pallas_call(
        _dispatch_kernel,
        grid_spec=pltpu.PrefetchScalarGridSpec(
            num_scalar_prefetch=4,
            grid=(n // TM_MOE,),
            in_specs=[
                pl.BlockSpec((TM_MOE, D), lambda i, *_: (i, 0)),
                pl.BlockSpec((TOP_K, TM_MOE), lambda i, *_: (0, i)),
                pl.BlockSpec(memory_space=pl.ANY),
            ],
            out_specs=pl.BlockSpec(memory_space=pl.ANY),
            scratch_shapes=[pltpu.VMEM((2, r, PK), jnp.uint32), pltpu.SemaphoreType.DMA((2,))],
        ),
        out_shape=jax.ShapeDtypeStruct(xs0.shape, jnp.uint32),
        input_output_aliases={6: 0},
        compiler_params=_cparams("arbitrary"),
        name="dispatch",
    )(*tabs, u2, pos, xs0)


def _expert_call(blk_e, nblk, xs, w1, w3, w2):
    nb_max = xs.shape[0] // BM
    row_blk = lambda i, be, nb: (jnp.minimum(i, nb[0] - 1), 0)
    return pl.pallas_call(
        _expert_kernel,
        grid_spec=pltpu.PrefetchScalarGridSpec(
            num_scalar_prefetch=2,
            grid=(nb_max,),
            in_specs=[
                pl.BlockSpec((BM, PK), row_blk),
                pl.BlockSpec((None, D, F_EXP), lambda i, be, nb: (be[i], 0, 0)),
                pl.BlockSpec((None, D, F_EXP), lambda i, be, nb: (be[i], 0, 0)),
                pl.BlockSpec((None, F_EXP, D), lambda i, be, nb: (be[i], 0, 0)),
            ],
            out_specs=pl.BlockSpec((BM, PK), row_blk),
            scratch_shapes=[pltpu.VMEM((D, F_EXP), BF16), pltpu.VMEM((D, F_EXP), BF16),
                            pltpu.VMEM((F_EXP, D), BF16)],
        ),
        out_shape=jax.ShapeDtypeStruct(xs.shape, jnp.uint32),
        input_output_aliases={2: 0},
        compiler_params=_cparams("arbitrary"),
        name="experts",
    )(blk_e, nblk, xs, w1, w3, w2)


def _combine_call(tabs, posw, u2, ys, s1, s3, s2, h, norms_l, mod_l, t):
    n = u2.shape[0]
    r = _sorted_rows(TM_MOE)
    tok = lambda w: pl.BlockSpec((TM_MOE, w), lambda i, *_: (i, 0))
    const = lambda shp: pl.BlockSpec(shp, lambda i, *_: (0,) * len(shp))
    return pl.pallas_call(
        _combine_kernel,
        grid_spec=pltpu.PrefetchScalarGridSpec(
            num_scalar_prefetch=4,
            grid=(n // TM_MOE,),
            in_specs=[
                tok(GATE_LANES),
                tok(D),
                pl.BlockSpec(memory_space=pl.ANY),
                const((D, F_EXP)), const((D, F_EXP)), const((F_EXP, D)),
                tok(D),
                const((4, D)),
                pl.BlockSpec((None, 6, D), lambda i, *_: (i * TM_MOE // t, 0, 0)),
            ],
            out_specs=tok(D),
            scratch_shapes=[pltpu.VMEM((2, r, PK), jnp.uint32), pltpu.SemaphoreType.DMA((2,))],
        ),
        out_shape=jax.ShapeDtypeStruct((n, D), F32),
        compiler_params=_cparams("arbitrary"),
        name="combine",
    )(*tabs, posw, u2, ys, s1, s3, s2, h, norms_l, mod_l)


def _glu_kernel(h_ref, g_ref, mod_ref, w_ref, b_ref, o_ref):
    u = (_rms(h_ref[...], g_ref[...]) * (1.0 + mod_ref[1:2, :]) + mod_ref[0:1, :]).astype(BF16)
    a = _dot(u, w_ref[:, :D]) + b_ref[:, :D]
    gate = _dot(u, w_ref[:, D:]) + b_ref[:, D:]
    o_ref[...] = (a * jax.nn.sigmoid(gate)).astype(BF16)


def _glu_call(h, g, mod_l, w1, b1, tm):
    b, t, _ = h.shape
    tok = lambda n: pl.BlockSpec((None, tm, n), lambda i, j: (i, j, 0))
    return pl.pallas_call(
        _glu_kernel,
        grid=(b, t // tm),
        in_specs=[
            tok(D),
            pl.BlockSpec((1, D), lambda i, j: (0, 0)),
            pl.BlockSpec((None, 6, D), lambda i, j: (i, 0, 0)),
            pl.BlockSpec((D, 2 * D), lambda i, j: (0, 0)),
            pl.BlockSpec((1, 2 * D), lambda i, j: (0, 0)),
        ],
        out_specs=tok(D),
        out_shape=jax.ShapeDtypeStruct((b, t, D), BF16),
        compiler_params=_cparams("parallel", "parallel"),
        name="glu",
    )(h, g, mod_l, w1, b1)


def _conv_kernel(cur_ref, prev_ref, next_ref, dw_ref, cv_ref, w2_ref, h_ref, n_ref, mod_ref, wr_ref,
                 h_out, u_out, lg_out, win):
    j = pl.program_id(1)
    tm = cur_ref.shape[0]
    pad = CONV_W // 2
    win[0:HALO, :] = jnp.where(j > 0, prev_ref[...].astype(F32), 0.0)
    win[HALO:HALO + tm, :] = cur_ref[...].astype(F32)
    win[HALO + tm:, :] = jnp.where(j < pl.num_programs(1) - 1, next_ref[...].astype(F32), 0.0)
    acc = jnp.zeros((tm, D), F32)
    for tap in range(CONV_W):
        off = HALO - pad + tap
        acc = acc + win[off:off + tm, :] * dw_ref[tap:tap + 1, :]
    uf = acc + cv_ref[0:1, :]
    mu = jnp.mean(uf, axis=-1, keepdims=True)
    var = jnp.mean(jnp.square(uf - mu), axis=-1, keepdims=True)
    uf = (uf - mu) * lax.rsqrt(var + LN_EPS) * cv_ref[1:2, :] + cv_ref[2:3, :]
    y = _dot(_silu(uf).astype(BF16), w2_ref[...]) + cv_ref[3:4, :]
    _mixer_epilogue(y, h_ref, n_ref, mod_ref, wr_ref, h_out, u_out, lg_out)


def _conv_call(glu, dw, cvec, w2, h, norms_l, mod_l, wr_t, tm):
    b, t, _ = h.shape
    tok, ep_in, wr_spec, out_specs, out_shape = _epilogue_specs(b, t, tm)
    r = tm // HALO
    last = t // HALO - 1
    return pl.pallas_call(
        _conv_kernel,
        grid=(b, t // tm),
        in_specs=[
            tok(D),
            pl.BlockSpec((None, HALO, D), lambda i, j: (i, jnp.maximum(j * r - 1, 0), 0)),
            pl.BlockSpec((None, HALO, D), lambda i, j: (i, jnp.minimum((j + 1) * r, last), 0)),
            pl.BlockSpec((CONV_W, D), lambda i, j: (0, 0)),
            pl.BlockSpec((4, D), lambda i, j: (0, 0)),
            pl.BlockSpec((D, D), lambda i, j: (0, 0)),
        ] + ep_in + [wr_spec],
        out_specs=out_specs,
        out_shape=out_shape,
        scratch_shapes=[pltpu.VMEM((tm + 2 * HALO, D), F32)],
        compiler_params=_cparams("parallel", "parallel"),
        name="conv",
    )(glu, glu, glu, dw, cvec, w2, h, norms_l, mod_l, wr_t)


def _tile(t, pref):
    return pref if t % pref == 0 else t


def kernel(x, c, ctx, c_ctx, mod_w, mod_b, norms, ret_w_in, ret_w_out, ret_decay, conv_w1, conv_b1, conv_dw,
           conv_b_dw, conv_ln_g, conv_ln_b, conv_w2, conv_b2, moe_router, moe_bias, moe_w1, moe_w3, moe_w2,
           shared_w1, shared_w3, shared_w2):
    b, t, _ = x.shape
    l = ctx.shape[1]
    tm = _tile(t, 512)

    rows = -(-(b + 1) // 8) * 8
    cs = jnp.concatenate([c, c_ctx[None, :], jnp.zeros((rows - b - 1, D), F32)], 0)
    mod = _mod_call(cs, mod_w, mod_b).reshape(mod_w.shape[0], rows, 6, D)

    idx = jnp.arange(t, dtype=jnp.int32)
    pos_lat = jnp.stack([jnp.full((t,), l, jnp.int32), idx // GRID_W, idx % GRID_W], -1)
    zl = jnp.zeros((l,), jnp.int32)
    pos_ctx = jnp.stack([jnp.arange(l, dtype=jnp.int32), zl, zl], -1)
    ang_lat, ang_ctx = _rope_angles(pos_lat), _rope_angles(pos_ctx)

    head_perm = np.concatenate([np.arange(0, DK, 2), np.arange(1, DK, 2)])
    perm = np.concatenate([hh * DK + head_perm for hh in range(2 * HEADS)] + [np.arange(2 * QK_TOT, IN_DIM)])
    w_in = ret_w_in[0][:, perm].astype(BF16)
    log_g = jax.nn.log_sigmoid(ret_decay[0].astype(F32))

    n = b * t
    nb_max = _max_blocks(n)

    def moe_layer(i, h1, u2, logits_t):
        pos, posw, pc_t, off_t = _route_call(logits_t, moe_bias[i])
        so, do, ln, tt, blk_e, nblk = _dispatch_tables(pc_t, off_t, nb_max)
        u2f = u2.reshape(n, D)
        xs = _dispatch_call((so, do, ln, tt), u2f, pos, nb_max)
        ys = _expert_call(blk_e, nblk, xs, moe_w1[i], moe_w3[i], moe_w2[i])
        out = _combine_call((so, do, ln, tt), posw, u2f, ys, shared_w1[i].astype(BF16), shared_w3[i].astype(BF16),
                            shared_w2[i].astype(BF16), h1.reshape(n, D), norms[i], mod[i], t)
        return out.reshape(b, t, D)

    q, k, v, gt = _inproj_call(x, norms[0, 0:1], mod[0], w_in, jnp.cos(ang_lat), jnp.sin(ang_lat), tm)
    sf0, sb0 = _ctxstate_call(log_g, ctx, norms[0, 0:1], mod[0, b], w_in[:, QK_TOT:2 * QK_TOT + V_TOT],
                              jnp.cos(ang_ctx), jnp.sin(ang_ctx))
    yn = _retention_call(log_g, q, k, v, sf0, sb0)
    h1, u2, logits_t = _readout_call(yn, gt, x, norms[0], mod[0], ret_w_out[0].astype(BF16),
                                     moe_router[0].T, tm)
    h2 = moe_layer(0, h1, u2, logits_t)

    glu = _glu_call(h2, norms[1, 0:1], mod[1], conv_w1[0].astype(BF16), conv_b1[0][None, :], tm)
    cvec = jnp.stack([conv_b_dw[0], conv_ln_g[0], conv_ln_b[0], conv_b2[0]], 0)
    h3, u2, logits_t = _conv_call(glu, conv_dw[0], cvec, conv_w2[0].astype(BF16), h2, norms[1], mod[1],
                                  moe_router[1].T, tm)
    return moe_layer(1, h3, u2, logits_t)
```

```python
import functools

import jax
import jax.numpy as jnp
import numpy as np
from jax import lax
from jax.experimental import pallas as pl
from jax.experimental.pallas import tpu as pltpu

F32 = jnp.float32
BF16 = jnp.bfloat16

D = 1024
HEADS = 4
DK = D // HEADS
DV = 2 * DK
QK_TOT = HEADS * DK
V_TOT = HEADS * DV
IN_DIM = 2 * QK_TOT + 2 * V_TOT
ROPE_AXES = (64, 96, 96)
ROPE_THETA = 10000.0
GRID_W = 64
CONV_W = 31
HALO = 16
SUBLANES = 8
CONV_LANES = 256
N_EXP = 64
N_GRP = 8
PER_GRP = N_EXP // N_GRP
TOP_K = 8
TOP_G = 4
F_EXP = 256
ROUTE_SCALE = 2.5
GATE_LANES = 128
MXU_DIM = 256
TM_MOE = 256
SEG_ALIGN = 8
BM = 512
EXP_SUB = 256
PK = D // 2
NORM_EPS = 1e-6
LN_EPS = 1e-5
RET_CHUNK = 256
VMEM_LIMIT = 56 * 1024 * 1024

_HI = lax.Precision.HIGHEST


def _cparams(*sem):
    return pltpu.CompilerParams(dimension_semantics=sem, vmem_limit_bytes=VMEM_LIMIT)


def _dot(a, b):
    return jnp.dot(a, b, preferred_element_type=F32)


def _rms(xf, g):
    return xf * lax.rsqrt(jnp.mean(xf * xf, axis=-1, keepdims=True) + NORM_EPS) * g


def _silu(x):
    return x * jax.nn.sigmoid(x)


def _mod_kernel(cs_ref, w_ref, b_ref, o_ref):
    s = _silu(cs_ref[...])
    o_ref[...] = jnp.dot(s, w_ref[...], preferred_element_type=F32, precision=_HI) + b_ref[...]


def _mod_call(cs, mod_w, mod_b):
    depth, _, n6 = mod_w.shape
    tn = 1536
    rows = cs.shape[0]
    return pl.pallas_call(
        _mod_kernel,
        grid=(depth, n6 // tn),
        in_specs=[
            pl.BlockSpec((rows, D), lambda i, j: (0, 0)),
            pl.BlockSpec((None, D, tn), lambda i, j: (i, 0, j)),
            pl.BlockSpec((None, 1, tn), lambda i, j: (i, 0, j)),
        ],
        out_specs=pl.BlockSpec((None, rows, tn), lambda i, j: (i, 0, j)),
        out_shape=jax.ShapeDtypeStruct((depth, rows, n6), F32),
        compiler_params=_cparams("parallel", "parallel"),
        name="mod",
    )(cs, mod_w, mod_b.reshape(depth, 1, n6))


def _rope_angles(pos):
    parts = []
    for a, d in enumerate(ROPE_AXES):
        inv = ROPE_THETA ** (-jnp.arange(0, d, 2, dtype=F32) / d)
        parts.append(pos[:, a:a + 1].astype(F32) * inv[None, :])
    return jnp.concatenate(parts, -1)


def _rope_store(z, cos, sin, scale, o_ref, col):
    half = DK // 2
    x0, x1 = z[:, :half], z[:, half:]
    o_ref[:, col:col + half] = ((x0 * cos - x1 * sin) * scale).astype(o_ref.dtype)
    o_ref[:, col + half:col + DK] = ((x0 * sin + x1 * cos) * scale).astype(o_ref.dtype)


def _inproj_kernel(x_ref, g_ref, mod_ref, w_ref, cos_ref, sin_ref, q_ref, k_ref, v_ref, gt_ref):
    u = (_rms(x_ref[...], g_ref[...]) * (1.0 + mod_ref[1:2, :]) + mod_ref[0:1, :]).astype(BF16)
    cos, sin = cos_ref[...], sin_ref[...]
    for h in range(HEADS):
        _rope_store(_dot(u, w_ref[:, h * DK:(h + 1) * DK]), cos, sin, DK ** -0.5, q_ref, h * DK)
        _rope_store(_dot(u, w_ref[:, QK_TOT + h * DK:QK_TOT + (h + 1) * DK]), cos, sin, 1.0, k_ref, h * DK)
    step = 512
    for j in range(V_TOT // step):
        c0 = 2 * QK_TOT + j * step
        v_ref[:, j * step:(j + 1) * step] = _dot(u, w_ref[:, c0:c0 + step]).astype(BF16)
        c1 = 2 * QK_TOT + V_TOT + j * step
        gt_ref[:, j * step:(j + 1) * step] = _dot(u, w_ref[:, c1:c1 + step]).astype(BF16)


def _inproj_call(x, g, mod_l, w_in, cos, sin, tm):
    b, t, _ = x.shape
    grid = (b, t // tm)
    tok = lambda n: pl.BlockSpec((None, tm, n), lambda i, j: (i, j, 0))
    return pl.pallas_call(
        _inproj_kernel,
        grid=grid,
        in_specs=[
            tok(D),
            pl.BlockSpec((1, D), lambda i, j: (0, 0)),
            pl.BlockSpec((None, 6, D), lambda i, j: (i, 0, 0)),
            pl.BlockSpec((D, IN_DIM), lambda i, j: (0, 0), pipeline_mode=pl.Buffered(1)),
            pl.BlockSpec((tm, DK // 2), lambda i, j: (j, 0)),
            pl.BlockSpec((tm, DK // 2), lambda i, j: (j, 0)),
        ],
        out_specs=[tok(QK_TOT), tok(QK_TOT), tok(V_TOT), tok(V_TOT)],
        out_shape=[
            jax.ShapeDtypeStruct((b, t, QK_TOT), BF16),
            jax.ShapeDtypeStruct((b, t, QK_TOT), BF16),
            jax.ShapeDtypeStruct((b, t, V_TOT), BF16),
            jax.ShapeDtypeStruct((b, t, V_TOT), BF16),
        ],
        compiler_params=_cparams("parallel", "parallel"),
        name="inproj",
    )(x, g, mod_l, w_in, cos, sin)


def _row_pow(lg, expo_fn, rows, cols):
    i = lax.broadcasted_iota(jnp.int32, (rows, cols), 0).astype(F32)
    return jnp.exp(lg * expo_fn(i))


def _ctxstate_kernel(lg_ref, c_ref, g_ref, mod_ref, w_ref, cos_ref, sin_ref, sf_ref, sb_ref, k_scr):
    l = c_ref.shape[0]
    u = (_rms(c_ref[...], g_ref[...]) * (1.0 + mod_ref[1:2, :]) + mod_ref[0:1, :]).astype(BF16)
    cos, sin = cos_ref[...], sin_ref[...]
    for h in range(HEADS):
        _rope_store(_dot(u, w_ref[:, h * DK:(h + 1) * DK]), cos, sin, 1.0, k_scr, h * DK)
    for h in range(HEADS):
        kh = k_scr[:, h * DK:(h + 1) * DK]
        vh = _dot(u, w_ref[:, QK_TOT + h * DV:QK_TOT + (h + 1) * DV]).astype(BF16)
        wf = _row_pow(lg_ref[0, h], lambda i: (l - 1.0) - i, l, DK)
        wb = _row_pow(lg_ref[1, h], lambda i: i, l, DK)
        sf_ref[h] = _dot((kh * wf).T.astype(BF16), vh)
        sb_ref[h] = _dot((kh * wb).T.astype(BF16), vh)


def _ctxstate_call(log_g, ctx, g, mod_row, w_kv, cos, sin):
    b, l, _ = ctx.shape
    st = jax.ShapeDtypeStruct((b, HEADS, DK, DV), F32)
    st_spec = pl.BlockSpec((None, HEADS, DK, DV), lambda i: (i, 0, 0, 0))
    return pl.pallas_call(
        _ctxstate_kernel,
        grid=(b,),
        in_specs=[
            pl.BlockSpec(memory_space=pltpu.SMEM),
            pl.BlockSpec((None, l, D), lambda i: (i, 0, 0)),
            pl.BlockSpec((1, D), lambda i: (0, 0)),
            pl.BlockSpec((6, D), lambda i: (0, 0)),
            pl.BlockSpec((D, QK_TOT + V_TOT), lambda i: (0, 0)),
            pl.BlockSpec((l, DK // 2), lambda i: (0, 0)),
            pl.BlockSpec((l, DK // 2), lambda i: (0, 0)),
        ],
        out_specs=[st_spec, st_spec],
        out_shape=[st, st],
        scratch_shapes=[pltpu.VMEM((l, QK_TOT), F32)],
        compiler_params=_cparams("parallel"),
        name="ctxstate",
    )(log_g, ctx, g, mod_row, w_kv, cos, sin)


def _retention_kernel(lg_ref, q_ref, k_ref, v_ref, sf0_ref, sb0_ref, o_ref, acc, sf, sb):
    h = pl.program_id(1)
    t = q_ref.shape[0]
    c = RET_CHUNK
    nc = t // c
    lgf, lgb = lg_ref[0, h], lg_ref[1, h]
    ri = lax.broadcasted_iota(jnp.int32, (c, c), 0).astype(F32)
    ci = lax.broadcasted_iota(jnp.int32, (c, c), 1).astype(F32)
    rel = ri - ci
    mask = (jnp.where(rel >= 0, jnp.exp(lgf * jnp.maximum(rel, 0.0)), 0.0)
            + jnp.where(rel <= 0, jnp.exp(lgb * jnp.maximum(-rel, 0.0)), 0.0))
    qdf = _row_pow(lgf, lambda i: i + 1.0, c, DV)
    qdb = _row_pow(lgb, lambda i: c - i, c, DV)
    kdf = _row_pow(lgf, lambda i: (c - 1.0) - i, c, DK)
    kdb = _row_pow(lgb, lambda i: i, c, DK)
    cdf = jnp.exp(jnp.full((1, DV), lgf * c, F32))
    cdb = jnp.exp(jnp.full((1, DV), lgb * c, F32))
    sf[...] = sf0_ref[...]
    sb[...] = sb0_ref[...]

    def fwd(ic, carry):
        r0 = pl.multiple_of(ic * c, c)
        q = q_ref[pl.ds(r0, c), :]
        k = k_ref[pl.ds(r0, c), :]
        v = v_ref[pl.ds(r0, c), :]
        s = lax.dot_general(q, k, (((1,), (1,)), ((), ())), preferred_element_type=F32)
        o = _dot((s * mask).astype(BF16), v) + qdf * _dot(q, sf[...].astype(BF16))
        acc[pl.ds(r0, c), :] = o
        sf[...] = sf[...] * cdf + _dot((k.astype(F32) * kdf).T.astype(BF16), v)
        return carry

    lax.fori_loop(0, nc, fwd, 0)

    def bwd(jc, carry):
        ic = nc - 1 - jc
        r0 = pl.multiple_of(ic * c, c)
        q = q_ref[pl.ds(r0, c), :]
        k = k_ref[pl.ds(r0, c), :]
        v = v_ref[pl.ds(r0, c), :]
        o = acc[pl.ds(r0, c), :] + qdb * _dot(q, sb[...].astype(BF16))
        o_ref[pl.ds(r0, c), :] = (o * lax.rsqrt(jnp.mean(o * o, axis=-1, keepdims=True) + NORM_EPS)).astype(BF16)
        sb[...] = sb[...] * cdb + _dot((k.astype(F32) * kdb).T.astype(BF16), v)
        return carry

    lax.fori_loop(0, nc, bwd, 0)


def _retention_call(log_g, q, k, v, sf0, sb0):
    b, t, _ = q.shape
    qk_spec = pl.BlockSpec((None, t, DK), lambda i, h: (i, 0, h))
    v_spec = pl.BlockSpec((None, t, DV), lambda i, h: (i, 0, h))
    s_spec = pl.BlockSpec((None, None, DK, DV), lambda i, h: (i, h, 0, 0))
    return pl.pallas_call(
        _retention_kernel,
        grid=(b, HEADS),
        in_specs=[pl.BlockSpec(memory_space=pltpu.SMEM), qk_spec, qk_spec, v_spec, s_spec, s_spec],
        out_specs=v_spec,
        out_shape=jax.ShapeDtypeStruct((b, t, V_TOT), BF16),
        scratch_shapes=[pltpu.VMEM((t, DV), F32), pltpu.VMEM((DK, DV), F32), pltpu.VMEM((DK, DV), F32)],
        compiler_params=_cparams("parallel", "parallel"),
        name="retention",
    )(log_g, q, k, v, sf0, sb0)


def _mixer_epilogue(y, h_ref, n_ref, mod_ref, wr_ref, h_out, u_out, lg_out):
    h1 = h_ref[...] + mod_ref[2:3, :] * _rms(y, n_ref[1:2, :])
    h_out[...] = h1
    u2 = _rms(h1, n_ref[2:3, :]) * (1.0 + mod_ref[4:5, :]) + mod_ref[3:4, :]
    u_out[...] = u2.astype(BF16)
    lg_out[...] = lax.dot_general(wr_ref[...], u2, (((1,), (1,)), ((), ())),
                                  preferred_element_type=F32, precision=_HI)


def _readout_kernel(y_ref, gt_ref, h_ref, n_ref, mod_ref, w_ref, wr_ref, h_out, u_out, lg_out):
    a = (y_ref[...].astype(F32) * _silu(gt_ref[...].astype(F32))).astype(BF16)
    _mixer_epilogue(_dot(a, w_ref[...]), h_ref, n_ref, mod_ref, wr_ref, h_out, u_out, lg_out)


def _epilogue_specs(b, t, tm):
    nt = t // tm
    tok = lambda n: pl.BlockSpec((None, tm, n), lambda i, j: (i, j, 0))
    in_specs = [
        tok(D),
        pl.BlockSpec((4, D), lambda i, j: (0, 0)),
        pl.BlockSpec((None, 6, D), lambda i, j: (i, 0, 0)),
    ]
    wr_spec = pl.BlockSpec((N_EXP, D), lambda i, j: (0, 0))
    out_specs = [tok(D), tok(D), pl.BlockSpec((N_EXP, tm), lambda i, j: (0, i * nt + j))]
    out_shape = [
        jax.ShapeDtypeStruct((b, t, D), F32),
        jax.ShapeDtypeStruct((b, t, D), BF16),
        jax.ShapeDtypeStruct((N_EXP, b * t), F32),
    ]
    return tok, in_specs, wr_spec, out_specs, out_shape


def _readout_call(yn, gt, h, norms_l, mod_l, w_out, wr_t, tm):
    b, t, _ = h.shape
    tok, ep_in, wr_spec, out_specs, out_shape = _epilogue_specs(b, t, tm)
    return pl.pallas_call(
        _readout_kernel,
        grid=(b, t // tm),
        in_specs=[tok(V_TOT), tok(V_TOT)] + ep_in
        + [pl.BlockSpec((V_TOT, D), lambda i, j: (0, 0)), wr_spec],
        out_specs=out_specs,
        out_shape=out_shape,
        compiler_params=_cparams("parallel", "parallel"),
        name="readout",
    )(yn, gt, h, norms_l, mod_l, w_out, wr_t)


def _route_kernel(lg_ref, b_ref, pos_ref, posw_ref, pc_ref, off_ref):
    i = pl.program_id(0)
    tr = lg_ref.shape[1]
    s = jax.nn.sigmoid(lg_ref[...])
    biased = s + b_ref[...]
    neg = -jnp.inf
    b3 = biased.reshape(N_GRP, PER_GRP, tr)
    io3 = lax.broadcasted_iota(jnp.int32, b3.shape, 1).astype(F32)
    m1 = jnp.max(b3, axis=1, keepdims=True)
    i1 = jnp.min(jnp.where(b3 == m1, io3, float(PER_GRP)), axis=1, keepdims=True)
    m2 = jnp.max(jnp.where(io3 == i1, neg, b3), axis=1, keepdims=True)
    gs = (m1 + m2).reshape(N_GRP, tr)
    iog = lax.broadcasted_iota(jnp.int32, gs.shape, 0).astype(F32)
    gsel = jnp.zeros(gs.shape, F32)
    work = gs
    for _ in range(TOP_G):
        m = jnp.max(work, axis=0, keepdims=True)
        gi = jnp.min(jnp.where(work == m, iog, float(N_GRP)), axis=0, keepdims=True)
        hit = iog == gi
        gsel = jnp.where(hit, 1.0, gsel)
        work = jnp.where(hit, neg, work)
    emask = jnp.broadcast_to(gsel.reshape(N_GRP, 1, tr), b3.shape).reshape(N_EXP, tr)
    work = jnp.where(emask > 0.0, biased, neg)
    ioe = lax.broadcasted_iota(jnp.int32, work.shape, 0).astype(F32)
    esel = jnp.zeros(work.shape, F32)
    hits = []
    for _ in range(TOP_K):
        m = jnp.max(work, axis=0, keepdims=True)
        ei = jnp.min(jnp.where(work == m, ioe, float(N_EXP)), axis=0, keepdims=True)
        hit = ioe == ei
        hits.append(hit)
        esel = jnp.where(hit, 1.0, esel)
        work = jnp.where(hit, neg, work)
    wsel = esel * s
    gate = wsel / jnp.sum(wsel, axis=0, keepdims=True) * ROUTE_SCALE

    t_r = lax.broadcasted_iota(jnp.int32, (tr, tr), 0)
    t_c = lax.broadcasted_iota(jnp.int32, (tr, tr), 1)
    rank = _dot(esel.astype(BF16), jnp.where(t_r < t_c, 1.0, 0.0).astype(BF16))
    cnt = jnp.sum(esel, axis=1, keepdims=True)
    pc = jnp.floor((cnt + (SEG_ALIGN - 1.0)) * (1.0 / SEG_ALIGN)) * SEG_ALIGN
    e_r = lax.broadcasted_iota(jnp.int32, (N_EXP, N_EXP), 0)
    e_c = lax.broadcasted_iota(jnp.int32, (N_EXP, N_EXP), 1)
    off = jnp.dot(jnp.where(e_c < e_r, 1.0, 0.0), jnp.broadcast_to(pc, (N_EXP, 128)),
                  preferred_element_type=F32, precision=_HI)[:, 0:1]
    p = off + rank
    posm = jnp.concatenate([jnp.sum(jnp.where(h, p, 0.0), axis=0, keepdims=True) for h in hits], 0)
    wm = jnp.concatenate([jnp.sum(jnp.where(h, gate, 0.0), axis=0, keepdims=True) for h in hits], 0)
    pos_ref[...] = posm
    posw_ref[...] = jnp.concatenate([posm, wm, jnp.zeros((GATE_LANES - 2 * TOP_K, tr), F32)], 0).T

    @pl.when(i == 0)
    def _():
        pc_ref[...] = jnp.zeros(pc_ref.shape, F32)
        off_ref[...] = jnp.zeros(off_ref.shape, F32)

    tile_lane = lax.broadcasted_iota(jnp.int32, pc_ref.shape, 1)
    pc_ref[...] = jnp.where(tile_lane == i, pc, pc_ref[...])
    off_ref[...] = jnp.where(tile_lane == i, off, off_ref[...])


def _route_call(logits_t, bias):
    n = logits_t.shape[1]
    nt = n // TM_MOE
    return pl.pallas_call(
        _route_kernel,
        grid=(nt,),
        in_specs=[pl.BlockSpec((N_EXP, TM_MOE), lambda i: (0, i)), pl.BlockSpec((N_EXP, 1), lambda i: (0, 0))],
        out_specs=[
            pl.BlockSpec((TOP_K, TM_MOE), lambda i: (0, i)),
            pl.BlockSpec((TM_MOE, GATE_LANES), lambda i: (i, 0)),
            pl.BlockSpec((N_EXP, nt), lambda i: (0, 0)),
            pl.BlockSpec((N_EXP, nt), lambda i: (0, 0)),
        ],
        out_shape=[
            jax.ShapeDtypeStruct((TOP_K, n), F32),
            jax.ShapeDtypeStruct((n, GATE_LANES), F32),
            jax.ShapeDtypeStruct((N_EXP, nt), F32),
            jax.ShapeDtypeStruct((N_EXP, nt), F32),
        ],
        compiler_params=_cparams("arbitrary"),
        name="route",
    )(logits_t, bias.reshape(N_EXP, 1))


def _dispatch_tables(pc_t, off_t, nb_max):
    pc = pc_t.astype(jnp.int32)
    used = pc.sum(1)
    region = (used + BM - 1) // BM * BM
    ends = jnp.cumsum(region)
    starts = ends - region
    dst = starts[:, None] + jnp.cumsum(pc, 1) - pc
    nblk = ends[-1] // BM
    blk = jnp.minimum(jnp.arange(nb_max, dtype=jnp.int32), nblk - 1)
    blk_e = jnp.minimum(jnp.sum(ends[None, :] <= blk[:, None] * BM, axis=1), N_EXP - 1).astype(jnp.int32)
    flat = lambda a: a.T.reshape(-1).astype(jnp.int32)
    seg = (flat(off_t), flat(dst), flat(pc), pc.sum(0).astype(jnp.int32))
    tail = ((starts + used).astype(jnp.int32), (region - used).astype(jnp.int32))
    return seg, tail, blk_e, nblk.reshape(1).astype(jnp.int32)


def _swiglu(u, w1, w3, w2):
    return _dot((_silu(_dot(u, w1)) * _dot(u, w3)).astype(BF16), w2)


def _round_bf16(x):
    return x.astype(BF16).astype(F32)


def _pack_pair(lo, hi):
    lo_b = lax.bitcast_convert_type(lo, jnp.uint32)
    hi_b = lax.bitcast_convert_type(hi, jnp.uint32)
    return (hi_b & jnp.uint32(0xFFFF0000)) | (lo_b >> 16)


def _unpack_pair(w):
    lo = lax.bitcast_convert_type(w << 16, F32).astype(BF16)
    hi = lax.bitcast_convert_type(w & jnp.uint32(0xFFFF0000), F32).astype(BF16)
    return lo, hi


def _segment_copies(so_ref, do_ref, ln_ref, tile, make_copy):
    def seg(e, carry):
        j = tile * N_EXP + e
        ln = pl.multiple_of(ln_ref[j], SEG_ALIGN)

        @pl.when(ln > 0)
        def _():
            make_copy(pl.multiple_of(so_ref[j], SEG_ALIGN), pl.multiple_of(do_ref[j], SEG_ALIGN), ln).start()

        return carry

    lax.fori_loop(0, N_EXP, seg, 0)


def _dispatch_kernel(so_ref, do_ref, ln_ref, tt_ref, td_ref, tl_ref, nb_ref, u_ref, pos_ref, xs_out, buf, zbuf, sem):
    i = pl.program_id(0)
    slot = i % 2
    r, tm = buf.shape[1], u_ref.shape[0]
    nb_max = xs_out.shape[0] // BM

    def strip(slot_, so, do, ln):
        return pltpu.make_async_copy(buf.at[slot_, pl.ds(so, ln)], xs_out.at[pl.ds(do, ln)], sem.at[slot_])

    def wait_tile(slot_, tile):
        strip(slot_, 0, 0, pl.multiple_of(tt_ref[tile], SEG_ALIGN)).wait()

    def zero_fills(act):
        def tail(e, carry):
            ln = pl.multiple_of(tl_ref[e], SEG_ALIGN)

            @pl.when(ln > 0)
            def _():
                act(pltpu.make_async_copy(zbuf.at[pl.ds(0, ln)],
                                          xs_out.at[pl.ds(pl.multiple_of(td_ref[e], SEG_ALIGN), ln)], sem.at[2]))

            return carry

        lax.fori_loop(0, N_EXP, tail, 0)

        def blk(j, carry):
            act(pltpu.make_async_copy(zbuf, xs_out.at[pl.ds(pl.multiple_of(j * BM, BM), BM)], sem.at[2]))
            return carry

        lax.fori_loop(nb_ref[0], nb_max, blk, 0)

    @pl.when(i == 0)
    def _():
        zbuf[...] = jnp.zeros(zbuf.shape, jnp.uint32)
        zero_fills(lambda cp: cp.start())

    @pl.when(i >= 2)
    def _():
        wait_tile(slot, i - 2)

    pos = pos_ref[...]
    row = lax.broadcasted_iota(jnp.int32, (r, tm), 0).astype(F32)
    perm = jnp.zeros((r, tm), F32)
    for k in range(TOP_K):
        perm = jnp.where(row == pos[k:k + 1, :], 1.0, perm)
    perm = perm.astype(BF16)
    step = 256
    for c in range(0, PK, step):
        buf[slot, :, c:c + step] = _pack_pair(_dot(perm, u_ref[:, c:c + step]),
                                              _dot(perm, u_ref[:, PK + c:PK + c + step]))
    _segment_copies(so_ref, do_ref, ln_ref, i, functools.partial(strip, slot))

    @pl.when(i == pl.num_programs(0) - 1)
    def _():
        wait_tile(slot, i)

        @pl.when(i >= 1)
        def _():
            wait_tile(1 - slot, i - 1)

        zero_fills(lambda cp: cp.wait())


def _expert_kernel(be_ref, nb_ref, x_ref, w1_ref, w3_ref, w2_ref, y_ref, w1b, w3b, w2b):
    i = pl.program_id(0)
    live = i < nb_ref[0]

    @pl.when(live & ((i == 0) | (be_ref[i] != be_ref[jnp.maximum(i - 1, 0)])))
    def _():
        w1b[...] = w1_ref[...].astype(BF16)
        w3b[...] = w3_ref[...].astype(BF16)
        w2b[...] = w2_ref[...].astype(BF16)

    @pl.when(live)
    def _():
        for r0 in range(0, BM, EXP_SUB):
            lo, hi = _unpack_pair(x_ref[r0:r0 + EXP_SUB, :])
            a = _dot(lo, w1b[:PK, :]) + _dot(hi, w1b[PK:, :])
            g = _dot(lo, w3b[:PK, :]) + _dot(hi, w3b[PK:, :])
            hid = (_silu(a) * g).astype(BF16)
            y_ref[r0:r0 + EXP_SUB, :] = _pack_pair(_round_bf16(_dot(hid, w2b[:, :PK])),
                                                   _round_bf16(_dot(hid, w2b[:, PK:])))

    @pl.when(jnp.logical_not(live))
    def _():
        y_ref[...] = jnp.zeros(y_ref.shape, jnp.uint32)


def _combine_kernel(so_ref, do_ref, ln_ref, tt_ref, posw_ref, u_ref, ys_ref, s1_ref, s3_ref, s2_ref, h_ref, n_ref,
                    mod_ref, o_ref, buf, sem):
    i = pl.program_id(0)
    nt = pl.num_programs(0)
    slot = i % 2
    r, tm = buf.shape[1], u_ref.shape[0]

    def strip(slot_, so, do, ln):
        return pltpu.make_async_copy(ys_ref.at[pl.ds(do, ln)], buf.at[slot_, pl.ds(so, ln)], sem.at[slot_])

    @pl.when(i == 0)
    def _():
        buf[...] = jnp.zeros(buf.shape, jnp.uint32)
        _segment_copies(so_ref, do_ref, ln_ref, 0, functools.partial(strip, 0))

    @pl.when(i + 1 < nt)
    def _():
        _segment_copies(so_ref, do_ref, ln_ref, i + 1, functools.partial(strip, 1 - slot))

    strip(slot, 0, 0, pl.multiple_of(tt_ref[i], SEG_ALIGN)).wait()

    pw = posw_ref[...]
    col = lax.broadcasted_iota(jnp.int32, (tm, r), 1).astype(F32)
    q = jnp.zeros((tm, r), F32)
    for k in range(TOP_K):
        q = jnp.where(col == pw[:, k:k + 1], pw[:, TOP_K + k:TOP_K + k + 1], q)
    q = q.astype(BF16)
    lo, hi = _unpack_pair(buf[slot])
    moe = jnp.concatenate([_dot(q, lo), _dot(q, hi)], axis=1)
    moe = moe + _swiglu(u_ref[...], s1_ref[...], s3_ref[...], s2_ref[...])
    o_ref[...] = h_ref[...] + mod_ref[5:6, :] * _rms(moe, n_ref[3:4, :])


def _sorted_rows(tm):
    return -(-(TOP_K * tm + N_EXP * (SEG_ALIGN - 1)) // MXU_DIM) * MXU_DIM


def _max_blocks(n):
    nt = n // TM_MOE
    rows = TOP_K * n + N_EXP * (SEG_ALIGN - 1) * nt + N_EXP * (BM - SEG_ALIGN)
    return -(-rows // BM)


def _dispatch_call(seg, tail, nblk, u2, pos, nb_max):
    n = u2.shape[0]
    r = _sorted_rows(TM_MOE)
    return pl.pallas_call(
        _dispatch_kernel,
        grid_spec=pltpu.PrefetchScalarGridSpec(
            num_scalar_prefetch=7,
            grid=(n // TM_MOE,),
            in_specs=[
                pl.BlockSpec((TM_MOE, D), lambda i, *_: (i, 0)),
                pl.BlockSpec((TOP_K, TM_MOE), lambda i, *_: (0, i)),
            ],
            out_specs=pl.BlockSpec(memory_space=pl.ANY),
            scratch_shapes=[pltpu.VMEM((2, r, PK), jnp.uint32), pltpu.VMEM((BM, PK), jnp.uint32),
                            pltpu.SemaphoreType.DMA((3,))],
        ),
        out_shape=jax.ShapeDtypeStruct((nb_max * BM, PK), jnp.uint32),
        compiler_params=_cparams("arbitrary"),
        name="dispatch",
    )(*seg, tail[0], tail[1], nblk, u2, pos)


def _expert_call(layer, blk_e, nblk, xs, w1, w3, w2):
    nb_max = xs.shape[0] // BM
    w_blk = lambda i, be, nb: (layer, be[i], 0, 0)
    return pl.pallas_call(
        _expert_kernel,
        grid_spec=pltpu.PrefetchScalarGridSpec(
            num_scalar_prefetch=2,
            grid=(nb_max,),
            in_specs=[
                pl.BlockSpec((BM, PK), lambda i, be, nb: (jnp.minimum(i, nb[0] - 1), 0)),
                pl.BlockSpec((None, None, D, F_EXP), w_blk),
                pl.BlockSpec((None, None, D, F_EXP), w_blk),
                pl.BlockSpec((None, None, F_EXP, D), w_blk),
            ],
            out_specs=pl.BlockSpec((BM, PK), lambda i, be, nb: (i, 0)),
            scratch_shapes=[pltpu.VMEM((D, F_EXP), BF16), pltpu.VMEM((D, F_EXP), BF16),
                            pltpu.VMEM((F_EXP, D), BF16)],
        ),
        out_shape=jax.ShapeDtypeStruct(xs.shape, jnp.uint32),
        compiler_params=_cparams("arbitrary"),
        name="experts",
    )(blk_e, nblk, xs, w1, w3, w2)


def _combine_call(tabs, posw, u2, ys, s1, s3, s2, h, norms_l, mod_l, t):
    n = u2.shape[0]
    r = _sorted_rows(TM_MOE)
    tok = lambda w: pl.BlockSpec((TM_MOE, w), lambda i, *_: (i, 0))
    const = lambda shp: pl.BlockSpec(shp, lambda i, *_: (0,) * len(shp))
    return pl.pallas_call(
        _combine_kernel,
        grid_spec=pltpu.PrefetchScalarGridSpec(
            num_scalar_prefetch=4,
            grid=(n // TM_MOE,),
            in_specs=[
                tok(GATE_LANES),
                tok(D),
                pl.BlockSpec(memory_space=pl.ANY),
                const((D, F_EXP)), const((D, F_EXP)), const((F_EXP, D)),
                tok(D),
                const((4, D)),
                pl.BlockSpec((None, 6, D), lambda i, *_: (i * TM_MOE // t, 0, 0)),
            ],
            out_specs=tok(D),
            scratch_shapes=[pltpu.VMEM((2, r, PK), jnp.uint32), pltpu.SemaphoreType.DMA((2,))],
        ),
        out_shape=jax.ShapeDtypeStruct((n, D), F32),
        compiler_params=_cparams("arbitrary"),
        name="combine",
    )(*tabs, posw, u2, ys, s1, s3, s2, h, norms_l, mod_l)


def _glu_kernel(h_ref, g_ref, mod_ref, w_ref, b_ref, o_ref):
    u = (_rms(h_ref[...], g_ref[...]) * (1.0 + mod_ref[1:2, :]) + mod_ref[0:1, :]).astype(BF16)
    a = _dot(u, w_ref[:, :D]) + b_ref[:, :D]
    gate = _dot(u, w_ref[:, D:]) + b_ref[:, D:]
    o_ref[...] = (a * jax.nn.sigmoid(gate)).astype(BF16)


def _glu_call(h, g, mod_l, w1, b1, tm):
    b, t, _ = h.shape
    tok = lambda n: pl.BlockSpec((None, tm, n), lambda i, j: (i, j, 0))
    return pl.pallas_call(
        _glu_kernel,
        grid=(b, t // tm),
        in_specs=[
            tok(D),
            pl.BlockSpec((1, D), lambda i, j: (0, 0)),
            pl.BlockSpec((None, 6, D), lambda i, j: (i, 0, 0)),
            pl.BlockSpec((D, 2 * D), lambda i, j: (0, 0)),
            pl.BlockSpec((1, 2 * D), lambda i, j: (0, 0)),
        ],
        out_specs=tok(D),
        out_shape=jax.ShapeDtypeStruct((b, t, D), BF16),
        compiler_params=_cparams("parallel", "parallel"),
        name="glu",
    )(h, g, mod_l, w1, b1)


def _conv_kernel(cur_ref, prev_ref, next_ref, dw_ref, cv_ref, w2_ref, h_ref, n_ref, mod_ref, wr_ref,
                 h_out, u_out, lg_out, win, shifted, conv):
    j = pl.program_id(1)
    tm = cur_ref.shape[0]
    pad = CONV_W // 2
    win[0:HALO, :] = jnp.where(j > 0, prev_ref[...].astype(F32), 0.0)
    win[HALO:HALO + tm, :] = cur_ref[...].astype(F32)
    win[HALO + tm:, :] = jnp.where(j < pl.num_programs(1) - 1, next_ref[...].astype(F32), 0.0)
    ext = shifted.shape[1]
    full = win[...]
    shifted[0] = full[:ext, :]
    for s in range(1, SUBLANES):
        shifted[s] = pltpu.roll(full, full.shape[0] - s, axis=0)[:ext, :]

    def rows(ib, carry):
        r0 = pl.multiple_of(ib * 2 * SUBLANES, 2 * SUBLANES)
        for c0 in range(0, D, CONV_LANES):
            cols = slice(c0, c0 + CONV_LANES)
            acc0 = jnp.zeros((SUBLANES, CONV_LANES), F32)
            acc1 = jnp.zeros((SUBLANES, CONV_LANES), F32)
            for s in range(SUBLANES):
                taps = [(off // SUBLANES, off - (HALO - pad)) for off in range(HALO - pad, HALO - pad + CONV_W)
                        if off % SUBLANES == s]
                tiles = {a: shifted[s, pl.ds(r0 + a * SUBLANES, SUBLANES), cols]
                         for a in range(taps[0][0], taps[-1][0] + 2)}
                for a, tap in taps:
                    w = dw_ref[tap * SUBLANES:(tap + 1) * SUBLANES, cols]
                    acc0 = acc0 + tiles[a] * w
                    acc1 = acc1 + tiles[a + 1] * w
            conv[pl.ds(r0, SUBLANES), cols] = acc0
            conv[pl.ds(r0 + SUBLANES, SUBLANES), cols] = acc1
        return carry

    lax.fori_loop(0, tm // (2 * SUBLANES), rows, 0)
    uf = conv[...] + cv_ref[0:1, :]
    mu = jnp.mean(uf, axis=-1, keepdims=True)
    var = jnp.mean(jnp.square(uf - mu), axis=-1, keepdims=True)
    uf = (uf - mu) * lax.rsqrt(var + LN_EPS) * cv_ref[1:2, :] + cv_ref[2:3, :]
    y = _dot(_silu(uf).astype(BF16), w2_ref[...]) + cv_ref[3:4, :]
    _mixer_epilogue(y, h_ref, n_ref, mod_ref, wr_ref, h_out, u_out, lg_out)


def _conv_call(glu, dw, cvec, w2, h, norms_l, mod_l, wr_t, tm):
    b, t, _ = h.shape
    tok, ep_in, wr_spec, out_specs, out_shape = _epilogue_specs(b, t, tm)
    r = tm // HALO
    last = t // HALO - 1
    return pl.pallas_call(
        _conv_kernel,
        grid=(b, t // tm),
        in_specs=[
            tok(D),
            pl.BlockSpec((None, HALO, D), lambda i, j: (i, jnp.maximum(j * r - 1, 0), 0)),
            pl.BlockSpec((None, HALO, D), lambda i, j: (i, jnp.minimum((j + 1) * r, last), 0)),
            pl.BlockSpec((CONV_W * SUBLANES, D), lambda i, j: (0, 0)),
            pl.BlockSpec((4, D), lambda i, j: (0, 0)),
            pl.BlockSpec((D, D), lambda i, j: (0, 0)),
        ] + ep_in + [wr_spec],
        out_specs=out_specs,
        out_shape=out_shape,
        scratch_shapes=[pltpu.VMEM((tm + 2 * HALO, D), F32),
                        pltpu.VMEM((SUBLANES, tm + (CONV_W // SUBLANES) * SUBLANES, D), F32),
                        pltpu.VMEM((tm, D), F32)],
        compiler_params=_cparams("parallel", "parallel"),
        name="conv",
    )(glu, glu, glu, dw, cvec, w2, h, norms_l, mod_l, wr_t)


def _tile(t, pref):
    return pref if t % pref == 0 else t


def kernel(x, c, ctx, c_ctx, mod_w, mod_b, norms, ret_w_in, ret_w_out, ret_decay, conv_w1, conv_b1, conv_dw,
           conv_b_dw, conv_ln_g, conv_ln_b, conv_w2, conv_b2, moe_router, moe_bias, moe_w1, moe_w3, moe_w2,
           shared_w1, shared_w3, shared_w2):
    b, t, _ = x.shape
    l = ctx.shape[1]
    tm = _tile(t, 512)

    rows = -(-(b + 1) // 8) * 8
    cs = jnp.concatenate([c, c_ctx[None, :], jnp.zeros((rows - b - 1, D), F32)], 0)
    mod = _mod_call(cs, mod_w, mod_b).reshape(mod_w.shape[0], rows, 6, D)

    idx = jnp.arange(t, dtype=jnp.int32)
    pos_lat = jnp.stack([jnp.full((t,), l, jnp.int32), idx // GRID_W, idx % GRID_W], -1)
    zl = jnp.zeros((l,), jnp.int32)
    pos_ctx = jnp.stack([jnp.arange(l, dtype=jnp.int32), zl, zl], -1)
    ang_lat, ang_ctx = _rope_angles(pos_lat), _rope_angles(pos_ctx)

    head_perm = np.concatenate([np.arange(0, DK, 2), np.arange(1, DK, 2)])
    perm = np.concatenate([hh * DK + head_perm for hh in range(2 * HEADS)] + [np.arange(2 * QK_TOT, IN_DIM)])
    w_in = ret_w_in[0][:, perm].astype(BF16)
    log_g = jax.nn.log_sigmoid(ret_decay[0].astype(F32))

    n = b * t
    nb_max = _max_blocks(n)

    def moe_layer(i, h1, u2, logits_t):
        pos, posw, pc_t, off_t = _route_call(logits_t, moe_bias[i])
        seg, tail, blk_e, nblk = _dispatch_tables(pc_t, off_t, nb_max)
        u2f = u2.reshape(n, D)
        xs = _dispatch_call(seg, tail, nblk, u2f, pos, nb_max)
        ys = _expert_call(i, blk_e, nblk, xs, moe_w1, moe_w3, moe_w2)
        out = _combine_call(seg, posw, u2f, ys, shared_w1[i].astype(BF16), shared_w3[i].astype(BF16),
                            shared_w2[i].astype(BF16), h1.reshape(n, D), norms[i], mod[i], t)
        return out.reshape(b, t, D)

    q, k, v, gt = _inproj_call(x, norms[0, 0:1], mod[0], w_in, jnp.cos(ang_lat), jnp.sin(ang_lat), tm)
    sf0, sb0 = _ctxstate_call(log_g, ctx, norms[0, 0:1], mod[0, b], w_in[:, QK_TOT:2 * QK_TOT + V_TOT],
                              jnp.cos(ang_ctx), jnp.sin(ang_ctx))
    yn = _retention_call(log_g, q, k, v, sf0, sb0)
    h1, u2, logits_t = _readout_call(yn, gt, x, norms[0], mod[0], ret_w_out[0].astype(BF16),
                                     moe_router[0].T, tm)
    h2 = moe_layer(0, h1, u2, logits_t)

    glu = _glu_call(h2, norms[1, 0:1], mod[1], conv_w1[0].astype(BF16), conv_b1[0][None, :], tm)
    cvec = jnp.stack([conv_b_dw[0], conv_ln_g[0], conv_ln_b[0], conv_b2[0]], 0)
    dw_rows = jnp.repeat(conv_dw[0], SUBLANES, axis=0)
    h3, u2, logits_t = _conv_call(glu, dw_rows, cvec, conv_w2[0].astype(BF16), h2, norms[1], mod[1],
                                  moe_router[1].T, tm)
    return moe_layer(1, h3, u2, logits_t)
```

```python
import functools

import jax
import jax.numpy as jnp
import numpy as np
from jax import lax
from jax.experimental import pallas as pl
from jax.experimental.pallas import tpu as pltpu

F32 = jnp.float32
BF16 = jnp.bfloat16

D = 1024
HEADS = 4
DK = D // HEADS
DV = 2 * DK
QK_TOT = HEADS * DK
V_TOT = HEADS * DV
IN_DIM = 2 * QK_TOT + 2 * V_TOT
ROPE_AXES = (64, 96, 96)
ROPE_THETA = 10000.0
GRID_W = 64
CONV_W = 31
HALO = 16
SUBLANES = 8
CONV_LANES = 256
N_EXP = 64
N_GRP = 8
PER_GRP = N_EXP // N_GRP
TOP_K = 8
TOP_G = 4
F_EXP = 256
ROUTE_SCALE = 2.5
GATE_LANES = 128
MXU_DIM = 256
TM_MOE = 256
SEG_ALIGN = 8
POS_RADIX = 256
BM = 512
EXP_SUB = 256
PK = D // 2
NORM_EPS = 1e-6
LN_EPS = 1e-5
RET_CHUNK = 256
VMEM_LIMIT = 56 * 1024 * 1024

_HI = lax.Precision.HIGHEST


def _cparams(*sem):
    return pltpu.CompilerParams(dimension_semantics=sem, vmem_limit_bytes=VMEM_LIMIT)


def _dot(a, b):
    return jnp.dot(a, b, preferred_element_type=F32)


def _rms(xf, g):
    return xf * lax.rsqrt(jnp.mean(xf * xf, axis=-1, keepdims=True) + NORM_EPS) * g


def _silu(x):
    return x * jax.nn.sigmoid(x)


def _mod_kernel(cs_ref, w_ref, b_ref, o_ref):
    s = _silu(cs_ref[...])
    o_ref[...] = jnp.dot(s, w_ref[...], preferred_element_type=F32, precision=_HI) + b_ref[...]


def _mod_call(cs, mod_w, mod_b):
    depth, _, n6 = mod_w.shape
    tn = 1536
    rows = cs.shape[0]
    return pl.pallas_call(
        _mod_kernel,
        grid=(depth, n6 // tn),
        in_specs=[
            pl.BlockSpec((rows, D), lambda i, j: (0, 0)),
            pl.BlockSpec((None, D, tn), lambda i, j: (i, 0, j)),
            pl.BlockSpec((None, 1, tn), lambda i, j: (i, 0, j)),
        ],
        out_specs=pl.BlockSpec((None, rows, tn), lambda i, j: (i, 0, j)),
        out_shape=jax.ShapeDtypeStruct((depth, rows, n6), F32),
        compiler_params=_cparams("parallel", "parallel"),
        name="mod",
    )(cs, mod_w, mod_b.reshape(depth, 1, n6))


def _rope_angles(pos):
    parts = []
    for a, d in enumerate(ROPE_AXES):
        inv = ROPE_THETA ** (-jnp.arange(0, d, 2, dtype=F32) / d)
        parts.append(pos[:, a:a + 1].astype(F32) * inv[None, :])
    return jnp.concatenate(parts, -1)


def _rope_store(z, cos, sin, scale, o_ref, col):
    half = DK // 2
    x0, x1 = z[:, :half], z[:, half:]
    o_ref[:, col:col + half] = ((x0 * cos - x1 * sin) * scale).astype(o_ref.dtype)
    o_ref[:, col + half:col + DK] = ((x0 * sin + x1 * cos) * scale).astype(o_ref.dtype)


def _inproj_kernel(x_ref, g_ref, mod_ref, w_ref, cos_ref, sin_ref, q_ref, k_ref, v_ref, gt_ref):
    u = (_rms(x_ref[...], g_ref[...]) * (1.0 + mod_ref[1:2, :]) + mod_ref[0:1, :]).astype(BF16)
    cos, sin = cos_ref[...], sin_ref[...]
    for h in range(HEADS):
        _rope_store(_dot(u, w_ref[:, h * DK:(h + 1) * DK]), cos, sin, DK ** -0.5, q_ref, h * DK)
        _rope_store(_dot(u, w_ref[:, QK_TOT + h * DK:QK_TOT + (h + 1) * DK]), cos, sin, 1.0, k_ref, h * DK)
    step = 512
    for j in range(V_TOT // step):
        c0 = 2 * QK_TOT + j * step
        v_ref[:, j * step:(j + 1) * step] = _dot(u, w_ref[:, c0:c0 + step]).astype(BF16)
        c1 = 2 * QK_TOT + V_TOT + j * step
        gt_ref[:, j * step:(j + 1) * step] = _dot(u, w_ref[:, c1:c1 + step]).astype(BF16)


def _inproj_call(x, g, mod_l, w_in, cos, sin, tm):
    b, t, _ = x.shape
    grid = (b, t // tm)
    tok = lambda n: pl.BlockSpec((None, tm, n), lambda i, j: (i, j, 0))
    return pl.pallas_call(
        _inproj_kernel,
        grid=grid,
        in_specs=[
            tok(D),
            pl.BlockSpec((1, D), lambda i, j: (0, 0)),
            pl.BlockSpec((None, 6, D), lambda i, j: (i, 0, 0)),
            pl.BlockSpec((D, IN_DIM), lambda i, j: (0, 0), pipeline_mode=pl.Buffered(1)),
            pl.BlockSpec((tm, DK // 2), lambda i, j: (j, 0)),
            pl.BlockSpec((tm, DK // 2), lambda i, j: (j, 0)),
        ],
        out_specs=[tok(QK_TOT), tok(QK_TOT), tok(V_TOT), tok(V_TOT)],
        out_shape=[
            jax.ShapeDtypeStruct((b, t, QK_TOT), BF16),
            jax.ShapeDtypeStruct((b, t, QK_TOT), BF16),
            jax.ShapeDtypeStruct((b, t, V_TOT), BF16),
            jax.ShapeDtypeStruct((b, t, V_TOT), BF16),
        ],
        compiler_params=_cparams("parallel", "parallel"),
        name="inproj",
    )(x, g, mod_l, w_in, cos, sin)


def _row_pow(lg, expo_fn, rows, cols):
    i = lax.broadcasted_iota(jnp.int32, (rows, cols), 0).astype(F32)
    return jnp.exp(lg * expo_fn(i))


def _ctxstate_kernel(lg_ref, c_ref, g_ref, mod_ref, w_ref, cos_ref, sin_ref, sf_ref, sb_ref, k_scr):
    l = c_ref.shape[0]
    u = (_rms(c_ref[...], g_ref[...]) * (1.0 + mod_ref[1:2, :]) + mod_ref[0:1, :]).astype(BF16)
    cos, sin = cos_ref[...], sin_ref[...]
    for h in range(HEADS):
        _rope_store(_dot(u, w_ref[:, h * DK:(h + 1) * DK]), cos, sin, 1.0, k_scr, h * DK)
    for h in range(HEADS):
        kh = k_scr[:, h * DK:(h + 1) * DK]
        vh = _dot(u, w_ref[:, QK_TOT + h * DV:QK_TOT + (h + 1) * DV]).astype(BF16)
        wf = _row_pow(lg_ref[0, h], lambda i: (l - 1.0) - i, l, DK)
        wb = _row_pow(lg_ref[1, h], lambda i: i, l, DK)
        sf_ref[h] = _dot((kh * wf).T.astype(BF16), vh)
        sb_ref[h] = _dot((kh * wb).T.astype(BF16), vh)


def _ctxstate_call(log_g, ctx, g, mod_row, w_kv, cos, sin):
    b, l, _ = ctx.shape
    st = jax.ShapeDtypeStruct((b, HEADS, DK, DV), F32)
    st_spec = pl.BlockSpec((None, HEADS, DK, DV), lambda i: (i, 0, 0, 0))
    return pl.pallas_call(
        _ctxstate_kernel,
        grid=(b,),
        in_specs=[
            pl.BlockSpec(memory_space=pltpu.SMEM),
            pl.BlockSpec((None, l, D), lambda i: (i, 0, 0)),
            pl.BlockSpec((1, D), lambda i: (0, 0)),
            pl.BlockSpec((6, D), lambda i: (0, 0)),
            pl.BlockSpec((D, QK_TOT + V_TOT), lambda i: (0, 0)),
            pl.BlockSpec((l, DK // 2), lambda i: (0, 0)),
            pl.BlockSpec((l, DK // 2), lambda i: (0, 0)),
        ],
        out_specs=[st_spec, st_spec],
        out_shape=[st, st],
        scratch_shapes=[pltpu.VMEM((l, QK_TOT), F32)],
        compiler_params=_cparams("parallel"),
        name="ctxstate",
    )(log_g, ctx, g, mod_row, w_kv, cos, sin)


def _retention_kernel(lg_ref, q_ref, k_ref, v_ref, sf0_ref, sb0_ref, o_ref, acc, sf, sb):
    h = pl.program_id(1)
    t = q_ref.shape[0]
    c = RET_CHUNK
    nc = t // c
    lgf, lgb = lg_ref[0, h], lg_ref[1, h]
    ri = lax.broadcasted_iota(jnp.int32, (c, c), 0).astype(F32)
    ci = lax.broadcasted_iota(jnp.int32, (c, c), 1).astype(F32)
    rel = ri - ci
    mask = (jnp.where(rel >= 0, jnp.exp(lgf * jnp.maximum(rel, 0.0)), 0.0)
            + jnp.where(rel <= 0, jnp.exp(lgb * jnp.maximum(-rel, 0.0)), 0.0))
    qdf = _row_pow(lgf, lambda i: i + 1.0, c, DV)
    qdb = _row_pow(lgb, lambda i: c - i, c, DV)
    kdf = _row_pow(lgf, lambda i: (c - 1.0) - i, c, DK)
    kdb = _row_pow(lgb, lambda i: i, c, DK)
    cdf = jnp.exp(jnp.full((1, DV), lgf * c, F32))
    cdb = jnp.exp(jnp.full((1, DV), lgb * c, F32))
    sf[...] = sf0_ref[...]
    sb[...] = sb0_ref[...]

    def fwd(ic, carry):
        r0 = pl.multiple_of(ic * c, c)
        q = q_ref[pl.ds(r0, c), :]
        k = k_ref[pl.ds(r0, c), :]
        v = v_ref[pl.ds(r0, c), :]
        s = lax.dot_general(q, k, (((1,), (1,)), ((), ())), preferred_element_type=F32)
        o = _dot((s * mask).astype(BF16), v) + qdf * _dot(q, sf[...].astype(BF16))
        acc[pl.ds(r0, c), :] = o
        sf[...] = sf[...] * cdf + _dot((k.astype(F32) * kdf).T.astype(BF16), v)
        return carry

    lax.fori_loop(0, nc, fwd, 0)

    def bwd(jc, carry):
        ic = nc - 1 - jc
        r0 = pl.multiple_of(ic * c, c)
        q = q_ref[pl.ds(r0, c), :]
        k = k_ref[pl.ds(r0, c), :]
        v = v_ref[pl.ds(r0, c), :]
        o = acc[pl.ds(r0, c), :] + qdb * _dot(q, sb[...].astype(BF16))
        o_ref[pl.ds(r0, c), :] = (o * lax.rsqrt(jnp.mean(o * o, axis=-1, keepdims=True) + NORM_EPS)).astype(BF16)
        sb[...] = sb[...] * cdb + _dot((k.astype(F32) * kdb).T.astype(BF16), v)
        return carry

    lax.fori_loop(0, nc, bwd, 0)


def _retention_call(log_g, q, k, v, sf0, sb0):
    b, t, _ = q.shape
    qk_spec = pl.BlockSpec((None, t, DK), lambda i, h: (i, 0, h))
    v_spec = pl.BlockSpec((None, t, DV), lambda i, h: (i, 0, h))
    s_spec = pl.BlockSpec((None, None, DK, DV), lambda i, h: (i, h, 0, 0))
    return pl.pallas_call(
        _retention_kernel,
        grid=(b, HEADS),
        in_specs=[pl.BlockSpec(memory_space=pltpu.SMEM), qk_spec, qk_spec, v_spec, s_spec, s_spec],
        out_specs=v_spec,
        out_shape=jax.ShapeDtypeStruct((b, t, V_TOT), BF16),
        scratch_shapes=[pltpu.VMEM((t, DV), F32), pltpu.VMEM((DK, DV), F32), pltpu.VMEM((DK, DV), F32)],
        compiler_params=_cparams("parallel", "parallel"),
        name="retention",
    )(log_g, q, k, v, sf0, sb0)


def _mixer_epilogue(y, h_ref, n_ref, mod_ref, wr_ref, h_out, u_out, lg_out):
    h1 = h_ref[...] + mod_ref[2:3, :] * _rms(y, n_ref[1:2, :])
    h_out[...] = h1
    u2 = _rms(h1, n_ref[2:3, :]) * (1.0 + mod_ref[4:5, :]) + mod_ref[3:4, :]
    u_out[...] = u2.astype(BF16)
    lg_out[...] = lax.dot_general(wr_ref[...], u2, (((1,), (1,)), ((), ())),
                                  preferred_element_type=F32, precision=_HI)


def _readout_kernel(y_ref, gt_ref, h_ref, n_ref, mod_ref, w_ref, wr_ref, h_out, u_out, lg_out):
    a = (y_ref[...].astype(F32) * _silu(gt_ref[...].astype(F32))).astype(BF16)
    _mixer_epilogue(_dot(a, w_ref[...]), h_ref, n_ref, mod_ref, wr_ref, h_out, u_out, lg_out)


def _epilogue_specs(b, t, tm):
    nt = t // tm
    tok = lambda n: pl.BlockSpec((None, tm, n), lambda i, j: (i, j, 0))
    in_specs = [
        tok(D),
        pl.BlockSpec((4, D), lambda i, j: (0, 0)),
        pl.BlockSpec((None, 6, D), lambda i, j: (i, 0, 0)),
    ]
    wr_spec = pl.BlockSpec((N_EXP, D), lambda i, j: (0, 0))
    out_specs = [tok(D), tok(D), pl.BlockSpec((N_EXP, tm), lambda i, j: (0, i * nt + j))]
    out_shape = [
        jax.ShapeDtypeStruct((b, t, D), F32),
        jax.ShapeDtypeStruct((b, t, D), BF16),
        jax.ShapeDtypeStruct((N_EXP, b * t), F32),
    ]
    return tok, in_specs, wr_spec, out_specs, out_shape


def _readout_call(yn, gt, h, norms_l, mod_l, w_out, wr_t, tm):
    b, t, _ = h.shape
    tok, ep_in, wr_spec, out_specs, out_shape = _epilogue_specs(b, t, tm)
    return pl.pallas_call(
        _readout_kernel,
        grid=(b, t // tm),
        in_specs=[tok(V_TOT), tok(V_TOT)] + ep_in
        + [pl.BlockSpec((V_TOT, D), lambda i, j: (0, 0)), wr_spec],
        out_specs=out_specs,
        out_shape=out_shape,
        compiler_params=_cparams("parallel", "parallel"),
        name="readout",
    )(yn, gt, h, norms_l, mod_l, w_out, wr_t)


def _route_kernel(lg_ref, b_ref, code_ref, code_t_ref, gate_t_ref, segc_ref, segr_ref, pc_ref, off_ref):
    i = pl.program_id(0)
    tr = lg_ref.shape[1]
    s = jax.nn.sigmoid(lg_ref[...])
    biased = s + b_ref[...]
    neg = -jnp.inf
    b3 = biased.reshape(N_GRP, PER_GRP, tr)
    io3 = lax.broadcasted_iota(jnp.int32, b3.shape, 1).astype(F32)
    m1 = jnp.max(b3, axis=1, keepdims=True)
    i1 = jnp.min(jnp.where(b3 == m1, io3, float(PER_GRP)), axis=1, keepdims=True)
    m2 = jnp.max(jnp.where(io3 == i1, neg, b3), axis=1, keepdims=True)
    gs = (m1 + m2).reshape(N_GRP, tr)
    iog = lax.broadcasted_iota(jnp.int32, gs.shape, 0).astype(F32)
    gsel = jnp.zeros(gs.shape, F32)
    work = gs
    for _ in range(TOP_G):
        m = jnp.max(work, axis=0, keepdims=True)
        gi = jnp.min(jnp.where(work == m, iog, float(N_GRP)), axis=0, keepdims=True)
        hit = iog == gi
        gsel = jnp.where(hit, 1.0, gsel)
        work = jnp.where(hit, neg, work)
    emask = jnp.broadcast_to(gsel.reshape(N_GRP, 1, tr), b3.shape).reshape(N_EXP, tr)
    work = jnp.where(emask > 0.0, biased, neg)
    ioe = lax.broadcasted_iota(jnp.int32, work.shape, 0).astype(F32)
    esel = jnp.zeros(work.shape, F32)
    for _ in range(TOP_K):
        m = jnp.max(work, axis=0, keepdims=True)
        ei = jnp.min(jnp.where(work == m, ioe, float(N_EXP)), axis=0, keepdims=True)
        hit = ioe == ei
        esel = jnp.where(hit, 1.0, esel)
        work = jnp.where(hit, neg, work)
    wsel = esel * s
    gate = wsel / jnp.sum(wsel, axis=0, keepdims=True) * ROUTE_SCALE

    t_r = lax.broadcasted_iota(jnp.int32, (tr, tr), 0)
    t_c = lax.broadcasted_iota(jnp.int32, (tr, tr), 1)
    rank = _dot(esel.astype(BF16), jnp.where(t_r < t_c, 1.0, 0.0).astype(BF16))
    cnt = jnp.sum(esel, axis=1, keepdims=True)
    pc = jnp.floor((cnt + (SEG_ALIGN - 1.0)) * (1.0 / SEG_ALIGN)) * SEG_ALIGN
    e_r = lax.broadcasted_iota(jnp.int32, (N_EXP, N_EXP), 0)
    e_c = lax.broadcasted_iota(jnp.int32, (N_EXP, N_EXP), 1)
    off = jnp.dot(jnp.where(e_c < e_r, 1.0, 0.0), jnp.broadcast_to(pc, (N_EXP, 128)),
                  preferred_element_type=F32, precision=_HI)[:, 0:1]
    p = off + rank
    p_hi = jnp.where(esel > 0.0, jnp.floor(p * (1.0 / POS_RADIX)) * POS_RADIX, -float(POS_RADIX))
    p_lo = jnp.where(esel > 0.0, p - p_hi, 0.0)
    minus_r = jnp.where(lax.broadcasted_iota(jnp.int32, (2 * N_EXP, tr), 0) < 2, -1.0, 0.0)
    code = jnp.concatenate([p_hi, p_lo, minus_r], 0)
    code_ref[...] = code.astype(BF16)
    code_t_ref[...] = jnp.concatenate([code[:2 * N_EXP].T, code[2 * N_EXP:].T], 1).astype(BF16)
    gate_t_ref[...] = jnp.concatenate([gate, jnp.zeros((GATE_LANES - N_EXP, tr), F32)], 0).T
    lane = lax.broadcasted_iota(jnp.int32, (2 * N_EXP, GATE_LANES), 1)
    off2 = jnp.concatenate([off, off], 0)
    end2 = jnp.concatenate([off + pc, off + pc], 0)
    bounds = jnp.where(lane == 0, off2, jnp.where(lane == 1, end2, 0.0))
    segc_ref[...] = bounds
    segr_ref[...] = bounds.T[0:SUBLANES, :]

    @pl.when(i == 0)
    def _():
        pc_ref[...] = jnp.zeros(pc_ref.shape, F32)
        off_ref[...] = jnp.zeros(off_ref.shape, F32)

    tile_lane = lax.broadcasted_iota(jnp.int32, pc_ref.shape, 1)
    pc_ref[...] = jnp.where(tile_lane == i, pc, pc_ref[...])
    off_ref[...] = jnp.where(tile_lane == i, off, off_ref[...])


def _route_call(logits_t, bias):
    n = logits_t.shape[1]
    nt = n // TM_MOE
    return pl.pallas_call(
        _route_kernel,
        grid=(nt,),
        in_specs=[pl.BlockSpec((N_EXP, TM_MOE), lambda i: (0, i)), pl.BlockSpec((N_EXP, 1), lambda i: (0, 0))],
        out_specs=[
            pl.BlockSpec((4 * N_EXP, TM_MOE), lambda i: (0, i)),
            pl.BlockSpec((TM_MOE, 4 * N_EXP), lambda i: (i, 0)),
            pl.BlockSpec((TM_MOE, GATE_LANES), lambda i: (i, 0)),
            pl.BlockSpec((None, 2 * N_EXP, GATE_LANES), lambda i: (i, 0, 0)),
            pl.BlockSpec((None, SUBLANES, GATE_LANES), lambda i: (i, 0, 0)),
            pl.BlockSpec((N_EXP, nt), lambda i: (0, 0)),
            pl.BlockSpec((N_EXP, nt), lambda i: (0, 0)),
        ],
        out_shape=[
            jax.ShapeDtypeStruct((4 * N_EXP, n), BF16),
            jax.ShapeDtypeStruct((n, 4 * N_EXP), BF16),
            jax.ShapeDtypeStruct((n, GATE_LANES), F32),
            jax.ShapeDtypeStruct((nt, 2 * N_EXP, GATE_LANES), F32),
            jax.ShapeDtypeStruct((nt, SUBLANES, GATE_LANES), F32),
            jax.ShapeDtypeStruct((N_EXP, nt), F32),
            jax.ShapeDtypeStruct((N_EXP, nt), F32),
        ],
        compiler_params=_cparams("arbitrary"),
        name="route",
    )(logits_t, bias.reshape(N_EXP, 1))


def _dispatch_tables(pc_t, off_t, nb_max):
    pc = pc_t.astype(jnp.int32)
    used = pc.sum(1)
    region = (used + BM - 1) // BM * BM
    ends = jnp.cumsum(region)
    starts = ends - region
    dst = starts[:, None] + jnp.cumsum(pc, 1) - pc
    nblk = ends[-1] // BM
    blk = jnp.minimum(jnp.arange(nb_max, dtype=jnp.int32), nblk - 1)
    blk_e = jnp.minimum(jnp.sum(ends[None, :] <= blk[:, None] * BM, axis=1), N_EXP - 1).astype(jnp.int32)
    flat = lambda a: a.T.reshape(-1).astype(jnp.int32)
    seg = (flat(off_t), flat(dst), flat(pc), pc.sum(0).astype(jnp.int32))
    tail = ((starts + used).astype(jnp.int32), (region - used).astype(jnp.int32))
    return seg, tail, blk_e, nblk.reshape(1).astype(jnp.int32)


def _swiglu(u, w1, w3, w2):
    return _dot((_silu(_dot(u, w1)) * _dot(u, w3)).astype(BF16), w2)


def _round_bf16(x):
    return x.astype(BF16).astype(F32)


def _pack_pair(lo, hi):
    lo_b = lax.bitcast_convert_type(lo, jnp.uint32)
    hi_b = lax.bitcast_convert_type(hi, jnp.uint32)
    return (hi_b & jnp.uint32(0xFFFF0000)) | (lo_b >> 16)


def _unpack_pair(w):
    lo = lax.bitcast_convert_type(w << 16, F32).astype(BF16)
    hi = lax.bitcast_convert_type(w & jnp.uint32(0xFFFF0000), F32).astype(BF16)
    return lo, hi


def _segment_copies(so_ref, do_ref, ln_ref, tile, make_copy):
    def seg(e, carry):
        j = tile * N_EXP + e
        ln = pl.multiple_of(ln_ref[j], SEG_ALIGN)

        @pl.when(ln > 0)
        def _():
            make_copy(pl.multiple_of(so_ref[j], SEG_ALIGN), pl.multiple_of(do_ref[j], SEG_ALIGN), ln).start()

        return carry

    lax.fori_loop(0, N_EXP, seg, 0)


def _dispatch_kernel(so_ref, do_ref, ln_ref, tt_ref, td_ref, tl_ref, nb_ref, u_ref, code_ref, segr_ref, digits_ref,
                     xs_out, buf, zbuf, sem):
    i = pl.program_id(0)
    slot = i % 2
    r, tm = buf.shape[1], u_ref.shape[0]
    nb_max = xs_out.shape[0] // BM

    def strip(slot_, so, do, ln):
        return pltpu.make_async_copy(buf.at[slot_, pl.ds(so, ln)], xs_out.at[pl.ds(do, ln)], sem.at[slot_])

    def wait_tile(slot_, tile):
        strip(slot_, 0, 0, pl.multiple_of(tt_ref[tile], SEG_ALIGN)).wait()

    def zero_fills(act):
        def tail(e, carry):
            ln = pl.multiple_of(tl_ref[e], SEG_ALIGN)

            @pl.when(ln > 0)
            def _():
                act(pltpu.make_async_copy(zbuf.at[pl.ds(0, ln)],
                                          xs_out.at[pl.ds(pl.multiple_of(td_ref[e], SEG_ALIGN), ln)], sem.at[2]))

            return carry

        lax.fori_loop(0, N_EXP, tail, 0)

        def blk(j, carry):
            act(pltpu.make_async_copy(zbuf, xs_out.at[pl.ds(pl.multiple_of(j * BM, BM), BM)], sem.at[2]))
            return carry

        lax.fori_loop(nb_ref[0], nb_max, blk, 0)

    @pl.when(i == 0)
    def _():
        zbuf[...] = jnp.zeros(zbuf.shape, jnp.uint32)
        zero_fills(lambda cp: cp.start())

    @pl.when(i >= 2)
    def _():
        wait_tile(slot, i - 2)

    row = lax.broadcasted_iota(jnp.int32, (r, GATE_LANES), 0).astype(F32)
    own = jnp.where(row >= segr_ref[0:1, :], jnp.where(row < segr_ref[1:2, :], 1.0, 0.0), 0.0).astype(BF16)
    z = _dot(jnp.concatenate([own, digits_ref[...]], axis=1), code_ref[...])
    perm = jnp.where(z == 0.0, 1.0, 0.0).astype(BF16)
    step = 256
    for c in range(0, PK, step):
        buf[slot, :, c:c + step] = _pack_pair(_dot(perm, u_ref[:, c:c + step]),
                                              _dot(perm, u_ref[:, PK + c:PK + c + step]))
    _segment_copies(so_ref, do_ref, ln_ref, i, functools.partial(strip, slot))

    @pl.when(i == pl.num_programs(0) - 1)
    def _():
        wait_tile(slot, i)

        @pl.when(i >= 1)
        def _():
            wait_tile(1 - slot, i - 1)

        zero_fills(lambda cp: cp.wait())


def _expert_kernel(be_ref, nb_ref, x_ref, w1_ref, w3_ref, w2_ref, y_ref, w1b, w3b, w2b):
    i = pl.program_id(0)
    live = i < nb_ref[0]

    @pl.when(live & ((i == 0) | (be_ref[i] != be_ref[jnp.maximum(i - 1, 0)])))
    def _():
        w1b[...] = w1_ref[...].astype(BF16)
        w3b[...] = w3_ref[...].astype(BF16)
        w2b[...] = w2_ref[...].astype(BF16)

    @pl.when(live)
    def _():
        up = []
        for r0 in range(0, BM, EXP_SUB):
            lo, hi = _unpack_pair(x_ref[r0:r0 + EXP_SUB, :])
            up.append((_dot(lo, w1b[:PK, :]) + _dot(hi, w1b[PK:, :]), _dot(lo, w3b[:PK, :]) + _dot(hi, w3b[PK:, :])))
        for j, (a, g) in enumerate(up):
            hid = (_silu(a) * g).astype(BF16)
            y_ref[j * EXP_SUB:(j + 1) * EXP_SUB, :] = _pack_pair(_round_bf16(_dot(hid, w2b[:, :PK])),
                                                                 _round_bf16(_dot(hid, w2b[:, PK:])))

    @pl.when(jnp.logical_not(live))
    def _():
        y_ref[...] = jnp.zeros(y_ref.shape, jnp.uint32)


def _combine_kernel(so_ref, do_ref, ln_ref, tt_ref, code_t_ref, gate_t_ref, segc_ref, digits_ref, u_ref, ys_ref,
                    s1_ref, s3_ref, s2_ref, h_ref, n_ref, mod_ref, o_ref, buf, sem):
    i = pl.program_id(0)
    nt = pl.num_programs(0)
    slot = i % 2
    r, tm = buf.shape[1], u_ref.shape[0]

    def strip(slot_, so, do, ln):
        return pltpu.make_async_copy(ys_ref.at[pl.ds(do, ln)], buf.at[slot_, pl.ds(so, ln)], sem.at[slot_])

    @pl.when(i == 0)
    def _():
        buf[...] = jnp.zeros(buf.shape, jnp.uint32)
        _segment_copies(so_ref, do_ref, ln_ref, 0, functools.partial(strip, 0))

    @pl.when(i + 1 < nt)
    def _():
        _segment_copies(so_ref, do_ref, ln_ref, i + 1, functools.partial(strip, 1 - slot))

    strip(slot, 0, 0, pl.multiple_of(tt_ref[i], SEG_ALIGN)).wait()

    col = lax.broadcasted_iota(jnp.int32, (2 * N_EXP, r), 1).astype(F32)
    own = jnp.where(col >= segc_ref[:, 0:1], jnp.where(col < segc_ref[:, 1:2], 1.0, 0.0), 0.0).astype(BF16)
    z = _dot(code_t_ref[...], jnp.concatenate([own, digits_ref[...]], axis=0))
    q = jnp.where(z == 0.0, _dot(gate_t_ref[...].astype(BF16), own), 0.0).astype(BF16)
    lo, hi = _unpack_pair(buf[slot])
    moe = jnp.concatenate([_dot(q, lo), _dot(q, hi)], axis=1)
    moe = moe + _swiglu(u_ref[...], s1_ref[...], s3_ref[...], s2_ref[...])
    o_ref[...] = h_ref[...] + mod_ref[5:6, :] * _rms(moe, n_ref[3:4, :])


def _sorted_rows(tm):
    return -(-(TOP_K * tm + N_EXP * (SEG_ALIGN - 1)) // MXU_DIM) * MXU_DIM


def _max_blocks(n):
    nt = n // TM_MOE
    rows = TOP_K * n + N_EXP * (SEG_ALIGN - 1) * nt + N_EXP * (BM - SEG_ALIGN)
    return -(-rows // BM)


def _row_digits(r):
    idx = np.arange(r)
    digits = np.zeros((r, GATE_LANES), np.float32)
    digits[:, 0] = idx // POS_RADIX * POS_RADIX
    digits[:, 1] = idx % POS_RADIX
    return jnp.asarray(digits, BF16)


def _dispatch_call(seg, tail, nblk, u2, code, segr, nb_max):
    n = u2.shape[0]
    r = _sorted_rows(TM_MOE)
    return pl.pallas_call(
        _dispatch_kernel,
        grid_spec=pltpu.PrefetchScalarGridSpec(
            num_scalar_prefetch=7,
            grid=(n // TM_MOE,),
            in_specs=[
                pl.BlockSpec((TM_MOE, D), lambda i, *_: (i, 0)),
                pl.BlockSpec((4 * N_EXP, TM_MOE), lambda i, *_: (0, i)),
                pl.BlockSpec((None, SUBLANES, GATE_LANES), lambda i, *_: (i, 0, 0)),
                pl.BlockSpec((r, GATE_LANES), lambda i, *_: (0, 0)),
            ],
            out_specs=pl.BlockSpec(memory_space=pl.ANY),
            scratch_shapes=[pltpu.VMEM((2, r, PK), jnp.uint32), pltpu.VMEM((BM, PK), jnp.uint32),
                            pltpu.SemaphoreType.DMA((3,))],
        ),
        out_shape=jax.ShapeDtypeStruct((nb_max * BM, PK), jnp.uint32),
        compiler_params=_cparams("arbitrary"),
        name="dispatch",
    )(*seg, tail[0], tail[1], nblk, u2, code, segr, _row_digits(r))


def _expert_call(layer, blk_e, nblk, xs, w1, w3, w2):
    nb_max = xs.shape[0] // BM
    w_blk = lambda i, be, nb: (layer, be[i], 0, 0)
    return pl.pallas_call(
        _expert_kernel,
        grid_spec=pltpu.PrefetchScalarGridSpec(
            num_scalar_prefetch=2,
            grid=(nb_max,),
            in_specs=[
                pl.BlockSpec((BM, PK), lambda i, be, nb: (jnp.maximum(jnp.minimum(i, nb[0] - 1), 0), 0)),
                pl.BlockSpec((None, None, D, F_EXP), w_blk),
                pl.BlockSpec((None, None, D, F_EXP), w_blk),
                pl.BlockSpec((None, None, F_EXP, D), w_blk),
            ],
            out_specs=pl.BlockSpec((BM, PK), lambda i, be, nb: (i, 0)),
            scratch_shapes=[pltpu.VMEM((D, F_EXP), BF16), pltpu.VMEM((D, F_EXP), BF16),
                            pltpu.VMEM((F_EXP, D), BF16)],
        ),
        out_shape=jax.ShapeDtypeStruct(xs.shape, jnp.uint32),
        compiler_params=_cparams("arbitrary"),
        name="experts",
    )(blk_e, nblk, xs, w1, w3, w2)


def _combine_call(tabs, code_t, gate_t, segc, u2, ys, s1, s3, s2, h, norms_l, mod_l, t):
    n = u2.shape[0]
    r = _sorted_rows(TM_MOE)
    tok = lambda w: pl.BlockSpec((TM_MOE, w), lambda i, *_: (i, 0))
    const = lambda shp: pl.BlockSpec(shp, lambda i, *_: (0,) * len(shp))
    return pl.pallas_call(
        _combine_kernel,
        grid_spec=pltpu.PrefetchScalarGridSpec(
            num_scalar_prefetch=4,
            grid=(n // TM_MOE,),
            in_specs=[
                tok(4 * N_EXP),
                tok(GATE_LANES),
                pl.BlockSpec((None, 2 * N_EXP, GATE_LANES), lambda i, *_: (i, 0, 0)),
                const((2 * N_EXP, r)),
                tok(D),
                pl.BlockSpec(memory_space=pl.ANY),
                const((D, F_EXP)), const((D, F_EXP)), const((F_EXP, D)),
                tok(D),
                const((4, D)),
                pl.BlockSpec((None, 6, D), lambda i, *_: (i * TM_MOE // t, 0, 0)),
            ],
            out_specs=tok(D),
            scratch_shapes=[pltpu.VMEM((2, r, PK), jnp.uint32), pltpu.SemaphoreType.DMA((2,))],
        ),
        out_shape=jax.ShapeDtypeStruct((n, D), F32),
        compiler_params=_cparams("arbitrary"),
        name="combine",
    )(*tabs, code_t, gate_t, segc, _row_digits(r).T, u2, ys, s1, s3, s2, h, norms_l, mod_l)


def _glu_kernel(h_ref, g_ref, mod_ref, w_ref, b_ref, o_ref):
    u = (_rms(h_ref[...], g_ref[...]) * (1.0 + mod_ref[1:2, :]) + mod_ref[0:1, :]).astype(BF16)
    a = _dot(u, w_ref[:, :D]) + b_ref[:, :D]
    gate = _dot(u, w_ref[:, D:]) + b_ref[:, D:]
    o_ref[...] = (a * jax.nn.sigmoid(gate)).astype(BF16)


def _glu_call(h, g, mod_l, w1, b1, tm):
    b, t, _ = h.shape
    tok = lambda n: pl.BlockSpec((None, tm, n), lambda i, j: (i, j, 0))
    return pl.pallas_call(
        _glu_kernel,
        grid=(b, t // tm),
        in_specs=[
            tok(D),
            pl.BlockSpec((1, D), lambda i, j: (0, 0)),
            pl.BlockSpec((None, 6, D), lambda i, j: (i, 0, 0)),
            pl.BlockSpec((D, 2 * D), lambda i, j: (0, 0)),
            pl.BlockSpec((1, 2 * D), lambda i, j: (0, 0)),
        ],
        out_specs=tok(D),
        out_shape=jax.ShapeDtypeStruct((b, t, D), BF16),
        compiler_params=_cparams("parallel", "parallel"),
        name="glu",
    )(h, g, mod_l, w1, b1)


def _conv_kernel(cur_ref, prev_ref, next_ref, dw_ref, cv_ref, w2_ref, h_ref, n_ref, mod_ref, wr_ref,
                 h_out, u_out, lg_out, win, shifted, conv):
    j = pl.program_id(1)
    tm = cur_ref.shape[0]
    pad = CONV_W // 2
    win[0:HALO, :] = jnp.where(j > 0, prev_ref[...].astype(F32), 0.0)
    win[HALO:HALO + tm, :] = cur_ref[...].astype(F32)
    win[HALO + tm:, :] = jnp.where(j < pl.num_programs(1) - 1, next_ref[...].astype(F32), 0.0)
    ext = shifted.shape[1]
    full = win[...]
    shifted[0] = full[:ext, :]
    for s in range(1, SUBLANES):
        shifted[s] = pltpu.roll(full, full.shape[0] - s, axis=0)[:ext, :]

    def rows(ib, carry):
        r0 = pl.multiple_of(ib * 2 * SUBLANES, 2 * SUBLANES)
        for c0 in range(0, D, CONV_LANES):
            cols = slice(c0, c0 + CONV_LANES)
            acc0 = jnp.zeros((SUBLANES, CONV_LANES), F32)
            acc1 = jnp.zeros((SUBLANES, CONV_LANES), F32)
            for s in range(SUBLANES):
                taps = [(off // SUBLANES, off - (HALO - pad)) for off in range(HALO - pad, HALO - pad + CONV_W)
                        if off % SUBLANES == s]
                tiles = {a: shifted[s, pl.ds(r0 + a * SUBLANES, SUBLANES), cols]
                         for a in range(taps[0][0], taps[-1][0] + 2)}
                for a, tap in taps:
                    w = dw_ref[tap * SUBLANES:(tap + 1) * SUBLANES, cols]
                    acc0 = acc0 + tiles[a] * w
                    acc1 = acc1 + tiles[a + 1] * w
            conv[pl.ds(r0, SUBLANES), cols] = acc0
            conv[pl.ds(r0 + SUBLANES, SUBLANES), cols] = acc1
        return carry

    lax.fori_loop(0, tm // (2 * SUBLANES), rows, 0)
    uf = conv[...] + cv_ref[0:1, :]
    mu = jnp.mean(uf, axis=-1, keepdims=True)
    var = jnp.mean(jnp.square(uf - mu), axis=-1, keepdims=True)
    uf = (uf - mu) * lax.rsqrt(var + LN_EPS) * cv_ref[1:2, :] + cv_ref[2:3, :]
    y = _dot(_silu(uf).astype(BF16), w2_ref[...]) + cv_ref[3:4, :]
    _mixer_epilogue(y, h_ref, n_ref, mod_ref, wr_ref, h_out, u_out, lg_out)


def _conv_call(glu, dw, cvec, w2, h, norms_l, mod_l, wr_t, tm):
    b, t, _ = h.shape
    tok, ep_in, wr_spec, out_specs, out_shape = _epilogue_specs(b, t, tm)
    r = tm // HALO
    last = t // HALO - 1
    return pl.pallas_call(
        _conv_kernel,
        grid=(b, t // tm),
        in_specs=[
            tok(D),
            pl.BlockSpec((None, HALO, D), lambda i, j: (i, jnp.maximum(j * r - 1, 0), 0)),
            pl.BlockSpec((None, HALO, D), lambda i, j: (i, jnp.minimum((j + 1) * r, last), 0)),
            pl.BlockSpec((CONV_W * SUBLANES, D), lambda i, j: (0, 0)),
            pl.BlockSpec((4, D), lambda i, j: (0, 0)),
            pl.BlockSpec((D, D), lambda i, j: (0, 0)),
        ] + ep_in + [wr_spec],
        out_specs=out_specs,
        out_shape=out_shape,
        scratch_shapes=[pltpu.VMEM((tm + 2 * HALO, D), F32),
                        pltpu.VMEM((SUBLANES, tm + (CONV_W // SUBLANES) * SUBLANES, D), F32),
                        pltpu.VMEM((tm, D), F32)],
        compiler_params=_cparams("parallel", "parallel"),
        name="conv",
    )(glu, glu, glu, dw, cvec, w2, h, norms_l, mod_l, wr_t)


def _tile(t, pref):
    return pref if t % pref == 0 else t


def kernel(x, c, ctx, c_ctx, mod_w, mod_b, norms, ret_w_in, ret_w_out, ret_decay, conv_w1, conv_b1, conv_dw,
           conv_b_dw, conv_ln_g, conv_ln_b, conv_w2, conv_b2, moe_router, moe_bias, moe_w1, moe_w3, moe_w2,
           shared_w1, shared_w3, shared_w2):
    b, t, _ = x.shape
    l = ctx.shape[1]
    tm = _tile(t, 512)

    rows = -(-(b + 1) // 8) * 8
    cs = jnp.concatenate([c, c_ctx[None, :], jnp.zeros((rows - b - 1, D), F32)], 0)
    mod = _mod_call(cs, mod_w, mod_b).reshape(mod_w.shape[0], rows, 6, D)

    idx = jnp.arange(t, dtype=jnp.int32)
    pos_lat = jnp.stack([jnp.full((t,), l, jnp.int32), idx // GRID_W, idx % GRID_W], -1)
    zl = jnp.zeros((l,), jnp.int32)
    pos_ctx = jnp.stack([jnp.arange(l, dtype=jnp.int32), zl, zl], -1)
    ang_lat, ang_ctx = _rope_angles(pos_lat), _rope_angles(pos_ctx)

    head_perm = np.concatenate([np.arange(0, DK, 2), np.arange(1, DK, 2)])
    perm = np.concatenate([hh * DK + head_perm for hh in range(2 * HEADS)] + [np.arange(2 * QK_TOT, IN_DIM)])
    w_in = ret_w_in[0][:, perm].astype(BF16)
    log_g = jax.nn.log_sigmoid(ret_decay[0].astype(F32))

    n = b * t
    nb_max = _max_blocks(n)

    def moe_layer(i, h1, u2, logits_t):
        code, code_t, gate_t, segc, segr, pc_t, off_t = _route_call(logits_t, moe_bias[i])
        seg, tail, blk_e, nblk = _dispatch_tables(pc_t, off_t, nb_max)
        u2f = u2.reshape(n, D)
        xs = _dispatch_call(seg, tail, nblk, u2f, code, segr, nb_max)
        ys = _expert_call(i, blk_e, nblk, xs, moe_w1, moe_w3, moe_w2)
        out = _combine_call(seg, code_t, gate_t, segc, u2f, ys, shared_w1[i].astype(BF16), shared_w3[i].astype(BF16),
                            shared_w2[i].astype(BF16), h1.reshape(n, D), norms[i], mod[i], t)
        return out.reshape(b, t, D)

    q, k, v, gt = _inproj_call(x, norms[0, 0:1], mod[0], w_in, jnp.cos(ang_lat), jnp.sin(ang_lat), tm)
    sf0, sb0 = _ctxstate_call(log_g, ctx, norms[0, 0:1], mod[0, b], w_in[:, QK_TOT:2 * QK_TOT + V_TOT],
                              jnp.cos(ang_ctx), jnp.sin(ang_ctx))
    yn = _retention_call(log_g, q, k, v, sf0, sb0)
    h1, u2, logits_t = _readout_call(yn, gt, x, norms[0], mod[0], ret_w_out[0].astype(BF16),
                                     moe_router[0].T, tm)
    h2 = moe_layer(0, h1, u2, logits_t)

    glu = _glu_call(h2, norms[1, 0:1], mod[1], conv_w1[0].astype(BF16), conv_b1[0][None, :], tm)
    cvec = jnp.stack([conv_b_dw[0], conv_ln_g[0], conv_ln_b[0], conv_b2[0]], 0)
    dw_rows = jnp.repeat(conv_dw[0], SUBLANES, axis=0)
    h3, u2, logits_t = _conv_call(glu, dw_rows, cvec, conv_w2[0].astype(BF16), h2, norms[1], mod[1],
                                  moe_router[1].T, tm)
    return moe_layer(1, h3, u2, logits_t)
```

```python
import functools

import jax
import jax.numpy as jnp
import numpy as np
from jax import lax
from jax.experimental import pallas as pl
from jax.experimental.pallas import tpu as pltpu

F32 = jnp.float32
BF16 = jnp.bfloat16

D = 1024
HEADS = 4
DK = D // HEADS
DV = 2 * DK
QK_TOT = HEADS * DK
V_TOT = HEADS * DV
IN_DIM = 2 * QK_TOT + 2 * V_TOT
ROPE_AXES = (64, 96, 96)
ROPE_THETA = 10000.0
GRID_W = 64
CONV_W = 31
HALO = 16
SUBLANES = 8
CONV_LANES = 256
N_EXP = 64
N_GRP = 8
PER_GRP = N_EXP // N_GRP
TOP_K = 8
TOP_G = 4
F_EXP = 256
ROUTE_SCALE = 2.5
GATE_LANES = 128
MXU_DIM = 256
TM_MOE = 256
SEG_ALIGN = 8
POS_RADIX = 256
BM = 1024
EXP_SUB = 256
PK = D // 2
NORM_EPS = 1e-6
LN_EPS = 1e-5
RET_CHUNK = 256
VMEM_LIMIT = 56 * 1024 * 1024

_HI = lax.Precision.HIGHEST


def _cparams(*sem):
    return pltpu.CompilerParams(dimension_semantics=sem, vmem_limit_bytes=VMEM_LIMIT)


def _dot(a, b):
    return jnp.dot(a, b, preferred_element_type=F32)


def _rms(xf, g):
    return xf * lax.rsqrt(jnp.mean(xf * xf, axis=-1, keepdims=True) + NORM_EPS) * g


def _silu(x):
    return x * jax.nn.sigmoid(x)


def _mod_kernel(cs_ref, w_ref, b_ref, o_ref):
    s = _silu(cs_ref[...])
    o_ref[...] = jnp.dot(s, w_ref[...], preferred_element_type=F32, precision=_HI) + b_ref[...]


def _mod_call(cs, mod_w, mod_b):
    depth, _, n6 = mod_w.shape
    tn = 1536
    rows = cs.shape[0]
    return pl.pallas_call(
        _mod_kernel,
        grid=(depth, n6 // tn),
        in_specs=[
            pl.BlockSpec((rows, D), lambda i, j: (0, 0)),
            pl.BlockSpec((None, D, tn), lambda i, j: (i, 0, j)),
            pl.BlockSpec((None, 1, tn), lambda i, j: (i, 0, j)),
        ],
        out_specs=pl.BlockSpec((None, rows, tn), lambda i, j: (i, 0, j)),
        out_shape=jax.ShapeDtypeStruct((depth, rows, n6), F32),
        compiler_params=_cparams("parallel", "parallel"),
        name="mod",
    )(cs, mod_w, mod_b.reshape(depth, 1, n6))


def _rope_angles(pos):
    parts = []
    for a, d in enumerate(ROPE_AXES):
        inv = ROPE_THETA ** (-jnp.arange(0, d, 2, dtype=F32) / d)
        parts.append(pos[:, a:a + 1].astype(F32) * inv[None, :])
    return jnp.concatenate(parts, -1)


def _rope_store(z, cos, sin, scale, o_ref, col):
    half = DK // 2
    x0, x1 = z[:, :half], z[:, half:]
    o_ref[:, col:col + half] = ((x0 * cos - x1 * sin) * scale).astype(o_ref.dtype)
    o_ref[:, col + half:col + DK] = ((x0 * sin + x1 * cos) * scale).astype(o_ref.dtype)


def _inproj_kernel(x_ref, g_ref, mod_ref, w_ref, cos_ref, sin_ref, q_ref, k_ref, v_ref, gt_ref):
    u = (_rms(x_ref[...], g_ref[...]) * (1.0 + mod_ref[1:2, :]) + mod_ref[0:1, :]).astype(BF16)
    cos, sin = cos_ref[...], sin_ref[...]
    for h in range(HEADS):
        _rope_store(_dot(u, w_ref[:, h * DK:(h + 1) * DK]), cos, sin, DK ** -0.5, q_ref, h * DK)
        _rope_store(_dot(u, w_ref[:, QK_TOT + h * DK:QK_TOT + (h + 1) * DK]), cos, sin, 1.0, k_ref, h * DK)
    step = 512
    for j in range(V_TOT // step):
        c0 = 2 * QK_TOT + j * step
        v_ref[:, j * step:(j + 1) * step] = _dot(u, w_ref[:, c0:c0 + step]).astype(BF16)
        c1 = 2 * QK_TOT + V_TOT + j * step
        gt_ref[:, j * step:(j + 1) * step] = _dot(u, w_ref[:, c1:c1 + step]).astype(BF16)


def _inproj_call(x, g, mod_l, w_in, cos, sin, tm):
    b, t, _ = x.shape
    grid = (b, t // tm)
    tok = lambda n: pl.BlockSpec((None, tm, n), lambda i, j: (i, j, 0))
    return pl.pallas_call(
        _inproj_kernel,
        grid=grid,
        in_specs=[
            tok(D),
            pl.BlockSpec((1, D), lambda i, j: (0, 0)),
            pl.BlockSpec((None, 6, D), lambda i, j: (i, 0, 0)),
            pl.BlockSpec((D, IN_DIM), lambda i, j: (0, 0), pipeline_mode=pl.Buffered(1)),
            pl.BlockSpec((tm, DK // 2), lambda i, j: (j, 0)),
            pl.BlockSpec((tm, DK // 2), lambda i, j: (j, 0)),
        ],
        out_specs=[tok(QK_TOT), tok(QK_TOT), tok(V_TOT), tok(V_TOT)],
        out_shape=[
            jax.ShapeDtypeStruct((b, t, QK_TOT), BF16),
            jax.ShapeDtypeStruct((b, t, QK_TOT), BF16),
            jax.ShapeDtypeStruct((b, t, V_TOT), BF16),
            jax.ShapeDtypeStruct((b, t, V_TOT), BF16),
        ],
        compiler_params=_cparams("parallel", "parallel"),
        name="inproj",
    )(x, g, mod_l, w_in, cos, sin)


def _row_pow(lg, expo_fn, rows, cols):
    i = lax.broadcasted_iota(jnp.int32, (rows, cols), 0).astype(F32)
    return jnp.exp(lg * expo_fn(i))


def _ctxstate_kernel(lg_ref, c_ref, g_ref, mod_ref, w_ref, cos_ref, sin_ref, sf_ref, sb_ref, k_scr):
    l = c_ref.shape[0]
    u = (_rms(c_ref[...], g_ref[...]) * (1.0 + mod_ref[1:2, :]) + mod_ref[0:1, :]).astype(BF16)
    cos, sin = cos_ref[...], sin_ref[...]
    for h in range(HEADS):
        _rope_store(_dot(u, w_ref[:, h * DK:(h + 1) * DK]), cos, sin, 1.0, k_scr, h * DK)
    for h in range(HEADS):
        kh = k_scr[:, h * DK:(h + 1) * DK]
        vh = _dot(u, w_ref[:, QK_TOT + h * DV:QK_TOT + (h + 1) * DV]).astype(BF16)
        wf = _row_pow(lg_ref[0, h], lambda i: (l - 1.0) - i, l, DK)
        wb = _row_pow(lg_ref[1, h], lambda i: i, l, DK)
        sf_ref[h] = _dot((kh * wf).T.astype(BF16), vh)
        sb_ref[h] = _dot((kh * wb).T.astype(BF16), vh)


def _ctxstate_call(log_g, ctx, g, mod_row, w_kv, cos, sin):
    b, l, _ = ctx.shape
    st = jax.ShapeDtypeStruct((b, HEADS, DK, DV), F32)
    st_spec = pl.BlockSpec((None, HEADS, DK, DV), lambda i: (i, 0, 0, 0))
    return pl.pallas_call(
        _ctxstate_kernel,
        grid=(b,),
        in_specs=[
            pl.BlockSpec(memory_space=pltpu.SMEM),
            pl.BlockSpec((None, l, D), lambda i: (i, 0, 0)),
            pl.BlockSpec((1, D), lambda i: (0, 0)),
            pl.BlockSpec((6, D), lambda i: (0, 0)),
            pl.BlockSpec((D, QK_TOT + V_TOT), lambda i: (0, 0)),
            pl.BlockSpec((l, DK // 2), lambda i: (0, 0)),
            pl.BlockSpec((l, DK // 2), lambda i: (0, 0)),
        ],
        out_specs=[st_spec, st_spec],
        out_shape=[st, st],
        scratch_shapes=[pltpu.VMEM((l, QK_TOT), F32)],
        compiler_params=_cparams("parallel"),
        name="ctxstate",
    )(log_g, ctx, g, mod_row, w_kv, cos, sin)


def _retention_kernel(lg_ref, q_ref, k_ref, v_ref, sf0_ref, sb0_ref, o_ref, acc, sf, sb):
    h = pl.program_id(1)
    t = q_ref.shape[0]
    c = RET_CHUNK
    nc = t // c
    lgf, lgb = lg_ref[0, h], lg_ref[1, h]
    ri = lax.broadcasted_iota(jnp.int32, (c, c), 0).astype(F32)
    ci = lax.broadcasted_iota(jnp.int32, (c, c), 1).astype(F32)
    rel = ri - ci
    mask = (jnp.where(rel >= 0, jnp.exp(lgf * jnp.maximum(rel, 0.0)), 0.0)
            + jnp.where(rel <= 0, jnp.exp(lgb * jnp.maximum(-rel, 0.0)), 0.0))
    qdf = _row_pow(lgf, lambda i: i + 1.0, c, DV)
    qdb = _row_pow(lgb, lambda i: c - i, c, DV)
    kdf = _row_pow(lgf, lambda i: (c - 1.0) - i, c, DK)
    kdb = _row_pow(lgb, lambda i: i, c, DK)
    cdf = jnp.exp(jnp.full((1, DV), lgf * c, F32))
    cdb = jnp.exp(jnp.full((1, DV), lgb * c, F32))
    sf[...] = sf0_ref[...]
    sb[...] = sb0_ref[...]

    def rows(ic):
        return pl.ds(pl.multiple_of(ic * c, c), c)

    def fwd(rs):
        q, k, v = q_ref[rs, :], k_ref[rs, :], v_ref[rs, :]
        s = lax.dot_general(q, k, (((1,), (1,)), ((), ())), preferred_element_type=F32)
        o = _dot((s * mask).astype(BF16), v) + qdf * _dot(q, sf[...].astype(BF16))
        sf[...] = sf[...] * cdf + _dot((k.astype(F32) * kdf).T.astype(BF16), v)
        return o

    def bwd(rs):
        q, k, v = q_ref[rs, :], k_ref[rs, :], v_ref[rs, :]
        o = qdb * _dot(q, sb[...].astype(BF16))
        sb[...] = sb[...] * cdb + _dot((k.astype(F32) * kdb).T.astype(BF16), v)
        return o

    def finish(rs, o):
        o_ref[rs, :] = (o * lax.rsqrt(jnp.mean(o * o, axis=-1, keepdims=True) + NORM_EPS)).astype(BF16)

    def first_half(j, carry):
        acc[rows(j), :] = fwd(rows(j))
        acc[rows(nc - 1 - j), :] = bwd(rows(nc - 1 - j))
        return carry

    def second_half(j, carry):
        finish(rows(j), acc[rows(j), :] + fwd(rows(j)))
        finish(rows(nc - 1 - j), acc[rows(nc - 1 - j), :] + bwd(rows(nc - 1 - j)))
        return carry

    lax.fori_loop(0, nc // 2, first_half, 0)
    lax.fori_loop(nc // 2, nc, second_half, 0)


def _retention_call(log_g, q, k, v, sf0, sb0):
    b, t, _ = q.shape
    qk_spec = pl.BlockSpec((None, t, DK), lambda i, h: (i, 0, h))
    v_spec = pl.BlockSpec((None, t, DV), lambda i, h: (i, 0, h))
    s_spec = pl.BlockSpec((None, None, DK, DV), lambda i, h: (i, h, 0, 0))
    return pl.pallas_call(
        _retention_kernel,
        grid=(b, HEADS),
        in_specs=[pl.BlockSpec(memory_space=pltpu.SMEM), qk_spec, qk_spec, v_spec, s_spec, s_spec],
        out_specs=v_spec,
        out_shape=jax.ShapeDtypeStruct((b, t, V_TOT), BF16),
        scratch_shapes=[pltpu.VMEM((t, DV), F32), pltpu.VMEM((DK, DV), F32), pltpu.VMEM((DK, DV), F32)],
        compiler_params=_cparams("parallel", "parallel"),
        name="retention",
    )(log_g, q, k, v, sf0, sb0)


def _mixer_epilogue(y, h_ref, n_ref, mod_ref, wr_ref, h_out, u_out, lg_out):
    h1 = h_ref[...] + mod_ref[2:3, :] * _rms(y, n_ref[1:2, :])
    h_out[...] = h1
    u2 = _rms(h1, n_ref[2:3, :]) * (1.0 + mod_ref[4:5, :]) + mod_ref[3:4, :]
    u_out[...] = u2.astype(BF16)
    lg_out[...] = lax.dot_general(wr_ref[...], u2, (((1,), (1,)), ((), ())),
                                  preferred_element_type=F32, precision=_HI)


def _readout_kernel(y_ref, gt_ref, h_ref, n_ref, mod_ref, w_ref, wr_ref, h_out, u_out, lg_out):
    a = (y_ref[...].astype(F32) * _silu(gt_ref[...].astype(F32))).astype(BF16)
    _mixer_epilogue(_dot(a, w_ref[...]), h_ref, n_ref, mod_ref, wr_ref, h_out, u_out, lg_out)


def _epilogue_specs(b, t, tm):
    nt = t // tm
    tok = lambda n: pl.BlockSpec((None, tm, n), lambda i, j: (i, j, 0))
    in_specs = [
        tok(D),
        pl.BlockSpec((4, D), lambda i, j: (0, 0)),
        pl.BlockSpec((None, 6, D), lambda i, j: (i, 0, 0)),
    ]
    wr_spec = pl.BlockSpec((N_EXP, D), lambda i, j: (0, 0))
    out_specs = [tok(D), tok(D), pl.BlockSpec((N_EXP, tm), lambda i, j: (0, i * nt + j))]
    out_shape = [
        jax.ShapeDtypeStruct((b, t, D), F32),
        jax.ShapeDtypeStruct((b, t, D), BF16),
        jax.ShapeDtypeStruct((N_EXP, b * t), F32),
    ]
    return tok, in_specs, wr_spec, out_specs, out_shape


def _readout_call(yn, gt, h, norms_l, mod_l, w_out, wr_t, tm):
    b, t, _ = h.shape
    tok, ep_in, wr_spec, out_specs, out_shape = _epilogue_specs(b, t, tm)
    return pl.pallas_call(
        _readout_kernel,
        grid=(b, t // tm),
        in_specs=[tok(V_TOT), tok(V_TOT)] + ep_in
        + [pl.BlockSpec((V_TOT, D), lambda i, j: (0, 0)), wr_spec],
        out_specs=out_specs,
        out_shape=out_shape,
        compiler_params=_cparams("parallel", "parallel"),
        name="readout",
    )(yn, gt, h, norms_l, mod_l, w_out, wr_t)


def _route_kernel(lg_ref, b_ref, code_ref, code_t_ref, gate_t_ref, segc_ref, segr_ref, pc_ref, off_ref):
    i = pl.program_id(0)
    tr = lg_ref.shape[1]
    s = jax.nn.sigmoid(lg_ref[...])
    biased = s + b_ref[...]
    neg = -jnp.inf
    b3 = biased.reshape(N_GRP, PER_GRP, tr)
    io3 = lax.broadcasted_iota(jnp.int32, b3.shape, 1).astype(F32)
    m1 = jnp.max(b3, axis=1, keepdims=True)
    i1 = jnp.min(jnp.where(b3 == m1, io3, float(PER_GRP)), axis=1, keepdims=True)
    m2 = jnp.max(jnp.where(io3 == i1, neg, b3), axis=1, keepdims=True)
    gs = (m1 + m2).reshape(N_GRP, tr)
    iog = lax.broadcasted_iota(jnp.int32, gs.shape, 0).astype(F32)
    gsel = jnp.zeros(gs.shape, F32)
    work = gs
    for _ in range(TOP_G):
        m = jnp.max(work, axis=0, keepdims=True)
        gi = jnp.min(jnp.where(work == m, iog, float(N_GRP)), axis=0, keepdims=True)
        hit = iog == gi
        gsel = jnp.where(hit, 1.0, gsel)
        work = jnp.where(hit, neg, work)
    emask = jnp.broadcast_to(gsel.reshape(N_GRP, 1, tr), b3.shape).reshape(N_EXP, tr)
    work = jnp.where(emask > 0.0, biased, neg)
    ioe = lax.broadcasted_iota(jnp.int32, work.shape, 0).astype(F32)
    esel = jnp.zeros(work.shape, F32)
    for _ in range(TOP_K):
        m = jnp.max(work, axis=0, keepdims=True)
        ei = jnp.min(jnp.where(work == m, ioe, float(N_EXP)), axis=0, keepdims=True)
        hit = ioe == ei
        esel = jnp.where(hit, 1.0, esel)
        work = jnp.where(hit, neg, work)
    wsel = esel * s
    gate = wsel / jnp.sum(wsel, axis=0, keepdims=True) * ROUTE_SCALE

    t_r = lax.broadcasted_iota(jnp.int32, (tr, tr), 0)
    t_c = lax.broadcasted_iota(jnp.int32, (tr, tr), 1)
    rank = _dot(esel.astype(BF16), jnp.where(t_r < t_c, 1.0, 0.0).astype(BF16))
    cnt = jnp.sum(esel, axis=1, keepdims=True)
    pc = jnp.floor((cnt + (SEG_ALIGN - 1.0)) * (1.0 / SEG_ALIGN)) * SEG_ALIGN
    e_r = lax.broadcasted_iota(jnp.int32, (N_EXP, N_EXP), 0)
    e_c = lax.broadcasted_iota(jnp.int32, (N_EXP, N_EXP), 1)
    off = jnp.dot(jnp.where(e_c < e_r, 1.0, 0.0), jnp.broadcast_to(pc, (N_EXP, 128)),
                  preferred_element_type=F32, precision=_HI)[:, 0:1]
    p = off + rank
    p_hi = jnp.where(esel > 0.0, jnp.floor(p * (1.0 / POS_RADIX)) * POS_RADIX, -float(POS_RADIX))
    p_lo = jnp.where(esel > 0.0, p - p_hi, 0.0)
    minus_r = jnp.where(lax.broadcasted_iota(jnp.int32, (2 * N_EXP, tr), 0) < 2, -1.0, 0.0)
    code = jnp.concatenate([p_hi, p_lo, minus_r], 0)
    code_ref[...] = code.astype(BF16)
    code_t_ref[...] = jnp.concatenate([code[:2 * N_EXP].T, code[2 * N_EXP:].T], 1).astype(BF16)
    gate_t_ref[...] = jnp.concatenate([gate, jnp.zeros((GATE_LANES - N_EXP, tr), F32)], 0).T
    lane = lax.broadcasted_iota(jnp.int32, (2 * N_EXP, GATE_LANES), 1)
    off2 = jnp.concatenate([off, off], 0)
    end2 = jnp.concatenate([off + pc, off + pc], 0)
    bounds = jnp.where(lane == 0, off2, jnp.where(lane == 1, end2, 0.0))
    segc_ref[...] = bounds
    segr_ref[...] = bounds.T[0:SUBLANES, :]

    @pl.when(i == 0)
    def _():
        pc_ref[...] = jnp.zeros(pc_ref.shape, F32)
        off_ref[...] = jnp.zeros(off_ref.shape, F32)

    tile_lane = lax.broadcasted_iota(jnp.int32, pc_ref.shape, 1)
    pc_ref[...] = jnp.where(tile_lane == i, pc, pc_ref[...])
    off_ref[...] = jnp.where(tile_lane == i, off, off_ref[...])


def _route_call(logits_t, bias):
    n = logits_t.shape[1]
    nt = n // TM_MOE
    return pl.pallas_call(
        _route_kernel,
        grid=(nt,),
        in_specs=[pl.BlockSpec((N_EXP, TM_MOE), lambda i: (0, i)), pl.BlockSpec((N_EXP, 1), lambda i: (0, 0))],
        out_specs=[
            pl.BlockSpec((4 * N_EXP, TM_MOE), lambda i: (0, i)),
            pl.BlockSpec((TM_MOE, 4 * N_EXP), lambda i: (i, 0)),
            pl.BlockSpec((TM_MOE, GATE_LANES), lambda i: (i, 0)),
            pl.BlockSpec((None, 2 * N_EXP, GATE_LANES), lambda i: (i, 0, 0)),
            pl.BlockSpec((None, SUBLANES, GATE_LANES), lambda i: (i, 0, 0)),
            pl.BlockSpec((N_EXP, nt), lambda i: (0, 0)),
            pl.BlockSpec((N_EXP, nt), lambda i: (0, 0)),
        ],
        out_shape=[
            jax.ShapeDtypeStruct((4 * N_EXP, n), BF16),
            jax.ShapeDtypeStruct((n, 4 * N_EXP), BF16),
            jax.ShapeDtypeStruct((n, GATE_LANES), F32),
            jax.ShapeDtypeStruct((nt, 2 * N_EXP, GATE_LANES), F32),
            jax.ShapeDtypeStruct((nt, SUBLANES, GATE_LANES), F32),
            jax.ShapeDtypeStruct((N_EXP, nt), F32),
            jax.ShapeDtypeStruct((N_EXP, nt), F32),
        ],
        compiler_params=_cparams("arbitrary"),
        name="route",
    )(logits_t, bias.reshape(N_EXP, 1))


def _dispatch_tables(pc_t, off_t, nb_max):
    pc = pc_t.astype(jnp.int32)
    used = pc.sum(1)
    region = (used + BM - 1) // BM * BM
    ends = jnp.cumsum(region)
    starts = ends - region
    dst = starts[:, None] + jnp.cumsum(pc, 1) - pc
    nblk = ends[-1] // BM
    blk = jnp.minimum(jnp.arange(nb_max, dtype=jnp.int32), nblk - 1)
    blk_e = jnp.minimum(jnp.sum(ends[None, :] <= blk[:, None] * BM, axis=1), N_EXP - 1).astype(jnp.int32)
    flat = lambda a: a.T.reshape(-1).astype(jnp.int32)
    seg = (flat(off_t), flat(dst), flat(pc), pc.sum(0).astype(jnp.int32))
    tail = ((starts + used).astype(jnp.int32), (region - used).astype(jnp.int32))
    return seg, tail, blk_e, nblk.reshape(1).astype(jnp.int32)


def _swiglu(u, w1, w3, w2):
    return _dot((_silu(_dot(u, w1)) * _dot(u, w3)).astype(BF16), w2)


def _round_bf16(x):
    return x.astype(BF16).astype(F32)


def _pack_pair(lo, hi):
    lo_b = lax.bitcast_convert_type(lo, jnp.uint32)
    hi_b = lax.bitcast_convert_type(hi, jnp.uint32)
    return (hi_b & jnp.uint32(0xFFFF0000)) | (lo_b >> 16)


def _unpack_pair(w):
    lo = lax.bitcast_convert_type(w << 16, F32).astype(BF16)
    hi = lax.bitcast_convert_type(w & jnp.uint32(0xFFFF0000), F32).astype(BF16)
    return lo, hi


def _segment_copies(so_ref, do_ref, ln_ref, tile, make_copy):
    def seg(e, carry):
        j = tile * N_EXP + e
        ln = pl.multiple_of(ln_ref[j], SEG_ALIGN)

        @pl.when(ln > 0)
        def _():
            make_copy(pl.multiple_of(so_ref[j], SEG_ALIGN), pl.multiple_of(do_ref[j], SEG_ALIGN), ln).start()

        return carry

    lax.fori_loop(0, N_EXP, seg, 0)


def _dispatch_kernel(so_ref, do_ref, ln_ref, tt_ref, td_ref, tl_ref, nb_ref, u_ref, code_ref, segr_ref, digits_ref,
                     xs_out, buf, zbuf, sem):
    i = pl.program_id(0)
    slot = i % 2
    r, tm = buf.shape[1], u_ref.shape[0]
    nb_max = xs_out.shape[0] // BM

    def strip(slot_, so, do, ln):
        return pltpu.make_async_copy(buf.at[slot_, pl.ds(so, ln)], xs_out.at[pl.ds(do, ln)], sem.at[slot_])

    def wait_tile(slot_, tile):
        strip(slot_, 0, 0, pl.multiple_of(tt_ref[tile], SEG_ALIGN)).wait()

    def zero_fills(act):
        def tail(e, carry):
            ln = pl.multiple_of(tl_ref[e], SEG_ALIGN)

            @pl.when(ln > 0)
            def _():
                act(pltpu.make_async_copy(zbuf.at[pl.ds(0, ln)],
                                          xs_out.at[pl.ds(pl.multiple_of(td_ref[e], SEG_ALIGN), ln)], sem.at[2]))

            return carry

        lax.fori_loop(0, N_EXP, tail, 0)

        def blk(j, carry):
            act(pltpu.make_async_copy(zbuf, xs_out.at[pl.ds(pl.multiple_of(j * BM, BM), BM)], sem.at[2]))
            return carry

        lax.fori_loop(nb_ref[0], nb_max, blk, 0)

    @pl.when(i == 0)
    def _():
        zbuf[...] = jnp.zeros(zbuf.shape, jnp.uint32)
        zero_fills(lambda cp: cp.start())

    @pl.when(i >= 2)
    def _():
        wait_tile(slot, i - 2)

    row = lax.broadcasted_iota(jnp.int32, (r, GATE_LANES), 0).astype(F32)
    own = jnp.where(row >= segr_ref[0:1, :], jnp.where(row < segr_ref[1:2, :], 1.0, 0.0), 0.0).astype(BF16)
    z = _dot(jnp.concatenate([own, digits_ref[...]], axis=1), code_ref[...])
    perm = jnp.where(z == 0.0, 1.0, 0.0).astype(BF16)
    step = 256
    for c in range(0, PK, step):
        buf[slot, :, c:c + step] = _pack_pair(_dot(perm, u_ref[:, c:c + step]),
                                              _dot(perm, u_ref[:, PK + c:PK + c + step]))
    _segment_copies(so_ref, do_ref, ln_ref, i, functools.partial(strip, slot))

    @pl.when(i == pl.num_programs(0) - 1)
    def _():
        wait_tile(slot, i)

        @pl.when(i >= 1)
        def _():
            wait_tile(1 - slot, i - 1)

        zero_fills(lambda cp: cp.wait())


def _expert_kernel(be_ref, nb_ref, x_ref, w1_ref, w3_ref, w2_ref, y_ref, w1b, w3b, w2b):
    i = pl.program_id(0)
    live = i < nb_ref[0]

    @pl.when(live & ((i == 0) | (be_ref[i] != be_ref[jnp.maximum(i - 1, 0)])))
    def _():
        w1b[...] = w1_ref[...].astype(BF16)
        w3b[...] = w3_ref[...].astype(BF16)
        w2b[...] = w2_ref[...].astype(BF16)

    @pl.when(live)
    def _():
        up = []
        for r0 in range(0, BM, EXP_SUB):
            lo, hi = _unpack_pair(x_ref[r0:r0 + EXP_SUB, :])
            up.append((_dot(lo, w1b[:PK, :]) + _dot(hi, w1b[PK:, :]), _dot(lo, w3b[:PK, :]) + _dot(hi, w3b[PK:, :])))
        for j, (a, g) in enumerate(up):
            hid = (_silu(a) * g).astype(BF16)
            y_ref[j * EXP_SUB:(j + 1) * EXP_SUB, :] = _pack_pair(_round_bf16(_dot(hid, w2b[:, :PK])),
                                                                 _round_bf16(_dot(hid, w2b[:, PK:])))

    @pl.when(jnp.logical_not(live))
    def _():
        y_ref[...] = jnp.zeros(y_ref.shape, jnp.uint32)


def _combine_kernel(so_ref, do_ref, ln_ref, tt_ref, code_t_ref, gate_t_ref, segc_ref, digits_ref, u_ref, ys_ref,
                    s1_ref, s3_ref, s2_ref, h_ref, n_ref, mod_ref, o_ref, buf, sem):
    i = pl.program_id(0)
    nt = pl.num_programs(0)
    slot = i % 2
    r, tm = buf.shape[1], u_ref.shape[0]

    def strip(slot_, so, do, ln):
        return pltpu.make_async_copy(ys_ref.at[pl.ds(do, ln)], buf.at[slot_, pl.ds(so, ln)], sem.at[slot_])

    @pl.when(i == 0)
    def _():
        buf[...] = jnp.zeros(buf.shape, jnp.uint32)
        _segment_copies(so_ref, do_ref, ln_ref, 0, functools.partial(strip, 0))

    @pl.when(i + 1 < nt)
    def _():
        _segment_copies(so_ref, do_ref, ln_ref, i + 1, functools.partial(strip, 1 - slot))

    strip(slot, 0, 0, pl.multiple_of(tt_ref[i], SEG_ALIGN)).wait()

    col = lax.broadcasted_iota(jnp.int32, (2 * N_EXP, r), 1).astype(F32)
    own = jnp.where(col >= segc_ref[:, 0:1], jnp.where(col < segc_ref[:, 1:2], 1.0, 0.0), 0.0).astype(BF16)
    z = _dot(code_t_ref[...], jnp.concatenate([own, digits_ref[...]], axis=0))
    q = jnp.where(z == 0.0, _dot(gate_t_ref[...].astype(BF16), own), 0.0).astype(BF16)
    lo, hi = _unpack_pair(buf[slot])
    moe = jnp.concatenate([_dot(q, lo), _dot(q, hi)], axis=1)
    moe = moe + _swiglu(u_ref[...], s1_ref[...], s3_ref[...], s2_ref[...])
    o_ref[...] = h_ref[...] + mod_ref[5:6, :] * _rms(moe, n_ref[3:4, :])


def _sorted_rows(tm):
    return -(-(TOP_K * tm + N_EXP * (SEG_ALIGN - 1)) // MXU_DIM) * MXU_DIM


def _max_blocks(n):
    nt = n // TM_MOE
    rows = TOP_K * n + N_EXP * (SEG_ALIGN - 1) * nt + N_EXP * (BM - SEG_ALIGN)
    return -(-rows // BM)


def _row_digits(r):
    idx = np.arange(r)
    digits = np.zeros((r, GATE_LANES), np.float32)
    digits[:, 0] = idx // POS_RADIX * POS_RADIX
    digits[:, 1] = idx % POS_RADIX
    return jnp.asarray(digits, BF16)


def _dispatch_call(seg, tail, nblk, u2, code, segr, nb_max):
    n = u2.shape[0]
    r = _sorted_rows(TM_MOE)
    return pl.pallas_call(
        _dispatch_kernel,
        grid_spec=pltpu.PrefetchScalarGridSpec(
            num_scalar_prefetch=7,
            grid=(n // TM_MOE,),
            in_specs=[
                pl.BlockSpec((TM_MOE, D), lambda i, *_: (i, 0)),
                pl.BlockSpec((4 * N_EXP, TM_MOE), lambda i, *_: (0, i)),
                pl.BlockSpec((None, SUBLANES, GATE_LANES), lambda i, *_: (i, 0, 0)),
                pl.BlockSpec((r, GATE_LANES), lambda i, *_: (0, 0)),
            ],
            out_specs=pl.BlockSpec(memory_space=pl.ANY),
            scratch_shapes=[pltpu.VMEM((2, r, PK), jnp.uint32), pltpu.VMEM((BM, PK), jnp.uint32),
                            pltpu.SemaphoreType.DMA((3,))],
        ),
        out_shape=jax.ShapeDtypeStruct((nb_max * BM, PK), jnp.uint32),
        compiler_params=_cparams("arbitrary"),
        name="dispatch",
    )(*seg, tail[0], tail[1], nblk, u2, code, segr, _row_digits(r))


def _expert_call(layer, blk_e, nblk, xs, w1, w3, w2):
    nb_max = xs.shape[0] // BM
    w_blk = lambda i, be, nb: (layer, be[i], 0, 0)
    return pl.pallas_call(
        _expert_kernel,
        grid_spec=pltpu.PrefetchScalarGridSpec(
            num_scalar_prefetch=2,
            grid=(nb_max,),
            in_specs=[
                pl.BlockSpec((BM, PK), lambda i, be, nb: (jnp.maximum(jnp.minimum(i, nb[0] - 1), 0), 0)),
                pl.BlockSpec((None, None, D, F_EXP), w_blk),
                pl.BlockSpec((None, None, D, F_EXP), w_blk),
                pl.BlockSpec((None, None, F_EXP, D), w_blk),
            ],
            out_specs=pl.BlockSpec((BM, PK), lambda i, be, nb: (i, 0)),
            scratch_shapes=[pltpu.VMEM((D, F_EXP), BF16), pltpu.VMEM((D, F_EXP), BF16),
                            pltpu.VMEM((F_EXP, D), BF16)],
        ),
        out_shape=jax.ShapeDtypeStruct(xs.shape, jnp.uint32),
        compiler_params=_cparams("arbitrary"),
        name="experts",
    )(blk_e, nblk, xs, w1, w3, w2)


def _combine_call(tabs, code_t, gate_t, segc, u2, ys, s1, s3, s2, h, norms_l, mod_l, t):
    n = u2.shape[0]
    r = _sorted_rows(TM_MOE)
    tok = lambda w: pl.BlockSpec((TM_MOE, w), lambda i, *_: (i, 0))
    const = lambda shp: pl.BlockSpec(shp, lambda i, *_: (0,) * len(shp))
    return pl.pallas_call(
        _combine_kernel,
        grid_spec=pltpu.PrefetchScalarGridSpec(
            num_scalar_prefetch=4,
            grid=(n // TM_MOE,),
            in_specs=[
                tok(4 * N_EXP),
                tok(GATE_LANES),
                pl.BlockSpec((None, 2 * N_EXP, GATE_LANES), lambda i, *_: (i, 0, 0)),
                const((2 * N_EXP, r)),
                tok(D),
                pl.BlockSpec(memory_space=pl.ANY),
                const((D, F_EXP)), const((D, F_EXP)), const((F_EXP, D)),
                tok(D),
                const((4, D)),
                pl.BlockSpec((None, 6, D), lambda i, *_: (i * TM_MOE // t, 0, 0)),
            ],
            out_specs=tok(D),
            scratch_shapes=[pltpu.VMEM((2, r, PK), jnp.uint32), pltpu.SemaphoreType.DMA((2,))],
        ),
        out_shape=jax.ShapeDtypeStruct((n, D), F32),
        compiler_params=_cparams("arbitrary"),
        name="combine",
    )(*tabs, code_t, gate_t, segc, _row_digits(r).T, u2, ys, s1, s3, s2, h, norms_l, mod_l)


def _glu_kernel(h_ref, g_ref, mod_ref, w_ref, b_ref, o_ref):
    u = (_rms(h_ref[...], g_ref[...]) * (1.0 + mod_ref[1:2, :]) + mod_ref[0:1, :]).astype(BF16)
    a = _dot(u, w_ref[:, :D]) + b_ref[:, :D]
    gate = _dot(u, w_ref[:, D:]) + b_ref[:, D:]
    o_ref[...] = (a * jax.nn.sigmoid(gate)).astype(BF16)


def _glu_call(h, g, mod_l, w1, b1, tm):
    b, t, _ = h.shape
    tok = lambda n: pl.BlockSpec((None, tm, n), lambda i, j: (i, j, 0))
    return pl.pallas_call(
        _glu_kernel,
        grid=(b, t // tm),
        in_specs=[
            tok(D),
            pl.BlockSpec((1, D), lambda i, j: (0, 0)),
            pl.BlockSpec((None, 6, D), lambda i, j: (i, 0, 0)),
            pl.BlockSpec((D, 2 * D), lambda i, j: (0, 0)),
            pl.BlockSpec((1, 2 * D), lambda i, j: (0, 0)),
        ],
        out_specs=tok(D),
        out_shape=jax.ShapeDtypeStruct((b, t, D), BF16),
        compiler_params=_cparams("parallel", "parallel"),
        name="glu",
    )(h, g, mod_l, w1, b1)


def _conv_kernel(cur_ref, prev_ref, next_ref, dw_ref, cv_ref, w2_ref, h_ref, n_ref, mod_ref, wr_ref,
                 h_out, u_out, lg_out, win, shifted, conv):
    j = pl.program_id(1)
    tm = cur_ref.shape[0]
    pad = CONV_W // 2
    win[0:HALO, :] = jnp.where(j > 0, prev_ref[...].astype(F32), 0.0)
    win[HALO:HALO + tm, :] = cur_ref[...].astype(F32)
    win[HALO + tm:, :] = jnp.where(j < pl.num_programs(1) - 1, next_ref[...].astype(F32), 0.0)
    ext = shifted.shape[1]
    full = win[...]
    shifted[0] = full[:ext, :]
    for s in range(1, SUBLANES):
        shifted[s] = pltpu.roll(full, full.shape[0] - s, axis=0)[:ext, :]

    def rows(ib, carry):
        r0 = pl.multiple_of(ib * 2 * SUBLANES, 2 * SUBLANES)
        for c0 in range(0, D, CONV_LANES):
            cols = slice(c0, c0 + CONV_LANES)
            acc0 = jnp.zeros((SUBLANES, CONV_LANES), F32)
            acc1 = jnp.zeros((SUBLANES, CONV_LANES), F32)
            for s in range(SUBLANES):
                taps = [(off // SUBLANES, off - (HALO - pad)) for off in range(HALO - pad, HALO - pad + CONV_W)
                        if off % SUBLANES == s]
                tiles = {a: shifted[s, pl.ds(r0 + a * SUBLANES, SUBLANES), cols]
                         for a in range(taps[0][0], taps[-1][0] + 2)}
                for a, tap in taps:
                    w = dw_ref[tap * SUBLANES:(tap + 1) * SUBLANES, cols]
                    acc0 = acc0 + tiles[a] * w
                    acc1 = acc1 + tiles[a + 1] * w
            conv[pl.ds(r0, SUBLANES), cols] = acc0
            conv[pl.ds(r0 + SUBLANES, SUBLANES), cols] = acc1
        return carry

    lax.fori_loop(0, tm // (2 * SUBLANES), rows, 0)
    uf = conv[...] + cv_ref[0:1, :]
    mu = jnp.mean(uf, axis=-1, keepdims=True)
    var = jnp.mean(jnp.square(uf - mu), axis=-1, keepdims=True)
    uf = (uf - mu) * lax.rsqrt(var + LN_EPS) * cv_ref[1:2, :] + cv_ref[2:3, :]
    y = _dot(_silu(uf).astype(BF16), w2_ref[...]) + cv_ref[3:4, :]
    _mixer_epilogue(y, h_ref, n_ref, mod_ref, wr_ref, h_out, u_out, lg_out)


def _conv_call(glu, dw, cvec, w2, h, norms_l, mod_l, wr_t, tm):
    b, t, _ = h.shape
    tok, ep_in, wr_spec, out_specs, out_shape = _epilogue_specs(b, t, tm)
    r = tm // HALO
    last = t // HALO - 1
    return pl.pallas_call(
        _conv_kernel,
        grid=(b, t // tm),
        in_specs=[
            tok(D),
            pl.BlockSpec((None, HALO, D), lambda i, j: (i, jnp.maximum(j * r - 1, 0), 0)),
            pl.BlockSpec((None, HALO, D), lambda i, j: (i, jnp.minimum((j + 1) * r, last), 0)),
            pl.BlockSpec((CONV_W * SUBLANES, D), lambda i, j: (0, 0)),
            pl.BlockSpec((4, D), lambda i, j: (0, 0)),
            pl.BlockSpec((D, D), lambda i, j: (0, 0)),
        ] + ep_in + [wr_spec],
        out_specs=out_specs,
        out_shape=out_shape,
        scratch_shapes=[pltpu.VMEM((tm + 2 * HALO, D), F32),
                        pltpu.VMEM((SUBLANES, tm + (CONV_W // SUBLANES) * SUBLANES, D), F32),
                        pltpu.VMEM((tm, D), F32)],
        compiler_params=_cparams("parallel", "parallel"),
        name="conv",
    )(glu, glu, glu, dw, cvec, w2, h, norms_l, mod_l, wr_t)


def _tile(t, pref):
    return pref if t % pref == 0 else t


def kernel(x, c, ctx, c_ctx, mod_w, mod_b, norms, ret_w_in, ret_w_out, ret_decay, conv_w1, conv_b1, conv_dw,
           conv_b_dw, conv_ln_g, conv_ln_b, conv_w2, conv_b2, moe_router, moe_bias, moe_w1, moe_w3, moe_w2,
           shared_w1, shared_w3, shared_w2):
    b, t, _ = x.shape
    l = ctx.shape[1]
    tm = _tile(t, 512)

    rows = -(-(b + 1) // 8) * 8
    cs = jnp.concatenate([c, c_ctx[None, :], jnp.zeros((rows - b - 1, D), F32)], 0)
    mod = _mod_call(cs, mod_w, mod_b).reshape(mod_w.shape[0], rows, 6, D)

    idx = jnp.arange(t, dtype=jnp.int32)
    pos_lat = jnp.stack([jnp.full((t,), l, jnp.int32), idx // GRID_W, idx % GRID_W], -1)
    zl = jnp.zeros((l,), jnp.int32)
    pos_ctx = jnp.stack([jnp.arange(l, dtype=jnp.int32), zl, zl], -1)
    ang_lat, ang_ctx = _rope_angles(pos_lat), _rope_angles(pos_ctx)

    head_perm = np.concatenate([np.arange(0, DK, 2), np.arange(1, DK, 2)])
    perm = np.concatenate([hh * DK + head_perm for hh in range(2 * HEADS)] + [np.arange(2 * QK_TOT, IN_DIM)])
    w_in = ret_w_in[0][:, perm].astype(BF16)
    log_g = jax.nn.log_sigmoid(ret_decay[0].astype(F32))

    n = b * t
    nb_max = _max_blocks(n)

    def moe_layer(i, h1, u2, logits_t):
        code, code_t, gate_t, segc, segr, pc_t, off_t = _route_call(logits_t, moe_bias[i])
        seg, tail, blk_e, nblk = _dispatch_tables(pc_t, off_t, nb_max)
        u2f = u2.reshape(n, D)
        xs = _dispatch_call(seg, tail, nblk, u2f, code, segr, nb_max)
        ys = _expert_call(i, blk_e, nblk, xs, moe_w1, moe_w3, moe_w2)
        out = _combine_call(seg, code_t, gate_t, segc, u2f, ys, shared_w1[i].astype(BF16), shared_w3[i].astype(BF16),
                            shared_w2[i].astype(BF16), h1.reshape(n, D), norms[i], mod[i], t)
        return out.reshape(b, t, D)

    q, k, v, gt = _inproj_call(x, norms[0, 0:1], mod[0], w_in, jnp.cos(ang_lat), jnp.sin(ang_lat), tm)
    sf0, sb0 = _ctxstate_call(log_g, ctx, norms[0, 0:1], mod[0, b], w_in[:, QK_TOT:2 * QK_TOT + V_TOT],
                              jnp.cos(ang_ctx), jnp.sin(ang_ctx))
    yn = _retention_call(log_g, q, k, v, sf0, sb0)
    h1, u2, logits_t = _readout_call(yn, gt, x, norms[0], mod[0], ret_w_out[0].astype(BF16),
                                     moe_router[0].T, tm)
    h2 = moe_layer(0, h1, u2, logits_t)

    glu = _glu_call(h2, norms[1, 0:1], mod[1], conv_w1[0].astype(BF16), conv_b1[0][None, :], tm)
    cvec = jnp.stack([conv_b_dw[0], conv_ln_g[0], conv_ln_b[0], conv_b2[0]], 0)
    dw_rows = jnp.repeat(conv_dw[0], SUBLANES, axis=0)
    h3, u2, logits_t = _conv_call(glu, dw_rows, cvec, conv_w2[0].astype(BF16), h2, norms[1], mod[1],
                                  moe_router[1].T, tm)
    return moe_layer(1, h3, u2, logits_t)
```

```python
import functools

import jax
import jax.numpy as jnp
import numpy as np
from jax import lax
from jax.experimental import pallas as pl
from jax.experimental.pallas import tpu as pltpu

F32 = jnp.float32
BF16 = jnp.bfloat16

D = 1024
HEADS = 4
DK = D // HEADS
DV = 2 * DK
QK_TOT = HEADS * DK
V_TOT = HEADS * DV
IN_DIM = 2 * QK_TOT + 2 * V_TOT
ROPE_AXES = (64, 96, 96)
ROPE_THETA = 10000.0
GRID_W = 64
CONV_W = 31
HALO = 16
SUBLANES = 8
CONV_LANES = 256
N_EXP = 64
N_GRP = 8
PER_GRP = N_EXP // N_GRP
TOP_K = 8
TOP_G = 4
F_EXP = 256
ROUTE_SCALE = 2.5
GATE_LANES = 128
MXU_DIM = 256
TM_MOE = 256
SEG_ALIGN = 8
POS_RADIX = 256
BM = 1024
DISPATCH_SLOTS = 3
EXP_SUB = 256
PK = D // 2
NORM_EPS = 1e-6
LN_EPS = 1e-5
RET_CHUNK = 256
VMEM_LIMIT = 56 * 1024 * 1024

_HI = lax.Precision.HIGHEST


def _cparams(*sem):
    return pltpu.CompilerParams(dimension_semantics=sem, vmem_limit_bytes=VMEM_LIMIT)


def _dot(a, b):
    return jnp.dot(a, b, preferred_element_type=F32)


def _rms(xf, g):
    return xf * lax.rsqrt(jnp.mean(xf * xf, axis=-1, keepdims=True) + NORM_EPS) * g


def _silu(x):
    return x * jax.nn.sigmoid(x)


def _mod_kernel(cs_ref, w_ref, b_ref, o_ref):
    s = _silu(cs_ref[...])
    o_ref[...] = jnp.dot(s, w_ref[...], preferred_element_type=F32, precision=_HI) + b_ref[...]


def _mod_call(cs, mod_w, mod_b):
    depth, _, n6 = mod_w.shape
    tn = 1536
    rows = cs.shape[0]
    return pl.pallas_call(
        _mod_kernel,
        grid=(depth, n6 // tn),
        in_specs=[
            pl.BlockSpec((rows, D), lambda i, j: (0, 0)),
            pl.BlockSpec((None, D, tn), lambda i, j: (i, 0, j)),
            pl.BlockSpec((None, 1, tn), lambda i, j: (i, 0, j)),
        ],
        out_specs=pl.BlockSpec((None, rows, tn), lambda i, j: (i, 0, j)),
        out_shape=jax.ShapeDtypeStruct((depth, rows, n6), F32),
        compiler_params=_cparams("parallel", "parallel"),
        name="mod",
    )(cs, mod_w, mod_b.reshape(depth, 1, n6))


def _rope_angles(pos):
    parts = []
    for a, d in enumerate(ROPE_AXES):
        inv = ROPE_THETA ** (-jnp.arange(0, d, 2, dtype=F32) / d)
        parts.append(pos[:, a:a + 1].astype(F32) * inv[None, :])
    return jnp.concatenate(parts, -1)


def _rope_store(z, cos, sin, scale, o_ref, col):
    half = DK // 2
    x0, x1 = z[:, :half], z[:, half:]
    o_ref[:, col:col + half] = ((x0 * cos - x1 * sin) * scale).astype(o_ref.dtype)
    o_ref[:, col + half:col + DK] = ((x0 * sin + x1 * cos) * scale).astype(o_ref.dtype)


def _inproj_kernel(x_ref, g_ref, mod_ref, w_ref, cos_ref, sin_ref, q_ref, k_ref, v_ref, gt_ref):
    u = (_rms(x_ref[...], g_ref[...]) * (1.0 + mod_ref[1:2, :]) + mod_ref[0:1, :]).astype(BF16)
    cos, sin = cos_ref[...], sin_ref[...]
    for h in range(HEADS):
        _rope_store(_dot(u, w_ref[:, h * DK:(h + 1) * DK]), cos, sin, DK ** -0.5, q_ref, h * DK)
        _rope_store(_dot(u, w_ref[:, QK_TOT + h * DK:QK_TOT + (h + 1) * DK]), cos, sin, 1.0, k_ref, h * DK)
    step = 512
    for j in range(V_TOT // step):
        c0 = 2 * QK_TOT + j * step
        v_ref[:, j * step:(j + 1) * step] = _dot(u, w_ref[:, c0:c0 + step]).astype(BF16)
        c1 = 2 * QK_TOT + V_TOT + j * step
        gt_ref[:, j * step:(j + 1) * step] = _dot(u, w_ref[:, c1:c1 + step]).astype(BF16)


def _inproj_call(x, g, mod_l, w_in, cos, sin, tm):
    b, t, _ = x.shape
    grid = (b, t // tm)
    tok = lambda n: pl.BlockSpec((None, tm, n), lambda i, j: (i, j, 0))
    return pl.pallas_call(
        _inproj_kernel,
        grid=grid,
        in_specs=[
            tok(D),
            pl.BlockSpec((1, D), lambda i, j: (0, 0)),
            pl.BlockSpec((None, 6, D), lambda i, j: (i, 0, 0)),
            pl.BlockSpec((D, IN_DIM), lambda i, j: (0, 0), pipeline_mode=pl.Buffered(1)),
            pl.BlockSpec((tm, DK // 2), lambda i, j: (j, 0)),
            pl.BlockSpec((tm, DK // 2), lambda i, j: (j, 0)),
        ],
        out_specs=[tok(QK_TOT), tok(QK_TOT), tok(V_TOT), tok(V_TOT)],
        out_shape=[
            jax.ShapeDtypeStruct((b, t, QK_TOT), BF16),
            jax.ShapeDtypeStruct((b, t, QK_TOT), BF16),
            jax.ShapeDtypeStruct((b, t, V_TOT), BF16),
            jax.ShapeDtypeStruct((b, t, V_TOT), BF16),
        ],
        compiler_params=_cparams("parallel", "parallel"),
        name="inproj",
    )(x, g, mod_l, w_in, cos, sin)


def _row_pow(lg, expo_fn, rows, cols):
    i = lax.broadcasted_iota(jnp.int32, (rows, cols), 0).astype(F32)
    return jnp.exp(lg * expo_fn(i))


def _ctxstate_kernel(lg_ref, c_ref, g_ref, mod_ref, w_ref, cos_ref, sin_ref, sf_ref, sb_ref, k_scr):
    l = c_ref.shape[0]
    u = (_rms(c_ref[...], g_ref[...]) * (1.0 + mod_ref[1:2, :]) + mod_ref[0:1, :]).astype(BF16)
    cos, sin = cos_ref[...], sin_ref[...]
    for h in range(HEADS):
        _rope_store(_dot(u, w_ref[:, h * DK:(h + 1) * DK]), cos, sin, 1.0, k_scr, h * DK)
    for h in range(HEADS):
        kh = k_scr[:, h * DK:(h + 1) * DK]
        vh = _dot(u, w_ref[:, QK_TOT + h * DV:QK_TOT + (h + 1) * DV]).astype(BF16)
        wf = _row_pow(lg_ref[0, h], lambda i: (l - 1.0) - i, l, DK)
        wb = _row_pow(lg_ref[1, h], lambda i: i, l, DK)
        sf_ref[h] = _dot((kh * wf).T.astype(BF16), vh)
        sb_ref[h] = _dot((kh * wb).T.astype(BF16), vh)


def _ctxstate_call(log_g, ctx, g, mod_row, w_kv, cos, sin):
    b, l, _ = ctx.shape
    st = jax.ShapeDtypeStruct((b, HEADS, DK, DV), F32)
    st_spec = pl.BlockSpec((None, HEADS, DK, DV), lambda i: (i, 0, 0, 0))
    return pl.pallas_call(
        _ctxstate_kernel,
        grid=(b,),
        in_specs=[
            pl.BlockSpec(memory_space=pltpu.SMEM),
            pl.BlockSpec((None, l, D), lambda i: (i, 0, 0)),
            pl.BlockSpec((1, D), lambda i: (0, 0)),
            pl.BlockSpec((6, D), lambda i: (0, 0)),
            pl.BlockSpec((D, QK_TOT + V_TOT), lambda i: (0, 0)),
            pl.BlockSpec((l, DK // 2), lambda i: (0, 0)),
            pl.BlockSpec((l, DK // 2), lambda i: (0, 0)),
        ],
        out_specs=[st_spec, st_spec],
        out_shape=[st, st],
        scratch_shapes=[pltpu.VMEM((l, QK_TOT), F32)],
        compiler_params=_cparams("parallel"),
        name="ctxstate",
    )(log_g, ctx, g, mod_row, w_kv, cos, sin)


def _retention_kernel(lg_ref, q_ref, k_ref, v_ref, sf0_ref, sb0_ref, o_ref, acc, sf, sb):
    h = pl.program_id(1)
    t = q_ref.shape[0]
    c = RET_CHUNK
    nc = t // c
    lgf, lgb = lg_ref[0, h], lg_ref[1, h]
    ri = lax.broadcasted_iota(jnp.int32, (c, c), 0).astype(F32)
    ci = lax.broadcasted_iota(jnp.int32, (c, c), 1).astype(F32)
    rel = ri - ci
    mask = (jnp.where(rel >= 0, jnp.exp(lgf * jnp.maximum(rel, 0.0)), 0.0)
            + jnp.where(rel <= 0, jnp.exp(lgb * jnp.maximum(-rel, 0.0)), 0.0))
    qdf = _row_pow(lgf, lambda i: i + 1.0, c, DV)
    qdb = _row_pow(lgb, lambda i: c - i, c, DV)
    kdf = _row_pow(lgf, lambda i: (c - 1.0) - i, c, DK)
    kdb = _row_pow(lgb, lambda i: i, c, DK)
    cdf = jnp.exp(jnp.full((1, DV), lgf * c, F32))
    cdb = jnp.exp(jnp.full((1, DV), lgb * c, F32))
    sf[...] = sf0_ref[...]
    sb[...] = sb0_ref[...]

    def rows(ic):
        return pl.ds(pl.multiple_of(ic * c, c), c)

    def fwd(rs):
        q, k, v = q_ref[rs, :], k_ref[rs, :], v_ref[rs, :]
        s = lax.dot_general(q, k, (((1,), (1,)), ((), ())), preferred_element_type=F32)
        o = _dot((s * mask).astype(BF16), v) + qdf * _dot(q, sf[...].astype(BF16))
        sf[...] = sf[...] * cdf + _dot((k.astype(F32) * kdf).T.astype(BF16), v)
        return o

    def bwd(rs):
        q, k, v = q_ref[rs, :], k_ref[rs, :], v_ref[rs, :]
        o = qdb * _dot(q, sb[...].astype(BF16))
        sb[...] = sb[...] * cdb + _dot((k.astype(F32) * kdb).T.astype(BF16), v)
        return o

    def finish(rs, o):
        o_ref[rs, :] = (o * lax.rsqrt(jnp.mean(o * o, axis=-1, keepdims=True) + NORM_EPS)).astype(BF16)

    def first_half(j, carry):
        acc[rows(j), :] = fwd(rows(j))
        acc[rows(nc - 1 - j), :] = bwd(rows(nc - 1 - j))
        return carry

    def second_half(j, carry):
        finish(rows(j), acc[rows(j), :] + fwd(rows(j)))
        finish(rows(nc - 1 - j), acc[rows(nc - 1 - j), :] + bwd(rows(nc - 1 - j)))
        return carry

    lax.fori_loop(0, nc // 2, first_half, 0)
    lax.fori_loop(nc // 2, nc, second_half, 0)


def _retention_call(log_g, q, k, v, sf0, sb0):
    b, t, _ = q.shape
    qk_spec = pl.BlockSpec((None, t, DK), lambda i, h: (i, 0, h))
    v_spec = pl.BlockSpec((None, t, DV), lambda i, h: (i, 0, h))
    s_spec = pl.BlockSpec((None, None, DK, DV), lambda i, h: (i, h, 0, 0))
    return pl.pallas_call(
        _retention_kernel,
        grid=(b, HEADS),
        in_specs=[pl.BlockSpec(memory_space=pltpu.SMEM), qk_spec, qk_spec, v_spec, s_spec, s_spec],
        out_specs=v_spec,
        out_shape=jax.ShapeDtypeStruct((b, t, V_TOT), BF16),
        scratch_shapes=[pltpu.VMEM((t, DV), F32), pltpu.VMEM((DK, DV), F32), pltpu.VMEM((DK, DV), F32)],
        compiler_params=_cparams("parallel", "parallel"),
        name="retention",
    )(log_g, q, k, v, sf0, sb0)


def _mixer_epilogue(y, h_ref, n_ref, mod_ref, wr_ref, h_out, u_out, lg_out):
    h1 = h_ref[...] + mod_ref[2:3, :] * _rms(y, n_ref[1:2, :])
    h_out[...] = h1
    u2 = _rms(h1, n_ref[2:3, :]) * (1.0 + mod_ref[4:5, :]) + mod_ref[3:4, :]
    u_out[...] = u2.astype(BF16)
    lg_out[...] = lax.dot_general(wr_ref[...], u2, (((1,), (1,)), ((), ())),
                                  preferred_element_type=F32, precision=_HI)


def _readout_kernel(y_ref, gt_ref, h_ref, n_ref, mod_ref, w_ref, wr_ref, h_out, u_out, lg_out):
    a = (y_ref[...].astype(F32) * _silu(gt_ref[...].astype(F32))).astype(BF16)
    _mixer_epilogue(_dot(a, w_ref[...]), h_ref, n_ref, mod_ref, wr_ref, h_out, u_out, lg_out)


def _epilogue_specs(b, t, tm):
    nt = t // tm
    tok = lambda n: pl.BlockSpec((None, tm, n), lambda i, j: (i, j, 0))
    in_specs = [
        tok(D),
        pl.BlockSpec((4, D), lambda i, j: (0, 0)),
        pl.BlockSpec((None, 6, D), lambda i, j: (i, 0, 0)),
    ]
    wr_spec = pl.BlockSpec((N_EXP, D), lambda i, j: (0, 0))
    out_specs = [tok(D), tok(D), pl.BlockSpec((N_EXP, tm), lambda i, j: (0, i * nt + j))]
    out_shape = [
        jax.ShapeDtypeStruct((b, t, D), F32),
        jax.ShapeDtypeStruct((b, t, D), BF16),
        jax.ShapeDtypeStruct((N_EXP, b * t), F32),
    ]
    return tok, in_specs, wr_spec, out_specs, out_shape


def _readout_call(yn, gt, h, norms_l, mod_l, w_out, wr_t, tm):
    b, t, _ = h.shape
    tok, ep_in, wr_spec, out_specs, out_shape = _epilogue_specs(b, t, tm)
    return pl.pallas_call(
        _readout_kernel,
        grid=(b, t // tm),
        in_specs=[tok(V_TOT), tok(V_TOT)] + ep_in
        + [pl.BlockSpec((V_TOT, D), lambda i, j: (0, 0)), wr_spec],
        out_specs=out_specs,
        out_shape=out_shape,
        compiler_params=_cparams("parallel", "parallel"),
        name="readout",
    )(yn, gt, h, norms_l, mod_l, w_out, wr_t)


def _route_kernel(lg_ref, b_ref, code_ref, code_t_ref, gate_t_ref, segc_ref, segr_ref, pc_ref, off_ref):
    i = pl.program_id(0)
    tr = lg_ref.shape[1]
    s = jax.nn.sigmoid(lg_ref[...])
    biased = s + b_ref[...]
    neg = -jnp.inf
    b3 = biased.reshape(N_GRP, PER_GRP, tr)
    io3 = lax.broadcasted_iota(jnp.int32, b3.shape, 1).astype(F32)
    m1 = jnp.max(b3, axis=1, keepdims=True)
    i1 = jnp.min(jnp.where(b3 == m1, io3, float(PER_GRP)), axis=1, keepdims=True)
    m2 = jnp.max(jnp.where(io3 == i1, neg, b3), axis=1, keepdims=True)
    gs = (m1 + m2).reshape(N_GRP, tr)
    iog = lax.broadcasted_iota(jnp.int32, gs.shape, 0).astype(F32)
    gsel = jnp.zeros(gs.shape, F32)
    work = gs
    for _ in range(TOP_G):
        m = jnp.max(work, axis=0, keepdims=True)
        gi = jnp.min(jnp.where(work == m, iog, float(N_GRP)), axis=0, keepdims=True)
        hit = iog == gi
        gsel = jnp.where(hit, 1.0, gsel)
        work = jnp.where(hit, neg, work)
    emask = jnp.broadcast_to(gsel.reshape(N_GRP, 1, tr), b3.shape).reshape(N_EXP, tr)
    work = jnp.where(emask > 0.0, biased, neg)
    ioe = lax.broadcasted_iota(jnp.int32, work.shape, 0).astype(F32)
    esel = jnp.zeros(work.shape, F32)
    for _ in range(TOP_K):
        m = jnp.max(work, axis=0, keepdims=True)
        ei = jnp.min(jnp.where(work == m, ioe, float(N_EXP)), axis=0, keepdims=True)
        hit = ioe == ei
        esel = jnp.where(hit, 1.0, esel)
        work = jnp.where(hit, neg, work)
    wsel = esel * s
    gate = wsel / jnp.sum(wsel, axis=0, keepdims=True) * ROUTE_SCALE

    t_r = lax.broadcasted_iota(jnp.int32, (tr, tr), 0)
    t_c = lax.broadcasted_iota(jnp.int32, (tr, tr), 1)
    rank = _dot(esel.astype(BF16), jnp.where(t_r < t_c, 1.0, 0.0).astype(BF16))
    cnt = jnp.sum(esel, axis=1, keepdims=True)
    pc = jnp.floor((cnt + (SEG_ALIGN - 1.0)) * (1.0 / SEG_ALIGN)) * SEG_ALIGN
    e_r = lax.broadcasted_iota(jnp.int32, (N_EXP, N_EXP), 0)
    e_c = lax.broadcasted_iota(jnp.int32, (N_EXP, N_EXP), 1)
    off = jnp.dot(jnp.where(e_c < e_r, 1.0, 0.0), jnp.broadcast_to(pc, (N_EXP, 128)),
                  preferred_element_type=F32, precision=_HI)[:, 0:1]
    p = off + rank
    p_hi = jnp.where(esel > 0.0, jnp.floor(p * (1.0 / POS_RADIX)) * POS_RADIX, -float(POS_RADIX))
    p_lo = jnp.where(esel > 0.0, p - p_hi, 0.0)
    minus_r = jnp.where(lax.broadcasted_iota(jnp.int32, (2 * N_EXP, tr), 0) < 2, -1.0, 0.0)
    code = jnp.concatenate([p_hi, p_lo, minus_r], 0)
    code_ref[...] = code.astype(BF16)
    code_t_ref[...] = jnp.concatenate([code[:2 * N_EXP].T, code[2 * N_EXP:].T], 1).astype(BF16)
    gate_t_ref[...] = jnp.concatenate([gate, jnp.zeros((GATE_LANES - N_EXP, tr), F32)], 0).T
    lane = lax.broadcasted_iota(jnp.int32, (2 * N_EXP, GATE_LANES), 1)
    off2 = jnp.concatenate([off, off], 0)
    end2 = jnp.concatenate([off + pc, off + pc], 0)
    bounds = jnp.where(lane == 0, off2, jnp.where(lane == 1, end2, 0.0))
    segc_ref[...] = bounds
    segr_ref[...] = bounds.T[0:SUBLANES, :]

    @pl.when(i == 0)
    def _():
        pc_ref[...] = jnp.zeros(pc_ref.shape, F32)
        off_ref[...] = jnp.zeros(off_ref.shape, F32)

    tile_lane = lax.broadcasted_iota(jnp.int32, pc_ref.shape, 1)
    pc_ref[...] = jnp.where(tile_lane == i, pc, pc_ref[...])
    off_ref[...] = jnp.where(tile_lane == i, off, off_ref[...])


def _route_call(logits_t, bias):
    n = logits_t.shape[1]
    nt = n // TM_MOE
    return pl.pallas_call(
        _route_kernel,
        grid=(nt,),
        in_specs=[pl.BlockSpec((N_EXP, TM_MOE), lambda i: (0, i)), pl.BlockSpec((N_EXP, 1), lambda i: (0, 0))],
        out_specs=[
            pl.BlockSpec((4 * N_EXP, TM_MOE), lambda i: (0, i)),
            pl.BlockSpec((TM_MOE, 4 * N_EXP), lambda i: (i, 0)),
            pl.BlockSpec((TM_MOE, GATE_LANES), lambda i: (i, 0)),
            pl.BlockSpec((None, 2 * N_EXP, GATE_LANES), lambda i: (i, 0, 0)),
            pl.BlockSpec((None, SUBLANES, GATE_LANES), lambda i: (i, 0, 0)),
            pl.BlockSpec((N_EXP, nt), lambda i: (0, 0)),
            pl.BlockSpec((N_EXP, nt), lambda i: (0, 0)),
        ],
        out_shape=[
            jax.ShapeDtypeStruct((4 * N_EXP, n), BF16),
            jax.ShapeDtypeStruct((n, 4 * N_EXP), BF16),
            jax.ShapeDtypeStruct((n, GATE_LANES), F32),
            jax.ShapeDtypeStruct((nt, 2 * N_EXP, GATE_LANES), F32),
            jax.ShapeDtypeStruct((nt, SUBLANES, GATE_LANES), F32),
            jax.ShapeDtypeStruct((N_EXP, nt), F32),
            jax.ShapeDtypeStruct((N_EXP, nt), F32),
        ],
        compiler_params=_cparams("arbitrary"),
        name="route",
    )(logits_t, bias.reshape(N_EXP, 1))


def _dispatch_tables(pc_t, off_t, nb_max):
    pc = pc_t.astype(jnp.int32)
    used = pc.sum(1)
    region = (used + BM - 1) // BM * BM
    ends = jnp.cumsum(region)
    starts = ends - region
    dst = starts[:, None] + jnp.cumsum(pc, 1) - pc
    nblk = ends[-1] // BM
    blk = jnp.minimum(jnp.arange(nb_max, dtype=jnp.int32), nblk - 1)
    blk_e = jnp.minimum(jnp.sum(ends[None, :] <= blk[:, None] * BM, axis=1), N_EXP - 1).astype(jnp.int32)
    flat = lambda a: a.T.reshape(-1).astype(jnp.int32)
    tail = ((starts + used).astype(jnp.int32), (region - used).astype(jnp.int32))
    nt = pc.shape[1]
    off = off_t.astype(jnp.int32)
    e_idx = jnp.arange(N_EXP, dtype=jnp.int32)[:, None]
    col = lambda v: jnp.broadcast_to(jnp.asarray(v, jnp.int32), (N_EXP, 1))
    table = lambda so, do, ln: (flat(so), flat(do), flat(ln), ln.sum(0).astype(jnp.int32))
    pc_c = jnp.concatenate([pc, col(0)], 1)
    spare_c = _sorted_rows(TM_MOE) + SEG_ALIGN * e_idx
    seg_comb = table(jnp.where(pc_c > 0, jnp.concatenate([off, col(0)], 1), spare_c),
                     jnp.where(pc_c > 0, jnp.concatenate([dst, col(0)], 1), 0), jnp.maximum(pc_c, SEG_ALIGN))
    pc_d = jnp.concatenate([col(0), pc], 1)
    entry = jnp.arange(nt + 1, dtype=jnp.int32)[None, :]
    spare_d = nb_max * BM + ((entry % DISPATCH_SLOTS) * N_EXP + e_idx) * SEG_ALIGN
    seg_disp = table(jnp.where(pc_d > 0, jnp.concatenate([col(0), off], 1), 0),
                     jnp.where(pc_d > 0, jnp.concatenate([col(0), dst], 1), spare_d), jnp.maximum(pc_d, SEG_ALIGN))
    return seg_disp, seg_comb, tail, blk_e, nblk.reshape(1).astype(jnp.int32)


def _swiglu(u, w1, w3, w2):
    return _dot((_silu(_dot(u, w1)) * _dot(u, w3)).astype(BF16), w2)


def _round_bf16(x):
    return x.astype(BF16).astype(F32)


def _pack_pair(lo, hi):
    lo_b = lax.bitcast_convert_type(lo, jnp.uint32)
    hi_b = lax.bitcast_convert_type(hi, jnp.uint32)
    return (hi_b & jnp.uint32(0xFFFF0000)) | (lo_b >> 16)


def _unpack_pair(w):
    lo = lax.bitcast_convert_type(w << 16, F32).astype(BF16)
    hi = lax.bitcast_convert_type(w & jnp.uint32(0xFFFF0000), F32).astype(BF16)
    return lo, hi


def _segment_copies_dense(so_ref, do_ref, ln_ref, tile, make_copy):
    for e in range(N_EXP):
        j = tile * N_EXP + e
        make_copy(pl.multiple_of(so_ref[j], SEG_ALIGN), pl.multiple_of(do_ref[j], SEG_ALIGN),
                  pl.multiple_of(ln_ref[j], SEG_ALIGN)).start()


def _dispatch_kernel(nt, so_ref, do_ref, ln_ref, tt_ref, td_ref, tl_ref, nb_ref, u_ref, code_ref, segr_ref,
                     digits_ref, xs_out, buf, zbuf, sem):
    i = pl.program_id(0)
    r = buf.shape[1]
    nb_max = (xs_out.shape[0] - DISPATCH_SLOTS * N_EXP * SEG_ALIGN) // BM
    fill_sem = DISPATCH_SLOTS
    slot = i % DISPATCH_SLOTS
    prev_slot = (i + DISPATCH_SLOTS - 1) % DISPATCH_SLOTS

    def strip(slot_, so, do, ln):
        return pltpu.make_async_copy(buf.at[slot_, pl.ds(so, ln)], xs_out.at[pl.ds(do, ln)], sem.at[slot_])

    def wait_entry(slot_, entry):
        strip(slot_, 0, 0, pl.multiple_of(tt_ref[entry], SEG_ALIGN)).wait()

    def zero_fills(act):
        def tail(e, carry):
            ln = pl.multiple_of(tl_ref[e], SEG_ALIGN)

            @pl.when(ln > 0)
            def _():
                act(pltpu.make_async_copy(zbuf.at[pl.ds(0, ln)],
                                          xs_out.at[pl.ds(pl.multiple_of(td_ref[e], SEG_ALIGN), ln)], sem.at[fill_sem]))

            return carry

        lax.fori_loop(0, N_EXP, tail, 0)

        def blk(j, carry):
            act(pltpu.make_async_copy(zbuf.at[pl.ds(0, BM)], xs_out.at[pl.ds(pl.multiple_of(j * BM, BM), BM)],
                                      sem.at[fill_sem]))
            return carry

        lax.fori_loop(nb_ref[0], nb_max, blk, 0)

    @pl.when(i == 0)
    def _():
        zbuf[...] = jnp.zeros(zbuf.shape, jnp.uint32)
        buf[...] = jnp.zeros(buf.shape, jnp.uint32)
        spare = pltpu.make_async_copy(zbuf.at[pl.ds(0, DISPATCH_SLOTS * N_EXP * SEG_ALIGN)],
                                      xs_out.at[pl.ds(nb_max * BM, DISPATCH_SLOTS * N_EXP * SEG_ALIGN)],
                                      sem.at[fill_sem])
        spare.start()
        spare.wait()
        zero_fills(lambda cp: cp.start())

    @pl.when(i >= DISPATCH_SLOTS - 1)
    def _():
        wait_entry(slot, i - (DISPATCH_SLOTS - 1))

    row = lax.broadcasted_iota(jnp.int32, (r, GATE_LANES), 0).astype(F32)
    own = jnp.where(row >= segr_ref[0:1, :], jnp.where(row < segr_ref[1:2, :], 1.0, 0.0), 0.0).astype(BF16)
    z = _dot(jnp.concatenate([own, digits_ref[...]], axis=1), code_ref[...])
    perm = jnp.where(z == 0.0, 1.0, 0.0).astype(BF16)
    step = 256
    for c in range(0, PK, step):
        buf[slot, :, c:c + step] = _pack_pair(_dot(perm, u_ref[:, c:c + step]),
                                              _dot(perm, u_ref[:, PK + c:PK + c + step]))
    _segment_copies_dense(so_ref, do_ref, ln_ref, i, functools.partial(strip, prev_slot))

    @pl.when(i == nt - 1)
    def _():
        _segment_copies_dense(so_ref, do_ref, ln_ref, nt, functools.partial(strip, slot))
        for entry in range(max(nt - DISPATCH_SLOTS + 1, 0), nt + 1):
            wait_entry((entry + DISPATCH_SLOTS - 1) % DISPATCH_SLOTS, entry)
        zero_fills(lambda cp: cp.wait())


def _expert_kernel(be_ref, nb_ref, x_ref, w1_ref, w3_ref, w2_ref, y_ref, w1b, w3b, w2b):
    i = pl.program_id(0)
    live = i < nb_ref[0]

    @pl.when(live & ((i == 0) | (be_ref[i] != be_ref[jnp.maximum(i - 1, 0)])))
    def _():
        w1b[...] = w1_ref[...].astype(BF16)
        w3b[...] = w3_ref[...].astype(BF16)
        w2b[...] = w2_ref[...].astype(BF16)

    @pl.when(live)
    def _():
        up = []
        for r0 in range(0, BM, EXP_SUB):
            lo, hi = _unpack_pair(x_ref[r0:r0 + EXP_SUB, :])
            up.append((_dot(lo, w1b[:PK, :]) + _dot(hi, w1b[PK:, :]), _dot(lo, w3b[:PK, :]) + _dot(hi, w3b[PK:, :])))
        for j, (a, g) in enumerate(up):
            hid = (_silu(a) * g).astype(BF16)
            y_ref[j * EXP_SUB:(j + 1) * EXP_SUB, :] = _pack_pair(_round_bf16(_dot(hid, w2b[:, :PK])),
                                                                 _round_bf16(_dot(hid, w2b[:, PK:])))

    @pl.when(jnp.logical_not(live))
    def _():
        y_ref[...] = jnp.zeros(y_ref.shape, jnp.uint32)


def _combine_kernel(so_ref, do_ref, ln_ref, tt_ref, code_t_ref, gate_t_ref, segc_ref, digits_ref, u_ref, ys_ref,
                    s1_ref, s3_ref, s2_ref, h_ref, n_ref, mod_ref, o_ref, buf, sem):
    i = pl.program_id(0)
    nt = pl.num_programs(0)
    slot = i % 2
    r = digits_ref.shape[1]

    def strip(slot_, so, do, ln):
        return pltpu.make_async_copy(ys_ref.at[pl.ds(do, ln)], buf.at[slot_, pl.ds(so, ln)], sem.at[slot_])

    @pl.when(i == 0)
    def _():
        buf[...] = jnp.zeros(buf.shape, jnp.uint32)
        _segment_copies_dense(so_ref, do_ref, ln_ref, 0, functools.partial(strip, 0))

    col = lax.broadcasted_iota(jnp.int32, (2 * N_EXP, r), 1).astype(F32)
    own = jnp.where(col >= segc_ref[:, 0:1], jnp.where(col < segc_ref[:, 1:2], 1.0, 0.0), 0.0).astype(BF16)
    z = _dot(code_t_ref[...], jnp.concatenate([own, digits_ref[...]], axis=0))
    q = jnp.where(z == 0.0, _dot(gate_t_ref[...].astype(BF16), own), 0.0).astype(BF16)
    shared = _swiglu(u_ref[...], s1_ref[...], s3_ref[...], s2_ref[...])
    _segment_copies_dense(so_ref, do_ref, ln_ref, i + 1, functools.partial(strip, 1 - slot))

    strip(slot, 0, 0, pl.multiple_of(tt_ref[i], SEG_ALIGN)).wait()
    lo, hi = _unpack_pair(buf[slot, 0:r, :])
    moe = jnp.concatenate([_dot(q, lo), _dot(q, hi)], axis=1) + shared
    o_ref[...] = h_ref[...] + mod_ref[5:6, :] * _rms(moe, n_ref[3:4, :])

    @pl.when(i == nt - 1)
    def _():
        strip(1 - slot, 0, 0, pl.multiple_of(tt_ref[nt], SEG_ALIGN)).wait()


def _sorted_rows(tm):
    return -(-(TOP_K * tm + N_EXP * (SEG_ALIGN - 1)) // MXU_DIM) * MXU_DIM


def _max_blocks(n):
    nt = n // TM_MOE
    rows = TOP_K * n + N_EXP * (SEG_ALIGN - 1) * nt + N_EXP * (BM - SEG_ALIGN)
    return -(-rows // BM)


def _row_digits(r):
    idx = np.arange(r)
    digits = np.zeros((r, GATE_LANES), np.float32)
    digits[:, 0] = idx // POS_RADIX * POS_RADIX
    digits[:, 1] = idx % POS_RADIX
    return jnp.asarray(digits, BF16)


def _dispatch_call(seg, tail, nblk, u2, code, segr, nb_max):
    n = u2.shape[0]
    r = _sorted_rows(TM_MOE)
    spare = DISPATCH_SLOTS * N_EXP * SEG_ALIGN
    return pl.pallas_call(
        functools.partial(_dispatch_kernel, n // TM_MOE),
        grid_spec=pltpu.PrefetchScalarGridSpec(
            num_scalar_prefetch=7,
            grid=(n // TM_MOE,),
            in_specs=[
                pl.BlockSpec((TM_MOE, D), lambda i, *_: (i, 0)),
                pl.BlockSpec((4 * N_EXP, TM_MOE), lambda i, *_: (0, i)),
                pl.BlockSpec((None, SUBLANES, GATE_LANES), lambda i, *_: (i, 0, 0)),
                pl.BlockSpec((r, GATE_LANES), lambda i, *_: (0, 0)),
            ],
            out_specs=pl.BlockSpec(memory_space=pl.ANY),
            scratch_shapes=[pltpu.VMEM((DISPATCH_SLOTS, r, PK), jnp.uint32),
                            pltpu.VMEM((max(BM, spare), PK), jnp.uint32),
                            pltpu.SemaphoreType.DMA((DISPATCH_SLOTS + 1,))],
        ),
        out_shape=jax.ShapeDtypeStruct((nb_max * BM + spare, PK), jnp.uint32),
        compiler_params=_cparams("arbitrary"),
        name="dispatch",
    )(*seg, tail[0], tail[1], nblk, u2, code, segr, _row_digits(r))


def _expert_call(layer, blk_e, nblk, xs, w1, w3, w2, nb_max):
    w_blk = lambda i, be, nb: (layer, be[i], 0, 0)
    return pl.pallas_call(
        _expert_kernel,
        grid_spec=pltpu.PrefetchScalarGridSpec(
            num_scalar_prefetch=2,
            grid=(nb_max,),
            in_specs=[
                pl.BlockSpec((BM, PK), lambda i, be, nb: (jnp.maximum(jnp.minimum(i, nb[0] - 1), 0), 0)),
                pl.BlockSpec((None, None, D, F_EXP), w_blk),
                pl.BlockSpec((None, None, D, F_EXP), w_blk),
                pl.BlockSpec((None, None, F_EXP, D), w_blk),
            ],
            out_specs=pl.BlockSpec((BM, PK), lambda i, be, nb: (i, 0)),
            scratch_shapes=[pltpu.VMEM((D, F_EXP), BF16), pltpu.VMEM((D, F_EXP), BF16),
                            pltpu.VMEM((F_EXP, D), BF16)],
        ),
        out_shape=jax.ShapeDtypeStruct((nb_max * BM, PK), jnp.uint32),
        compiler_params=_cparams("arbitrary"),
        name="experts",
    )(blk_e, nblk, xs, w1, w3, w2)


def _combine_call(tabs, code_t, gate_t, segc, u2, ys, s1, s3, s2, h, norms_l, mod_l, t):
    n = u2.shape[0]
    r = _sorted_rows(TM_MOE)
    tok = lambda w: pl.BlockSpec((TM_MOE, w), lambda i, *_: (i, 0))
    const = lambda shp: pl.BlockSpec(shp, lambda i, *_: (0,) * len(shp))
    return pl.pallas_call(
        _combine_kernel,
        grid_spec=pltpu.PrefetchScalarGridSpec(
            num_scalar_prefetch=4,
            grid=(n // TM_MOE,),
            in_specs=[
                tok(4 * N_EXP),
                tok(GATE_LANES),
                pl.BlockSpec((None, 2 * N_EXP, GATE_LANES), lambda i, *_: (i, 0, 0)),
                const((2 * N_EXP, r)),
                tok(D),
                pl.BlockSpec(memory_space=pl.ANY),
                const((D, F_EXP)), const((D, F_EXP)), const((F_EXP, D)),
                tok(D),
                const((4, D)),
                pl.BlockSpec((None, 6, D), lambda i, *_: (i * TM_MOE // t, 0, 0)),
            ],
            out_specs=tok(D),
            scratch_shapes=[pltpu.VMEM((2, r + N_EXP * SEG_ALIGN, PK), jnp.uint32),
                            pltpu.SemaphoreType.DMA((2,))],
        ),
        out_shape=jax.ShapeDtypeStruct((n, D), F32),
        compiler_params=_cparams("arbitrary"),
        name="combine",
    )(*tabs, code_t, gate_t, segc, _row_digits(r).T, u2, ys, s1, s3, s2, h, norms_l, mod_l)


def _glu_kernel(h_ref, g_ref, mod_ref, w_ref, b_ref, o_ref):
    u = (_rms(h_ref[...], g_ref[...]) * (1.0 + mod_ref[1:2, :]) + mod_ref[0:1, :]).astype(BF16)
    a = _dot(u, w_ref[:, :D]) + b_ref[:, :D]
    gate = _dot(u, w_ref[:, D:]) + b_ref[:, D:]
    o_ref[...] = (a * jax.nn.sigmoid(gate)).astype(BF16)


def _glu_call(h, g, mod_l, w1, b1, tm):
    b, t, _ = h.shape
    tok = lambda n: pl.BlockSpec((None, tm, n), lambda i, j: (i, j, 0))
    return pl.pallas_call(
        _glu_kernel,
        grid=(b, t // tm),
        in_specs=[
            tok(D),
            pl.BlockSpec((1, D), lambda i, j: (0, 0)),
            pl.BlockSpec((None, 6, D), lambda i, j: (i, 0, 0)),
            pl.BlockSpec((D, 2 * D), lambda i, j: (0, 0)),
            pl.BlockSpec((1, 2 * D), lambda i, j: (0, 0)),
        ],
        out_specs=tok(D),
        out_shape=jax.ShapeDtypeStruct((b, t, D), BF16),
        compiler_params=_cparams("parallel", "parallel"),
        name="glu",
    )(h, g, mod_l, w1, b1)


def _conv_kernel(cur_ref, prev_ref, next_ref, dw_ref, cv_ref, w2_ref, h_ref, n_ref, mod_ref, wr_ref,
                 h_out, u_out, lg_out, win, shifted, conv):
    j = pl.program_id(1)
    tm = cur_ref.shape[0]
    pad = CONV_W // 2
    win[0:HALO, :] = jnp.where(j > 0, prev_ref[...].astype(F32), 0.0)
    win[HALO:HALO + tm, :] = cur_ref[...].astype(F32)
    win[HALO + tm:, :] = jnp.where(j < pl.num_programs(1) - 1, next_ref[...].astype(F32), 0.0)
    ext = shifted.shape[1]
    full = win[...]
    shifted[0] = full[:ext, :]
    for s in range(1, SUBLANES):
        shifted[s] = pltpu.roll(full, full.shape[0] - s, axis=0)[:ext, :]

    def rows(ib, carry):
        r0 = pl.multiple_of(ib * 2 * SUBLANES, 2 * SUBLANES)
        for c0 in range(0, D, CONV_LANES):
            cols = slice(c0, c0 + CONV_LANES)
            acc0 = jnp.zeros((SUBLANES, CONV_LANES), F32)
            acc1 = jnp.zeros((SUBLANES, CONV_LANES), F32)
            for s in range(SUBLANES):
                taps = [(off // SUBLANES, off - (HALO - pad)) for off in range(HALO - pad, HALO - pad + CONV_W)
                        if off % SUBLANES == s]
                tiles = {a: shifted[s, pl.ds(r0 + a * SUBLANES, SUBLANES), cols]
                         for a in range(taps[0][0], taps[-1][0] + 2)}
                for a, tap in taps:
                    w = dw_ref[tap * SUBLANES:(tap + 1) * SUBLANES, cols]
                    acc0 = acc0 + tiles[a] * w
                    acc1 = acc1 + tiles[a + 1] * w
            conv[pl.ds(r0, SUBLANES), cols] = acc0
            conv[pl.ds(r0 + SUBLANES, SUBLANES), cols] = acc1
        return carry

    lax.fori_loop(0, tm // (2 * SUBLANES), rows, 0)
    uf = conv[...] + cv_ref[0:1, :]
    mu = jnp.mean(uf, axis=-1, keepdims=True)
    var = jnp.mean(jnp.square(uf - mu), axis=-1, keepdims=True)
    uf = (uf - mu) * lax.rsqrt(var + LN_EPS) * cv_ref[1:2, :] + cv_ref[2:3, :]
    y = _dot(_silu(uf).astype(BF16), w2_ref[...]) + cv_ref[3:4, :]
    _mixer_epilogue(y, h_ref, n_ref, mod_ref, wr_ref, h_out, u_out, lg_out)


def _conv_call(glu, dw, cvec, w2, h, norms_l, mod_l, wr_t, tm):
    b, t, _ = h.shape
    tok, ep_in, wr_spec, out_specs, out_shape = _epilogue_specs(b, t, tm)
    r = tm // HALO
    last = t // HALO - 1
    return pl.pallas_call(
        _conv_kernel,
        grid=(b, t // tm),
        in_specs=[
            tok(D),
            pl.BlockSpec((None, HALO, D), lambda i, j: (i, jnp.maximum(j * r - 1, 0), 0)),
            pl.BlockSpec((None, HALO, D), lambda i, j: (i, jnp.minimum((j + 1) * r, last), 0)),
            pl.BlockSpec((CONV_W * SUBLANES, D), lambda i, j: (0, 0)),
            pl.BlockSpec((4, D), lambda i, j: (0, 0)),
            pl.BlockSpec((D, D), lambda i, j: (0, 0)),
        ] + ep_in + [wr_spec],
        out_specs=out_specs,
        out_shape=out_shape,
        scratch_shapes=[pltpu.VMEM((tm + 2 * HALO, D), F32),
                        pltpu.VMEM((SUBLANES, tm + (CONV_W // SUBLANES) * SUBLANES, D), F32),
                        pltpu.VMEM((tm, D), F32)],
        compiler_params=_cparams("parallel", "parallel"),
        name="conv",
    )(glu, glu, glu, dw, cvec, w2, h, norms_l, mod_l, wr_t)


def _tile(t, pref):
    return pref if t % pref == 0 else t


def kernel(x, c, ctx, c_ctx, mod_w, mod_b, norms, ret_w_in, ret_w_out, ret_decay, conv_w1, conv_b1, conv_dw,
           conv_b_dw, conv_ln_g, conv_ln_b, conv_w2, conv_b2, moe_router, moe_bias, moe_w1, moe_w3, moe_w2,
           shared_w1, shared_w3, shared_w2):
    b, t, _ = x.shape
    l = ctx.shape[1]
    tm = _tile(t, 512)

    rows = -(-(b + 1) // 8) * 8
    cs = jnp.concatenate([c, c_ctx[None, :], jnp.zeros((rows - b - 1, D), F32)], 0)
    mod = _mod_call(cs, mod_w, mod_b).reshape(mod_w.shape[0], rows, 6, D)

    idx = jnp.arange(t, dtype=jnp.int32)
    pos_lat = jnp.stack([jnp.full((t,), l, jnp.int32), idx // GRID_W, idx % GRID_W], -1)
    zl = jnp.zeros((l,), jnp.int32)
    pos_ctx = jnp.stack([jnp.arange(l, dtype=jnp.int32), zl, zl], -1)
    ang_lat, ang_ctx = _rope_angles(pos_lat), _rope_angles(pos_ctx)

    head_perm = np.concatenate([np.arange(0, DK, 2), np.arange(1, DK, 2)])
    perm = np.concatenate([hh * DK + head_perm for hh in range(2 * HEADS)] + [np.arange(2 * QK_TOT, IN_DIM)])
    w_in = ret_w_in[0][:, perm].astype(BF16)
    log_g = jax.nn.log_sigmoid(ret_decay[0].astype(F32))

    n = b * t
    nb_max = _max_blocks(n)

    def moe_layer(i, h1, u2, logits_t):
        code, code_t, gate_t, segc, segr, pc_t, off_t = _route_call(logits_t, moe_bias[i])
        seg_disp, seg_comb, tail, blk_e, nblk = _dispatch_tables(pc_t, off_t, nb_max)
        u2f = u2.reshape(n, D)
        xs = _dispatch_call(seg_disp, tail, nblk, u2f, code, segr, nb_max)
        ys = _expert_call(i, blk_e, nblk, xs, moe_w1, moe_w3, moe_w2, nb_max)
        out = _combine_call(seg_comb, code_t, gate_t, segc, u2f, ys, shared_w1[i].astype(BF16), shared_w3[i].astype(BF16),
                            shared_w2[i].astype(BF16), h1.reshape(n, D), norms[i], mod[i], t)
        return out.reshape(b, t, D)

    q, k, v, gt = _inproj_call(x, norms[0, 0:1], mod[0], w_in, jnp.cos(ang_lat), jnp.sin(ang_lat), tm)
    sf0, sb0 = _ctxstate_call(log_g, ctx, norms[0, 0:1], mod[0, b], w_in[:, QK_TOT:2 * QK_TOT + V_TOT],
                              jnp.cos(ang_ctx), jnp.sin(ang_ctx))
    yn = _retention_call(log_g, q, k, v, sf0, sb0)
    h1, u2, logits_t = _readout_call(yn, gt, x, norms[0], mod[0], ret_w_out[0].astype(BF16),
                                     moe_router[0].T, tm)
    h2 = moe_layer(0, h1, u2, logits_t)

    glu = _glu_call(h2, norms[1, 0:1], mod[1], conv_w1[0].astype(BF16), conv_b1[0][None, :], tm)
    cvec = jnp.stack([conv_b_dw[0], conv_ln_g[0], conv_ln_b[0], conv_b2[0]], 0)
    dw_rows = jnp.repeat(conv_dw[0], SUBLANES, axis=0)
    h3, u2, logits_t = _conv_call(glu, dw_rows, cvec, conv_w2[0].astype(BF16), h2, norms[1], mod[1],
                                  moe_router[1].T, tm)
    return moe_layer(1, h3, u2, logits_t)
```

```python
import functools

import jax
import jax.numpy as jnp
import numpy as np
from jax import lax
from jax.experimental import pallas as pl
from jax.experimental.pallas import tpu as pltpu

F32 = jnp.float32
BF16 = jnp.bfloat16

D = 1024
HEADS = 4
DK = D // HEADS
DV = 2 * DK
QK_TOT = HEADS * DK
V_TOT = HEADS * DV
IN_DIM = 2 * QK_TOT + 2 * V_TOT
ROPE_AXES = (64, 96, 96)
ROPE_THETA = 10000.0
GRID_W = 64
CONV_W = 31
HALO = 16
SUBLANES = 8
CONV_LANES = 256
N_EXP = 64
N_GRP = 8
PER_GRP = N_EXP // N_GRP
TOP_K = 8
TOP_G = 4
F_EXP = 256
ROUTE_SCALE = 2.5
GATE_LANES = 128
MXU_DIM = 256
TM_MOE = 256
ROUTE_TILES = 4
SEG_ALIGN = 8
POS_RADIX = 256
BM = 1024
DISPATCH_SLOTS = 3
EXP_SUB = 256
PK = D // 2
NORM_EPS = 1e-6
LN_EPS = 1e-5
RET_CHUNK = 256
VMEM_LIMIT = 56 * 1024 * 1024

_HI = lax.Precision.HIGHEST


def _cparams(*sem):
    return pltpu.CompilerParams(dimension_semantics=sem, vmem_limit_bytes=VMEM_LIMIT)


def _dot(a, b):
    return jnp.dot(a, b, preferred_element_type=F32)


def _rms(xf, g):
    return xf * lax.rsqrt(jnp.mean(xf * xf, axis=-1, keepdims=True) + NORM_EPS) * g


def _silu(x):
    return x * jax.nn.sigmoid(x)


def _mod_kernel(cs_ref, w_ref, b_ref, o_ref):
    s = _silu(cs_ref[...])
    o_ref[...] = jnp.dot(s, w_ref[...], preferred_element_type=F32, precision=_HI) + b_ref[...]


def _mod_call(cs, mod_w, mod_b):
    depth, _, n6 = mod_w.shape
    tn = 1536
    rows = cs.shape[0]
    return pl.pallas_call(
        _mod_kernel,
        grid=(depth, n6 // tn),
        in_specs=[
            pl.BlockSpec((rows, D), lambda i, j: (0, 0)),
            pl.BlockSpec((None, D, tn), lambda i, j: (i, 0, j)),
            pl.BlockSpec((None, 1, tn), lambda i, j: (i, 0, j)),
        ],
        out_specs=pl.BlockSpec((None, rows, tn), lambda i, j: (i, 0, j)),
        out_shape=jax.ShapeDtypeStruct((depth, rows, n6), F32),
        compiler_params=_cparams("parallel", "parallel"),
        name="mod",
    )(cs, mod_w, mod_b.reshape(depth, 1, n6))


def _rope_angles(pos):
    parts = []
    for a, d in enumerate(ROPE_AXES):
        inv = ROPE_THETA ** (-jnp.arange(0, d, 2, dtype=F32) / d)
        parts.append(pos[:, a:a + 1].astype(F32) * inv[None, :])
    return jnp.concatenate(parts, -1)


def _rope_store(z, cos, sin, scale, o_ref, col):
    half = DK // 2
    x0, x1 = z[:, :half], z[:, half:]
    o_ref[:, col:col + half] = ((x0 * cos - x1 * sin) * scale).astype(o_ref.dtype)
    o_ref[:, col + half:col + DK] = ((x0 * sin + x1 * cos) * scale).astype(o_ref.dtype)


def _inproj_kernel(x_ref, g_ref, mod_ref, w_ref, cos_ref, sin_ref, q_ref, k_ref, v_ref, gt_ref):
    u = (_rms(x_ref[...], g_ref[...]) * (1.0 + mod_ref[1:2, :]) + mod_ref[0:1, :]).astype(BF16)
    cos, sin = cos_ref[...], sin_ref[...]
    for h in range(HEADS):
        _rope_store(_dot(u, w_ref[:, h * DK:(h + 1) * DK]), cos, sin, DK ** -0.5, q_ref, h * DK)
        _rope_store(_dot(u, w_ref[:, QK_TOT + h * DK:QK_TOT + (h + 1) * DK]), cos, sin, 1.0, k_ref, h * DK)
    step = 512
    for j in range(V_TOT // step):
        c0 = 2 * QK_TOT + j * step
        v_ref[:, j * step:(j + 1) * step] = _dot(u, w_ref[:, c0:c0 + step]).astype(BF16)
        c1 = 2 * QK_TOT + V_TOT + j * step
        gt_ref[:, j * step:(j + 1) * step] = _dot(u, w_ref[:, c1:c1 + step]).astype(BF16)


def _inproj_call(x, g, mod_l, w_in, cos, sin, tm):
    b, t, _ = x.shape
    grid = (b, t // tm)
    tok = lambda n: pl.BlockSpec((None, tm, n), lambda i, j: (i, j, 0))
    return pl.pallas_call(
        _inproj_kernel,
        grid=grid,
        in_specs=[
            tok(D),
            pl.BlockSpec((1, D), lambda i, j: (0, 0)),
            pl.BlockSpec((None, 6, D), lambda i, j: (i, 0, 0)),
            pl.BlockSpec((D, IN_DIM), lambda i, j: (0, 0), pipeline_mode=pl.Buffered(1)),
            pl.BlockSpec((tm, DK // 2), lambda i, j: (j, 0)),
            pl.BlockSpec((tm, DK // 2), lambda i, j: (j, 0)),
        ],
        out_specs=[tok(QK_TOT), tok(QK_TOT), tok(V_TOT), tok(V_TOT)],
        out_shape=[
            jax.ShapeDtypeStruct((b, t, QK_TOT), BF16),
            jax.ShapeDtypeStruct((b, t, QK_TOT), BF16),
            jax.ShapeDtypeStruct((b, t, V_TOT), BF16),
            jax.ShapeDtypeStruct((b, t, V_TOT), BF16),
        ],
        compiler_params=_cparams("parallel", "parallel"),
        name="inproj",
    )(x, g, mod_l, w_in, cos, sin)


def _row_pow(lg, expo_fn, rows, cols):
    i = lax.broadcasted_iota(jnp.int32, (rows, cols), 0).astype(F32)
    return jnp.exp(lg * expo_fn(i))


def _ctxstate_kernel(lg_ref, c_ref, g_ref, mod_ref, w_ref, cos_ref, sin_ref, sf_ref, sb_ref, k_scr):
    l = c_ref.shape[0]
    u = (_rms(c_ref[...], g_ref[...]) * (1.0 + mod_ref[1:2, :]) + mod_ref[0:1, :]).astype(BF16)
    cos, sin = cos_ref[...], sin_ref[...]
    for h in range(HEADS):
        _rope_store(_dot(u, w_ref[:, h * DK:(h + 1) * DK]), cos, sin, 1.0, k_scr, h * DK)
    for h in range(HEADS):
        kh = k_scr[:, h * DK:(h + 1) * DK]
        vh = _dot(u, w_ref[:, QK_TOT + h * DV:QK_TOT + (h + 1) * DV]).astype(BF16)
        wf = _row_pow(lg_ref[0, h], lambda i: (l - 1.0) - i, l, DK)
        wb = _row_pow(lg_ref[1, h], lambda i: i, l, DK)
        sf_ref[h] = _dot((kh * wf).T.astype(BF16), vh)
        sb_ref[h] = _dot((kh * wb).T.astype(BF16), vh)


def _ctxstate_call(log_g, ctx, g, mod_row, w_kv, cos, sin):
    b, l, _ = ctx.shape
    st = jax.ShapeDtypeStruct((b, HEADS, DK, DV), F32)
    st_spec = pl.BlockSpec((None, HEADS, DK, DV), lambda i: (i, 0, 0, 0))
    return pl.pallas_call(
        _ctxstate_kernel,
        grid=(b,),
        in_specs=[
            pl.BlockSpec(memory_space=pltpu.SMEM),
            pl.BlockSpec((None, l, D), lambda i: (i, 0, 0)),
            pl.BlockSpec((1, D), lambda i: (0, 0)),
            pl.BlockSpec((6, D), lambda i: (0, 0)),
            pl.BlockSpec((D, QK_TOT + V_TOT), lambda i: (0, 0)),
            pl.BlockSpec((l, DK // 2), lambda i: (0, 0)),
            pl.BlockSpec((l, DK // 2), lambda i: (0, 0)),
        ],
        out_specs=[st_spec, st_spec],
        out_shape=[st, st],
        scratch_shapes=[pltpu.VMEM((l, QK_TOT), F32)],
        compiler_params=_cparams("parallel"),
        name="ctxstate",
    )(log_g, ctx, g, mod_row, w_kv, cos, sin)


def _retention_kernel(lg_ref, q_ref, k_ref, v_ref, sf0_ref, sb0_ref, o_ref, acc, sf, sb):
    h = pl.program_id(1)
    t = q_ref.shape[0]
    c = RET_CHUNK
    nc = t // c
    lgf, lgb = lg_ref[0, h], lg_ref[1, h]
    ri = lax.broadcasted_iota(jnp.int32, (c, c), 0).astype(F32)
    ci = lax.broadcasted_iota(jnp.int32, (c, c), 1).astype(F32)
    rel = ri - ci
    mask = (jnp.where(rel >= 0, jnp.exp(lgf * jnp.maximum(rel, 0.0)), 0.0)
            + jnp.where(rel <= 0, jnp.exp(lgb * jnp.maximum(-rel, 0.0)), 0.0))
    qdf = _row_pow(lgf, lambda i: i + 1.0, c, DV)
    qdb = _row_pow(lgb, lambda i: c - i, c, DV)
    kdf = _row_pow(lgf, lambda i: (c - 1.0) - i, c, DK)
    kdb = _row_pow(lgb, lambda i: i, c, DK)
    cdf = jnp.exp(jnp.full((1, DV), lgf * c, F32))
    cdb = jnp.exp(jnp.full((1, DV), lgb * c, F32))
    sf[...] = sf0_ref[...]
    sb[...] = sb0_ref[...]

    def rows(ic):
        return pl.ds(pl.multiple_of(ic * c, c), c)

    def fwd(rs):
        q, k, v = q_ref[rs, :], k_ref[rs, :], v_ref[rs, :]
        s = lax.dot_general(q, k, (((1,), (1,)), ((), ())), preferred_element_type=F32)
        o = _dot((s * mask).astype(BF16), v) + qdf * _dot(q, sf[...].astype(BF16))
        sf[...] = sf[...] * cdf + _dot((k.astype(F32) * kdf).T.astype(BF16), v)
        return o

    def bwd(rs):
        q, k, v = q_ref[rs, :], k_ref[rs, :], v_ref[rs, :]
        o = qdb * _dot(q, sb[...].astype(BF16))
        sb[...] = sb[...] * cdb + _dot((k.astype(F32) * kdb).T.astype(BF16), v)
        return o

    def finish(rs, o):
        o_ref[rs, :] = (o * lax.rsqrt(jnp.mean(o * o, axis=-1, keepdims=True) + NORM_EPS)).astype(BF16)

    def first_half(j, carry):
        acc[rows(j), :] = fwd(rows(j))
        acc[rows(nc - 1 - j), :] = bwd(rows(nc - 1 - j))
        return carry

    def second_half(j, carry):
        finish(rows(j), acc[rows(j), :] + fwd(rows(j)))
        finish(rows(nc - 1 - j), acc[rows(nc - 1 - j), :] + bwd(rows(nc - 1 - j)))
        return carry

    lax.fori_loop(0, nc // 2, first_half, 0)
    lax.fori_loop(nc // 2, nc, second_half, 0)


def _retention_call(log_g, q, k, v, sf0, sb0):
    b, t, _ = q.shape
    qk_spec = pl.BlockSpec((None, t, DK), lambda i, h: (i, 0, h))
    v_spec = pl.BlockSpec((None, t, DV), lambda i, h: (i, 0, h))
    s_spec = pl.BlockSpec((None, None, DK, DV), lambda i, h: (i, h, 0, 0))
    return pl.pallas_call(
        _retention_kernel,
        grid=(b, HEADS),
        in_specs=[pl.BlockSpec(memory_space=pltpu.SMEM), qk_spec, qk_spec, v_spec, s_spec, s_spec],
        out_specs=v_spec,
        out_shape=jax.ShapeDtypeStruct((b, t, V_TOT), BF16),
        scratch_shapes=[pltpu.VMEM((t, DV), F32), pltpu.VMEM((DK, DV), F32), pltpu.VMEM((DK, DV), F32)],
        compiler_params=_cparams("parallel", "parallel"),
        name="retention",
    )(log_g, q, k, v, sf0, sb0)


def _mixer_epilogue(y, h_ref, n_ref, mod_ref, wr_ref, h_out, u_out, lg_out):
    h1 = h_ref[...] + mod_ref[2:3, :] * _rms(y, n_ref[1:2, :])
    h_out[...] = h1
    u2 = _rms(h1, n_ref[2:3, :]) * (1.0 + mod_ref[4:5, :]) + mod_ref[3:4, :]
    u_out[...] = u2.astype(BF16)
    lg_out[...] = lax.dot_general(wr_ref[...], u2, (((1,), (1,)), ((), ())),
                                  preferred_element_type=F32, precision=_HI)


def _readout_kernel(y_ref, gt_ref, h_ref, n_ref, mod_ref, w_ref, wr_ref, h_out, u_out, lg_out):
    a = (y_ref[...].astype(F32) * _silu(gt_ref[...].astype(F32))).astype(BF16)
    _mixer_epilogue(_dot(a, w_ref[...]), h_ref, n_ref, mod_ref, wr_ref, h_out, u_out, lg_out)


def _epilogue_specs(b, t, tm):
    nt = t // tm
    tok = lambda n: pl.BlockSpec((None, tm, n), lambda i, j: (i, j, 0))
    in_specs = [
        tok(D),
        pl.BlockSpec((4, D), lambda i, j: (0, 0)),
        pl.BlockSpec((None, 6, D), lambda i, j: (i, 0, 0)),
    ]
    wr_spec = pl.BlockSpec((N_EXP, D), lambda i, j: (0, 0))
    out_specs = [tok(D), tok(D), pl.BlockSpec((N_EXP, tm), lambda i, j: (0, i * nt + j))]
    out_shape = [
        jax.ShapeDtypeStruct((b, t, D), F32),
        jax.ShapeDtypeStruct((b, t, D), BF16),
        jax.ShapeDtypeStruct((N_EXP, b * t), F32),
    ]
    return tok, in_specs, wr_spec, out_specs, out_shape


def _readout_call(yn, gt, h, norms_l, mod_l, w_out, wr_t, tm):
    b, t, _ = h.shape
    tok, ep_in, wr_spec, out_specs, out_shape = _epilogue_specs(b, t, tm)
    return pl.pallas_call(
        _readout_kernel,
        grid=(b, t // tm),
        in_specs=[tok(V_TOT), tok(V_TOT)] + ep_in
        + [pl.BlockSpec((V_TOT, D), lambda i, j: (0, 0)), wr_spec],
        out_specs=out_specs,
        out_shape=out_shape,
        compiler_params=_cparams("parallel", "parallel"),
        name="readout",
    )(yn, gt, h, norms_l, mod_l, w_out, wr_t)


def _route_tile(logits, bias):
    tr = logits.shape[1]
    s = jax.nn.sigmoid(logits)
    biased = s + bias
    neg = -jnp.inf
    b3 = biased.reshape(N_GRP, PER_GRP, tr)
    io3 = lax.broadcasted_iota(jnp.int32, b3.shape, 1).astype(F32)
    m1 = jnp.max(b3, axis=1, keepdims=True)
    i1 = jnp.min(jnp.where(b3 == m1, io3, float(PER_GRP)), axis=1, keepdims=True)
    m2 = jnp.max(jnp.where(io3 == i1, neg, b3), axis=1, keepdims=True)
    gs = (m1 + m2).reshape(N_GRP, tr)
    iog = lax.broadcasted_iota(jnp.int32, gs.shape, 0).astype(F32)
    gsel = jnp.zeros(gs.shape, F32)
    work = gs
    for _ in range(TOP_G):
        m = jnp.max(work, axis=0, keepdims=True)
        gi = jnp.min(jnp.where(work == m, iog, float(N_GRP)), axis=0, keepdims=True)
        hit = iog == gi
        gsel = jnp.where(hit, 1.0, gsel)
        work = jnp.where(hit, neg, work)
    emask = jnp.broadcast_to(gsel.reshape(N_GRP, 1, tr), b3.shape).reshape(N_EXP, tr)
    work = jnp.where(emask > 0.0, biased, neg)
    ioe = lax.broadcasted_iota(jnp.int32, work.shape, 0).astype(F32)
    esel = jnp.zeros(work.shape, F32)
    for _ in range(TOP_K):
        m = jnp.max(work, axis=0, keepdims=True)
        ei = jnp.min(jnp.where(work == m, ioe, float(N_EXP)), axis=0, keepdims=True)
        hit = ioe == ei
        esel = jnp.where(hit, 1.0, esel)
        work = jnp.where(hit, neg, work)
    wsel = esel * s
    gate = wsel / jnp.sum(wsel, axis=0, keepdims=True) * ROUTE_SCALE

    t_r = lax.broadcasted_iota(jnp.int32, (tr, tr), 0)
    t_c = lax.broadcasted_iota(jnp.int32, (tr, tr), 1)
    rank = _dot(esel.astype(BF16), jnp.where(t_r < t_c, 1.0, 0.0).astype(BF16))
    cnt = jnp.sum(esel, axis=1, keepdims=True)
    pc = jnp.floor((cnt + (SEG_ALIGN - 1.0)) * (1.0 / SEG_ALIGN)) * SEG_ALIGN
    e_r = lax.broadcasted_iota(jnp.int32, (N_EXP, N_EXP), 0)
    e_c = lax.broadcasted_iota(jnp.int32, (N_EXP, N_EXP), 1)
    off = jnp.dot(jnp.where(e_c < e_r, 1.0, 0.0), jnp.broadcast_to(pc, (N_EXP, 128)),
                  preferred_element_type=F32, precision=_HI)[:, 0:1]
    p = off + rank
    p_hi = jnp.where(esel > 0.0, jnp.floor(p * (1.0 / POS_RADIX)) * POS_RADIX, -float(POS_RADIX))
    p_lo = jnp.where(esel > 0.0, p - p_hi, 0.0)
    minus_r = jnp.where(lax.broadcasted_iota(jnp.int32, (2 * N_EXP, tr), 0) < 2, -1.0, 0.0)
    code = jnp.concatenate([p_hi, p_lo, minus_r], 0)
    code_t = jnp.concatenate([code[:2 * N_EXP].T, code[2 * N_EXP:].T], 1)
    gate_t = jnp.concatenate([gate, jnp.zeros((GATE_LANES - N_EXP, tr), F32)], 0).T
    lane = lax.broadcasted_iota(jnp.int32, (2 * N_EXP, GATE_LANES), 1)
    off2 = jnp.concatenate([off, off], 0)
    end2 = jnp.concatenate([off + pc, off + pc], 0)
    bounds = jnp.where(lane == 0, off2, jnp.where(lane == 1, end2, 0.0))
    return code.astype(BF16), code_t.astype(BF16), gate_t, bounds, bounds.T[0:SUBLANES, :], pc, off


def _route_kernel(lg_ref, b_ref, code_ref, code_t_ref, gate_t_ref, segc_ref, segr_ref, pc_ref, off_ref):
    i = pl.program_id(0)

    @pl.when(i == 0)
    def _():
        pc_ref[...] = jnp.zeros(pc_ref.shape, F32)
        off_ref[...] = jnp.zeros(off_ref.shape, F32)

    tile_lane = lax.broadcasted_iota(jnp.int32, pc_ref.shape, 1)
    for k in range(ROUTE_TILES):
        cols = slice(k * TM_MOE, (k + 1) * TM_MOE)
        code, code_t, gate_t, bounds, bounds_t, pc, off = _route_tile(lg_ref[:, cols], b_ref[...])
        code_ref[:, cols] = code
        code_t_ref[cols, :] = code_t
        gate_t_ref[cols, :] = gate_t
        segc_ref[k] = bounds
        segr_ref[k] = bounds_t
        pc_ref[...] = jnp.where(tile_lane == i * ROUTE_TILES + k, pc, pc_ref[...])
        off_ref[...] = jnp.where(tile_lane == i * ROUTE_TILES + k, off, off_ref[...])


def _route_call(logits_t, bias):
    n = logits_t.shape[1]
    nt = n // TM_MOE
    tr = ROUTE_TILES * TM_MOE
    return pl.pallas_call(
        _route_kernel,
        grid=(nt // ROUTE_TILES,),
        in_specs=[pl.BlockSpec((N_EXP, tr), lambda i: (0, i)), pl.BlockSpec((N_EXP, 1), lambda i: (0, 0))],
        out_specs=[
            pl.BlockSpec((4 * N_EXP, tr), lambda i: (0, i)),
            pl.BlockSpec((tr, 4 * N_EXP), lambda i: (i, 0)),
            pl.BlockSpec((tr, GATE_LANES), lambda i: (i, 0)),
            pl.BlockSpec((ROUTE_TILES, 2 * N_EXP, GATE_LANES), lambda i: (i, 0, 0)),
            pl.BlockSpec((ROUTE_TILES, SUBLANES, GATE_LANES), lambda i: (i, 0, 0)),
            pl.BlockSpec((N_EXP, nt), lambda i: (0, 0)),
            pl.BlockSpec((N_EXP, nt), lambda i: (0, 0)),
        ],
        out_shape=[
            jax.ShapeDtypeStruct((4 * N_EXP, n), BF16),
            jax.ShapeDtypeStruct((n, 4 * N_EXP), BF16),
            jax.ShapeDtypeStruct((n, GATE_LANES), F32),
            jax.ShapeDtypeStruct((nt, 2 * N_EXP, GATE_LANES), F32),
            jax.ShapeDtypeStruct((nt, SUBLANES, GATE_LANES), F32),
            jax.ShapeDtypeStruct((N_EXP, nt), F32),
            jax.ShapeDtypeStruct((N_EXP, nt), F32),
        ],
        compiler_params=_cparams("arbitrary"),
        name="route",
    )(logits_t, bias.reshape(N_EXP, 1))


def _dispatch_tables(pc_t, off_t, nb_max):
    pc = pc_t.astype(jnp.int32)
    used = pc.sum(1)
    region = (used + BM - 1) // BM * BM
    ends = jnp.cumsum(region)
    starts = ends - region
    dst = starts[:, None] + jnp.cumsum(pc, 1) - pc
    nblk = ends[-1] // BM
    blk = jnp.minimum(jnp.arange(nb_max, dtype=jnp.int32), nblk - 1)
    blk_e = jnp.minimum(jnp.sum(ends[None, :] <= blk[:, None] * BM, axis=1), N_EXP - 1).astype(jnp.int32)
    has = region > 0
    ids = jnp.arange(N_EXP, dtype=jnp.int32)
    slot_e = (jnp.cumsum(has) - 1) % 2
    later = has[None, :] & (ids[None, :] > ids[:, None])
    next_e = jnp.where(later.any(1), jnp.argmax(later, axis=1), -1)
    blocks = (blk_e, nblk.reshape(1).astype(jnp.int32), slot_e[blk_e].astype(jnp.int32),
              next_e[blk_e].astype(jnp.int32))
    flat = lambda a: a.T.reshape(-1).astype(jnp.int32)
    tail = ((starts + used).astype(jnp.int32), (region - used).astype(jnp.int32))
    nt = pc.shape[1]
    off = off_t.astype(jnp.int32)
    e_idx = jnp.arange(N_EXP, dtype=jnp.int32)[:, None]
    col = lambda v: jnp.broadcast_to(jnp.asarray(v, jnp.int32), (N_EXP, 1))
    table = lambda so, do, ln: (flat(so), flat(do), flat(ln), ln.sum(0).astype(jnp.int32))
    pc_c = jnp.concatenate([pc, col(0)], 1)
    spare_c = _sorted_rows(TM_MOE) + SEG_ALIGN * e_idx
    seg_comb = table(jnp.where(pc_c > 0, jnp.concatenate([off, col(0)], 1), spare_c),
                     jnp.where(pc_c > 0, jnp.concatenate([dst, col(0)], 1), 0), jnp.maximum(pc_c, SEG_ALIGN))
    pc_d = jnp.concatenate([col(0), pc], 1)
    entry = jnp.arange(nt + 1, dtype=jnp.int32)[None, :]
    spare_d = nb_max * BM + ((entry % DISPATCH_SLOTS) * N_EXP + e_idx) * SEG_ALIGN
    seg_disp = table(jnp.where(pc_d > 0, jnp.concatenate([col(0), off], 1), 0),
                     jnp.where(pc_d > 0, jnp.concatenate([col(0), dst], 1), spare_d), jnp.maximum(pc_d, SEG_ALIGN))
    return seg_disp, seg_comb, tail, blocks


def _swiglu(u, w1, w3, w2):
    return _dot((_silu(_dot(u, w1)) * _dot(u, w3)).astype(BF16), w2)


def _round_bf16(x):
    return x.astype(BF16).astype(F32)


def _pack_pair(lo, hi):
    lo_b = lax.bitcast_convert_type(lo, jnp.uint32)
    hi_b = lax.bitcast_convert_type(hi, jnp.uint32)
    return (hi_b & jnp.uint32(0xFFFF0000)) | (lo_b >> 16)


def _unpack_pair(w):
    lo = lax.bitcast_convert_type(w << 16, F32).astype(BF16)
    hi = lax.bitcast_convert_type(w & jnp.uint32(0xFFFF0000), F32).astype(BF16)
    return lo, hi


def _segment_copies_dense(so_ref, do_ref, ln_ref, tile, make_copy):
    for e in range(N_EXP):
        j = tile * N_EXP + e
        make_copy(pl.multiple_of(so_ref[j], SEG_ALIGN), pl.multiple_of(do_ref[j], SEG_ALIGN),
                  pl.multiple_of(ln_ref[j], SEG_ALIGN)).start()


def _dispatch_kernel(nt, so_ref, do_ref, ln_ref, tt_ref, td_ref, tl_ref, nb_ref, u_ref, code_ref, segr_ref,
                     digits_ref, xs_out, buf, zbuf, sem):
    i = pl.program_id(0)
    r = buf.shape[1]
    nb_max = (xs_out.shape[0] - DISPATCH_SLOTS * N_EXP * SEG_ALIGN) // BM
    fill_sem = DISPATCH_SLOTS
    slot = i % DISPATCH_SLOTS
    prev_slot = (i + DISPATCH_SLOTS - 1) % DISPATCH_SLOTS

    def strip(slot_, so, do, ln):
        return pltpu.make_async_copy(buf.at[slot_, pl.ds(so, ln)], xs_out.at[pl.ds(do, ln)], sem.at[slot_])

    def wait_entry(slot_, entry):
        strip(slot_, 0, 0, pl.multiple_of(tt_ref[entry], SEG_ALIGN)).wait()

    def zero_fills(act):
        def tail(e, carry):
            ln = pl.multiple_of(tl_ref[e], SEG_ALIGN)

            @pl.when(ln > 0)
            def _():
                act(pltpu.make_async_copy(zbuf.at[pl.ds(0, ln)],
                                          xs_out.at[pl.ds(pl.multiple_of(td_ref[e], SEG_ALIGN), ln)], sem.at[fill_sem]))

            return carry

        lax.fori_loop(0, N_EXP, tail, 0)

        def blk(j, carry):
            act(pltpu.make_async_copy(zbuf.at[pl.ds(0, BM)], xs_out.at[pl.ds(pl.multiple_of(j * BM, BM), BM)],
                                      sem.at[fill_sem]))
            return carry

        lax.fori_loop(nb_ref[0], nb_max, blk, 0)

    @pl.when(i == 0)
    def _():
        zbuf[...] = jnp.zeros(zbuf.shape, jnp.uint32)
        buf[...] = jnp.zeros(buf.shape, jnp.uint32)
        spare = pltpu.make_async_copy(zbuf.at[pl.ds(0, DISPATCH_SLOTS * N_EXP * SEG_ALIGN)],
                                      xs_out.at[pl.ds(nb_max * BM, DISPATCH_SLOTS * N_EXP * SEG_ALIGN)],
                                      sem.at[fill_sem])
        spare.start()
        spare.wait()
        zero_fills(lambda cp: cp.start())

    @pl.when(i >= DISPATCH_SLOTS - 1)
    def _():
        wait_entry(slot, i - (DISPATCH_SLOTS - 1))

    row = lax.broadcasted_iota(jnp.int32, (r, GATE_LANES), 0).astype(F32)
    own = jnp.where(row >= segr_ref[0:1, :], jnp.where(row < segr_ref[1:2, :], 1.0, 0.0), 0.0).astype(BF16)
    z = _dot(jnp.concatenate([own, digits_ref[...]], axis=1), code_ref[...])
    perm = jnp.where(z == 0.0, 1.0, 0.0).astype(BF16)
    step = 256
    for c in range(0, PK, step):
        buf[slot, :, c:c + step] = _pack_pair(_dot(perm, u_ref[:, c:c + step]),
                                              _dot(perm, u_ref[:, PK + c:PK + c + step]))
    _segment_copies_dense(so_ref, do_ref, ln_ref, i, functools.partial(strip, prev_slot))

    @pl.when(i == nt - 1)
    def _():
        _segment_copies_dense(so_ref, do_ref, ln_ref, nt, functools.partial(strip, slot))
        for entry in range(max(nt - DISPATCH_SLOTS + 1, 0), nt + 1):
            wait_entry((entry + DISPATCH_SLOTS - 1) % DISPATCH_SLOTS, entry)
        zero_fills(lambda cp: cp.wait())


def _expert_kernel(layer, be_ref, nb_ref, ws_ref, nx_ref, x_ref, w1_hbm, w3_hbm, w2_hbm, y_ref,
                   w1f, w3f, w2f, w1b, w3b, w2b, sem):
    i = pl.program_id(0)
    live = i < nb_ref[0]

    def fetch(e, slot):
        return [pltpu.make_async_copy(src.at[layer, e], dst.at[slot], sem.at[slot, j])
                for j, (src, dst) in enumerate(((w1_hbm, w1f), (w3_hbm, w3f), (w2_hbm, w2f)))]

    @pl.when(live & ((i == 0) | (be_ref[i] != be_ref[jnp.maximum(i - 1, 0)])))
    def _():
        slot = ws_ref[i]

        @pl.when(i == 0)
        def _():
            for cp in fetch(be_ref[0], slot):
                cp.start()

        for cp in fetch(be_ref[i], slot):
            cp.wait()
        w1b[...] = w1f[slot].astype(BF16)
        w3b[...] = w3f[slot].astype(BF16)
        w2b[...] = w2f[slot].astype(BF16)

        @pl.when(nx_ref[i] >= 0)
        def _():
            for cp in fetch(nx_ref[i], 1 - slot):
                cp.start()

    @pl.when(live)
    def _():
        up = []
        for r0 in range(0, BM, EXP_SUB):
            lo, hi = _unpack_pair(x_ref[r0:r0 + EXP_SUB, :])
            up.append((_dot(lo, w1b[:PK, :]) + _dot(hi, w1b[PK:, :]), _dot(lo, w3b[:PK, :]) + _dot(hi, w3b[PK:, :])))
        for j, (a, g) in enumerate(up):
            hid = (_silu(a) * g).astype(BF16)
            y_ref[j * EXP_SUB:(j + 1) * EXP_SUB, :] = _pack_pair(_round_bf16(_dot(hid, w2b[:, :PK])),
                                                                 _round_bf16(_dot(hid, w2b[:, PK:])))


def _combine_kernel(so_ref, do_ref, ln_ref, tt_ref, code_t_ref, gate_t_ref, segc_ref, digits_ref, u_ref, ys_ref,
                    s1_ref, s3_ref, s2_ref, h_ref, n_ref, mod_ref, o_ref, buf, sem):
    i = pl.program_id(0)
    nt = pl.num_programs(0)
    slot = i % 2
    r = digits_ref.shape[1]

    def strip(slot_, so, do, ln):
        return pltpu.make_async_copy(ys_ref.at[pl.ds(do, ln)], buf.at[slot_, pl.ds(so, ln)], sem.at[slot_])

    @pl.when(i == 0)
    def _():
        buf[...] = jnp.zeros(buf.shape, jnp.uint32)
        _segment_copies_dense(so_ref, do_ref, ln_ref, 0, functools.partial(strip, 0))

    col = lax.broadcasted_iota(jnp.int32, (2 * N_EXP, r), 1).astype(F32)
    own = jnp.where(col >= segc_ref[:, 0:1], jnp.where(col < segc_ref[:, 1:2], 1.0, 0.0), 0.0).astype(BF16)
    z = _dot(code_t_ref[...], jnp.concatenate([own, digits_ref[...]], axis=0))
    q = jnp.where(z == 0.0, _dot(gate_t_ref[...].astype(BF16), own), 0.0).astype(BF16)
    shared = _swiglu(u_ref[...], s1_ref[...], s3_ref[...], s2_ref[...])
    _segment_copies_dense(so_ref, do_ref, ln_ref, i + 1, functools.partial(strip, 1 - slot))

    strip(slot, 0, 0, pl.multiple_of(tt_ref[i], SEG_ALIGN)).wait()
    lo, hi = _unpack_pair(buf[slot, 0:r, :])
    moe = jnp.concatenate([_dot(q, lo), _dot(q, hi)], axis=1) + shared
    o_ref[...] = h_ref[...] + mod_ref[5:6, :] * _rms(moe, n_ref[3:4, :])

    @pl.when(i == nt - 1)
    def _():
        strip(1 - slot, 0, 0, pl.multiple_of(tt_ref[nt], SEG_ALIGN)).wait()


def _sorted_rows(tm):
    return -(-(TOP_K * tm + N_EXP * (SEG_ALIGN - 1)) // MXU_DIM) * MXU_DIM


def _max_blocks(n):
    nt = n // TM_MOE
    rows = TOP_K * n + N_EXP * (SEG_ALIGN - 1) * nt + N_EXP * (BM - SEG_ALIGN)
    return -(-rows // BM)


def _row_digits(r):
    idx = np.arange(r)
    digits = np.zeros((r, GATE_LANES), np.float32)
    digits[:, 0] = idx // POS_RADIX * POS_RADIX
    digits[:, 1] = idx % POS_RADIX
    return jnp.asarray(digits, BF16)


def _dispatch_call(seg, tail, nblk, u2, code, segr, nb_max):
    n = u2.shape[0]
    r = _sorted_rows(TM_MOE)
    spare = DISPATCH_SLOTS * N_EXP * SEG_ALIGN
    return pl.pallas_call(
        functools.partial(_dispatch_kernel, n // TM_MOE),
        grid_spec=pltpu.PrefetchScalarGridSpec(
            num_scalar_prefetch=7,
            grid=(n // TM_MOE,),
            in_specs=[
                pl.BlockSpec((TM_MOE, D), lambda i, *_: (i, 0)),
                pl.BlockSpec((4 * N_EXP, TM_MOE), lambda i, *_: (0, i)),
                pl.BlockSpec((None, SUBLANES, GATE_LANES), lambda i, *_: (i, 0, 0)),
                pl.BlockSpec((r, GATE_LANES), lambda i, *_: (0, 0)),
            ],
            out_specs=pl.BlockSpec(memory_space=pl.ANY),
            scratch_shapes=[pltpu.VMEM((DISPATCH_SLOTS, r, PK), jnp.uint32),
                            pltpu.VMEM((max(BM, spare), PK), jnp.uint32),
                            pltpu.SemaphoreType.DMA((DISPATCH_SLOTS + 1,))],
        ),
        out_shape=jax.ShapeDtypeStruct((nb_max * BM + spare, PK), jnp.uint32),
        compiler_params=_cparams("arbitrary"),
        name="dispatch",
    )(*seg, tail[0], tail[1], nblk, u2, code, segr, _row_digits(r))


def _expert_call(layer, blk, xs, w1, w3, w2, nb_max):
    row_blk = lambda i, be, nb, *_: (jnp.maximum(jnp.minimum(i, nb[0] - 1), 0), 0)
    hbm = pl.BlockSpec(memory_space=pl.ANY)
    return pl.pallas_call(
        functools.partial(_expert_kernel, layer),
        grid_spec=pltpu.PrefetchScalarGridSpec(
            num_scalar_prefetch=4,
            grid=(nb_max,),
            in_specs=[pl.BlockSpec((BM, PK), row_blk), hbm, hbm, hbm],
            out_specs=pl.BlockSpec((BM, PK), row_blk),
            scratch_shapes=[pltpu.VMEM((2, D, F_EXP), F32), pltpu.VMEM((2, D, F_EXP), F32),
                            pltpu.VMEM((2, F_EXP, D), F32),
                            pltpu.VMEM((D, F_EXP), BF16), pltpu.VMEM((D, F_EXP), BF16),
                            pltpu.VMEM((F_EXP, D), BF16), pltpu.SemaphoreType.DMA((2, 3))],
        ),
        out_shape=jax.ShapeDtypeStruct(xs.shape, jnp.uint32),
        input_output_aliases={4: 0},
        compiler_params=_cparams("arbitrary"),
        name="experts",
    )(*blk, xs, w1, w3, w2)


def _combine_call(tabs, code_t, gate_t, segc, u2, ys, s1, s3, s2, h, norms_l, mod_l, t):
    n = u2.shape[0]
    r = _sorted_rows(TM_MOE)
    tok = lambda w: pl.BlockSpec((TM_MOE, w), lambda i, *_: (i, 0))
    const = lambda shp: pl.BlockSpec(shp, lambda i, *_: (0,) * len(shp))
    return pl.pallas_call(
        _combine_kernel,
        grid_spec=pltpu.PrefetchScalarGridSpec(
            num_scalar_prefetch=4,
            grid=(n // TM_MOE,),
            in_specs=[
                tok(4 * N_EXP),
                tok(GATE_LANES),
                pl.BlockSpec((None, 2 * N_EXP, GATE_LANES), lambda i, *_: (i, 0, 0)),
                const((2 * N_EXP, r)),
                tok(D),
                pl.BlockSpec(memory_space=pl.ANY),
                const((D, F_EXP)), const((D, F_EXP)), const((F_EXP, D)),
                tok(D),
                const((4, D)),
                pl.BlockSpec((None, 6, D), lambda i, *_: (i * TM_MOE // t, 0, 0)),
            ],
            out_specs=tok(D),
            scratch_shapes=[pltpu.VMEM((2, r + N_EXP * SEG_ALIGN, PK), jnp.uint32),
                            pltpu.SemaphoreType.DMA((2,))],
        ),
        out_shape=jax.ShapeDtypeStruct((n, D), F32),
        compiler_params=_cparams("arbitrary"),
        name="combine",
    )(*tabs, code_t, gate_t, segc, _row_digits(r).T, u2, ys, s1, s3, s2, h, norms_l, mod_l)


def _glu_kernel(h_ref, g_ref, mod_ref, w_ref, b_ref, o_ref):
    u = (_rms(h_ref[...], g_ref[...]) * (1.0 + mod_ref[1:2, :]) + mod_ref[0:1, :]).astype(BF16)
    a = _dot(u, w_ref[:, :D]) + b_ref[:, :D]
    gate = _dot(u, w_ref[:, D:]) + b_ref[:, D:]
    o_ref[...] = (a * jax.nn.sigmoid(gate)).astype(BF16)


def _glu_call(h, g, mod_l, w1, b1, tm):
    b, t, _ = h.shape
    tok = lambda n: pl.BlockSpec((None, tm, n), lambda i, j: (i, j, 0))
    return pl.pallas_call(
        _glu_kernel,
        grid=(b, t // tm),
        in_specs=[
            tok(D),
            pl.BlockSpec((1, D), lambda i, j: (0, 0)),
            pl.BlockSpec((None, 6, D), lambda i, j: (i, 0, 0)),
            pl.BlockSpec((D, 2 * D), lambda i, j: (0, 0)),
            pl.BlockSpec((1, 2 * D), lambda i, j: (0, 0)),
        ],
        out_specs=tok(D),
        out_shape=jax.ShapeDtypeStruct((b, t, D), BF16),
        compiler_params=_cparams("parallel", "parallel"),
        name="glu",
    )(h, g, mod_l, w1, b1)


def _conv_kernel(cur_ref, prev_ref, next_ref, dw_ref, cv_ref, w2_ref, h_ref, n_ref, mod_ref, wr_ref,
                 h_out, u_out, lg_out, win, shifted, conv):
    j = pl.program_id(1)
    tm = cur_ref.shape[0]
    pad = CONV_W // 2
    win[0:HALO, :] = jnp.where(j > 0, prev_ref[...].astype(F32), 0.0)
    win[HALO:HALO + tm, :] = cur_ref[...].astype(F32)
    win[HALO + tm:, :] = jnp.where(j < pl.num_programs(1) - 1, next_ref[...].astype(F32), 0.0)
    ext = shifted.shape[1]
    full = win[...]
    shifted[0] = full[:ext, :]
    for s in range(1, SUBLANES):
        shifted[s] = pltpu.roll(full, full.shape[0] - s, axis=0)[:ext, :]

    def rows(ib, carry):
        r0 = pl.multiple_of(ib * 2 * SUBLANES, 2 * SUBLANES)
        for c0 in range(0, D, CONV_LANES):
            cols = slice(c0, c0 + CONV_LANES)
            acc0 = jnp.zeros((SUBLANES, CONV_LANES), F32)
            acc1 = jnp.zeros((SUBLANES, CONV_LANES), F32)
            for s in range(SUBLANES):
                taps = [(off // SUBLANES, off - (HALO - pad)) for off in range(HALO - pad, HALO - pad + CONV_W)
                        if off % SUBLANES == s]
                tiles = {a: shifted[s, pl.ds(r0 + a * SUBLANES, SUBLANES), cols]
                         for a in range(taps[0][0], taps[-1][0] + 2)}
                for a, tap in taps:
                    w = dw_ref[tap * SUBLANES:(tap + 1) * SUBLANES, cols]
                    acc0 = acc0 + tiles[a] * w
                    acc1 = acc1 + tiles[a + 1] * w
            conv[pl.ds(r0, SUBLANES), cols] = acc0
            conv[pl.ds(r0 + SUBLANES, SUBLANES), cols] = acc1
        return carry

    lax.fori_loop(0, tm // (2 * SUBLANES), rows, 0)
    uf = conv[...] + cv_ref[0:1, :]
    mu = jnp.mean(uf, axis=-1, keepdims=True)
    var = jnp.mean(jnp.square(uf - mu), axis=-1, keepdims=True)
    uf = (uf - mu) * lax.rsqrt(var + LN_EPS) * cv_ref[1:2, :] + cv_ref[2:3, :]
    y = _dot(_silu(uf).astype(BF16), w2_ref[...]) + cv_ref[3:4, :]
    _mixer_epilogue(y, h_ref, n_ref, mod_ref, wr_ref, h_out, u_out, lg_out)


def _conv_call(glu, dw, cvec, w2, h, norms_l, mod_l, wr_t, tm):
    b, t, _ = h.shape
    tok, ep_in, wr_spec, out_specs, out_shape = _epilogue_specs(b, t, tm)
    r = tm // HALO
    last = t // HALO - 1
    return pl.pallas_call(
        _conv_kernel,
        grid=(b, t // tm),
        in_specs=[
            tok(D),
            pl.BlockSpec((None, HALO, D), lambda i, j: (i, jnp.maximum(j * r - 1, 0), 0)),
            pl.BlockSpec((None, HALO, D), lambda i, j: (i, jnp.minimum((j + 1) * r, last), 0)),
            pl.BlockSpec((CONV_W * SUBLANES, D), lambda i, j: (0, 0)),
            pl.BlockSpec((4, D), lambda i, j: (0, 0)),
            pl.BlockSpec((D, D), lambda i, j: (0, 0)),
        ] + ep_in + [wr_spec],
        out_specs=out_specs,
        out_shape=out_shape,
        scratch_shapes=[pltpu.VMEM((tm + 2 * HALO, D), F32),
                        pltpu.VMEM((SUBLANES, tm + (CONV_W // SUBLANES) * SUBLANES, D), F32),
                        pltpu.VMEM((tm, D), F32)],
        compiler_params=_cparams("parallel", "parallel"),
        name="conv",
    )(glu, glu, glu, dw, cvec, w2, h, norms_l, mod_l, wr_t)


def _tile(t, pref):
    return pref if t % pref == 0 else t


def kernel(x, c, ctx, c_ctx, mod_w, mod_b, norms, ret_w_in, ret_w_out, ret_decay, conv_w1, conv_b1, conv_dw,
           conv_b_dw, conv_ln_g, conv_ln_b, conv_w2, conv_b2, moe_router, moe_bias, moe_w1, moe_w3, moe_w2,
           shared_w1, shared_w3, shared_w2):
    b, t, _ = x.shape
    l = ctx.shape[1]
    tm = _tile(t, 512)

    rows = -(-(b + 1) // 8) * 8
    cs = jnp.concatenate([c, c_ctx[None, :], jnp.zeros((rows - b - 1, D), F32)], 0)
    mod = _mod_call(cs, mod_w, mod_b).reshape(mod_w.shape[0], rows, 6, D)

    idx = jnp.arange(t, dtype=jnp.int32)
    pos_lat = jnp.stack([jnp.full((t,), l, jnp.int32), idx // GRID_W, idx % GRID_W], -1)
    zl = jnp.zeros((l,), jnp.int32)
    pos_ctx = jnp.stack([jnp.arange(l, dtype=jnp.int32), zl, zl], -1)
    ang_lat, ang_ctx = _rope_angles(pos_lat), _rope_angles(pos_ctx)

    head_perm = np.concatenate([np.arange(0, DK, 2), np.arange(1, DK, 2)])
    perm = np.concatenate([hh * DK + head_perm for hh in range(2 * HEADS)] + [np.arange(2 * QK_TOT, IN_DIM)])
    w_in = ret_w_in[0][:, perm].astype(BF16)
    log_g = jax.nn.log_sigmoid(ret_decay[0].astype(F32))

    n = b * t
    nb_max = _max_blocks(n)

    def moe_layer(i, h1, u2, logits_t):
        code, code_t, gate_t, segc, segr, pc_t, off_t = _route_call(logits_t, moe_bias[i])
        seg_disp, seg_comb, tail, blocks = _dispatch_tables(pc_t, off_t, nb_max)
        u2f = u2.reshape(n, D)
        xs = _dispatch_call(seg_disp, tail, blocks[1], u2f, code, segr, nb_max)
        ys = _expert_call(i, blocks, xs, moe_w1, moe_w3, moe_w2, nb_max)
        out = _combine_call(seg_comb, code_t, gate_t, segc, u2f, ys, shared_w1[i].astype(BF16), shared_w3[i].astype(BF16),
                            shared_w2[i].astype(BF16), h1.reshape(n, D), norms[i], mod[i], t)
        return out.reshape(b, t, D)

    q, k, v, gt = _inproj_call(x, norms[0, 0:1], mod[0], w_in, jnp.cos(ang_lat), jnp.sin(ang_lat), tm)
    sf0, sb0 = _ctxstate_call(log_g, ctx, norms[0, 0:1], mod[0, b], w_in[:, QK_TOT:2 * QK_TOT + V_TOT],
                              jnp.cos(ang_ctx), jnp.sin(ang_ctx))
    yn = _retention_call(log_g, q, k, v, sf0, sb0)
    h1, u2, logits_t = _readout_call(yn, gt, x, norms[0], mod[0], ret_w_out[0].astype(BF16),
                                     moe_router[0].T, tm)
    h2 = moe_layer(0, h1, u2, logits_t)

    glu = _glu_call(h2, norms[1, 0:1], mod[1], conv_w1[0].astype(BF16), conv_b1[0][None, :], tm)
    cvec = jnp.stack([conv_b_dw[0], conv_ln_g[0], conv_ln_b[0], conv_b2[0]], 0)
    dw_rows = jnp.repeat(conv_dw[0], SUBLANES, axis=0)
    h3, u2, logits_t = _conv_call(glu, dw_rows, cvec, conv_w2[0].astype(BF16), h2, norms[1], mod[1],
                                  moe_router[1].T, tm)
    return moe_layer(1, h3, u2, logits_t)
```

```python
import functools

import jax
import jax.numpy as jnp
import numpy as np
from jax import lax
from jax.experimental import pallas as pl
from jax.experimental.pallas import tpu as pltpu

F32 = jnp.float32
BF16 = jnp.bfloat16

D = 1024
HEADS = 4
DK = D // HEADS
DV = 2 * DK
QK_TOT = HEADS * DK
V_TOT = HEADS * DV
IN_DIM = 2 * QK_TOT + 2 * V_TOT
ROPE_AXES = (64, 96, 96)
ROPE_THETA = 10000.0
GRID_W = 64
CONV_W = 31
HALO = 16
SUBLANES = 8
CONV_LANES = 256
N_EXP = 64
N_GRP = 8
PER_GRP = N_EXP // N_GRP
TOP_K = 8
TOP_G = 4
F_EXP = 256
ROUTE_SCALE = 2.5
GATE_LANES = 128
MXU_DIM = 256
TM_MOE = 256
ROUTE_TILES = 4
SEG_ALIGN = 8
POS_RADIX = 256
BM = 1024
DISPATCH_SLOTS = 3
EXP_SUB = 256
PK = D // 2
NORM_EPS = 1e-6
LN_EPS = 1e-5
RET_CHUNK = 256
VMEM_LIMIT = 56 * 1024 * 1024

_HI = lax.Precision.HIGHEST


def _cparams(*sem):
    return pltpu.CompilerParams(dimension_semantics=sem, vmem_limit_bytes=VMEM_LIMIT)


def _dot(a, b):
    return jnp.dot(a, b, preferred_element_type=F32)


def _rms(xf, g):
    return xf * lax.rsqrt(jnp.mean(xf * xf, axis=-1, keepdims=True) + NORM_EPS) * g


def _silu(x):
    return x * jax.nn.sigmoid(x)


def _mod_kernel(cs_ref, w_ref, b_ref, o_ref):
    s = _silu(cs_ref[...])
    o_ref[...] = jnp.dot(s, w_ref[...], preferred_element_type=F32, precision=_HI) + b_ref[...]


def _mod_call(cs, mod_w, mod_b):
    depth, _, n6 = mod_w.shape
    tn = 1536
    rows = cs.shape[0]
    return pl.pallas_call(
        _mod_kernel,
        grid=(depth, n6 // tn),
        in_specs=[
            pl.BlockSpec((rows, D), lambda i, j: (0, 0)),
            pl.BlockSpec((None, D, tn), lambda i, j: (i, 0, j)),
            pl.BlockSpec((None, 1, tn), lambda i, j: (i, 0, j)),
        ],
        out_specs=pl.BlockSpec((None, rows, tn), lambda i, j: (i, 0, j)),
        out_shape=jax.ShapeDtypeStruct((depth, rows, n6), F32),
        compiler_params=_cparams("parallel", "parallel"),
        name="mod",
    )(cs, mod_w, mod_b.reshape(depth, 1, n6))


def _rope_angles(pos):
    parts = []
    for a, d in enumerate(ROPE_AXES):
        inv = ROPE_THETA ** (-jnp.arange(0, d, 2, dtype=F32) / d)
        parts.append(pos[:, a:a + 1].astype(F32) * inv[None, :])
    return jnp.concatenate(parts, -1)


def _rope_store(z, cos, sin, scale, o_ref, col):
    half = DK // 2
    x0, x1 = z[:, :half], z[:, half:]
    o_ref[:, col:col + half] = ((x0 * cos - x1 * sin) * scale).astype(o_ref.dtype)
    o_ref[:, col + half:col + DK] = ((x0 * sin + x1 * cos) * scale).astype(o_ref.dtype)


def _inproj_kernel(x_ref, g_ref, mod_ref, w_ref, cos_ref, sin_ref, q_ref, k_ref, v_ref, gt_ref):
    u = (_rms(x_ref[...], g_ref[...]) * (1.0 + mod_ref[1:2, :]) + mod_ref[0:1, :]).astype(BF16)
    cos, sin = cos_ref[...], sin_ref[...]
    for h in range(HEADS):
        _rope_store(_dot(u, w_ref[:, h * DK:(h + 1) * DK]), cos, sin, DK ** -0.5, q_ref, h * DK)
        _rope_store(_dot(u, w_ref[:, QK_TOT + h * DK:QK_TOT + (h + 1) * DK]), cos, sin, 1.0, k_ref, h * DK)
    step = 512
    for j in range(V_TOT // step):
        c0 = 2 * QK_TOT + j * step
        v_ref[:, j * step:(j + 1) * step] = _dot(u, w_ref[:, c0:c0 + step]).astype(BF16)
        c1 = 2 * QK_TOT + V_TOT + j * step
        gt_ref[:, j * step:(j + 1) * step] = _dot(u, w_ref[:, c1:c1 + step]).astype(BF16)


def _inproj_call(x, g, mod_l, w_in, cos, sin, tm):
    b, t, _ = x.shape
    grid = (b, t // tm)
    tok = lambda n: pl.BlockSpec((None, tm, n), lambda i, j: (i, j, 0))
    return pl.pallas_call(
        _inproj_kernel,
        grid=grid,
        in_specs=[
            tok(D),
            pl.BlockSpec((1, D), lambda i, j: (0, 0)),
            pl.BlockSpec((None, 6, D), lambda i, j: (i, 0, 0)),
            pl.BlockSpec((D, IN_DIM), lambda i, j: (0, 0), pipeline_mode=pl.Buffered(1)),
            pl.BlockSpec((tm, DK // 2), lambda i, j: (j, 0)),
            pl.BlockSpec((tm, DK // 2), lambda i, j: (j, 0)),
        ],
        out_specs=[tok(QK_TOT), tok(QK_TOT), tok(V_TOT), tok(V_TOT)],
        out_shape=[
            jax.ShapeDtypeStruct((b, t, QK_TOT), BF16),
            jax.ShapeDtypeStruct((b, t, QK_TOT), BF16),
            jax.ShapeDtypeStruct((b, t, V_TOT), BF16),
            jax.ShapeDtypeStruct((b, t, V_TOT), BF16),
        ],
        compiler_params=_cparams("parallel", "parallel"),
        name="inproj",
    )(x, g, mod_l, w_in, cos, sin)


def _row_pow(lg, expo_fn, rows, cols):
    i = lax.broadcasted_iota(jnp.int32, (rows, cols), 0).astype(F32)
    return jnp.exp(lg * expo_fn(i))


def _ctxstate_kernel(lg_ref, c_ref, g_ref, mod_ref, w_ref, cos_ref, sin_ref, sf_ref, sb_ref, k_scr):
    l = c_ref.shape[0]
    u = (_rms(c_ref[...], g_ref[...]) * (1.0 + mod_ref[1:2, :]) + mod_ref[0:1, :]).astype(BF16)
    cos, sin = cos_ref[...], sin_ref[...]
    for h in range(HEADS):
        _rope_store(_dot(u, w_ref[:, h * DK:(h + 1) * DK]), cos, sin, 1.0, k_scr, h * DK)
    for h in range(HEADS):
        kh = k_scr[:, h * DK:(h + 1) * DK]
        vh = _dot(u, w_ref[:, QK_TOT + h * DV:QK_TOT + (h + 1) * DV]).astype(BF16)
        wf = _row_pow(lg_ref[0, h], lambda i: (l - 1.0) - i, l, DK)
        wb = _row_pow(lg_ref[1, h], lambda i: i, l, DK)
        sf_ref[h] = _dot((kh * wf).T.astype(BF16), vh)
        sb_ref[h] = _dot((kh * wb).T.astype(BF16), vh)


def _ctxstate_call(log_g, ctx, g, mod_row, w_kv, cos, sin):
    b, l, _ = ctx.shape
    st = jax.ShapeDtypeStruct((b, HEADS, DK, DV), F32)
    st_spec = pl.BlockSpec((None, HEADS, DK, DV), lambda i: (i, 0, 0, 0))
    return pl.pallas_call(
        _ctxstate_kernel,
        grid=(b,),
        in_specs=[
            pl.BlockSpec(memory_space=pltpu.SMEM),
            pl.BlockSpec((None, l, D), lambda i: (i, 0, 0)),
            pl.BlockSpec((1, D), lambda i: (0, 0)),
            pl.BlockSpec((6, D), lambda i: (0, 0)),
            pl.BlockSpec((D, QK_TOT + V_TOT), lambda i: (0, 0)),
            pl.BlockSpec((l, DK // 2), lambda i: (0, 0)),
            pl.BlockSpec((l, DK // 2), lambda i: (0, 0)),
        ],
        out_specs=[st_spec, st_spec],
        out_shape=[st, st],
        scratch_shapes=[pltpu.VMEM((l, QK_TOT), F32)],
        compiler_params=_cparams("parallel"),
        name="ctxstate",
    )(log_g, ctx, g, mod_row, w_kv, cos, sin)


def _retention_kernel(lg_ref, q_ref, k_ref, v_ref, sf0_ref, sb0_ref, o_ref, acc, sf, sb):
    h = pl.program_id(1)
    t = q_ref.shape[0]
    c = RET_CHUNK
    nc = t // c
    lgf, lgb = lg_ref[0, h], lg_ref[1, h]
    ri = lax.broadcasted_iota(jnp.int32, (c, c), 0).astype(F32)
    ci = lax.broadcasted_iota(jnp.int32, (c, c), 1).astype(F32)
    rel = ri - ci
    mask = (jnp.where(rel >= 0, jnp.exp(lgf * jnp.maximum(rel, 0.0)), 0.0)
            + jnp.where(rel <= 0, jnp.exp(lgb * jnp.maximum(-rel, 0.0)), 0.0))
    qdf = _row_pow(lgf, lambda i: i + 1.0, c, DV)
    qdb = _row_pow(lgb, lambda i: c - i, c, DV)
    kdf = _row_pow(lgf, lambda i: (c - 1.0) - i, c, DK)
    kdb = _row_pow(lgb, lambda i: i, c, DK)
    cdf = jnp.exp(jnp.full((1, DV), lgf * c, F32))
    cdb = jnp.exp(jnp.full((1, DV), lgb * c, F32))
    sf[...] = sf0_ref[...]
    sb[...] = sb0_ref[...]

    def rows(ic):
        return pl.ds(pl.multiple_of(ic * c, c), c)

    def fwd(rs):
        q, k, v = q_ref[rs, :], k_ref[rs, :], v_ref[rs, :]
        s = lax.dot_general(q, k, (((1,), (1,)), ((), ())), preferred_element_type=F32)
        o = _dot((s * mask).astype(BF16), v) + qdf * _dot(q, sf[...].astype(BF16))
        sf[...] = sf[...] * cdf + _dot((k.astype(F32) * kdf).T.astype(BF16), v)
        return o

    def bwd(rs):
        q, k, v = q_ref[rs, :], k_ref[rs, :], v_ref[rs, :]
        o = qdb * _dot(q, sb[...].astype(BF16))
        sb[...] = sb[...] * cdb + _dot((k.astype(F32) * kdb).T.astype(BF16), v)
        return o

    def finish(rs, o):
        o_ref[rs, :] = (o * lax.rsqrt(jnp.mean(o * o, axis=-1, keepdims=True) + NORM_EPS)).astype(BF16)

    def first_half(j, carry):
        acc[rows(j), :] = fwd(rows(j))
        acc[rows(nc - 1 - j), :] = bwd(rows(nc - 1 - j))
        return carry

    def second_half(j, carry):
        finish(rows(j), acc[rows(j), :] + fwd(rows(j)))
        finish(rows(nc - 1 - j), acc[rows(nc - 1 - j), :] + bwd(rows(nc - 1 - j)))
        return carry

    lax.fori_loop(0, nc // 2, first_half, 0)
    lax.fori_loop(nc // 2, nc, second_half, 0)


def _retention_call(log_g, q, k, v, sf0, sb0):
    b, t, _ = q.shape
    qk_spec = pl.BlockSpec((None, t, DK), lambda i, h: (i, 0, h))
    v_spec = pl.BlockSpec((None, t, DV), lambda i, h: (i, 0, h))
    s_spec = pl.BlockSpec((None, None, DK, DV), lambda i, h: (i, h, 0, 0))
    return pl.pallas_call(
        _retention_kernel,
        grid=(b, HEADS),
        in_specs=[pl.BlockSpec(memory_space=pltpu.SMEM), qk_spec, qk_spec, v_spec, s_spec, s_spec],
        out_specs=v_spec,
        out_shape=jax.ShapeDtypeStruct((b, t, V_TOT), BF16),
        scratch_shapes=[pltpu.VMEM((t, DV), F32), pltpu.VMEM((DK, DV), F32), pltpu.VMEM((DK, DV), F32)],
        compiler_params=_cparams("parallel", "parallel"),
        name="retention",
    )(log_g, q, k, v, sf0, sb0)


def _mixer_epilogue(y, h_ref, n_ref, mod_ref, wr_ref, h_out, u_out, lg_out):
    h1 = h_ref[...] + mod_ref[2:3, :] * _rms(y, n_ref[1:2, :])
    h_out[...] = h1
    u2 = _rms(h1, n_ref[2:3, :]) * (1.0 + mod_ref[4:5, :]) + mod_ref[3:4, :]
    u_out[...] = u2.astype(BF16)
    lg_out[...] = lax.dot_general(wr_ref[...], u2, (((1,), (1,)), ((), ())),
                                  preferred_element_type=F32, precision=_HI)


def _readout_kernel(y_ref, gt_ref, h_ref, n_ref, mod_ref, w_ref, wr_ref, h_out, u_out, lg_out):
    a = (y_ref[...].astype(F32) * _silu(gt_ref[...].astype(F32))).astype(BF16)
    _mixer_epilogue(_dot(a, w_ref[...]), h_ref, n_ref, mod_ref, wr_ref, h_out, u_out, lg_out)


def _epilogue_specs(b, t, tm):
    nt = t // tm
    tok = lambda n: pl.BlockSpec((None, tm, n), lambda i, j: (i, j, 0))
    in_specs = [
        tok(D),
        pl.BlockSpec((4, D), lambda i, j: (0, 0)),
        pl.BlockSpec((None, 6, D), lambda i, j: (i, 0, 0)),
    ]
    wr_spec = pl.BlockSpec((N_EXP, D), lambda i, j: (0, 0))
    out_specs = [tok(D), tok(D), pl.BlockSpec((N_EXP, tm), lambda i, j: (0, i * nt + j))]
    out_shape = [
        jax.ShapeDtypeStruct((b, t, D), F32),
        jax.ShapeDtypeStruct((b, t, D), BF16),
        jax.ShapeDtypeStruct((N_EXP, b * t), F32),
    ]
    return tok, in_specs, wr_spec, out_specs, out_shape


def _readout_call(yn, gt, h, norms_l, mod_l, w_out, wr_t, tm):
    b, t, _ = h.shape
    tok, ep_in, wr_spec, out_specs, out_shape = _epilogue_specs(b, t, tm)
    return pl.pallas_call(
        _readout_kernel,
        grid=(b, t // tm),
        in_specs=[tok(V_TOT), tok(V_TOT)] + ep_in
        + [pl.BlockSpec((V_TOT, D), lambda i, j: (0, 0)), wr_spec],
        out_specs=out_specs,
        out_shape=out_shape,
        compiler_params=_cparams("parallel", "parallel"),
        name="readout",
    )(yn, gt, h, norms_l, mod_l, w_out, wr_t)


def _route_tile(logits, bias):
    tr = logits.shape[1]
    s = jax.nn.sigmoid(logits)
    biased = s + bias
    neg = -jnp.inf
    b3 = biased.reshape(N_GRP, PER_GRP, tr)
    io3 = lax.broadcasted_iota(jnp.int32, b3.shape, 1).astype(F32)
    m1 = jnp.max(b3, axis=1, keepdims=True)
    i1 = jnp.min(jnp.where(b3 == m1, io3, float(PER_GRP)), axis=1, keepdims=True)
    m2 = jnp.max(jnp.where(io3 == i1, neg, b3), axis=1, keepdims=True)
    gs = (m1 + m2).reshape(N_GRP, tr)
    iog = lax.broadcasted_iota(jnp.int32, gs.shape, 0).astype(F32)
    gsel = jnp.zeros(gs.shape, F32)
    work = gs
    for _ in range(TOP_G):
        m = jnp.max(work, axis=0, keepdims=True)
        gi = jnp.min(jnp.where(work == m, iog, float(N_GRP)), axis=0, keepdims=True)
        hit = iog == gi
        gsel = jnp.where(hit, 1.0, gsel)
        work = jnp.where(hit, neg, work)
    emask = jnp.broadcast_to(gsel.reshape(N_GRP, 1, tr), b3.shape).reshape(N_EXP, tr)
    work = jnp.where(emask > 0.0, biased, neg)
    ioe = lax.broadcasted_iota(jnp.int32, work.shape, 0).astype(F32)
    esel = jnp.zeros(work.shape, F32)
    for _ in range(TOP_K):
        m = jnp.max(work, axis=0, keepdims=True)
        ei = jnp.min(jnp.where(work == m, ioe, float(N_EXP)), axis=0, keepdims=True)
        hit = ioe == ei
        esel = jnp.where(hit, 1.0, esel)
        work = jnp.where(hit, neg, work)
    wsel = esel * s
    gate = wsel / jnp.sum(wsel, axis=0, keepdims=True) * ROUTE_SCALE

    t_r = lax.broadcasted_iota(jnp.int32, (tr, tr), 0)
    t_c = lax.broadcasted_iota(jnp.int32, (tr, tr), 1)
    rank = _dot(esel.astype(BF16), jnp.where(t_r < t_c, 1.0, 0.0).astype(BF16))
    cnt = jnp.sum(esel, axis=1, keepdims=True)
    pc = jnp.floor((cnt + (SEG_ALIGN - 1.0)) * (1.0 / SEG_ALIGN)) * SEG_ALIGN
    e_r = lax.broadcasted_iota(jnp.int32, (N_EXP, N_EXP), 0)
    e_c = lax.broadcasted_iota(jnp.int32, (N_EXP, N_EXP), 1)
    off = jnp.dot(jnp.where(e_c < e_r, 1.0, 0.0), jnp.broadcast_to(pc, (N_EXP, 128)),
                  preferred_element_type=F32, precision=_HI)[:, 0:1]
    p = off + rank
    p_hi = jnp.where(esel > 0.0, jnp.floor(p * (1.0 / POS_RADIX)) * POS_RADIX, -float(POS_RADIX))
    p_lo = jnp.where(esel > 0.0, p - p_hi, 0.0)
    minus_r = jnp.where(lax.broadcasted_iota(jnp.int32, (2 * N_EXP, tr), 0) < 2, -1.0, 0.0)
    code = jnp.concatenate([p_hi, p_lo, minus_r], 0)
    code_t = jnp.concatenate([code[:2 * N_EXP].T, code[2 * N_EXP:].T], 1)
    gate_t = jnp.concatenate([gate, jnp.zeros((GATE_LANES - N_EXP, tr), F32)], 0).T
    lane = lax.broadcasted_iota(jnp.int32, (2 * N_EXP, GATE_LANES), 1)
    off2 = jnp.concatenate([off, off], 0)
    end2 = jnp.concatenate([off + pc, off + pc], 0)
    bounds = jnp.where(lane == 0, off2, jnp.where(lane == 1, end2, 0.0))
    return code.astype(BF16), code_t.astype(BF16), gate_t, bounds, bounds.T[0:SUBLANES, :], pc, off


def _route_kernel(lg_ref, b_ref, code_ref, code_t_ref, gate_t_ref, segc_ref, segr_ref, pc_ref, off_ref):
    i = pl.program_id(0)

    @pl.when(i == 0)
    def _():
        pc_ref[...] = jnp.zeros(pc_ref.shape, F32)
        off_ref[...] = jnp.zeros(off_ref.shape, F32)

    tile_lane = lax.broadcasted_iota(jnp.int32, pc_ref.shape, 1)
    for k in range(ROUTE_TILES):
        cols = slice(k * TM_MOE, (k + 1) * TM_MOE)
        code, code_t, gate_t, bounds, bounds_t, pc, off = _route_tile(lg_ref[:, cols], b_ref[...])
        code_ref[:, cols] = code
        code_t_ref[cols, :] = code_t
        gate_t_ref[cols, :] = gate_t
        segc_ref[k] = bounds
        segr_ref[k] = bounds_t
        pc_ref[...] = jnp.where(tile_lane == i * ROUTE_TILES + k, pc, pc_ref[...])
        off_ref[...] = jnp.where(tile_lane == i * ROUTE_TILES + k, off, off_ref[...])


def _route_call(logits_t, bias):
    n = logits_t.shape[1]
    nt = n // TM_MOE
    tr = ROUTE_TILES * TM_MOE
    return pl.pallas_call(
        _route_kernel,
        grid=(nt // ROUTE_TILES,),
        in_specs=[pl.BlockSpec((N_EXP, tr), lambda i: (0, i)), pl.BlockSpec((N_EXP, 1), lambda i: (0, 0))],
        out_specs=[
            pl.BlockSpec((4 * N_EXP, tr), lambda i: (0, i)),
            pl.BlockSpec((tr, 4 * N_EXP), lambda i: (i, 0)),
            pl.BlockSpec((tr, GATE_LANES), lambda i: (i, 0)),
            pl.BlockSpec((ROUTE_TILES, 2 * N_EXP, GATE_LANES), lambda i: (i, 0, 0)),
            pl.BlockSpec((ROUTE_TILES, SUBLANES, GATE_LANES), lambda i: (i, 0, 0)),
            pl.BlockSpec((N_EXP, nt), lambda i: (0, 0)),
            pl.BlockSpec((N_EXP, nt), lambda i: (0, 0)),
        ],
        out_shape=[
            jax.ShapeDtypeStruct((4 * N_EXP, n), BF16),
            jax.ShapeDtypeStruct((n, 4 * N_EXP), BF16),
            jax.ShapeDtypeStruct((n, GATE_LANES), F32),
            jax.ShapeDtypeStruct((nt, 2 * N_EXP, GATE_LANES), F32),
            jax.ShapeDtypeStruct((nt, SUBLANES, GATE_LANES), F32),
            jax.ShapeDtypeStruct((N_EXP, nt), F32),
            jax.ShapeDtypeStruct((N_EXP, nt), F32),
        ],
        compiler_params=_cparams("arbitrary"),
        name="route",
    )(logits_t, bias.reshape(N_EXP, 1))


def _dispatch_tables(pc_t, off_t, nb_max):
    nt = pc_t.shape[1]
    ids = jnp.arange(N_EXP, dtype=jnp.int32)
    psum = lambda a, m: jnp.dot(a, m, precision=_HI, preferred_element_type=F32)
    before_tile = (jnp.arange(nt)[:, None] < jnp.arange(nt)[None, :]).astype(F32)
    upto_expert = (ids[:, None] <= ids[None, :]).astype(F32)
    used_f = pc_t.sum(1)
    region_f = jnp.ceil(used_f * (1.0 / BM)) * BM
    ends_f = psum(region_f[None, :], upto_expert)[0]
    pc = pc_t.astype(jnp.int32)
    used, region, ends = used_f.astype(jnp.int32), region_f.astype(jnp.int32), ends_f.astype(jnp.int32)
    starts = ends - region
    dst = starts[:, None] + psum(pc_t, before_tile).astype(jnp.int32)
    nblk = ends[-1] // BM
    blk = jnp.minimum(jnp.arange(nb_max, dtype=jnp.int32), nblk - 1)
    blk_e = jnp.minimum(jnp.sum(ends[None, :] <= blk[:, None] * BM, axis=1), N_EXP - 1).astype(jnp.int32)
    has = region > 0
    slot_e = (psum(has.astype(F32)[None, :], upto_expert)[0].astype(jnp.int32) - 1) % 2
    later = has[None, :] & (ids[None, :] > ids[:, None])
    next_e = jnp.where(later.any(1), jnp.argmax(later, axis=1), -1)
    of_blk = lambda per_e: jnp.sum(jnp.where(blk_e[:, None] == ids[None, :], per_e[None, :], 0), axis=1)
    blocks = (blk_e, nblk.reshape(1).astype(jnp.int32), of_blk(slot_e).astype(jnp.int32),
              of_blk(next_e).astype(jnp.int32))
    flat = lambda a: a.reshape(-1).astype(jnp.int32)
    tail = ((starts + used).astype(jnp.int32), (region - used).astype(jnp.int32))
    off = off_t.astype(jnp.int32)
    e_idx = jnp.arange(N_EXP, dtype=jnp.int32)[:, None]
    col = lambda v: jnp.broadcast_to(jnp.asarray(v, jnp.int32), (N_EXP, 1))
    table = lambda so, do, ln: (flat(so), flat(do), flat(ln), ln.sum(0).astype(jnp.int32))
    pc_c = jnp.concatenate([pc, col(0)], 1)
    spare_c = _sorted_rows(TM_MOE) + SEG_ALIGN * e_idx
    seg_comb = table(jnp.where(pc_c > 0, jnp.concatenate([off, col(0)], 1), spare_c),
                     jnp.where(pc_c > 0, jnp.concatenate([dst, col(0)], 1), 0), jnp.maximum(pc_c, SEG_ALIGN))
    pc_d = jnp.concatenate([col(0), pc], 1)
    entry = jnp.arange(nt + 1, dtype=jnp.int32)[None, :]
    spare_d = nb_max * BM + ((entry % DISPATCH_SLOTS) * N_EXP + e_idx) * SEG_ALIGN
    seg_disp = table(jnp.where(pc_d > 0, jnp.concatenate([col(0), off], 1), 0),
                     jnp.where(pc_d > 0, jnp.concatenate([col(0), dst], 1), spare_d), jnp.maximum(pc_d, SEG_ALIGN))
    return seg_disp, seg_comb, tail, blocks


def _swiglu(u, w1, w3, w2):
    return _dot((_silu(_dot(u, w1)) * _dot(u, w3)).astype(BF16), w2)


def _round_bf16(x):
    return x.astype(BF16).astype(F32)


def _pack_pair(lo, hi):
    lo_b = lax.bitcast_convert_type(lo, jnp.uint32)
    hi_b = lax.bitcast_convert_type(hi, jnp.uint32)
    return (hi_b & jnp.uint32(0xFFFF0000)) | (lo_b >> 16)


def _unpack_pair(w):
    lo = lax.bitcast_convert_type(w << 16, F32).astype(BF16)
    hi = lax.bitcast_convert_type(w & jnp.uint32(0xFFFF0000), F32).astype(BF16)
    return lo, hi


def _segment_copies_dense(so_ref, do_ref, ln_ref, entry, entries, make_copy):
    for e in range(N_EXP):
        j = e * entries + entry
        make_copy(pl.multiple_of(so_ref[j], SEG_ALIGN), pl.multiple_of(do_ref[j], SEG_ALIGN),
                  pl.multiple_of(ln_ref[j], SEG_ALIGN)).start()


def _dispatch_kernel(nt, so_ref, do_ref, ln_ref, tt_ref, td_ref, tl_ref, nb_ref, u_ref, code_ref, segr_ref,
                     digits_ref, xs_out, buf, zbuf, sem):
    i = pl.program_id(0)
    r = buf.shape[1]
    nb_max = (xs_out.shape[0] - DISPATCH_SLOTS * N_EXP * SEG_ALIGN) // BM
    fill_sem = DISPATCH_SLOTS
    slot = i % DISPATCH_SLOTS
    prev_slot = (i + DISPATCH_SLOTS - 1) % DISPATCH_SLOTS

    def strip(slot_, so, do, ln):
        return pltpu.make_async_copy(buf.at[slot_, pl.ds(so, ln)], xs_out.at[pl.ds(do, ln)], sem.at[slot_])

    def wait_entry(slot_, entry):
        strip(slot_, 0, 0, pl.multiple_of(tt_ref[entry], SEG_ALIGN)).wait()

    def zero_fills(act):
        def tail(e, carry):
            ln = pl.multiple_of(tl_ref[e], SEG_ALIGN)

            @pl.when(ln > 0)
            def _():
                act(pltpu.make_async_copy(zbuf.at[pl.ds(0, ln)],
                                          xs_out.at[pl.ds(pl.multiple_of(td_ref[e], SEG_ALIGN), ln)], sem.at[fill_sem]))

            return carry

        lax.fori_loop(0, N_EXP, tail, 0)

        def blk(j, carry):
            act(pltpu.make_async_copy(zbuf.at[pl.ds(0, BM)], xs_out.at[pl.ds(pl.multiple_of(j * BM, BM), BM)],
                                      sem.at[fill_sem]))
            return carry

        lax.fori_loop(nb_ref[0], nb_max, blk, 0)

    @pl.when(i == 0)
    def _():
        zbuf[...] = jnp.zeros(zbuf.shape, jnp.uint32)
        buf[...] = jnp.zeros(buf.shape, jnp.uint32)
        spare = pltpu.make_async_copy(zbuf.at[pl.ds(0, DISPATCH_SLOTS * N_EXP * SEG_ALIGN)],
                                      xs_out.at[pl.ds(nb_max * BM, DISPATCH_SLOTS * N_EXP * SEG_ALIGN)],
                                      sem.at[fill_sem])
        spare.start()
        spare.wait()
        zero_fills(lambda cp: cp.start())

    @pl.when(i >= DISPATCH_SLOTS - 1)
    def _():
        wait_entry(slot, i - (DISPATCH_SLOTS - 1))

    row = lax.broadcasted_iota(jnp.int32, (r, GATE_LANES), 0).astype(F32)
    own = jnp.where(row >= segr_ref[0:1, :], jnp.where(row < segr_ref[1:2, :], 1.0, 0.0), 0.0).astype(BF16)
    z = _dot(jnp.concatenate([own, digits_ref[...]], axis=1), code_ref[...])
    perm = jnp.where(z == 0.0, 1.0, 0.0).astype(BF16)
    step = 256
    for c in range(0, PK, step):
        buf[slot, :, c:c + step] = _pack_pair(_dot(perm, u_ref[:, c:c + step]),
                                              _dot(perm, u_ref[:, PK + c:PK + c + step]))
    _segment_copies_dense(so_ref, do_ref, ln_ref, i, nt + 1, functools.partial(strip, prev_slot))

    @pl.when(i == nt - 1)
    def _():
        _segment_copies_dense(so_ref, do_ref, ln_ref, nt, nt + 1, functools.partial(strip, slot))
        for entry in range(max(nt - DISPATCH_SLOTS + 1, 0), nt + 1):
            wait_entry((entry + DISPATCH_SLOTS - 1) % DISPATCH_SLOTS, entry)
        zero_fills(lambda cp: cp.wait())


def _expert_kernel(layer, be_ref, nb_ref, ws_ref, nx_ref, x_ref, w1_hbm, w3_hbm, w2_hbm, y_ref,
                   w1f, w3f, w2f, w1b, w3b, w2b, sem):
    i = pl.program_id(0)
    live = i < nb_ref[0]

    def fetch(e, slot):
        return [pltpu.make_async_copy(src.at[layer, e], dst.at[slot], sem.at[slot, j])
                for j, (src, dst) in enumerate(((w1_hbm, w1f), (w3_hbm, w3f), (w2_hbm, w2f)))]

    @pl.when(live & ((i == 0) | (be_ref[i] != be_ref[jnp.maximum(i - 1, 0)])))
    def _():
        slot = ws_ref[i]

        @pl.when(i == 0)
        def _():
            for cp in fetch(be_ref[0], slot):
                cp.start()

        for cp in fetch(be_ref[i], slot):
            cp.wait()
        w1b[...] = w1f[slot].astype(BF16)
        w3b[...] = w3f[slot].astype(BF16)
        w2b[...] = w2f[slot].astype(BF16)

        @pl.when(nx_ref[i] >= 0)
        def _():
            for cp in fetch(nx_ref[i], 1 - slot):
                cp.start()

    @pl.when(live)
    def _():
        up = []
        for r0 in range(0, BM, EXP_SUB):
            lo, hi = _unpack_pair(x_ref[r0:r0 + EXP_SUB, :])
            up.append((_dot(lo, w1b[:PK, :]) + _dot(hi, w1b[PK:, :]), _dot(lo, w3b[:PK, :]) + _dot(hi, w3b[PK:, :])))
        for j, (a, g) in enumerate(up):
            hid = (_silu(a) * g).astype(BF16)
            y_ref[j * EXP_SUB:(j + 1) * EXP_SUB, :] = _pack_pair(_round_bf16(_dot(hid, w2b[:, :PK])),
                                                                 _round_bf16(_dot(hid, w2b[:, PK:])))


def _combine_kernel(nt, so_ref, do_ref, ln_ref, tt_ref, code_t_ref, gate_t_ref, segc_ref, digits_ref, u_ref, ys_ref,
                    s1_ref, s3_ref, s2_ref, h_ref, n_ref, mod_ref, o_ref, buf, sem):
    i = pl.program_id(0)
    slot = i % 2
    r = digits_ref.shape[1]

    def strip(slot_, so, do, ln):
        return pltpu.make_async_copy(ys_ref.at[pl.ds(do, ln)], buf.at[slot_, pl.ds(so, ln)], sem.at[slot_])

    @pl.when(i == 0)
    def _():
        buf[...] = jnp.zeros(buf.shape, jnp.uint32)
        _segment_copies_dense(so_ref, do_ref, ln_ref, 0, nt + 1, functools.partial(strip, 0))

    col = lax.broadcasted_iota(jnp.int32, (2 * N_EXP, r), 1).astype(F32)
    own = jnp.where(col >= segc_ref[:, 0:1], jnp.where(col < segc_ref[:, 1:2], 1.0, 0.0), 0.0).astype(BF16)
    z = _dot(code_t_ref[...], jnp.concatenate([own, digits_ref[...]], axis=0))
    q = jnp.where(z == 0.0, _dot(gate_t_ref[...].astype(BF16), own), 0.0).astype(BF16)
    shared = _swiglu(u_ref[...], s1_ref[...], s3_ref[...], s2_ref[...])
    _segment_copies_dense(so_ref, do_ref, ln_ref, i + 1, nt + 1, functools.partial(strip, 1 - slot))

    strip(slot, 0, 0, pl.multiple_of(tt_ref[i], SEG_ALIGN)).wait()
    lo, hi = _unpack_pair(buf[slot, 0:r, :])
    moe = jnp.concatenate([_dot(q, lo), _dot(q, hi)], axis=1) + shared
    o_ref[...] = h_ref[...] + mod_ref[5:6, :] * _rms(moe, n_ref[3:4, :])

    @pl.when(i == nt - 1)
    def _():
        strip(1 - slot, 0, 0, pl.multiple_of(tt_ref[nt], SEG_ALIGN)).wait()


def _sorted_rows(tm):
    return -(-(TOP_K * tm + N_EXP * (SEG_ALIGN - 1)) // MXU_DIM) * MXU_DIM


def _max_blocks(n):
    nt = n // TM_MOE
    rows = TOP_K * n + N_EXP * (SEG_ALIGN - 1) * nt + N_EXP * (BM - SEG_ALIGN)
    return -(-rows // BM)


def _row_digits(r):
    idx = np.arange(r)
    digits = np.zeros((r, GATE_LANES), np.float32)
    digits[:, 0] = idx // POS_RADIX * POS_RADIX
    digits[:, 1] = idx % POS_RADIX
    return jnp.asarray(digits, BF16)


def _dispatch_call(seg, tail, nblk, u2, code, segr, nb_max):
    n = u2.shape[0]
    r = _sorted_rows(TM_MOE)
    spare = DISPATCH_SLOTS * N_EXP * SEG_ALIGN
    return pl.pallas_call(
        functools.partial(_dispatch_kernel, n // TM_MOE),
        grid_spec=pltpu.PrefetchScalarGridSpec(
            num_scalar_prefetch=7,
            grid=(n // TM_MOE,),
            in_specs=[
                pl.BlockSpec((TM_MOE, D), lambda i, *_: (i, 0)),
                pl.BlockSpec((4 * N_EXP, TM_MOE), lambda i, *_: (0, i)),
                pl.BlockSpec((None, SUBLANES, GATE_LANES), lambda i, *_: (i, 0, 0)),
                pl.BlockSpec((r, GATE_LANES), lambda i, *_: (0, 0)),
            ],
            out_specs=pl.BlockSpec(memory_space=pl.ANY),
            scratch_shapes=[pltpu.VMEM((DISPATCH_SLOTS, r, PK), jnp.uint32),
                            pltpu.VMEM((max(BM, spare), PK), jnp.uint32),
                            pltpu.SemaphoreType.DMA((DISPATCH_SLOTS + 1,))],
        ),
        out_shape=jax.ShapeDtypeStruct((nb_max * BM + spare, PK), jnp.uint32),
        compiler_params=_cparams("arbitrary"),
        name="dispatch",
    )(*seg, tail[0], tail[1], nblk, u2, code, segr, _row_digits(r))


def _expert_call(layer, blk, xs, w1, w3, w2, nb_max):
    row_blk = lambda i, be, nb, *_: (jnp.maximum(jnp.minimum(i, nb[0] - 1), 0), 0)
    hbm = pl.BlockSpec(memory_space=pl.ANY)
    return pl.pallas_call(
        functools.partial(_expert_kernel, layer),
        grid_spec=pltpu.PrefetchScalarGridSpec(
            num_scalar_prefetch=4,
            grid=(nb_max,),
            in_specs=[pl.BlockSpec((BM, PK), row_blk), hbm, hbm, hbm],
            out_specs=pl.BlockSpec((BM, PK), row_blk),
            scratch_shapes=[pltpu.VMEM((2, D, F_EXP), F32), pltpu.VMEM((2, D, F_EXP), F32),
                            pltpu.VMEM((2, F_EXP, D), F32),
                            pltpu.VMEM((D, F_EXP), BF16), pltpu.VMEM((D, F_EXP), BF16),
                            pltpu.VMEM((F_EXP, D), BF16), pltpu.SemaphoreType.DMA((2, 3))],
        ),
        out_shape=jax.ShapeDtypeStruct(xs.shape, jnp.uint32),
        input_output_aliases={4: 0},
        compiler_params=_cparams("arbitrary"),
        name="experts",
    )(*blk, xs, w1, w3, w2)


def _combine_call(tabs, code_t, gate_t, segc, u2, ys, s1, s3, s2, h, norms_l, mod_l, t):
    n = u2.shape[0]
    r = _sorted_rows(TM_MOE)
    tok = lambda w: pl.BlockSpec((TM_MOE, w), lambda i, *_: (i, 0))
    const = lambda shp: pl.BlockSpec(shp, lambda i, *_: (0,) * len(shp))
    return pl.pallas_call(
        functools.partial(_combine_kernel, n // TM_MOE),
        grid_spec=pltpu.PrefetchScalarGridSpec(
            num_scalar_prefetch=4,
            grid=(n // TM_MOE,),
            in_specs=[
                tok(4 * N_EXP),
                tok(GATE_LANES),
                pl.BlockSpec((None, 2 * N_EXP, GATE_LANES), lambda i, *_: (i, 0, 0)),
                const((2 * N_EXP, r)),
                tok(D),
                pl.BlockSpec(memory_space=pl.ANY),
                const((D, F_EXP)), const((D, F_EXP)), const((F_EXP, D)),
                tok(D),
                const((4, D)),
                pl.BlockSpec((None, 6, D), lambda i, *_: (i * TM_MOE // t, 0, 0)),
            ],
            out_specs=tok(D),
            scratch_shapes=[pltpu.VMEM((2, r + N_EXP * SEG_ALIGN, PK), jnp.uint32),
                            pltpu.SemaphoreType.DMA((2,))],
        ),
        out_shape=jax.ShapeDtypeStruct((n, D), F32),
        compiler_params=_cparams("arbitrary"),
        name="combine",
    )(*tabs, code_t, gate_t, segc, _row_digits(r).T, u2, ys, s1, s3, s2, h, norms_l, mod_l)


def _glu_kernel(h_ref, g_ref, mod_ref, w_ref, b_ref, o_ref):
    u = (_rms(h_ref[...], g_ref[...]) * (1.0 + mod_ref[1:2, :]) + mod_ref[0:1, :]).astype(BF16)
    a = _dot(u, w_ref[:, :D]) + b_ref[:, :D]
    gate = _dot(u, w_ref[:, D:]) + b_ref[:, D:]
    o_ref[...] = (a * jax.nn.sigmoid(gate)).astype(BF16)


def _glu_call(h, g, mod_l, w1, b1, tm):
    b, t, _ = h.shape
    tok = lambda n: pl.BlockSpec((None, tm, n), lambda i, j: (i, j, 0))
    return pl.pallas_call(
        _glu_kernel,
        grid=(b, t // tm),
        in_specs=[
            tok(D),
            pl.BlockSpec((1, D), lambda i, j: (0, 0)),
            pl.BlockSpec((None, 6, D), lambda i, j: (i, 0, 0)),
            pl.BlockSpec((D, 2 * D), lambda i, j: (0, 0)),
            pl.BlockSpec((1, 2 * D), lambda i, j: (0, 0)),
        ],
        out_specs=tok(D),
        out_shape=jax.ShapeDtypeStruct((b, t, D), BF16),
        compiler_params=_cparams("parallel", "parallel"),
        name="glu",
    )(h, g, mod_l, w1, b1)


def _conv_kernel(cur_ref, prev_ref, next_ref, dw_ref, cv_ref, w2_ref, h_ref, n_ref, mod_ref, wr_ref,
                 h_out, u_out, lg_out, win, shifted, conv):
    j = pl.program_id(1)
    tm = cur_ref.shape[0]
    pad = CONV_W // 2
    win[0:HALO, :] = jnp.where(j > 0, prev_ref[...].astype(F32), 0.0)
    win[HALO:HALO + tm, :] = cur_ref[...].astype(F32)
    win[HALO + tm:, :] = jnp.where(j < pl.num_programs(1) - 1, next_ref[...].astype(F32), 0.0)
    ext = shifted.shape[1]
    full = win[...]
    shifted[0] = full[:ext, :]
    for s in range(1, SUBLANES):
        shifted[s] = pltpu.roll(full, full.shape[0] - s, axis=0)[:ext, :]

    def rows(ib, carry):
        r0 = pl.multiple_of(ib * 2 * SUBLANES, 2 * SUBLANES)
        for c0 in range(0, D, CONV_LANES):
            cols = slice(c0, c0 + CONV_LANES)
            acc0 = jnp.zeros((SUBLANES, CONV_LANES), F32)
            acc1 = jnp.zeros((SUBLANES, CONV_LANES), F32)
            for s in range(SUBLANES):
                taps = [(off // SUBLANES, off - (HALO - pad)) for off in range(HALO - pad, HALO - pad + CONV_W)
                        if off % SUBLANES == s]
                tiles = {a: shifted[s, pl.ds(r0 + a * SUBLANES, SUBLANES), cols]
                         for a in range(taps[0][0], taps[-1][0] + 2)}
                for a, tap in taps:
                    w = dw_ref[tap * SUBLANES:(tap + 1) * SUBLANES, cols]
                    acc0 = acc0 + tiles[a] * w
                    acc1 = acc1 + tiles[a + 1] * w
            conv[pl.ds(r0, SUBLANES), cols] = acc0
            conv[pl.ds(r0 + SUBLANES, SUBLANES), cols] = acc1
        return carry

    lax.fori_loop(0, tm // (2 * SUBLANES), rows, 0)
    uf = conv[...] + cv_ref[0:1, :]
    mu = jnp.mean(uf, axis=-1, keepdims=True)
    var = jnp.mean(jnp.square(uf - mu), axis=-1, keepdims=True)
    uf = (uf - mu) * lax.rsqrt(var + LN_EPS) * cv_ref[1:2, :] + cv_ref[2:3, :]
    y = _dot(_silu(uf).astype(BF16), w2_ref[...]) + cv_ref[3:4, :]
    _mixer_epilogue(y, h_ref, n_ref, mod_ref, wr_ref, h_out, u_out, lg_out)


def _conv_call(glu, dw, cvec, w2, h, norms_l, mod_l, wr_t, tm):
    b, t, _ = h.shape
    tok, ep_in, wr_spec, out_specs, out_shape = _epilogue_specs(b, t, tm)
    r = tm // HALO
    last = t // HALO - 1
    return pl.pallas_call(
        _conv_kernel,
        grid=(b, t // tm),
        in_specs=[
            tok(D),
            pl.BlockSpec((None, HALO, D), lambda i, j: (i, jnp.maximum(j * r - 1, 0), 0)),
            pl.BlockSpec((None, HALO, D), lambda i, j: (i, jnp.minimum((j + 1) * r, last), 0)),
            pl.BlockSpec((CONV_W * SUBLANES, D), lambda i, j: (0, 0)),
            pl.BlockSpec((4, D), lambda i, j: (0, 0)),
            pl.BlockSpec((D, D), lambda i, j: (0, 0)),
        ] + ep_in + [wr_spec],
        out_specs=out_specs,
        out_shape=out_shape,
        scratch_shapes=[pltpu.VMEM((tm + 2 * HALO, D), F32),
                        pltpu.VMEM((SUBLANES, tm + (CONV_W // SUBLANES) * SUBLANES, D), F32),
                        pltpu.VMEM((tm, D), F32)],
        compiler_params=_cparams("parallel", "parallel"),
        name="conv",
    )(glu, glu, glu, dw, cvec, w2, h, norms_l, mod_l, wr_t)


def _tile(t, pref):
    return pref if t % pref == 0 else t


def kernel(x, c, ctx, c_ctx, mod_w, mod_b, norms, ret_w_in, ret_w_out, ret_decay, conv_w1, conv_b1, conv_dw,
           conv_b_dw, conv_ln_g, conv_ln_b, conv_w2, conv_b2, moe_router, moe_bias, moe_w1, moe_w3, moe_w2,
           shared_w1, shared_w3, shared_w2):
    b, t, _ = x.shape
    l = ctx.shape[1]
    tm = _tile(t, 512)

    rows = -(-(b + 1) // 8) * 8
    cs = jnp.concatenate([c, c_ctx[None, :], jnp.zeros((rows - b - 1, D), F32)], 0)
    mod = _mod_call(cs, mod_w, mod_b).reshape(mod_w.shape[0], rows, 6, D)

    idx = jnp.arange(t, dtype=jnp.int32)
    pos_lat = jnp.stack([jnp.full((t,), l, jnp.int32), idx // GRID_W, idx % GRID_W], -1)
    zl = jnp.zeros((l,), jnp.int32)
    pos_ctx = jnp.stack([jnp.arange(l, dtype=jnp.int32), zl, zl], -1)
    ang_lat, ang_ctx = _rope_angles(pos_lat), _rope_angles(pos_ctx)

    w_qk = ret_w_in[0][:, :2 * QK_TOT].reshape(D, 2 * HEADS, DK // 2, 2).swapaxes(-1, -2).reshape(D, 2 * QK_TOT)
    w_in = jnp.concatenate([w_qk, ret_w_in[0][:, 2 * QK_TOT:]], axis=1).astype(BF16)
    log_g = jax.nn.log_sigmoid(ret_decay[0].astype(F32))

    n = b * t
    nb_max = _max_blocks(n)

    def moe_layer(i, h1, u2, logits_t):
        code, code_t, gate_t, segc, segr, pc_t, off_t = _route_call(logits_t, moe_bias[i])
        seg_disp, seg_comb, tail, blocks = _dispatch_tables(pc_t, off_t, nb_max)
        u2f = u2.reshape(n, D)
        xs = _dispatch_call(seg_disp, tail, blocks[1], u2f, code, segr, nb_max)
        ys = _expert_call(i, blocks, xs, moe_w1, moe_w3, moe_w2, nb_max)
        out = _combine_call(seg_comb, code_t, gate_t, segc, u2f, ys, shared_w1[i].astype(BF16), shared_w3[i].astype(BF16),
                            shared_w2[i].astype(BF16), h1.reshape(n, D), norms[i], mod[i], t)
        return out.reshape(b, t, D)

    q, k, v, gt = _inproj_call(x, norms[0, 0:1], mod[0], w_in, jnp.cos(ang_lat), jnp.sin(ang_lat), tm)
    sf0, sb0 = _ctxstate_call(log_g, ctx, norms[0, 0:1], mod[0, b], w_in[:, QK_TOT:2 * QK_TOT + V_TOT],
                              jnp.cos(ang_ctx), jnp.sin(ang_ctx))
    yn = _retention_call(log_g, q, k, v, sf0, sb0)
    h1, u2, logits_t = _readout_call(yn, gt, x, norms[0], mod[0], ret_w_out[0].astype(BF16),
                                     moe_router[0].T, tm)
    h2 = moe_layer(0, h1, u2, logits_t)

    glu = _glu_call(h2, norms[1, 0:1], mod[1], conv_w1[0].astype(BF16), conv_b1[0][None, :], tm)
    cvec = jnp.stack([conv_b_dw[0], conv_ln_g[0], conv_ln_b[0], conv_b2[0]], 0)
    dw_rows = jnp.repeat(conv_dw[0], SUBLANES, axis=0)
    h3, u2, logits_t = _conv_call(glu, dw_rows, cvec, conv_w2[0].astype(BF16), h2, norms[1], mod[1],
                                  moe_router[1].T, tm)
    return moe_layer(1, h3, u2, logits_t)
```

```python
import functools

import jax
import jax.numpy as jnp
import numpy as np
from jax import lax
from jax.experimental import pallas as pl
from jax.experimental.pallas import tpu as pltpu

F32 = jnp.float32
BF16 = jnp.bfloat16

D = 1024
HEADS = 4
DK = D // HEADS
DV = 2 * DK
QK_TOT = HEADS * DK
V_TOT = HEADS * DV
IN_DIM = 2 * QK_TOT + 2 * V_TOT
ROPE_AXES = (64, 96, 96)
ROPE_THETA = 10000.0
GRID_W = 64
CONV_W = 31
HALO = 16
SUBLANES = 8
CONV_LANES = 256
N_EXP = 64
N_GRP = 8
PER_GRP = N_EXP // N_GRP
TOP_K = 8
TOP_G = 4
F_EXP = 256
ROUTE_SCALE = 2.5
GATE_LANES = 128
MXU_DIM = 256
TM_MOE = 256
ROUTE_TILES = 4
SEG_ALIGN = 8
POS_RADIX = 256
POS_SCALE = 8
BM = 1024
DISPATCH_SLOTS = 3
EXP_SUB = 256
PK = D // 2
NORM_EPS = 1e-6
LN_EPS = 1e-5
RET_CHUNK = 256
EPI_ROWS = 512
VMEM_LIMIT = 56 * 1024 * 1024

_HI = lax.Precision.HIGHEST


def _cparams(*sem):
    return pltpu.CompilerParams(dimension_semantics=sem, vmem_limit_bytes=VMEM_LIMIT)


def _dot(a, b):
    return jnp.dot(a, b, preferred_element_type=F32)


def _rms(xf, g):
    return xf * lax.rsqrt(jnp.mean(xf * xf, axis=-1, keepdims=True) + NORM_EPS) * g


def _silu(x):
    return x * jax.nn.sigmoid(x)


def _mod_kernel(cs_ref, w_ref, b_ref, o_ref):
    s = _silu(cs_ref[...])
    o_ref[...] = jnp.dot(s, w_ref[...], preferred_element_type=F32, precision=_HI) + b_ref[...]


def _mod_call(cs, mod_w, mod_b):
    depth, _, n6 = mod_w.shape
    tn = 1536
    rows = cs.shape[0]
    return pl.pallas_call(
        _mod_kernel,
        grid=(depth, n6 // tn),
        in_specs=[
            pl.BlockSpec((rows, D), lambda i, j: (0, 0)),
            pl.BlockSpec((None, D, tn), lambda i, j: (i, 0, j)),
            pl.BlockSpec((None, 1, tn), lambda i, j: (i, 0, j)),
        ],
        out_specs=pl.BlockSpec((None, rows, tn), lambda i, j: (i, 0, j)),
        out_shape=jax.ShapeDtypeStruct((depth, rows, n6), F32),
        compiler_params=_cparams("parallel", "parallel"),
        name="mod",
    )(cs, mod_w, mod_b.reshape(depth, 1, n6))


def _rope_angles(pos):
    parts = []
    for a, d in enumerate(ROPE_AXES):
        inv = ROPE_THETA ** (-jnp.arange(0, d, 2, dtype=F32) / d)
        parts.append(pos[:, a:a + 1].astype(F32) * inv[None, :])
    return jnp.concatenate(parts, -1)


def _rope_store(z, cos, sin, scale, o_ref, col):
    half = DK // 2
    x0, x1 = z[:, :half], z[:, half:]
    o_ref[:, col:col + half] = ((x0 * cos - x1 * sin) * scale).astype(o_ref.dtype)
    o_ref[:, col + half:col + DK] = ((x0 * sin + x1 * cos) * scale).astype(o_ref.dtype)


def _inproj_kernel(x_ref, g_ref, mod_ref, w_ref, cos_ref, sin_ref, q_ref, k_ref, v_ref, gt_ref):
    u = (_rms(x_ref[...], g_ref[...]) * (1.0 + mod_ref[1:2, :]) + mod_ref[0:1, :]).astype(BF16)
    cos, sin = cos_ref[...], sin_ref[...]
    for h in range(HEADS):
        _rope_store(_dot(u, w_ref[:, h * DK:(h + 1) * DK]), cos, sin, DK ** -0.5, q_ref, h * DK)
        _rope_store(_dot(u, w_ref[:, QK_TOT + h * DK:QK_TOT + (h + 1) * DK]), cos, sin, 1.0, k_ref, h * DK)
    step = 512
    for j in range(V_TOT // step):
        c0 = 2 * QK_TOT + j * step
        v_ref[:, j * step:(j + 1) * step] = _dot(u, w_ref[:, c0:c0 + step]).astype(BF16)
        c1 = 2 * QK_TOT + V_TOT + j * step
        gt_ref[:, j * step:(j + 1) * step] = _silu(_dot(u, w_ref[:, c1:c1 + step])).astype(BF16)


def _inproj_call(x, g, mod_l, w_in, cos, sin, tm):
    b, t, _ = x.shape
    grid = (b, t // tm)
    tok = lambda n: pl.BlockSpec((None, tm, n), lambda i, j: (i, j, 0))
    return pl.pallas_call(
        _inproj_kernel,
        grid=grid,
        in_specs=[
            tok(D),
            pl.BlockSpec((1, D), lambda i, j: (0, 0)),
            pl.BlockSpec((None, 6, D), lambda i, j: (i, 0, 0)),
            pl.BlockSpec((D, IN_DIM), lambda i, j: (0, 0), pipeline_mode=pl.Buffered(1)),
            pl.BlockSpec((tm, DK // 2), lambda i, j: (j, 0)),
            pl.BlockSpec((tm, DK // 2), lambda i, j: (j, 0)),
        ],
        out_specs=[tok(QK_TOT), tok(QK_TOT), tok(V_TOT), tok(V_TOT)],
        out_shape=[
            jax.ShapeDtypeStruct((b, t, QK_TOT), BF16),
            jax.ShapeDtypeStruct((b, t, QK_TOT), BF16),
            jax.ShapeDtypeStruct((b, t, V_TOT), BF16),
            jax.ShapeDtypeStruct((b, t, V_TOT), BF16),
        ],
        compiler_params=_cparams("parallel", "parallel"),
        name="inproj",
    )(x, g, mod_l, w_in, cos, sin)


def _row_pow(lg, expo_fn, rows, cols):
    i = lax.broadcasted_iota(jnp.int32, (rows, cols), 0).astype(F32)
    return jnp.exp(lg * expo_fn(i))


def _ctxstate_kernel(lg_ref, c_ref, g_ref, mod_ref, w_ref, cos_ref, sin_ref, sf_ref, sb_ref, k_scr):
    l = c_ref.shape[0]
    u = (_rms(c_ref[...], g_ref[...]) * (1.0 + mod_ref[1:2, :]) + mod_ref[0:1, :]).astype(BF16)
    cos, sin = cos_ref[...], sin_ref[...]
    for h in range(HEADS):
        _rope_store(_dot(u, w_ref[:, h * DK:(h + 1) * DK]), cos, sin, 1.0, k_scr, h * DK)
    for h in range(HEADS):
        kh = k_scr[:, h * DK:(h + 1) * DK]
        vh = _dot(u, w_ref[:, QK_TOT + h * DV:QK_TOT + (h + 1) * DV]).astype(BF16)
        wf = _row_pow(lg_ref[0, h], lambda i: (l - 1.0) - i, l, DK)
        wb = _row_pow(lg_ref[1, h], lambda i: i, l, DK)
        sf_ref[h] = _dot((kh * wf).T.astype(BF16), vh)
        sb_ref[h] = _dot((kh * wb).T.astype(BF16), vh)


def _ctxstate_call(log_g, ctx, g, mod_row, w_kv, cos, sin):
    b, l, _ = ctx.shape
    st = jax.ShapeDtypeStruct((b, HEADS, DK, DV), F32)
    st_spec = pl.BlockSpec((None, HEADS, DK, DV), lambda i: (i, 0, 0, 0))
    return pl.pallas_call(
        _ctxstate_kernel,
        grid=(b,),
        in_specs=[
            pl.BlockSpec(memory_space=pltpu.SMEM),
            pl.BlockSpec((None, l, D), lambda i: (i, 0, 0)),
            pl.BlockSpec((1, D), lambda i: (0, 0)),
            pl.BlockSpec((6, D), lambda i: (0, 0)),
            pl.BlockSpec((D, QK_TOT + V_TOT), lambda i: (0, 0)),
            pl.BlockSpec((l, DK // 2), lambda i: (0, 0)),
            pl.BlockSpec((l, DK // 2), lambda i: (0, 0)),
        ],
        out_specs=[st_spec, st_spec],
        out_shape=[st, st],
        scratch_shapes=[pltpu.VMEM((l, QK_TOT), F32)],
        compiler_params=_cparams("parallel"),
        name="ctxstate",
    )(log_g, ctx, g, mod_row, w_kv, cos, sin)


def _retention_kernel(lg_ref, q_ref, k_ref, v_ref, sf0_ref, sb0_ref, o_ref, acc, sf, sb):
    h = pl.program_id(1)
    t = q_ref.shape[0]
    c = RET_CHUNK
    nc = t // c
    lgf, lgb = lg_ref[0, h], lg_ref[1, h]
    ri = lax.broadcasted_iota(jnp.int32, (c, c), 0).astype(F32)
    ci = lax.broadcasted_iota(jnp.int32, (c, c), 1).astype(F32)
    rel = ri - ci
    mask = (jnp.where(rel >= 0, jnp.exp(lgf * jnp.maximum(rel, 0.0)), 0.0)
            + jnp.where(rel <= 0, jnp.exp(lgb * jnp.maximum(-rel, 0.0)), 0.0))
    qdf = _row_pow(lgf, lambda i: i + 1.0, c, DV)
    qdb = _row_pow(lgb, lambda i: c - i, c, DV)
    kdf = _row_pow(lgf, lambda i: (c - 1.0) - i, c, DK)
    kdb = _row_pow(lgb, lambda i: i, c, DK)
    cdf = jnp.exp(jnp.full((1, DV), lgf * c, F32))
    cdb = jnp.exp(jnp.full((1, DV), lgb * c, F32))
    sf[...] = sf0_ref[...]
    sb[...] = sb0_ref[...]

    def rows(ic):
        return pl.ds(pl.multiple_of(ic * c, c), c)

    def fwd(rs):
        q, k, v = q_ref[rs, :], k_ref[rs, :], v_ref[rs, :]
        s = lax.dot_general(q, k, (((1,), (1,)), ((), ())), preferred_element_type=F32)
        o = _dot((s * mask).astype(BF16), v) + qdf * _dot(q, sf[...].astype(BF16))
        sf[...] = sf[...] * cdf + _dot((k.astype(F32) * kdf).T.astype(BF16), v)
        return o

    def bwd(rs):
        q, k, v = q_ref[rs, :], k_ref[rs, :], v_ref[rs, :]
        o = qdb * _dot(q, sb[...].astype(BF16))
        sb[...] = sb[...] * cdb + _dot((k.astype(F32) * kdb).T.astype(BF16), v)
        return o

    def finish(rs, o):
        o_ref[rs, :] = (o * lax.rsqrt(jnp.mean(o * o, axis=-1, keepdims=True) + NORM_EPS)).astype(BF16)

    def first_half(j, carry):
        acc[rows(j), :] = fwd(rows(j))
        acc[rows(nc - 1 - j), :] = bwd(rows(nc - 1 - j))
        return carry

    def second_half(j, carry):
        finish(rows(j), acc[rows(j), :] + fwd(rows(j)))
        finish(rows(nc - 1 - j), acc[rows(nc - 1 - j), :] + bwd(rows(nc - 1 - j)))
        return carry

    lax.fori_loop(0, nc // 2, first_half, 0)
    lax.fori_loop(nc // 2, nc, second_half, 0)


def _retention_call(log_g, q, k, v, sf0, sb0):
    b, t, _ = q.shape
    qk_spec = pl.BlockSpec((None, t, DK), lambda i, h: (i, 0, h))
    v_spec = pl.BlockSpec((None, t, DV), lambda i, h: (i, 0, h))
    s_spec = pl.BlockSpec((None, None, DK, DV), lambda i, h: (i, h, 0, 0))
    return pl.pallas_call(
        _retention_kernel,
        grid=(b, HEADS),
        in_specs=[pl.BlockSpec(memory_space=pltpu.SMEM), qk_spec, qk_spec, v_spec, s_spec, s_spec],
        out_specs=v_spec,
        out_shape=jax.ShapeDtypeStruct((b, t, V_TOT), BF16),
        scratch_shapes=[pltpu.VMEM((t, DV), F32), pltpu.VMEM((DK, DV), F32), pltpu.VMEM((DK, DV), F32)],
        compiler_params=_cparams("parallel", "parallel"),
        name="retention",
    )(log_g, q, k, v, sf0, sb0)


def _mixer_epilogue(y, rows, h_ref, n_ref, mod_ref, wr_ref, h_out, u_out, lg_out):
    h1 = h_ref[rows, :] + mod_ref[2:3, :] * _rms(y, n_ref[1:2, :])
    h_out[rows, :] = h1
    u2 = _rms(h1, n_ref[2:3, :]) * (1.0 + mod_ref[4:5, :]) + mod_ref[3:4, :]
    u_hi = u2.astype(BF16)
    u_out[rows, :] = u_hi
    u_lo = (u2 - u_hi.astype(F32)).astype(BF16)
    w = wr_ref[...]
    w_hi = w.astype(BF16)
    w_lo = (w - w_hi.astype(F32)).astype(BF16)
    nt_dot = lambda a, b: lax.dot_general(a, b, (((1,), (1,)), ((), ())), preferred_element_type=F32)
    lg_out[:, rows] = nt_dot(w_hi, u_hi) + (nt_dot(w_hi, u_lo) + nt_dot(w_lo, u_hi))


def _row_blocks(tm):
    return [slice(r0, r0 + EPI_ROWS) for r0 in range(0, tm, EPI_ROWS)]


def _readout_kernel(y_ref, gt_ref, h_ref, n_ref, mod_ref, w_ref, wr_ref, h_out, u_out, lg_out):
    blocks = _row_blocks(y_ref.shape[0])
    ys = [_dot(y_ref[rows, :] * gt_ref[rows, :], w_ref[...]) for rows in blocks]
    for rows, y in zip(blocks, ys):
        _mixer_epilogue(y, rows, h_ref, n_ref, mod_ref, wr_ref, h_out, u_out, lg_out)


def _epilogue_specs(b, t, tm):
    nt = t // tm
    tok = lambda n: pl.BlockSpec((None, tm, n), lambda i, j: (i, j, 0))
    in_specs = [
        tok(D),
        pl.BlockSpec((4, D), lambda i, j: (0, 0)),
        pl.BlockSpec((None, 6, D), lambda i, j: (i, 0, 0)),
    ]
    wr_spec = pl.BlockSpec((N_EXP, D), lambda i, j: (0, 0))
    out_specs = [tok(D), tok(D), pl.BlockSpec((N_EXP, tm), lambda i, j: (0, i * nt + j))]
    out_shape = [
        jax.ShapeDtypeStruct((b, t, D), F32),
        jax.ShapeDtypeStruct((b, t, D), BF16),
        jax.ShapeDtypeStruct((N_EXP, b * t), F32),
    ]
    return tok, in_specs, wr_spec, out_specs, out_shape


def _readout_call(yn, gt, h, norms_l, mod_l, w_out, wr_t, tm):
    b, t, _ = h.shape
    tok, ep_in, wr_spec, out_specs, out_shape = _epilogue_specs(b, t, tm)
    return pl.pallas_call(
        _readout_kernel,
        grid=(b, t // tm),
        in_specs=[tok(V_TOT), tok(V_TOT)] + ep_in
        + [pl.BlockSpec((V_TOT, D), lambda i, j: (0, 0), pipeline_mode=pl.Buffered(1)), wr_spec],
        out_specs=out_specs,
        out_shape=out_shape,
        compiler_params=_cparams("parallel", "parallel"),
        name="readout",
    )(yn, gt, h, norms_l, mod_l, w_out, wr_t)


def _route_tile(logits, bias):
    tr = logits.shape[1]
    s = jax.nn.sigmoid(logits)
    biased = s + bias
    neg = -jnp.inf
    b3 = biased.reshape(N_GRP, PER_GRP, tr)
    io3 = lax.broadcasted_iota(jnp.int32, b3.shape, 1).astype(F32)
    m1 = jnp.max(b3, axis=1, keepdims=True)
    i1 = jnp.min(jnp.where(b3 == m1, io3, float(PER_GRP)), axis=1, keepdims=True)
    m2 = jnp.max(jnp.where(io3 == i1, neg, b3), axis=1, keepdims=True)
    gs = (m1 + m2).reshape(N_GRP, tr)
    iog = lax.broadcasted_iota(jnp.int32, gs.shape, 0).astype(F32)
    gsel = jnp.zeros(gs.shape, F32)
    work = gs
    for _ in range(TOP_G):
        m = jnp.max(work, axis=0, keepdims=True)
        gi = jnp.min(jnp.where(work == m, iog, float(N_GRP)), axis=0, keepdims=True)
        hit = iog == gi
        gsel = jnp.where(hit, 1.0, gsel)
        work = jnp.where(hit, neg, work)
    emask = jnp.broadcast_to(gsel.reshape(N_GRP, 1, tr), b3.shape).reshape(N_EXP, tr)
    work = jnp.where(emask > 0.0, biased, neg)
    ioe = lax.broadcasted_iota(jnp.int32, work.shape, 0).astype(F32)
    esel = jnp.zeros(work.shape, F32)
    for _ in range(TOP_K):
        m = jnp.max(work, axis=0, keepdims=True)
        ei = jnp.min(jnp.where(work == m, ioe, float(N_EXP)), axis=0, keepdims=True)
        hit = ioe == ei
        esel = jnp.where(hit, 1.0, esel)
        work = jnp.where(hit, neg, work)
    wsel = esel * s
    gate = wsel / jnp.sum(wsel, axis=0, keepdims=True) * ROUTE_SCALE

    t_r = lax.broadcasted_iota(jnp.int32, (tr, tr), 0)
    t_c = lax.broadcasted_iota(jnp.int32, (tr, tr), 1)
    rank = _dot(esel.astype(BF16), jnp.where(t_r < t_c, 1.0, 0.0).astype(BF16))
    cnt = jnp.sum(esel, axis=1, keepdims=True)
    pc = jnp.floor((cnt + (SEG_ALIGN - 1.0)) * (1.0 / SEG_ALIGN)) * SEG_ALIGN
    e_r = lax.broadcasted_iota(jnp.int32, (N_EXP, N_EXP), 0)
    e_c = lax.broadcasted_iota(jnp.int32, (N_EXP, N_EXP), 1)
    off = jnp.dot(jnp.where(e_c < e_r, 1.0, 0.0), jnp.broadcast_to(pc, (N_EXP, 128)),
                  preferred_element_type=F32, precision=_HI)[:, 0:1]
    p = off + rank
    p_hi = jnp.where(esel > 0.0, jnp.floor(p * (1.0 / POS_RADIX)) * POS_RADIX, -float(POS_RADIX))
    p_lo = jnp.where(esel > 0.0, p - p_hi, 0.0)
    minus_r = jnp.where(lax.broadcasted_iota(jnp.int32, (N_EXP, tr), 0) < 2, -float(POS_SCALE), 0.0)
    code = jnp.concatenate([p_hi * POS_SCALE, p_lo * POS_SCALE, gate, minus_r], 0)
    code_t = jnp.concatenate([code[:2 * N_EXP].T, code[2 * N_EXP:].T], 1)
    lane = lax.broadcasted_iota(jnp.int32, (2 * N_EXP, GATE_LANES), 1)
    off2 = jnp.concatenate([off, off], 0)
    end2 = jnp.concatenate([off + pc, off + pc], 0)
    bounds = jnp.where(lane == 0, off2, jnp.where(lane == 1, end2, 0.0))
    return code.astype(BF16), code_t.astype(BF16), bounds, bounds.T[0:SUBLANES, :], pc, off


def _route_kernel(lg_ref, b_ref, code_ref, code_t_ref, segc_ref, segr_ref, pc_ref, off_ref):
    i = pl.program_id(0)

    @pl.when(i == 0)
    def _():
        pc_ref[...] = jnp.zeros(pc_ref.shape, F32)
        off_ref[...] = jnp.zeros(off_ref.shape, F32)

    tile_lane = lax.broadcasted_iota(jnp.int32, pc_ref.shape, 1)
    for k in range(ROUTE_TILES):
        cols = slice(k * TM_MOE, (k + 1) * TM_MOE)
        code, code_t, bounds, bounds_t, pc, off = _route_tile(lg_ref[:, cols], b_ref[...])
        code_ref[:, cols] = code
        code_t_ref[cols, :] = code_t
        segc_ref[k] = bounds
        segr_ref[k] = bounds_t
        pc_ref[...] = jnp.where(tile_lane == i * ROUTE_TILES + k, pc, pc_ref[...])
        off_ref[...] = jnp.where(tile_lane == i * ROUTE_TILES + k, off, off_ref[...])


def _route_call(logits_t, bias):
    n = logits_t.shape[1]
    nt = n // TM_MOE
    tr = ROUTE_TILES * TM_MOE
    return pl.pallas_call(
        _route_kernel,
        grid=(nt // ROUTE_TILES,),
        in_specs=[pl.BlockSpec((N_EXP, tr), lambda i: (0, i)), pl.BlockSpec((N_EXP, 1), lambda i: (0, 0))],
        out_specs=[
            pl.BlockSpec((4 * N_EXP, tr), lambda i: (0, i)),
            pl.BlockSpec((tr, 4 * N_EXP), lambda i: (i, 0)),
            pl.BlockSpec((ROUTE_TILES, 2 * N_EXP, GATE_LANES), lambda i: (i, 0, 0)),
            pl.BlockSpec((ROUTE_TILES, SUBLANES, GATE_LANES), lambda i: (i, 0, 0)),
            pl.BlockSpec((N_EXP, nt), lambda i: (0, 0)),
            pl.BlockSpec((N_EXP, nt), lambda i: (0, 0)),
        ],
        out_shape=[
            jax.ShapeDtypeStruct((4 * N_EXP, n), BF16),
            jax.ShapeDtypeStruct((n, 4 * N_EXP), BF16),
            jax.ShapeDtypeStruct((nt, 2 * N_EXP, GATE_LANES), F32),
            jax.ShapeDtypeStruct((nt, SUBLANES, GATE_LANES), F32),
            jax.ShapeDtypeStruct((N_EXP, nt), F32),
            jax.ShapeDtypeStruct((N_EXP, nt), F32),
        ],
        compiler_params=_cparams("arbitrary"),
        name="route",
    )(logits_t, bias.reshape(N_EXP, 1))


def _dispatch_tables(pc_t, off_t, nb_max):
    nt = pc_t.shape[1]
    ids = jnp.arange(N_EXP, dtype=jnp.int32)
    psum = lambda a, m: jnp.dot(a, m, precision=_HI, preferred_element_type=F32)
    before_tile = (jnp.arange(nt)[:, None] < jnp.arange(nt)[None, :]).astype(F32)
    upto_expert = (ids[:, None] <= ids[None, :]).astype(F32)
    used_f = pc_t.sum(1)
    region_f = jnp.ceil(used_f * (1.0 / BM)) * BM
    ends_f = psum(region_f[None, :], upto_expert)[0]
    pc = pc_t.astype(jnp.int32)
    used, region, ends = used_f.astype(jnp.int32), region_f.astype(jnp.int32), ends_f.astype(jnp.int32)
    starts = ends - region
    dst = starts[:, None] + psum(pc_t, before_tile).astype(jnp.int32)
    nblk = ends[-1] // BM
    blk = jnp.minimum(jnp.arange(nb_max, dtype=jnp.int32), nblk - 1)
    blk_e = jnp.minimum(jnp.sum(ends[None, :] <= blk[:, None] * BM, axis=1), N_EXP - 1).astype(jnp.int32)
    has = region > 0
    slot_e = (psum(has.astype(F32)[None, :], upto_expert)[0].astype(jnp.int32) - 1) % 2
    later = has[None, :] & (ids[None, :] > ids[:, None])
    next_e = jnp.where(later.any(1), jnp.argmax(later, axis=1), -1)
    of_blk = lambda per_e: jnp.sum(jnp.where(blk_e[:, None] == ids[None, :], per_e[None, :], 0), axis=1)
    blocks = (blk_e, nblk.reshape(1).astype(jnp.int32), of_blk(slot_e).astype(jnp.int32),
              of_blk(next_e).astype(jnp.int32))
    flat = lambda a: a.reshape(-1).astype(jnp.int32)
    tail = ((starts + used).astype(jnp.int32), (region - used).astype(jnp.int32))
    off = off_t.astype(jnp.int32)
    e_idx = jnp.arange(N_EXP, dtype=jnp.int32)[:, None]
    col = lambda v: jnp.broadcast_to(jnp.asarray(v, jnp.int32), (N_EXP, 1))
    table = lambda so, do, ln: (flat(so), flat(do), flat(ln), ln.sum(0).astype(jnp.int32))
    pc_c = jnp.concatenate([pc, col(0)], 1)
    spare_c = _sorted_rows(TM_MOE) + SEG_ALIGN * e_idx
    seg_comb = table(jnp.where(pc_c > 0, jnp.concatenate([off, col(0)], 1), spare_c),
                     jnp.where(pc_c > 0, jnp.concatenate([dst, col(0)], 1), 0), jnp.maximum(pc_c, SEG_ALIGN))
    pc_d = jnp.concatenate([col(0), pc], 1)
    entry = jnp.arange(nt + 1, dtype=jnp.int32)[None, :]
    spare_d = nb_max * BM + ((entry % DISPATCH_SLOTS) * N_EXP + e_idx) * SEG_ALIGN
    seg_disp = table(jnp.where(pc_d > 0, jnp.concatenate([col(0), off], 1), 0),
                     jnp.where(pc_d > 0, jnp.concatenate([col(0), dst], 1), spare_d), jnp.maximum(pc_d, SEG_ALIGN))
    return seg_disp, seg_comb, tail, blocks


def _swiglu(u, w1, w3, w2):
    return _dot((_silu(_dot(u, w1)) * _dot(u, w3)).astype(BF16), w2)


def _round_bf16(x):
    return x.astype(BF16).astype(F32)


def _pack_pair(lo, hi):
    lo_b = lax.bitcast_convert_type(lo, jnp.uint32)
    hi_b = lax.bitcast_convert_type(hi, jnp.uint32)
    return (hi_b & jnp.uint32(0xFFFF0000)) | (lo_b >> 16)


def _unpack_pair(w):
    lo = lax.bitcast_convert_type(w << 16, F32).astype(BF16)
    hi = lax.bitcast_convert_type(w & jnp.uint32(0xFFFF0000), F32).astype(BF16)
    return lo, hi


def _segment_copies_dense(so_ref, do_ref, ln_ref, entry, entries, make_copy):
    for e in range(N_EXP):
        j = e * entries + entry
        make_copy(pl.multiple_of(so_ref[j], SEG_ALIGN), pl.multiple_of(do_ref[j], SEG_ALIGN),
                  pl.multiple_of(ln_ref[j], SEG_ALIGN)).start()


def _dispatch_kernel(nt, so_ref, do_ref, ln_ref, tt_ref, td_ref, tl_ref, nb_ref, u_ref, code_ref, segr_ref,
                     digits_ref, xs_out, buf, zbuf, sem):
    i = pl.program_id(0)
    r = buf.shape[1]
    nb_max = (xs_out.shape[0] - DISPATCH_SLOTS * N_EXP * SEG_ALIGN) // BM
    fill_sem = DISPATCH_SLOTS
    slot = i % DISPATCH_SLOTS
    prev_slot = (i + DISPATCH_SLOTS - 1) % DISPATCH_SLOTS

    def strip(slot_, so, do, ln):
        return pltpu.make_async_copy(buf.at[slot_, pl.ds(so, ln)], xs_out.at[pl.ds(do, ln)], sem.at[slot_])

    def wait_entry(slot_, entry):
        strip(slot_, 0, 0, pl.multiple_of(tt_ref[entry], SEG_ALIGN)).wait()

    def zero_fills(act):
        def tail(e, carry):
            ln = pl.multiple_of(tl_ref[e], SEG_ALIGN)

            @pl.when(ln > 0)
            def _():
                act(pltpu.make_async_copy(zbuf.at[pl.ds(0, ln)],
                                          xs_out.at[pl.ds(pl.multiple_of(td_ref[e], SEG_ALIGN), ln)], sem.at[fill_sem]))

            return carry

        lax.fori_loop(0, N_EXP, tail, 0)

        def blk(j, carry):
            act(pltpu.make_async_copy(zbuf.at[pl.ds(0, BM)], xs_out.at[pl.ds(pl.multiple_of(j * BM, BM), BM)],
                                      sem.at[fill_sem]))
            return carry

        lax.fori_loop(nb_ref[0], nb_max, blk, 0)

    @pl.when(i == 0)
    def _():
        zbuf[...] = jnp.zeros(zbuf.shape, jnp.uint32)
        buf[...] = jnp.zeros(buf.shape, jnp.uint32)
        spare = pltpu.make_async_copy(zbuf.at[pl.ds(0, DISPATCH_SLOTS * N_EXP * SEG_ALIGN)],
                                      xs_out.at[pl.ds(nb_max * BM, DISPATCH_SLOTS * N_EXP * SEG_ALIGN)],
                                      sem.at[fill_sem])
        spare.start()
        spare.wait()
        zero_fills(lambda cp: cp.start())

    @pl.when(i >= DISPATCH_SLOTS - 1)
    def _():
        wait_entry(slot, i - (DISPATCH_SLOTS - 1))

    row = lax.broadcasted_iota(jnp.int32, (r, GATE_LANES), 0).astype(F32)
    own = jnp.where(row >= segr_ref[0:1, :], jnp.where(row < segr_ref[1:2, :], 1.0, 0.0), 0.0).astype(BF16)
    z = _dot(jnp.concatenate([own, digits_ref[...]], axis=1), code_ref[...])
    perm = jnp.where(z == 0.0, 1.0, 0.0).astype(BF16)
    step = 256
    for c in range(0, PK, step):
        buf[slot, :, c:c + step] = _pack_pair(_dot(perm, u_ref[:, c:c + step]),
                                              _dot(perm, u_ref[:, PK + c:PK + c + step]))
    _segment_copies_dense(so_ref, do_ref, ln_ref, i, nt + 1, functools.partial(strip, prev_slot))

    @pl.when(i == nt - 1)
    def _():
        _segment_copies_dense(so_ref, do_ref, ln_ref, nt, nt + 1, functools.partial(strip, slot))
        for entry in range(max(nt - DISPATCH_SLOTS + 1, 0), nt + 1):
            wait_entry((entry + DISPATCH_SLOTS - 1) % DISPATCH_SLOTS, entry)
        zero_fills(lambda cp: cp.wait())


def _expert_kernel(layer, be_ref, nb_ref, ws_ref, nx_ref, x_ref, w1_hbm, w3_hbm, w2_hbm, y_ref,
                   w1f, w3f, w2f, w1b, w3b, w2b, sem):
    i = pl.program_id(0)
    live = i < nb_ref[0]

    def fetch(e, slot):
        return [pltpu.make_async_copy(src.at[layer, e], dst.at[slot], sem.at[slot, j])
                for j, (src, dst) in enumerate(((w1_hbm, w1f), (w3_hbm, w3f), (w2_hbm, w2f)))]

    @pl.when(live & ((i == 0) | (be_ref[i] != be_ref[jnp.maximum(i - 1, 0)])))
    def _():
        slot = ws_ref[i]

        @pl.when(i == 0)
        def _():
            for cp in fetch(be_ref[0], slot):
                cp.start()

        for cp in fetch(be_ref[i], slot):
            cp.wait()
        w1b[...] = w1f[slot].astype(BF16)
        w3b[...] = w3f[slot].astype(BF16)
        w2b[...] = w2f[slot].astype(BF16)

        @pl.when(nx_ref[i] >= 0)
        def _():
            for cp in fetch(nx_ref[i], 1 - slot):
                cp.start()

    @pl.when(live)
    def _():
        up = []
        for r0 in range(0, BM, EXP_SUB):
            lo, hi = _unpack_pair(x_ref[r0:r0 + EXP_SUB, :])
            up.append((_dot(lo, w1b[:PK, :]) + _dot(hi, w1b[PK:, :]), _dot(lo, w3b[:PK, :]) + _dot(hi, w3b[PK:, :])))
        for j, (a, g) in enumerate(up):
            hid = (_silu(a) * g).astype(BF16)
            y_ref[j * EXP_SUB:(j + 1) * EXP_SUB, :] = _pack_pair(_round_bf16(_dot(hid, w2b[:, :PK])),
                                                                 _round_bf16(_dot(hid, w2b[:, PK:])))


def _combine_kernel(nt, so_ref, do_ref, ln_ref, tt_ref, code_t_ref, segc_ref, digits_ref, u_ref, ys_ref,
                    s1_ref, s3_ref, s2_ref, h_ref, n_ref, mod_ref, o_ref, buf, sem):
    i = pl.program_id(0)
    slot = i % 2
    r = digits_ref.shape[1]

    def strip(slot_, so, do, ln):
        return pltpu.make_async_copy(ys_ref.at[pl.ds(do, ln)], buf.at[slot_, pl.ds(so, ln)], sem.at[slot_])

    @pl.when(i == 0)
    def _():
        buf[...] = jnp.zeros(buf.shape, jnp.uint32)
        _segment_copies_dense(so_ref, do_ref, ln_ref, 0, nt + 1, functools.partial(strip, 0))

    col = lax.broadcasted_iota(jnp.int32, (2 * N_EXP, r), 1).astype(F32)
    own = jnp.where(col >= segc_ref[:, 0:1], jnp.where(col < segc_ref[:, 1:2], 1.0, 0.0), 0.0).astype(BF16)
    z = _dot(code_t_ref[...], jnp.concatenate([own, own[:N_EXP], digits_ref[...]], axis=0))
    q = jnp.where(jnp.abs(z) < 0.5 * POS_SCALE, z, 0.0).astype(BF16)
    shared = _swiglu(u_ref[...], s1_ref[...], s3_ref[...], s2_ref[...])
    _segment_copies_dense(so_ref, do_ref, ln_ref, i + 1, nt + 1, functools.partial(strip, 1 - slot))

    strip(slot, 0, 0, pl.multiple_of(tt_ref[i], SEG_ALIGN)).wait()
    lo, hi = _unpack_pair(buf[slot, 0:r, :])
    moe = jnp.concatenate([_dot(q, lo), _dot(q, hi)], axis=1) + shared
    o_ref[...] = h_ref[...] + mod_ref[5:6, :] * _rms(moe, n_ref[3:4, :])

    @pl.when(i == nt - 1)
    def _():
        strip(1 - slot, 0, 0, pl.multiple_of(tt_ref[nt], SEG_ALIGN)).wait()


def _sorted_rows(tm):
    return -(-(TOP_K * tm + N_EXP * (SEG_ALIGN - 1)) // MXU_DIM) * MXU_DIM


def _max_blocks(n):
    nt = n // TM_MOE
    rows = TOP_K * n + N_EXP * (SEG_ALIGN - 1) * nt + N_EXP * (BM - SEG_ALIGN)
    return -(-rows // BM)


def _row_digits(r, width, first):
    idx = np.arange(r)
    digits = np.zeros((r, width), np.float32)
    digits[:, first] = idx // POS_RADIX * POS_RADIX
    digits[:, first + 1] = idx % POS_RADIX
    return jnp.asarray(digits, BF16)


def _dispatch_call(seg, tail, nblk, u2, code, segr, nb_max):
    n = u2.shape[0]
    r = _sorted_rows(TM_MOE)
    spare = DISPATCH_SLOTS * N_EXP * SEG_ALIGN
    return pl.pallas_call(
        functools.partial(_dispatch_kernel, n // TM_MOE),
        grid_spec=pltpu.PrefetchScalarGridSpec(
            num_scalar_prefetch=7,
            grid=(n // TM_MOE,),
            in_specs=[
                pl.BlockSpec((TM_MOE, D), lambda i, *_: (i, 0)),
                pl.BlockSpec((4 * N_EXP, TM_MOE), lambda i, *_: (0, i)),
                pl.BlockSpec((None, SUBLANES, GATE_LANES), lambda i, *_: (i, 0, 0)),
                pl.BlockSpec((r, GATE_LANES), lambda i, *_: (0, 0)),
            ],
            out_specs=pl.BlockSpec(memory_space=pl.ANY),
            scratch_shapes=[pltpu.VMEM((DISPATCH_SLOTS, r, PK), jnp.uint32),
                            pltpu.VMEM((max(BM, spare), PK), jnp.uint32),
                            pltpu.SemaphoreType.DMA((DISPATCH_SLOTS + 1,))],
        ),
        out_shape=jax.ShapeDtypeStruct((nb_max * BM + spare, PK), jnp.uint32),
        compiler_params=_cparams("arbitrary"),
        name="dispatch",
    )(*seg, tail[0], tail[1], nblk, u2, code, segr, _row_digits(r, GATE_LANES, N_EXP))


def _expert_call(layer, blk, xs, w1, w3, w2, nb_max):
    row_blk = lambda i, be, nb, *_: (jnp.maximum(jnp.minimum(i, nb[0] - 1), 0), 0)
    hbm = pl.BlockSpec(memory_space=pl.ANY)
    return pl.pallas_call(
        functools.partial(_expert_kernel, layer),
        grid_spec=pltpu.PrefetchScalarGridSpec(
            num_scalar_prefetch=4,
            grid=(nb_max,),
            in_specs=[pl.BlockSpec((BM, PK), row_blk), hbm, hbm, hbm],
            out_specs=pl.BlockSpec((BM, PK), row_blk),
            scratch_shapes=[pltpu.VMEM((2, D, F_EXP), F32), pltpu.VMEM((2, D, F_EXP), F32),
                            pltpu.VMEM((2, F_EXP, D), F32),
                            pltpu.VMEM((D, F_EXP), BF16), pltpu.VMEM((D, F_EXP), BF16),
                            pltpu.VMEM((F_EXP, D), BF16), pltpu.SemaphoreType.DMA((2, 3))],
        ),
        out_shape=jax.ShapeDtypeStruct(xs.shape, jnp.uint32),
        input_output_aliases={4: 0},
        compiler_params=_cparams("arbitrary"),
        name="experts",
    )(*blk, xs, w1, w3, w2)


def _combine_call(tabs, code_t, segc, u2, ys, s1, s3, s2, h, norms_l, mod_l, t):
    n = u2.shape[0]
    r = _sorted_rows(TM_MOE)
    tok = lambda w: pl.BlockSpec((TM_MOE, w), lambda i, *_: (i, 0))
    const = lambda shp: pl.BlockSpec(shp, lambda i, *_: (0,) * len(shp))
    return pl.pallas_call(
        functools.partial(_combine_kernel, n // TM_MOE),
        grid_spec=pltpu.PrefetchScalarGridSpec(
            num_scalar_prefetch=4,
            grid=(n // TM_MOE,),
            in_specs=[
                tok(4 * N_EXP),
                pl.BlockSpec((None, 2 * N_EXP, GATE_LANES), lambda i, *_: (i, 0, 0)),
                const((N_EXP, r)),
                tok(D),
                pl.BlockSpec(memory_space=pl.ANY),
                const((D, F_EXP)), const((D, F_EXP)), const((F_EXP, D)),
                tok(D),
                const((4, D)),
                pl.BlockSpec((None, 6, D), lambda i, *_: (i * TM_MOE // t, 0, 0)),
            ],
            out_specs=tok(D),
            scratch_shapes=[pltpu.VMEM((2, r + N_EXP * SEG_ALIGN, PK), jnp.uint32),
                            pltpu.SemaphoreType.DMA((2,))],
        ),
        out_shape=jax.ShapeDtypeStruct((n, D), F32),
        compiler_params=_cparams("arbitrary"),
        name="combine",
    )(*tabs, code_t, segc, _row_digits(r, N_EXP, 0).T, u2, ys, s1, s3, s2, h, norms_l, mod_l)


def _glu_kernel(h_ref, g_ref, mod_ref, w_ref, b_ref, o_ref):
    u = (_rms(h_ref[...], g_ref[...]) * (1.0 + mod_ref[1:2, :]) + mod_ref[0:1, :]).astype(BF16)
    a = _dot(u, w_ref[:, :D]) + b_ref[:, :D]
    gate = _dot(u, w_ref[:, D:]) + b_ref[:, D:]
    o_ref[...] = (a * jax.nn.sigmoid(gate)).astype(BF16)


def _glu_call(h, g, mod_l, w1, b1, tm):
    b, t, _ = h.shape
    tok = lambda n: pl.BlockSpec((None, tm, n), lambda i, j: (i, j, 0))
    return pl.pallas_call(
        _glu_kernel,
        grid=(b, t // tm),
        in_specs=[
            tok(D),
            pl.BlockSpec((1, D), lambda i, j: (0, 0)),
            pl.BlockSpec((None, 6, D), lambda i, j: (i, 0, 0)),
            pl.BlockSpec((D, 2 * D), lambda i, j: (0, 0)),
            pl.BlockSpec((1, 2 * D), lambda i, j: (0, 0)),
        ],
        out_specs=tok(D),
        out_shape=jax.ShapeDtypeStruct((b, t, D), BF16),
        compiler_params=_cparams("parallel", "parallel"),
        name="glu",
    )(h, g, mod_l, w1, b1)


def _conv_kernel(cur_ref, prev_ref, next_ref, dw_ref, cv_ref, w2_ref, h_ref, n_ref, mod_ref, wr_ref,
                 h_out, u_out, lg_out, win, shifted, conv):
    j = pl.program_id(1)
    tm = cur_ref.shape[0]
    pad = CONV_W // 2
    win[0:HALO, :] = jnp.where(j > 0, prev_ref[...].astype(F32), 0.0)
    win[HALO:HALO + tm, :] = cur_ref[...].astype(F32)
    win[HALO + tm:, :] = jnp.where(j < pl.num_programs(1) - 1, next_ref[...].astype(F32), 0.0)
    ext = shifted.shape[1]
    full = win[...]
    shifted[0] = full[:ext, :]
    for s in range(1, SUBLANES):
        shifted[s] = pltpu.roll(full, full.shape[0] - s, axis=0)[:ext, :]

    def rows(ib, carry):
        r0 = pl.multiple_of(ib * 2 * SUBLANES, 2 * SUBLANES)
        for c0 in range(0, D, CONV_LANES):
            cols = slice(c0, c0 + CONV_LANES)
            acc0 = jnp.zeros((SUBLANES, CONV_LANES), F32)
            acc1 = jnp.zeros((SUBLANES, CONV_LANES), F32)
            for s in range(SUBLANES):
                taps = [(off // SUBLANES, off - (HALO - pad)) for off in range(HALO - pad, HALO - pad + CONV_W)
                        if off % SUBLANES == s]
                tiles = {a: shifted[s, pl.ds(r0 + a * SUBLANES, SUBLANES), cols]
                         for a in range(taps[0][0], taps[-1][0] + 2)}
                for a, tap in taps:
                    w = dw_ref[tap * SUBLANES:(tap + 1) * SUBLANES, cols]
                    acc0 = acc0 + tiles[a] * w
                    acc1 = acc1 + tiles[a + 1] * w
            conv[pl.ds(r0, SUBLANES), cols] = acc0
            conv[pl.ds(r0 + SUBLANES, SUBLANES), cols] = acc1
        return carry

    lax.fori_loop(0, tm // (2 * SUBLANES), rows, 0)
    for rows in _row_blocks(tm):
        uf = conv[rows, :] + cv_ref[0:1, :]
        mu = jnp.mean(uf, axis=-1, keepdims=True)
        var = jnp.mean(jnp.square(uf - mu), axis=-1, keepdims=True)
        uf = (uf - mu) * lax.rsqrt(var + LN_EPS) * cv_ref[1:2, :] + cv_ref[2:3, :]
        y = _dot(_silu(uf).astype(BF16), w2_ref[...]) + cv_ref[3:4, :]
        _mixer_epilogue(y, rows, h_ref, n_ref, mod_ref, wr_ref, h_out, u_out, lg_out)


def _conv_call(glu, dw, cvec, w2, h, norms_l, mod_l, wr_t, tm):
    b, t, _ = h.shape
    tok, ep_in, wr_spec, out_specs, out_shape = _epilogue_specs(b, t, tm)
    r = tm // HALO
    last = t // HALO - 1
    return pl.pallas_call(
        _conv_kernel,
        grid=(b, t // tm),
        in_specs=[
            tok(D),
            pl.BlockSpec((None, HALO, D), lambda i, j: (i, jnp.maximum(j * r - 1, 0), 0)),
            pl.BlockSpec((None, HALO, D), lambda i, j: (i, jnp.minimum((j + 1) * r, last), 0)),
            pl.BlockSpec((CONV_W * SUBLANES, D), lambda i, j: (0, 0)),
            pl.BlockSpec((4, D), lambda i, j: (0, 0)),
            pl.BlockSpec((D, D), lambda i, j: (0, 0)),
        ] + ep_in + [wr_spec],
        out_specs=out_specs,
        out_shape=out_shape,
        scratch_shapes=[pltpu.VMEM((tm + 2 * HALO, D), F32),
                        pltpu.VMEM((SUBLANES, tm + (CONV_W // SUBLANES) * SUBLANES, D), F32),
                        pltpu.VMEM((tm, D), F32)],
        compiler_params=_cparams("parallel", "parallel"),
        name="conv",
    )(glu, glu, glu, dw, cvec, w2, h, norms_l, mod_l, wr_t)


def _tile(t, pref):
    return pref if t % pref == 0 else t


def kernel(x, c, ctx, c_ctx, mod_w, mod_b, norms, ret_w_in, ret_w_out, ret_decay, conv_w1, conv_b1, conv_dw,
           conv_b_dw, conv_ln_g, conv_ln_b, conv_w2, conv_b2, moe_router, moe_bias, moe_w1, moe_w3, moe_w2,
           shared_w1, shared_w3, shared_w2):
    b, t, _ = x.shape
    l = ctx.shape[1]
    tm = _tile(t, 512)

    rows = -(-(b + 1) // 8) * 8
    cs = jnp.concatenate([c, c_ctx[None, :], jnp.zeros((rows - b - 1, D), F32)], 0)
    mod = _mod_call(cs, mod_w, mod_b).reshape(mod_w.shape[0], rows, 6, D)

    idx = jnp.arange(t, dtype=jnp.int32)
    pos_lat = jnp.stack([jnp.full((t,), l, jnp.int32), idx // GRID_W, idx % GRID_W], -1)
    zl = jnp.zeros((l,), jnp.int32)
    pos_ctx = jnp.stack([jnp.arange(l, dtype=jnp.int32), zl, zl], -1)
    ang_lat, ang_ctx = _rope_angles(pos_lat), _rope_angles(pos_ctx)

    w_qk = ret_w_in[0][:, :2 * QK_TOT].reshape(D, 2 * HEADS, DK // 2, 2).swapaxes(-1, -2).reshape(D, 2 * QK_TOT)
    w_in = jnp.concatenate([w_qk, ret_w_in[0][:, 2 * QK_TOT:]], axis=1).astype(BF16)
    log_g = jax.nn.log_sigmoid(ret_decay[0].astype(F32))

    n = b * t
    nb_max = _max_blocks(n)

    def moe_layer(i, h1, u2, logits_t):
        code, code_t, segc, segr, pc_t, off_t = _route_call(logits_t, moe_bias[i])
        seg_disp, seg_comb, tail, blocks = _dispatch_tables(pc_t, off_t, nb_max)
        u2f = u2.reshape(n, D)
        xs = _dispatch_call(seg_disp, tail, blocks[1], u2f, code, segr, nb_max)
        ys = _expert_call(i, blocks, xs, moe_w1, moe_w3, moe_w2, nb_max)
        out = _combine_call(seg_comb, code_t, segc, u2f, ys, shared_w1[i].astype(BF16), shared_w3[i].astype(BF16),
                            shared_w2[i].astype(BF16), h1.reshape(n, D), norms[i], mod[i], t)
        return out.reshape(b, t, D)

    q, k, v, gt = _inproj_call(x, norms[0, 0:1], mod[0], w_in, jnp.cos(ang_lat), jnp.sin(ang_lat), tm)
    sf0, sb0 = _ctxstate_call(log_g, ctx, norms[0, 0:1], mod[0, b], w_in[:, QK_TOT:2 * QK_TOT + V_TOT],
                              jnp.cos(ang_ctx), jnp.sin(ang_ctx))
    yn = _retention_call(log_g, q, k, v, sf0, sb0)
    h1, u2, logits_t = _readout_call(yn, gt, x, norms[0], mod[0], ret_w_out[0].astype(BF16),
                                     moe_router[0].T, _tile(t, 2 * EPI_ROWS))
    h2 = moe_layer(0, h1, u2, logits_t)

    glu = _glu_call(h2, norms[1, 0:1], mod[1], conv_w1[0].astype(BF16), conv_b1[0][None, :], tm)
    cvec = jnp.stack([conv_b_dw[0], conv_ln_g[0], conv_ln_b[0], conv_b2[0]], 0)
    dw_rows = jnp.repeat(conv_dw[0], SUBLANES, axis=0)
    h3, u2, logits_t = _conv_call(glu, dw_rows, cvec, conv_w2[0].astype(BF16), h2, norms[1], mod[1],
                                  moe_router[1].T, tm)
    return moe_layer(1, h3, u2, logits_t)
```

```python
import functools

import jax
import jax.numpy as jnp
import numpy as np
from jax import lax
from jax.experimental import pallas as pl
from jax.experimental.pallas import tpu as pltpu

F32 = jnp.float32
BF16 = jnp.bfloat16

D = 1024
HEADS = 4
DK = D // HEADS
DV = 2 * DK
QK_TOT = HEADS * DK
V_TOT = HEADS * DV
IN_DIM = 2 * QK_TOT + 2 * V_TOT
ROPE_AXES = (64, 96, 96)
ROPE_THETA = 10000.0
GRID_W = 64
CONV_W = 31
HALO = 16
SUBLANES = 8
CONV_LANES = 256
N_EXP = 64
N_GRP = 8
PER_GRP = N_EXP // N_GRP
TOP_K = 8
TOP_G = 4
F_EXP = 256
ROUTE_SCALE = 2.5
GATE_LANES = 128
MXU_DIM = 256
TM_MOE = 256
ROUTE_TILES = 4
SEG_ALIGN = 8
POS_RADIX = 256
POS_SCALE = 8
BM = 1024
DISPATCH_SLOTS = 3
EXP_SUB = 256
PK = D // 2
NORM_EPS = 1e-6
LN_EPS = 1e-5
RET_CHUNK = 256
EPI_ROWS = 512
VMEM_LIMIT = 56 * 1024 * 1024

_HI = lax.Precision.HIGHEST


def _cparams(*sem):
    return pltpu.CompilerParams(dimension_semantics=sem, vmem_limit_bytes=VMEM_LIMIT)


def _dot(a, b):
    return jnp.dot(a, b, preferred_element_type=F32)


def _rms(xf, g):
    return xf * lax.rsqrt(jnp.mean(xf * xf, axis=-1, keepdims=True) + NORM_EPS) * g


def _silu(x):
    return x * jax.nn.sigmoid(x)


def _mod_kernel(cs_ref, w_ref, b_ref, o_ref):
    s = _silu(cs_ref[...])
    o_ref[...] = jnp.dot(s, w_ref[...], preferred_element_type=F32, precision=_HI) + b_ref[...]


def _mod_call(cs, mod_w, mod_b):
    depth, _, n6 = mod_w.shape
    tn = 1536
    rows = cs.shape[0]
    return pl.pallas_call(
        _mod_kernel,
        grid=(depth, n6 // tn),
        in_specs=[
            pl.BlockSpec((rows, D), lambda i, j: (0, 0)),
            pl.BlockSpec((None, D, tn), lambda i, j: (i, 0, j)),
            pl.BlockSpec((None, 1, tn), lambda i, j: (i, 0, j)),
        ],
        out_specs=pl.BlockSpec((None, rows, tn), lambda i, j: (i, 0, j)),
        out_shape=jax.ShapeDtypeStruct((depth, rows, n6), F32),
        compiler_params=_cparams("parallel", "parallel"),
        name="mod",
    )(cs, mod_w, mod_b.reshape(depth, 1, n6))


def _rope_angles(pos):
    parts = []
    for a, d in enumerate(ROPE_AXES):
        inv = ROPE_THETA ** (-jnp.arange(0, d, 2, dtype=F32) / d)
        parts.append(pos[:, a:a + 1].astype(F32) * inv[None, :])
    return jnp.concatenate(parts, -1)


def _rope_store(z, cos, sin, scale, o_ref, col):
    half = DK // 2
    x0, x1 = z[:, :half], z[:, half:]
    o_ref[:, col:col + half] = ((x0 * cos - x1 * sin) * scale).astype(o_ref.dtype)
    o_ref[:, col + half:col + DK] = ((x0 * sin + x1 * cos) * scale).astype(o_ref.dtype)


def _inproj_kernel(x_ref, g_ref, mod_ref, w_ref, cos_ref, sin_ref, q_ref, k_ref, v_ref, gt_ref):
    u = (_rms(x_ref[...], g_ref[...]) * (1.0 + mod_ref[1:2, :]) + mod_ref[0:1, :]).astype(BF16)
    cos, sin = cos_ref[...], sin_ref[...]
    step = 512
    for j in range(V_TOT // step):
        c1 = 2 * QK_TOT + V_TOT + j * step
        gt_ref[:, j * step:(j + 1) * step] = _silu(_dot(u, w_ref[:, c1:c1 + step])).astype(BF16)
    for h in range(HEADS):
        _rope_store(_dot(u, w_ref[:, h * DK:(h + 1) * DK]), cos, sin, DK ** -0.5, q_ref, h * DK)
        _rope_store(_dot(u, w_ref[:, QK_TOT + h * DK:QK_TOT + (h + 1) * DK]), cos, sin, 1.0, k_ref, h * DK)
    for j in range(V_TOT // step):
        c0 = 2 * QK_TOT + j * step
        v_ref[:, j * step:(j + 1) * step] = _dot(u, w_ref[:, c0:c0 + step]).astype(BF16)


def _inproj_call(x, g, mod_l, w_in, cos, sin, tm):
    b, t, _ = x.shape
    grid = (b, t // tm)
    tok = lambda n: pl.BlockSpec((None, tm, n), lambda i, j: (i, j, 0))
    return pl.pallas_call(
        _inproj_kernel,
        grid=grid,
        in_specs=[
            tok(D),
            pl.BlockSpec((1, D), lambda i, j: (0, 0)),
            pl.BlockSpec((None, 6, D), lambda i, j: (i, 0, 0)),
            pl.BlockSpec((D, IN_DIM), lambda i, j: (0, 0), pipeline_mode=pl.Buffered(1)),
            pl.BlockSpec((tm, DK // 2), lambda i, j: (j, 0)),
            pl.BlockSpec((tm, DK // 2), lambda i, j: (j, 0)),
        ],
        out_specs=[tok(QK_TOT), tok(QK_TOT), tok(V_TOT), tok(V_TOT)],
        out_shape=[
            jax.ShapeDtypeStruct((b, t, QK_TOT), BF16),
            jax.ShapeDtypeStruct((b, t, QK_TOT), BF16),
            jax.ShapeDtypeStruct((b, t, V_TOT), BF16),
            jax.ShapeDtypeStruct((b, t, V_TOT), BF16),
        ],
        compiler_params=_cparams("parallel", "parallel"),
        name="inproj",
    )(x, g, mod_l, w_in, cos, sin)


def _row_pow(lg, expo_fn, rows, cols):
    i = lax.broadcasted_iota(jnp.int32, (rows, cols), 0).astype(F32)
    return jnp.exp(lg * expo_fn(i))


def _ctxstate_kernel(lg_ref, c_ref, g_ref, mod_ref, w_ref, cos_ref, sin_ref, sf_ref, sb_ref, k_scr):
    l = c_ref.shape[0]
    u = (_rms(c_ref[...], g_ref[...]) * (1.0 + mod_ref[1:2, :]) + mod_ref[0:1, :]).astype(BF16)
    cos, sin = cos_ref[...], sin_ref[...]
    for h in range(HEADS):
        _rope_store(_dot(u, w_ref[:, h * DK:(h + 1) * DK]), cos, sin, 1.0, k_scr, h * DK)
    for h in range(HEADS):
        kh = k_scr[:, h * DK:(h + 1) * DK]
        vh = _dot(u, w_ref[:, QK_TOT + h * DV:QK_TOT + (h + 1) * DV]).astype(BF16)
        wf = _row_pow(lg_ref[0, h], lambda i: (l - 1.0) - i, l, DK)
        wb = _row_pow(lg_ref[1, h], lambda i: i, l, DK)
        sf_ref[h] = _dot((kh * wf).T.astype(BF16), vh)
        sb_ref[h] = _dot((kh * wb).T.astype(BF16), vh)


def _ctxstate_call(log_g, ctx, g, mod_row, w_kv, cos, sin):
    b, l, _ = ctx.shape
    st = jax.ShapeDtypeStruct((b, HEADS, DK, DV), F32)
    st_spec = pl.BlockSpec((None, HEADS, DK, DV), lambda i: (i, 0, 0, 0))
    return pl.pallas_call(
        _ctxstate_kernel,
        grid=(b,),
        in_specs=[
            pl.BlockSpec(memory_space=pltpu.SMEM),
            pl.BlockSpec((None, l, D), lambda i: (i, 0, 0)),
            pl.BlockSpec((1, D), lambda i: (0, 0)),
            pl.BlockSpec((6, D), lambda i: (0, 0)),
            pl.BlockSpec((D, QK_TOT + V_TOT), lambda i: (0, 0)),
            pl.BlockSpec((l, DK // 2), lambda i: (0, 0)),
            pl.BlockSpec((l, DK // 2), lambda i: (0, 0)),
        ],
        out_specs=[st_spec, st_spec],
        out_shape=[st, st],
        scratch_shapes=[pltpu.VMEM((l, QK_TOT), F32)],
        compiler_params=_cparams("parallel"),
        name="ctxstate",
    )(log_g, ctx, g, mod_row, w_kv, cos, sin)


def _retention_kernel(lg_ref, q_ref, k_ref, v_ref, sf0_ref, sb0_ref, o_ref, acc, sf, sb):
    h = pl.program_id(1)
    t = q_ref.shape[0]
    c = RET_CHUNK
    nc = t // c
    lgf, lgb = lg_ref[0, h], lg_ref[1, h]
    ri = lax.broadcasted_iota(jnp.int32, (c, c), 0).astype(F32)
    ci = lax.broadcasted_iota(jnp.int32, (c, c), 1).astype(F32)
    rel = ri - ci
    mask = (jnp.where(rel >= 0, jnp.exp(lgf * jnp.maximum(rel, 0.0)), 0.0)
            + jnp.where(rel <= 0, jnp.exp(lgb * jnp.maximum(-rel, 0.0)), 0.0))
    qdf = _row_pow(lgf, lambda i: i + 1.0, c, DV)
    qdb = _row_pow(lgb, lambda i: c - i, c, DV)
    kdf = _row_pow(lgf, lambda i: (c - 1.0) - i, c, DK)
    kdb = _row_pow(lgb, lambda i: i, c, DK)
    cdf = jnp.exp(jnp.full((1, DV), lgf * c, F32))
    cdb = jnp.exp(jnp.full((1, DV), lgb * c, F32))
    sf[...] = sf0_ref[...]
    sb[...] = sb0_ref[...]

    def rows(ic):
        return pl.ds(pl.multiple_of(ic * c, c), c)

    def fwd(rs):
        q, k, v = q_ref[rs, :], k_ref[rs, :], v_ref[rs, :]
        s = lax.dot_general(q, k, (((1,), (1,)), ((), ())), preferred_element_type=F32)
        o = _dot((s * mask).astype(BF16), v) + qdf * _dot(q, sf[...].astype(BF16))
        sf[...] = sf[...] * cdf + _dot((k.astype(F32) * kdf).T.astype(BF16), v)
        return o

    def bwd(rs):
        q, k, v = q_ref[rs, :], k_ref[rs, :], v_ref[rs, :]
        o = qdb * _dot(q, sb[...].astype(BF16))
        sb[...] = sb[...] * cdb + _dot((k.astype(F32) * kdb).T.astype(BF16), v)
        return o

    def finish(rs, o):
        o_ref[rs, :] = (o * lax.rsqrt(jnp.mean(o * o, axis=-1, keepdims=True) + NORM_EPS)).astype(BF16)

    def first_half(j, carry):
        of, ob = fwd(rows(j)), bwd(rows(nc - 1 - j))
        acc[rows(j), :] = of
        acc[rows(nc - 1 - j), :] = ob
        return carry

    def second_half(j, carry):
        of, ob = fwd(rows(j)), bwd(rows(nc - 1 - j))
        finish(rows(j), acc[rows(j), :] + of)
        finish(rows(nc - 1 - j), acc[rows(nc - 1 - j), :] + ob)
        return carry

    lax.fori_loop(0, nc // 2, first_half, 0, unroll=True)
    lax.fori_loop(nc // 2, nc, second_half, 0, unroll=True)


def _retention_call(log_g, q, k, v, sf0, sb0):
    b, t, _ = q.shape
    qk_spec = pl.BlockSpec((None, t, DK), lambda i, h: (i, 0, h))
    v_spec = pl.BlockSpec((None, t, DV), lambda i, h: (i, 0, h))
    s_spec = pl.BlockSpec((None, None, DK, DV), lambda i, h: (i, h, 0, 0))
    return pl.pallas_call(
        _retention_kernel,
        grid=(b, HEADS),
        in_specs=[pl.BlockSpec(memory_space=pltpu.SMEM), qk_spec, qk_spec, v_spec, s_spec, s_spec],
        out_specs=v_spec,
        out_shape=jax.ShapeDtypeStruct((b, t, V_TOT), BF16),
        scratch_shapes=[pltpu.VMEM((t, DV), F32), pltpu.VMEM((DK, DV), F32), pltpu.VMEM((DK, DV), F32)],
        compiler_params=_cparams("parallel", "parallel"),
        name="retention",
    )(log_g, q, k, v, sf0, sb0)


def _mixer_epilogue(y, rows, h_ref, n_ref, mod_ref, wr_ref, h_out, u_out, lg_out):
    h1 = h_ref[rows, :] + mod_ref[2:3, :] * _rms(y, n_ref[1:2, :])
    h_out[rows, :] = h1
    u2 = _rms(h1, n_ref[2:3, :]) * (1.0 + mod_ref[4:5, :]) + mod_ref[3:4, :]
    u_hi = u2.astype(BF16)
    u_out[rows, :] = u_hi
    u_lo = (u2 - u_hi.astype(F32)).astype(BF16)
    w = wr_ref[...]
    w_hi = w.astype(BF16)
    w_lo = (w - w_hi.astype(F32)).astype(BF16)
    nt_dot = lambda a, b: lax.dot_general(a, b, (((1,), (1,)), ((), ())), preferred_element_type=F32)
    lg_out[:, rows] = nt_dot(w_hi, u_hi) + (nt_dot(w_hi, u_lo) + nt_dot(w_lo, u_hi))


def _row_blocks(tm):
    return [slice(r0, r0 + EPI_ROWS) for r0 in range(0, tm, EPI_ROWS)]


def _readout_kernel(y_ref, gt_ref, h_ref, n_ref, mod_ref, w_ref, wr_ref, h_out, u_out, lg_out):
    blocks = _row_blocks(y_ref.shape[0])
    ys = [_dot(y_ref[rows, :] * gt_ref[rows, :], w_ref[...]) for rows in blocks]
    for rows, y in zip(blocks, ys):
        _mixer_epilogue(y, rows, h_ref, n_ref, mod_ref, wr_ref, h_out, u_out, lg_out)


def _epilogue_specs(b, t, tm):
    nt = t // tm
    tok = lambda n: pl.BlockSpec((None, tm, n), lambda i, j: (i, j, 0))
    in_specs = [
        tok(D),
        pl.BlockSpec((4, D), lambda i, j: (0, 0)),
        pl.BlockSpec((None, 6, D), lambda i, j: (i, 0, 0)),
    ]
    wr_spec = pl.BlockSpec((N_EXP, D), lambda i, j: (0, 0))
    out_specs = [tok(D), tok(D), pl.BlockSpec((N_EXP, tm), lambda i, j: (0, i * nt + j))]
    out_shape = [
        jax.ShapeDtypeStruct((b, t, D), F32),
        jax.ShapeDtypeStruct((b, t, D), BF16),
        jax.ShapeDtypeStruct((N_EXP, b * t), F32),
    ]
    return tok, in_specs, wr_spec, out_specs, out_shape


def _readout_call(yn, gt, h, norms_l, mod_l, w_out, wr_t, tm):
    b, t, _ = h.shape
    tok, ep_in, wr_spec, out_specs, out_shape = _epilogue_specs(b, t, tm)
    return pl.pallas_call(
        _readout_kernel,
        grid=(b, t // tm),
        in_specs=[tok(V_TOT), tok(V_TOT)] + ep_in
        + [pl.BlockSpec((V_TOT, D), lambda i, j: (0, 0), pipeline_mode=pl.Buffered(1)), wr_spec],
        out_specs=out_specs,
        out_shape=out_shape,
        compiler_params=_cparams("parallel", "parallel"),
        name="readout",
    )(yn, gt, h, norms_l, mod_l, w_out, wr_t)


def _route_tile(logits, bias):
    tr = logits.shape[1]
    s = jax.nn.sigmoid(logits)
    biased = s + bias
    neg = -jnp.inf
    b3 = biased.reshape(N_GRP, PER_GRP, tr)
    io3 = lax.broadcasted_iota(jnp.int32, b3.shape, 1).astype(F32)
    m1 = jnp.max(b3, axis=1, keepdims=True)
    i1 = jnp.min(jnp.where(b3 == m1, io3, float(PER_GRP)), axis=1, keepdims=True)
    m2 = jnp.max(jnp.where(io3 == i1, neg, b3), axis=1, keepdims=True)
    gs = (m1 + m2).reshape(N_GRP, tr)
    iog = lax.broadcasted_iota(jnp.int32, gs.shape, 0).astype(F32)
    gsel = jnp.zeros(gs.shape, F32)
    work = gs
    for _ in range(TOP_G):
        m = jnp.max(work, axis=0, keepdims=True)
        gi = jnp.min(jnp.where(work == m, iog, float(N_GRP)), axis=0, keepdims=True)
        hit = iog == gi
        gsel = jnp.where(hit, 1.0, gsel)
        work = jnp.where(hit, neg, work)
    emask = jnp.broadcast_to(gsel.reshape(N_GRP, 1, tr), b3.shape).reshape(N_EXP, tr)
    work = jnp.where(emask > 0.0, biased, neg)
    ioe = lax.broadcasted_iota(jnp.int32, work.shape, 0).astype(F32)
    esel = jnp.zeros(work.shape, F32)
    for _ in range(TOP_K):
        m = jnp.max(work, axis=0, keepdims=True)
        ei = jnp.min(jnp.where(work == m, ioe, float(N_EXP)), axis=0, keepdims=True)
        hit = ioe == ei
        esel = jnp.where(hit, 1.0, esel)
        work = jnp.where(hit, neg, work)
    wsel = esel * s
    gate = wsel / jnp.sum(wsel, axis=0, keepdims=True) * ROUTE_SCALE

    t_r = lax.broadcasted_iota(jnp.int32, (tr, tr), 0)
    t_c = lax.broadcasted_iota(jnp.int32, (tr, tr), 1)
    rank = _dot(esel.astype(BF16), jnp.where(t_r < t_c, 1.0, 0.0).astype(BF16))
    cnt = jnp.sum(esel, axis=1, keepdims=True)
    pc = jnp.floor((cnt + (SEG_ALIGN - 1.0)) * (1.0 / SEG_ALIGN)) * SEG_ALIGN
    e_r = lax.broadcasted_iota(jnp.int32, (N_EXP, N_EXP), 0)
    e_c = lax.broadcasted_iota(jnp.int32, (N_EXP, N_EXP), 1)
    off = jnp.dot(jnp.where(e_c < e_r, 1.0, 0.0), jnp.broadcast_to(pc, (N_EXP, 128)),
                  preferred_element_type=F32, precision=_HI)[:, 0:1]
    p = off + rank
    p_hi = jnp.where(esel > 0.0, jnp.floor(p * (1.0 / POS_RADIX)) * POS_RADIX, -float(POS_RADIX))
    p_lo = jnp.where(esel > 0.0, p - p_hi, 0.0)
    minus_r = jnp.where(lax.broadcasted_iota(jnp.int32, (N_EXP, tr), 0) < 2, -float(POS_SCALE), 0.0)
    code = jnp.concatenate([p_hi * POS_SCALE, p_lo * POS_SCALE, gate, minus_r], 0)
    code_t = jnp.concatenate([code[:2 * N_EXP].T, code[2 * N_EXP:].T], 1)
    lane = lax.broadcasted_iota(jnp.int32, (2 * N_EXP, GATE_LANES), 1)
    off2 = jnp.concatenate([off, off], 0)
    end2 = jnp.concatenate([off + pc, off + pc], 0)
    bounds = jnp.where(lane == 0, off2, jnp.where(lane == 1, end2, 0.0))
    return code.astype(BF16), code_t.astype(BF16), bounds, bounds.T[0:SUBLANES, :], pc, off


def _route_kernel(lg_ref, b_ref, code_ref, code_t_ref, segc_ref, segr_ref, pc_ref, off_ref):
    i = pl.program_id(0)

    @pl.when(i == 0)
    def _():
        pc_ref[...] = jnp.zeros(pc_ref.shape, F32)
        off_ref[...] = jnp.zeros(off_ref.shape, F32)

    tile_lane = lax.broadcasted_iota(jnp.int32, pc_ref.shape, 1)
    for k in range(ROUTE_TILES):
        cols = slice(k * TM_MOE, (k + 1) * TM_MOE)
        code, code_t, bounds, bounds_t, pc, off = _route_tile(lg_ref[:, cols], b_ref[...])
        code_ref[:, cols] = code
        code_t_ref[cols, :] = code_t
        segc_ref[k] = bounds
        segr_ref[k] = bounds_t
        pc_ref[...] = jnp.where(tile_lane == i * ROUTE_TILES + k, pc, pc_ref[...])
        off_ref[...] = jnp.where(tile_lane == i * ROUTE_TILES + k, off, off_ref[...])


def _route_call(logits_t, bias):
    n = logits_t.shape[1]
    nt = n // TM_MOE
    tr = ROUTE_TILES * TM_MOE
    return pl.pallas_call(
        _route_kernel,
        grid=(nt // ROUTE_TILES,),
        in_specs=[pl.BlockSpec((N_EXP, tr), lambda i: (0, i)), pl.BlockSpec((N_EXP, 1), lambda i: (0, 0))],
        out_specs=[
            pl.BlockSpec((4 * N_EXP, tr), lambda i: (0, i)),
            pl.BlockSpec((tr, 4 * N_EXP), lambda i: (i, 0)),
            pl.BlockSpec((ROUTE_TILES, 2 * N_EXP, GATE_LANES), lambda i: (i, 0, 0)),
            pl.BlockSpec((ROUTE_TILES, SUBLANES, GATE_LANES), lambda i: (i, 0, 0)),
            pl.BlockSpec((N_EXP, nt), lambda i: (0, 0)),
            pl.BlockSpec((N_EXP, nt), lambda i: (0, 0)),
        ],
        out_shape=[
            jax.ShapeDtypeStruct((4 * N_EXP, n), BF16),
            jax.ShapeDtypeStruct((n, 4 * N_EXP), BF16),
            jax.ShapeDtypeStruct((nt, 2 * N_EXP, GATE_LANES), F32),
            jax.ShapeDtypeStruct((nt, SUBLANES, GATE_LANES), F32),
            jax.ShapeDtypeStruct((N_EXP, nt), F32),
            jax.ShapeDtypeStruct((N_EXP, nt), F32),
        ],
        compiler_params=_cparams("arbitrary"),
        name="route",
    )(logits_t, bias.reshape(N_EXP, 1))


def _dispatch_tables(pc_t, off_t, nb_max):
    nt = pc_t.shape[1]
    ids = jnp.arange(N_EXP, dtype=jnp.int32)
    psum = lambda a, m: jnp.dot(a, m, precision=_HI, preferred_element_type=F32)
    before_tile = (jnp.arange(nt)[:, None] < jnp.arange(nt)[None, :]).astype(F32)
    upto_expert = (ids[:, None] <= ids[None, :]).astype(F32)
    used_f = pc_t.sum(1)
    region_f = jnp.ceil(used_f * (1.0 / BM)) * BM
    ends_f = psum(region_f[None, :], upto_expert)[0]
    pc = pc_t.astype(jnp.int32)
    used, region, ends = used_f.astype(jnp.int32), region_f.astype(jnp.int32), ends_f.astype(jnp.int32)
    starts = ends - region
    dst = starts[:, None] + psum(pc_t, before_tile).astype(jnp.int32)
    nblk = ends[-1] // BM
    blk = jnp.minimum(jnp.arange(nb_max, dtype=jnp.int32), nblk - 1)
    blk_e = jnp.minimum(jnp.sum(ends[None, :] <= blk[:, None] * BM, axis=1), N_EXP - 1).astype(jnp.int32)
    has = region > 0
    slot_e = (psum(has.astype(F32)[None, :], upto_expert)[0].astype(jnp.int32) - 1) % 2
    later = has[None, :] & (ids[None, :] > ids[:, None])
    next_e = jnp.where(later.any(1), jnp.argmax(later, axis=1), -1)
    of_blk = lambda per_e: jnp.sum(jnp.where(blk_e[:, None] == ids[None, :], per_e[None, :], 0), axis=1)
    blocks = (blk_e, nblk.reshape(1).astype(jnp.int32), of_blk(slot_e).astype(jnp.int32),
              of_blk(next_e).astype(jnp.int32))
    flat = lambda a: a.reshape(-1).astype(jnp.int32)
    tail = ((starts + used).astype(jnp.int32), (region - used).astype(jnp.int32))
    off = off_t.astype(jnp.int32)
    e_idx = jnp.arange(N_EXP, dtype=jnp.int32)[:, None]
    col = lambda v: jnp.broadcast_to(jnp.asarray(v, jnp.int32), (N_EXP, 1))
    table = lambda so, do, ln: (flat(so), flat(do), flat(ln), ln.sum(0).astype(jnp.int32))
    pc_c = jnp.concatenate([pc, col(0)], 1)
    spare_c = _sorted_rows(TM_MOE) + SEG_ALIGN * e_idx
    seg_comb = table(jnp.where(pc_c > 0, jnp.concatenate([off, col(0)], 1), spare_c),
                     jnp.where(pc_c > 0, jnp.concatenate([dst, col(0)], 1), 0), jnp.maximum(pc_c, SEG_ALIGN))
    pc_d = jnp.concatenate([col(0), pc], 1)
    entry = jnp.arange(nt + 1, dtype=jnp.int32)[None, :]
    spare_d = nb_max * BM + ((entry % DISPATCH_SLOTS) * N_EXP + e_idx) * SEG_ALIGN
    seg_disp = table(jnp.where(pc_d > 0, jnp.concatenate([col(0), off], 1), 0),
                     jnp.where(pc_d > 0, jnp.concatenate([col(0), dst], 1), spare_d), jnp.maximum(pc_d, SEG_ALIGN))
    return seg_disp, seg_comb, tail, blocks


def _swiglu(u, w1, w3, w2):
    return _dot((_silu(_dot(u, w1)) * _dot(u, w3)).astype(BF16), w2)


def _round_bf16(x):
    return x.astype(BF16).astype(F32)


def _pack_pair(lo, hi):
    lo_b = lax.bitcast_convert_type(lo, jnp.uint32)
    hi_b = lax.bitcast_convert_type(hi, jnp.uint32)
    return (hi_b & jnp.uint32(0xFFFF0000)) | (lo_b >> 16)


def _unpack_pair(w):
    lo = lax.bitcast_convert_type(w << 16, F32).astype(BF16)
    hi = lax.bitcast_convert_type(w & jnp.uint32(0xFFFF0000), F32).astype(BF16)
    return lo, hi


def _segment_copies_dense(so_ref, do_ref, ln_ref, entry, entries, make_copy):
    for e in range(N_EXP):
        j = e * entries + entry
        make_copy(pl.multiple_of(so_ref[j], SEG_ALIGN), pl.multiple_of(do_ref[j], SEG_ALIGN),
                  pl.multiple_of(ln_ref[j], SEG_ALIGN)).start()


def _dispatch_kernel(nt, so_ref, do_ref, ln_ref, tt_ref, td_ref, tl_ref, nb_ref, u_ref, code_ref, segr_ref,
                     digits_ref, xs_out, buf, zbuf, sem):
    i = pl.program_id(0)
    r = buf.shape[1]
    nb_max = (xs_out.shape[0] - DISPATCH_SLOTS * N_EXP * SEG_ALIGN) // BM
    fill_sem = DISPATCH_SLOTS
    slot = i % DISPATCH_SLOTS
    prev_slot = (i + DISPATCH_SLOTS - 1) % DISPATCH_SLOTS

    def strip(slot_, so, do, ln):
        return pltpu.make_async_copy(buf.at[slot_, pl.ds(so, ln)], xs_out.at[pl.ds(do, ln)], sem.at[slot_])

    def wait_entry(slot_, entry):
        strip(slot_, 0, 0, pl.multiple_of(tt_ref[entry], SEG_ALIGN)).wait()

    def zero_fills(act):
        def tail(e, carry):
            ln = pl.multiple_of(tl_ref[e], SEG_ALIGN)

            @pl.when(ln > 0)
            def _():
                act(pltpu.make_async_copy(zbuf.at[pl.ds(0, ln)],
                                          xs_out.at[pl.ds(pl.multiple_of(td_ref[e], SEG_ALIGN), ln)], sem.at[fill_sem]))

            return carry

        lax.fori_loop(0, N_EXP, tail, 0)

        def blk(j, carry):
            act(pltpu.make_async_copy(zbuf.at[pl.ds(0, BM)], xs_out.at[pl.ds(pl.multiple_of(j * BM, BM), BM)],
                                      sem.at[fill_sem]))
            return carry

        lax.fori_loop(nb_ref[0], nb_max, blk, 0)

    @pl.when(i == 0)
    def _():
        zbuf[...] = jnp.zeros(zbuf.shape, jnp.uint32)
        buf[...] = jnp.zeros(buf.shape, jnp.uint32)
        spare = pltpu.make_async_copy(zbuf.at[pl.ds(0, DISPATCH_SLOTS * N_EXP * SEG_ALIGN)],
                                      xs_out.at[pl.ds(nb_max * BM, DISPATCH_SLOTS * N_EXP * SEG_ALIGN)],
                                      sem.at[fill_sem])
        spare.start()
        spare.wait()
        zero_fills(lambda cp: cp.start())

    @pl.when(i >= DISPATCH_SLOTS - 1)
    def _():
        wait_entry(slot, i - (DISPATCH_SLOTS - 1))

    row = lax.broadcasted_iota(jnp.int32, (r, GATE_LANES), 0).astype(F32)
    own = jnp.where(row >= segr_ref[0:1, :], jnp.where(row < segr_ref[1:2, :], 1.0, 0.0), 0.0).astype(BF16)
    z = _dot(jnp.concatenate([own, digits_ref[...]], axis=1), code_ref[...])
    perm = jnp.where(z == 0.0, 1.0, 0.0).astype(BF16)
    step = 256
    for c in range(0, PK, step):
        buf[slot, :, c:c + step] = _pack_pair(_dot(perm, u_ref[:, c:c + step]),
                                              _dot(perm, u_ref[:, PK + c:PK + c + step]))
    _segment_copies_dense(so_ref, do_ref, ln_ref, i, nt + 1, functools.partial(strip, prev_slot))

    @pl.when(i == nt - 1)
    def _():
        _segment_copies_dense(so_ref, do_ref, ln_ref, nt, nt + 1, functools.partial(strip, slot))
        for entry in range(max(nt - DISPATCH_SLOTS + 1, 0), nt + 1):
            wait_entry((entry + DISPATCH_SLOTS - 1) % DISPATCH_SLOTS, entry)
        zero_fills(lambda cp: cp.wait())


def _expert_kernel(layer, be_ref, nb_ref, ws_ref, nx_ref, x_ref, w1_hbm, w3_hbm, w2_hbm, y_ref,
                   w1f, w3f, w2f, w1b, w3b, w2b, sem):
    i = pl.program_id(0)
    live = i < nb_ref[0]

    def fetch(e, slot):
        return [pltpu.make_async_copy(src.at[layer, e], dst.at[slot], sem.at[slot, j])
                for j, (src, dst) in enumerate(((w1_hbm, w1f), (w3_hbm, w3f), (w2_hbm, w2f)))]

    @pl.when(live & ((i == 0) | (be_ref[i] != be_ref[jnp.maximum(i - 1, 0)])))
    def _():
        slot = ws_ref[i]

        @pl.when(i == 0)
        def _():
            for cp in fetch(be_ref[0], slot):
                cp.start()

        for cp in fetch(be_ref[i], slot):
            cp.wait()
        w1b[...] = w1f[slot].astype(BF16)
        w3b[...] = w3f[slot].astype(BF16)
        w2b[...] = w2f[slot].astype(BF16)

        @pl.when(nx_ref[i] >= 0)
        def _():
            for cp in fetch(nx_ref[i], 1 - slot):
                cp.start()

    @pl.when(live)
    def _():
        up = []
        for r0 in range(0, BM, EXP_SUB):
            lo, hi = _unpack_pair(x_ref[r0:r0 + EXP_SUB, :])
            up.append((_dot(lo, w1b[:PK, :]) + _dot(hi, w1b[PK:, :]), _dot(lo, w3b[:PK, :]) + _dot(hi, w3b[PK:, :])))
        for j, (a, g) in enumerate(up):
            hid = (_silu(a) * g).astype(BF16)
            y_ref[j * EXP_SUB:(j + 1) * EXP_SUB, :] = _pack_pair(_round_bf16(_dot(hid, w2b[:, :PK])),
                                                                 _round_bf16(_dot(hid, w2b[:, PK:])))


def _combine_kernel(nt, so_ref, do_ref, ln_ref, tt_ref, code_t_ref, segc_ref, digits_ref, u_ref, ys_ref,
                    s1_ref, s3_ref, s2_ref, h_ref, n_ref, mod_ref, o_ref, buf, sem):
    i = pl.program_id(0)
    slot = i % 2
    r = digits_ref.shape[1]

    def strip(slot_, so, do, ln):
        return pltpu.make_async_copy(ys_ref.at[pl.ds(do, ln)], buf.at[slot_, pl.ds(so, ln)], sem.at[slot_])

    @pl.when(i == 0)
    def _():
        buf[...] = jnp.zeros(buf.shape, jnp.uint32)
        _segment_copies_dense(so_ref, do_ref, ln_ref, 0, nt + 1, functools.partial(strip, 0))

    col = lax.broadcasted_iota(jnp.int32, (2 * N_EXP, r), 1).astype(F32)
    own = jnp.where(col >= segc_ref[:, 0:1], jnp.where(col < segc_ref[:, 1:2], 1.0, 0.0), 0.0).astype(BF16)
    z = _dot(code_t_ref[...], jnp.concatenate([own, own[:N_EXP], digits_ref[...]], axis=0))
    q = jnp.where(jnp.abs(z) < 0.5 * POS_SCALE, z, 0.0).astype(BF16)
    shared = _swiglu(u_ref[...], s1_ref[...], s3_ref[...], s2_ref[...])
    _segment_copies_dense(so_ref, do_ref, ln_ref, i + 1, nt + 1, functools.partial(strip, 1 - slot))

    strip(slot, 0, 0, pl.multiple_of(tt_ref[i], SEG_ALIGN)).wait()
    lo, hi = _unpack_pair(buf[slot, 0:r, :])
    moe = jnp.concatenate([_dot(q, lo), _dot(q, hi)], axis=1) + shared
    o_ref[...] = h_ref[...] + mod_ref[5:6, :] * _rms(moe, n_ref[3:4, :])

    @pl.when(i == nt - 1)
    def _():
        strip(1 - slot, 0, 0, pl.multiple_of(tt_ref[nt], SEG_ALIGN)).wait()


def _sorted_rows(tm):
    return -(-(TOP_K * tm + N_EXP * (SEG_ALIGN - 1)) // MXU_DIM) * MXU_DIM


def _max_blocks(n):
    nt = n // TM_MOE
    rows = TOP_K * n + N_EXP * (SEG_ALIGN - 1) * nt + N_EXP * (BM - SEG_ALIGN)
    return -(-rows // BM)


def _row_digits(r, width, first):
    idx = np.arange(r)
    digits = np.zeros((r, width), np.float32)
    digits[:, first] = idx // POS_RADIX * POS_RADIX
    digits[:, first + 1] = idx % POS_RADIX
    return jnp.asarray(digits, BF16)


def _dispatch_call(seg, tail, nblk, u2, code, segr, nb_max):
    n = u2.shape[0]
    r = _sorted_rows(TM_MOE)
    spare = DISPATCH_SLOTS * N_EXP * SEG_ALIGN
    return pl.pallas_call(
        functools.partial(_dispatch_kernel, n // TM_MOE),
        grid_spec=pltpu.PrefetchScalarGridSpec(
            num_scalar_prefetch=7,
            grid=(n // TM_MOE,),
            in_specs=[
                pl.BlockSpec((TM_MOE, D), lambda i, *_: (i, 0)),
                pl.BlockSpec((4 * N_EXP, TM_MOE), lambda i, *_: (0, i)),
                pl.BlockSpec((None, SUBLANES, GATE_LANES), lambda i, *_: (i, 0, 0)),
                pl.BlockSpec((r, GATE_LANES), lambda i, *_: (0, 0)),
            ],
            out_specs=pl.BlockSpec(memory_space=pl.ANY),
            scratch_shapes=[pltpu.VMEM((DISPATCH_SLOTS, r, PK), jnp.uint32),
                            pltpu.VMEM((max(BM, spare), PK), jnp.uint32),
                            pltpu.SemaphoreType.DMA((DISPATCH_SLOTS + 1,))],
        ),
        out_shape=jax.ShapeDtypeStruct((nb_max * BM + spare, PK), jnp.uint32),
        compiler_params=_cparams("arbitrary"),
        name="dispatch",
    )(*seg, tail[0], tail[1], nblk, u2, code, segr, _row_digits(r, GATE_LANES, N_EXP))


def _expert_call(layer, blk, xs, w1, w3, w2, nb_max):
    row_blk = lambda i, be, nb, *_: (jnp.maximum(jnp.minimum(i, nb[0] - 1), 0), 0)
    hbm = pl.BlockSpec(memory_space=pl.ANY)
    return pl.pallas_call(
        functools.partial(_expert_kernel, layer),
        grid_spec=pltpu.PrefetchScalarGridSpec(
            num_scalar_prefetch=4,
            grid=(nb_max,),
            in_specs=[pl.BlockSpec((BM, PK), row_blk), hbm, hbm, hbm],
            out_specs=pl.BlockSpec((BM, PK), row_blk),
            scratch_shapes=[pltpu.VMEM((2, D, F_EXP), F32), pltpu.VMEM((2, D, F_EXP), F32),
                            pltpu.VMEM((2, F_EXP, D), F32),
                            pltpu.VMEM((D, F_EXP), BF16), pltpu.VMEM((D, F_EXP), BF16),
                            pltpu.VMEM((F_EXP, D), BF16), pltpu.SemaphoreType.DMA((2, 3))],
        ),
        out_shape=jax.ShapeDtypeStruct(xs.shape, jnp.uint32),
        input_output_aliases={4: 0},
        compiler_params=_cparams("arbitrary"),
        name="experts",
    )(*blk, xs, w1, w3, w2)


def _combine_call(tabs, code_t, segc, u2, ys, s1, s3, s2, h, norms_l, mod_l, t):
    n = u2.shape[0]
    r = _sorted_rows(TM_MOE)
    tok = lambda w: pl.BlockSpec((TM_MOE, w), lambda i, *_: (i, 0))
    const = lambda shp: pl.BlockSpec(shp, lambda i, *_: (0,) * len(shp))
    return pl.pallas_call(
        functools.partial(_combine_kernel, n // TM_MOE),
        grid_spec=pltpu.PrefetchScalarGridSpec(
            num_scalar_prefetch=4,
            grid=(n // TM_MOE,),
            in_specs=[
                tok(4 * N_EXP),
                pl.BlockSpec((None, 2 * N_EXP, GATE_LANES), lambda i, *_: (i, 0, 0)),
                const((N_EXP, r)),
                tok(D),
                pl.BlockSpec(memory_space=pl.ANY),
                const((D, F_EXP)), const((D, F_EXP)), const((F_EXP, D)),
                tok(D),
                const((4, D)),
                pl.BlockSpec((None, 6, D), lambda i, *_: (i * TM_MOE // t, 0, 0)),
            ],
            out_specs=tok(D),
            scratch_shapes=[pltpu.VMEM((2, r + N_EXP * SEG_ALIGN, PK), jnp.uint32),
                            pltpu.SemaphoreType.DMA((2,))],
        ),
        out_shape=jax.ShapeDtypeStruct((n, D), F32),
        compiler_params=_cparams("arbitrary"),
        name="combine",
    )(*tabs, code_t, segc, _row_digits(r, N_EXP, 0).T, u2, ys, s1, s3, s2, h, norms_l, mod_l)


def _glu_kernel(h_ref, g_ref, mod_ref, w_ref, b_ref, o_ref):
    u = (_rms(h_ref[...], g_ref[...]) * (1.0 + mod_ref[1:2, :]) + mod_ref[0:1, :]).astype(BF16)
    step = 256
    for c in range(0, D, step):
        a = _dot(u, w_ref[:, c:c + step]) + b_ref[:, c:c + step]
        gate = _dot(u, w_ref[:, D + c:D + c + step]) + b_ref[:, D + c:D + c + step]
        o_ref[:, c:c + step] = (a * jax.nn.sigmoid(gate)).astype(BF16)


def _glu_call(h, g, mod_l, w1, b1, tm):
    b, t, _ = h.shape
    tok = lambda n: pl.BlockSpec((None, tm, n), lambda i, j: (i, j, 0))
    return pl.pallas_call(
        _glu_kernel,
        grid=(b, t // tm),
        in_specs=[
            tok(D),
            pl.BlockSpec((1, D), lambda i, j: (0, 0)),
            pl.BlockSpec((None, 6, D), lambda i, j: (i, 0, 0)),
            pl.BlockSpec((D, 2 * D), lambda i, j: (0, 0)),
            pl.BlockSpec((1, 2 * D), lambda i, j: (0, 0)),
        ],
        out_specs=tok(D),
        out_shape=jax.ShapeDtypeStruct((b, t, D), BF16),
        compiler_params=_cparams("parallel", "parallel"),
        name="glu",
    )(h, g, mod_l, w1, b1)


def _conv_kernel(cur_ref, prev_ref, next_ref, dw_ref, cv_ref, w2_ref, h_ref, n_ref, mod_ref, wr_ref,
                 h_out, u_out, lg_out, win, shifted, conv):
    j = pl.program_id(1)
    tm = cur_ref.shape[0]
    pad = CONV_W // 2
    win[0:HALO, :] = jnp.where(j > 0, prev_ref[...].astype(F32), 0.0)
    win[HALO:HALO + tm, :] = cur_ref[...].astype(F32)
    win[HALO + tm:, :] = jnp.where(j < pl.num_programs(1) - 1, next_ref[...].astype(F32), 0.0)
    ext = shifted.shape[1]
    full = win[...]
    shifted[0] = full[:ext, :]
    for s in range(1, SUBLANES):
        shifted[s] = pltpu.roll(full, full.shape[0] - s, axis=0)[:ext, :]

    def rows(ib, carry):
        r0 = pl.multiple_of(ib * 2 * SUBLANES, 2 * SUBLANES)
        for c0 in range(0, D, CONV_LANES):
            cols = slice(c0, c0 + CONV_LANES)
            acc0 = jnp.zeros((SUBLANES, CONV_LANES), F32)
            acc1 = jnp.zeros((SUBLANES, CONV_LANES), F32)
            for s in range(SUBLANES):
                taps = [(off // SUBLANES, off - (HALO - pad)) for off in range(HALO - pad, HALO - pad + CONV_W)
                        if off % SUBLANES == s]
                tiles = {a: shifted[s, pl.ds(r0 + a * SUBLANES, SUBLANES), cols]
                         for a in range(taps[0][0], taps[-1][0] + 2)}
                for a, tap in taps:
                    w = dw_ref[tap * SUBLANES:(tap + 1) * SUBLANES, cols]
                    acc0 = acc0 + tiles[a] * w
                    acc1 = acc1 + tiles[a + 1] * w
            conv[pl.ds(r0, SUBLANES), cols] = acc0
            conv[pl.ds(r0 + SUBLANES, SUBLANES), cols] = acc1
        return carry

    lax.fori_loop(0, tm // (2 * SUBLANES), rows, 0)
    for rows in _row_blocks(tm):
        uf = conv[rows, :] + cv_ref[0:1, :]
        mu = jnp.mean(uf, axis=-1, keepdims=True)
        var = jnp.mean(jnp.square(uf - mu), axis=-1, keepdims=True)
        uf = (uf - mu) * lax.rsqrt(var + LN_EPS) * cv_ref[1:2, :] + cv_ref[2:3, :]
        y = _dot(_silu(uf).astype(BF16), w2_ref[...]) + cv_ref[3:4, :]
        _mixer_epilogue(y, rows, h_ref, n_ref, mod_ref, wr_ref, h_out, u_out, lg_out)


def _conv_call(glu, dw, cvec, w2, h, norms_l, mod_l, wr_t, tm):
    b, t, _ = h.shape
    tok, ep_in, wr_spec, out_specs, out_shape = _epilogue_specs(b, t, tm)
    r = tm // HALO
    last = t // HALO - 1
    return pl.pallas_call(
        _conv_kernel,
        grid=(b, t // tm),
        in_specs=[
            tok(D),
            pl.BlockSpec((None, HALO, D), lambda i, j: (i, jnp.maximum(j * r - 1, 0), 0)),
            pl.BlockSpec((None, HALO, D), lambda i, j: (i, jnp.minimum((j + 1) * r, last), 0)),
            pl.BlockSpec((CONV_W * SUBLANES, D), lambda i, j: (0, 0)),
            pl.BlockSpec((4, D), lambda i, j: (0, 0)),
            pl.BlockSpec((D, D), lambda i, j: (0, 0)),
        ] + ep_in + [wr_spec],
        out_specs=out_specs,
        out_shape=out_shape,
        scratch_shapes=[pltpu.VMEM((tm + 2 * HALO, D), F32),
                        pltpu.VMEM((SUBLANES, tm + (CONV_W // SUBLANES) * SUBLANES, D), F32),
                        pltpu.VMEM((tm, D), F32)],
        compiler_params=_cparams("parallel", "parallel"),
        name="conv",
    )(glu, glu, glu, dw, cvec, w2, h, norms_l, mod_l, wr_t)


def _tile(t, pref):
    return pref if t % pref == 0 else t


def kernel(x, c, ctx, c_ctx, mod_w, mod_b, norms, ret_w_in, ret_w_out, ret_decay, conv_w1, conv_b1, conv_dw,
           conv_b_dw, conv_ln_g, conv_ln_b, conv_w2, conv_b2, moe_router, moe_bias, moe_w1, moe_w3, moe_w2,
           shared_w1, shared_w3, shared_w2):
    b, t, _ = x.shape
    l = ctx.shape[1]
    tm = _tile(t, 512)

    rows = -(-(b + 1) // 8) * 8
    cs = jnp.concatenate([c, c_ctx[None, :], jnp.zeros((rows - b - 1, D), F32)], 0)
    mod = _mod_call(cs, mod_w, mod_b).reshape(mod_w.shape[0], rows, 6, D)

    idx = jnp.arange(t, dtype=jnp.int32)
    pos_lat = jnp.stack([jnp.full((t,), l, jnp.int32), idx // GRID_W, idx % GRID_W], -1)
    zl = jnp.zeros((l,), jnp.int32)
    pos_ctx = jnp.stack([jnp.arange(l, dtype=jnp.int32), zl, zl], -1)
    ang_lat, ang_ctx = _rope_angles(pos_lat), _rope_angles(pos_ctx)

    w_qk = ret_w_in[0][:, :2 * QK_TOT].reshape(D, 2 * HEADS, DK // 2, 2).swapaxes(-1, -2).reshape(D, 2 * QK_TOT)
    w_in = jnp.concatenate([w_qk, ret_w_in[0][:, 2 * QK_TOT:]], axis=1).astype(BF16)
    log_g = jax.nn.log_sigmoid(ret_decay[0].astype(F32))

    n = b * t
    nb_max = _max_blocks(n)

    def moe_layer(i, h1, u2, logits_t):
        code, code_t, segc, segr, pc_t, off_t = _route_call(logits_t, moe_bias[i])
        seg_disp, seg_comb, tail, blocks = _dispatch_tables(pc_t, off_t, nb_max)
        u2f = u2.reshape(n, D)
        xs = _dispatch_call(seg_disp, tail, blocks[1], u2f, code, segr, nb_max)
        ys = _expert_call(i, blocks, xs, moe_w1, moe_w3, moe_w2, nb_max)
        out = _combine_call(seg_comb, code_t, segc, u2f, ys, shared_w1[i].astype(BF16), shared_w3[i].astype(BF16),
                            shared_w2[i].astype(BF16), h1.reshape(n, D), norms[i], mod[i], t)
        return out.reshape(b, t, D)

    q, k, v, gt = _inproj_call(x, norms[0, 0:1], mod[0], w_in, jnp.cos(ang_lat), jnp.sin(ang_lat), tm)
    sf0, sb0 = _ctxstate_call(log_g, ctx, norms[0, 0:1], mod[0, b], w_in[:, QK_TOT:2 * QK_TOT + V_TOT],
                              jnp.cos(ang_ctx), jnp.sin(ang_ctx))
    yn = _retention_call(log_g, q, k, v, sf0, sb0)
    h1, u2, logits_t = _readout_call(yn, gt, x, norms[0], mod[0], ret_w_out[0].astype(BF16),
                                     moe_router[0].T, _tile(t, 2 * EPI_ROWS))
    h2 = moe_layer(0, h1, u2, logits_t)

    glu = _glu_call(h2, norms[1, 0:1], mod[1], conv_w1[0].astype(BF16), conv_b1[0][None, :], tm)
    cvec = jnp.stack([conv_b_dw[0], conv_ln_g[0], conv_ln_b[0], conv_b2[0]], 0)
    dw_rows = jnp.repeat(conv_dw[0], SUBLANES, axis=0)
    h3, u2, logits_t = _conv_call(glu, dw_rows, cvec, conv_w2[0].astype(BF16), h2, norms[1], mod[1],
                                  moe_router[1].T, tm)
    return moe_layer(1, h3, u2, logits_t)
```

```python
import functools

import jax
import jax.numpy as jnp
import numpy as np
from jax import lax
from jax.experimental import pallas as pl
from jax.experimental.pallas import tpu as pltpu

F32 = jnp.float32
BF16 = jnp.bfloat16

D = 1024
HEADS = 4
DK = D // HEADS
DV = 2 * DK
QK_TOT = HEADS * DK
V_TOT = HEADS * DV
IN_DIM = 2 * QK_TOT + 2 * V_TOT
ROPE_AXES = (64, 96, 96)
ROPE_THETA = 10000.0
GRID_W = 64
CONV_W = 31
HALO = 16
SUBLANES = 8
CONV_LANES = 256
CONV_TILES = 4
N_EXP = 64
N_GRP = 8
PER_GRP = N_EXP // N_GRP
TOP_K = 8
TOP_G = 4
F_EXP = 256
ROUTE_SCALE = 2.5
GATE_LANES = 128
MXU_DIM = 256
TM_MOE = 256
ROUTE_TILES = 4
SEG_ALIGN = 8
POS_RADIX = 256
POS_SCALE = 8
BM = 1024
DISPATCH_SLOTS = 4
EXP_SUB = 256
PK = D // 2
NORM_EPS = 1e-6
LN_EPS = 1e-5
RET_CHUNK = 256
EPI_ROWS = 512
VMEM_LIMIT = 56 * 1024 * 1024

_HI = lax.Precision.HIGHEST


def _cparams(*sem):
    return pltpu.CompilerParams(dimension_semantics=sem, vmem_limit_bytes=VMEM_LIMIT)


def _dot(a, b):
    return jnp.dot(a, b, preferred_element_type=F32)


def _rms(xf, g):
    return xf * lax.rsqrt(jnp.mean(xf * xf, axis=-1, keepdims=True) + NORM_EPS) * g


def _silu(x):
    return x * jax.nn.sigmoid(x)


def _mod_kernel(cs_ref, w_ref, b_ref, o_ref):
    s = _silu(cs_ref[...])
    o_ref[...] = jnp.dot(s, w_ref[...], preferred_element_type=F32, precision=_HI) + b_ref[...]


def _mod_call(cs, mod_w, mod_b):
    depth, _, n6 = mod_w.shape
    tn = 1536
    rows = cs.shape[0]
    return pl.pallas_call(
        _mod_kernel,
        grid=(depth, n6 // tn),
        in_specs=[
            pl.BlockSpec((rows, D), lambda i, j: (0, 0)),
            pl.BlockSpec((None, D, tn), lambda i, j: (i, 0, j)),
            pl.BlockSpec((None, 1, tn), lambda i, j: (i, 0, j)),
        ],
        out_specs=pl.BlockSpec((None, rows, tn), lambda i, j: (i, 0, j)),
        out_shape=jax.ShapeDtypeStruct((depth, rows, n6), F32),
        compiler_params=_cparams("parallel", "parallel"),
        name="mod",
    )(cs, mod_w, mod_b.reshape(depth, 1, n6))


def _rope_angles(pos):
    parts = []
    for a, d in enumerate(ROPE_AXES):
        inv = ROPE_THETA ** (-jnp.arange(0, d, 2, dtype=F32) / d)
        parts.append(pos[:, a:a + 1].astype(F32) * inv[None, :])
    return jnp.concatenate(parts, -1)


def _rope_store(z, cos, sin, scale, o_ref, col):
    half = DK // 2
    x0, x1 = z[:, :half], z[:, half:]
    o_ref[:, col:col + half] = ((x0 * cos - x1 * sin) * scale).astype(o_ref.dtype)
    o_ref[:, col + half:col + DK] = ((x0 * sin + x1 * cos) * scale).astype(o_ref.dtype)


def _inproj_kernel(x_ref, g_ref, mod_ref, w_ref, cos_ref, sin_ref, q_ref, k_ref, v_ref, gt_ref):
    u = (_rms(x_ref[...], g_ref[...]) * (1.0 + mod_ref[1:2, :]) + mod_ref[0:1, :]).astype(BF16)
    cos, sin = cos_ref[...], sin_ref[...]
    step = 512
    for j in range(V_TOT // step):
        c1 = 2 * QK_TOT + V_TOT + j * step
        gt_ref[:, j * step:(j + 1) * step] = _silu(_dot(u, w_ref[:, c1:c1 + step])).astype(BF16)
    for h in range(HEADS):
        _rope_store(_dot(u, w_ref[:, h * DK:(h + 1) * DK]), cos, sin, DK ** -0.5, q_ref, h * DK)
        _rope_store(_dot(u, w_ref[:, QK_TOT + h * DK:QK_TOT + (h + 1) * DK]), cos, sin, 1.0, k_ref, h * DK)
    for j in range(V_TOT // step):
        c0 = 2 * QK_TOT + j * step
        v_ref[:, j * step:(j + 1) * step] = _dot(u, w_ref[:, c0:c0 + step]).astype(BF16)


def _inproj_call(x, g, mod_l, w_in, cos, sin, tm):
    b, t, _ = x.shape
    grid = (b, t // tm)
    tok = lambda n: pl.BlockSpec((None, tm, n), lambda i, j: (i, j, 0))
    return pl.pallas_call(
        _inproj_kernel,
        grid=grid,
        in_specs=[
            tok(D),
            pl.BlockSpec((1, D), lambda i, j: (0, 0)),
            pl.BlockSpec((None, 6, D), lambda i, j: (i, 0, 0)),
            pl.BlockSpec((D, IN_DIM), lambda i, j: (0, 0), pipeline_mode=pl.Buffered(1)),
            pl.BlockSpec((tm, DK // 2), lambda i, j: (j, 0)),
            pl.BlockSpec((tm, DK // 2), lambda i, j: (j, 0)),
        ],
        out_specs=[tok(QK_TOT), tok(QK_TOT), tok(V_TOT), tok(V_TOT)],
        out_shape=[
            jax.ShapeDtypeStruct((b, t, QK_TOT), BF16),
            jax.ShapeDtypeStruct((b, t, QK_TOT), BF16),
            jax.ShapeDtypeStruct((b, t, V_TOT), BF16),
            jax.ShapeDtypeStruct((b, t, V_TOT), BF16),
        ],
        compiler_params=_cparams("parallel", "parallel"),
        name="inproj",
    )(x, g, mod_l, w_in, cos, sin)


def _row_pow(lg, expo_fn, rows, cols):
    i = lax.broadcasted_iota(jnp.int32, (rows, cols), 0).astype(F32)
    return jnp.exp(lg * expo_fn(i))


def _ctxstate_kernel(lg_ref, c_ref, g_ref, mod_ref, w_ref, cos_ref, sin_ref, sf_ref, sb_ref, k_scr):
    l = c_ref.shape[0]
    u = (_rms(c_ref[...], g_ref[...]) * (1.0 + mod_ref[1:2, :]) + mod_ref[0:1, :]).astype(BF16)
    cos, sin = cos_ref[...], sin_ref[...]
    for h in range(HEADS):
        _rope_store(_dot(u, w_ref[:, h * DK:(h + 1) * DK]), cos, sin, 1.0, k_scr, h * DK)
    for h in range(HEADS):
        kh = k_scr[:, h * DK:(h + 1) * DK]
        vh = _dot(u, w_ref[:, QK_TOT + h * DV:QK_TOT + (h + 1) * DV]).astype(BF16)
        wf = _row_pow(lg_ref[0, h], lambda i: (l - 1.0) - i, l, DK)
        wb = _row_pow(lg_ref[1, h], lambda i: i, l, DK)
        sf_ref[h] = _dot((kh * wf).T.astype(BF16), vh)
        sb_ref[h] = _dot((kh * wb).T.astype(BF16), vh)


def _ctxstate_call(log_g, ctx, g, mod_row, w_kv, cos, sin):
    b, l, _ = ctx.shape
    st = jax.ShapeDtypeStruct((b, HEADS, DK, DV), F32)
    st_spec = pl.BlockSpec((None, HEADS, DK, DV), lambda i: (i, 0, 0, 0))
    return pl.pallas_call(
        _ctxstate_kernel,
        grid=(b,),
        in_specs=[
            pl.BlockSpec(memory_space=pltpu.SMEM),
            pl.BlockSpec((None, l, D), lambda i: (i, 0, 0)),
            pl.BlockSpec((1, D), lambda i: (0, 0)),
            pl.BlockSpec((6, D), lambda i: (0, 0)),
            pl.BlockSpec((D, QK_TOT + V_TOT), lambda i: (0, 0)),
            pl.BlockSpec((l, DK // 2), lambda i: (0, 0)),
            pl.BlockSpec((l, DK // 2), lambda i: (0, 0)),
        ],
        out_specs=[st_spec, st_spec],
        out_shape=[st, st],
        scratch_shapes=[pltpu.VMEM((l, QK_TOT), F32)],
        compiler_params=_cparams("parallel"),
        name="ctxstate",
    )(log_g, ctx, g, mod_row, w_kv, cos, sin)


def _retention_kernel(lg_ref, q_ref, k_ref, v_ref, sf0_ref, sb0_ref, o_ref, acc, sf, sb):
    h = pl.program_id(1)
    t = q_ref.shape[0]
    c = RET_CHUNK
    nc = t // c
    lgf, lgb = lg_ref[0, h], lg_ref[1, h]
    ri = lax.broadcasted_iota(jnp.int32, (c, c), 0).astype(F32)
    ci = lax.broadcasted_iota(jnp.int32, (c, c), 1).astype(F32)
    rel = ri - ci
    mask = (jnp.where(rel >= 0, jnp.exp(lgf * jnp.maximum(rel, 0.0)), 0.0)
            + jnp.where(rel <= 0, jnp.exp(lgb * jnp.maximum(-rel, 0.0)), 0.0))
    qdf = _row_pow(lgf, lambda i: i + 1.0, c, DV)
    qdb = _row_pow(lgb, lambda i: c - i, c, DV)
    kdf = _row_pow(lgf, lambda i: (c - 1.0) - i, c, DK)
    kdb = _row_pow(lgb, lambda i: i, c, DK)
    cdf = jnp.exp(jnp.full((1, DV), lgf * c, F32))
    cdb = jnp.exp(jnp.full((1, DV), lgb * c, F32))
    sf[...] = sf0_ref[...]
    sb[...] = sb0_ref[...]

    def rows(ic):
        return pl.ds(pl.multiple_of(ic * c, c), c)

    def fwd(rs):
        q, k, v = q_ref[rs, :], k_ref[rs, :], v_ref[rs, :]
        s = lax.dot_general(q, k, (((1,), (1,)), ((), ())), preferred_element_type=F32)
        o = _dot((s * mask).astype(BF16), v) + qdf * _dot(q, sf[...].astype(BF16))
        sf[...] = sf[...] * cdf + _dot((k.astype(F32) * kdf).T.astype(BF16), v)
        return o

    def bwd(rs):
        q, k, v = q_ref[rs, :], k_ref[rs, :], v_ref[rs, :]
        o = qdb * _dot(q, sb[...].astype(BF16))
        sb[...] = sb[...] * cdb + _dot((k.astype(F32) * kdb).T.astype(BF16), v)
        return o

    def finish(rs, o):
        o_ref[rs, :] = (o * lax.rsqrt(jnp.mean(o * o, axis=-1, keepdims=True) + NORM_EPS)).astype(BF16)

    def first_half(j, carry):
        of, ob = fwd(rows(j)), bwd(rows(nc - 1 - j))
        acc[rows(j), :] = of
        acc[rows(nc - 1 - j), :] = ob
        return carry

    def second_half(j, carry):
        of, ob = fwd(rows(j)), bwd(rows(nc - 1 - j))
        finish(rows(j), acc[rows(j), :] + of)
        finish(rows(nc - 1 - j), acc[rows(nc - 1 - j), :] + ob)
        return carry

    lax.fori_loop(0, nc // 2, first_half, 0, unroll=True)
    lax.fori_loop(nc // 2, nc, second_half, 0, unroll=True)


def _retention_call(log_g, q, k, v, sf0, sb0):
    b, t, _ = q.shape
    qk_spec = pl.BlockSpec((None, t, DK), lambda i, h: (i, 0, h))
    v_spec = pl.BlockSpec((None, t, DV), lambda i, h: (i, 0, h))
    s_spec = pl.BlockSpec((None, None, DK, DV), lambda i, h: (i, h, 0, 0))
    return pl.pallas_call(
        _retention_kernel,
        grid=(b, HEADS),
        in_specs=[pl.BlockSpec(memory_space=pltpu.SMEM), qk_spec, qk_spec, v_spec, s_spec, s_spec],
        out_specs=v_spec,
        out_shape=jax.ShapeDtypeStruct((b, t, V_TOT), BF16),
        scratch_shapes=[pltpu.VMEM((t, DV), F32), pltpu.VMEM((DK, DV), F32), pltpu.VMEM((DK, DV), F32)],
        compiler_params=_cparams("parallel", "parallel"),
        name="retention",
    )(log_g, q, k, v, sf0, sb0)


def _mixer_epilogue(y, rows, h_ref, n_ref, mod_ref, wr_ref, h_out, u_out, lg_out):
    h1 = h_ref[rows, :] + mod_ref[2:3, :] * _rms(y, n_ref[1:2, :])
    h_out[rows, :] = h1
    u2 = _rms(h1, n_ref[2:3, :]) * (1.0 + mod_ref[4:5, :]) + mod_ref[3:4, :]
    u_hi = u2.astype(BF16)
    u_out[rows, :] = u_hi
    u_lo = (u2 - u_hi.astype(F32)).astype(BF16)
    w = wr_ref[...]
    w_hi = w.astype(BF16)
    w_lo = (w - w_hi.astype(F32)).astype(BF16)
    nt_dot = lambda a, b: lax.dot_general(a, b, (((1,), (1,)), ((), ())), preferred_element_type=F32)
    lg_out[:, rows] = nt_dot(w_hi, u_hi) + (nt_dot(w_hi, u_lo) + nt_dot(w_lo, u_hi))


def _row_blocks(tm):
    return [slice(r0, r0 + EPI_ROWS) for r0 in range(0, tm, EPI_ROWS)]


def _readout_kernel(y_ref, gt_ref, h_ref, n_ref, mod_ref, w_ref, wr_ref, h_out, u_out, lg_out):
    blocks = _row_blocks(y_ref.shape[0])
    ys = [_dot(y_ref[rows, :] * gt_ref[rows, :], w_ref[...]) for rows in blocks]
    for rows, y in zip(blocks, ys):
        _mixer_epilogue(y, rows, h_ref, n_ref, mod_ref, wr_ref, h_out, u_out, lg_out)


def _epilogue_specs(b, t, tm):
    nt = t // tm
    tok = lambda n: pl.BlockSpec((None, tm, n), lambda i, j: (i, j, 0))
    in_specs = [
        tok(D),
        pl.BlockSpec((4, D), lambda i, j: (0, 0)),
        pl.BlockSpec((None, 6, D), lambda i, j: (i, 0, 0)),
    ]
    wr_spec = pl.BlockSpec((N_EXP, D), lambda i, j: (0, 0))
    out_specs = [tok(D), tok(D), pl.BlockSpec((N_EXP, tm), lambda i, j: (0, i * nt + j))]
    out_shape = [
        jax.ShapeDtypeStruct((b, t, D), F32),
        jax.ShapeDtypeStruct((b, t, D), BF16),
        jax.ShapeDtypeStruct((N_EXP, b * t), F32),
    ]
    return tok, in_specs, wr_spec, out_specs, out_shape


def _readout_call(yn, gt, h, norms_l, mod_l, w_out, wr_t, tm):
    b, t, _ = h.shape
    tok, ep_in, wr_spec, out_specs, out_shape = _epilogue_specs(b, t, tm)
    return pl.pallas_call(
        _readout_kernel,
        grid=(b, t // tm),
        in_specs=[tok(V_TOT), tok(V_TOT)] + ep_in
        + [pl.BlockSpec((V_TOT, D), lambda i, j: (0, 0), pipeline_mode=pl.Buffered(1)), wr_spec],
        out_specs=out_specs,
        out_shape=out_shape,
        compiler_params=_cparams("parallel", "parallel"),
        name="readout",
    )(yn, gt, h, norms_l, mod_l, w_out, wr_t)


def _route_tile(logits, bias):
    tr = logits.shape[1]
    s = jax.nn.sigmoid(logits)
    biased = s + bias
    neg = -jnp.inf
    b3 = biased.reshape(N_GRP, PER_GRP, tr)
    io3 = lax.broadcasted_iota(jnp.int32, b3.shape, 1).astype(F32)
    m1 = jnp.max(b3, axis=1, keepdims=True)
    i1 = jnp.min(jnp.where(b3 == m1, io3, float(PER_GRP)), axis=1, keepdims=True)
    m2 = jnp.max(jnp.where(io3 == i1, neg, b3), axis=1, keepdims=True)
    gs = (m1 + m2).reshape(N_GRP, tr)
    iog = lax.broadcasted_iota(jnp.int32, gs.shape, 0).astype(F32)
    gsel = jnp.zeros(gs.shape, F32)
    work = gs
    for _ in range(TOP_G):
        m = jnp.max(work, axis=0, keepdims=True)
        gi = jnp.min(jnp.where(work == m, iog, float(N_GRP)), axis=0, keepdims=True)
        hit = iog == gi
        gsel = jnp.where(hit, 1.0, gsel)
        work = jnp.where(hit, neg, work)
    emask = jnp.broadcast_to(gsel.reshape(N_GRP, 1, tr), b3.shape).reshape(N_EXP, tr)
    work = jnp.where(emask > 0.0, biased, neg)
    ioe = lax.broadcasted_iota(jnp.int32, work.shape, 0).astype(F32)
    esel = jnp.zeros(work.shape, F32)
    for _ in range(TOP_K):
        m = jnp.max(work, axis=0, keepdims=True)
        ei = jnp.min(jnp.where(work == m, ioe, float(N_EXP)), axis=0, keepdims=True)
        hit = ioe == ei
        esel = jnp.where(hit, 1.0, esel)
        work = jnp.where(hit, neg, work)
    wsel = esel * s
    gate = wsel / jnp.sum(wsel, axis=0, keepdims=True) * ROUTE_SCALE

    t_r = lax.broadcasted_iota(jnp.int32, (tr, tr), 0)
    t_c = lax.broadcasted_iota(jnp.int32, (tr, tr), 1)
    rank = _dot(esel.astype(BF16), jnp.where(t_r < t_c, 1.0, 0.0).astype(BF16))
    cnt = jnp.sum(esel, axis=1, keepdims=True)
    pc = jnp.floor((cnt + (SEG_ALIGN - 1.0)) * (1.0 / SEG_ALIGN)) * SEG_ALIGN
    e_r = lax.broadcasted_iota(jnp.int32, (N_EXP, N_EXP), 0)
    e_c = lax.broadcasted_iota(jnp.int32, (N_EXP, N_EXP), 1)
    off = jnp.dot(jnp.where(e_c < e_r, 1.0, 0.0), jnp.broadcast_to(pc, (N_EXP, 128)),
                  preferred_element_type=F32, precision=_HI)[:, 0:1]
    p = off + rank
    p_hi = jnp.where(esel > 0.0, jnp.floor(p * (1.0 / POS_RADIX)) * POS_RADIX, -float(POS_RADIX))
    p_lo = jnp.where(esel > 0.0, p - p_hi, 0.0)
    minus_r = jnp.where(lax.broadcasted_iota(jnp.int32, (N_EXP, tr), 0) < 2, -float(POS_SCALE), 0.0)
    code = jnp.concatenate([p_hi * POS_SCALE, p_lo * POS_SCALE, gate, minus_r], 0)
    code_t = jnp.concatenate([code[:2 * N_EXP].T, code[2 * N_EXP:].T], 1)
    lane = lax.broadcasted_iota(jnp.int32, (2 * N_EXP, GATE_LANES), 1)
    off2 = jnp.concatenate([off, off], 0)
    end2 = jnp.concatenate([off + pc, off + pc], 0)
    bounds = jnp.where(lane == 0, off2, jnp.where(lane == 1, end2, 0.0))
    return code.astype(BF16), code_t.astype(BF16), bounds, bounds.T[0:SUBLANES, :], pc, off


def _route_kernel(lg_ref, b_ref, code_ref, code_t_ref, segc_ref, segr_ref, pc_ref, off_ref):
    i = pl.program_id(0)

    @pl.when(i == 0)
    def _():
        pc_ref[...] = jnp.zeros(pc_ref.shape, F32)
        off_ref[...] = jnp.zeros(off_ref.shape, F32)

    tile_lane = lax.broadcasted_iota(jnp.int32, pc_ref.shape, 1)
    for k in range(ROUTE_TILES):
        cols = slice(k * TM_MOE, (k + 1) * TM_MOE)
        code, code_t, bounds, bounds_t, pc, off = _route_tile(lg_ref[:, cols], b_ref[...])
        code_ref[:, cols] = code
        code_t_ref[cols, :] = code_t
        segc_ref[k] = bounds
        segr_ref[k] = bounds_t
        pc_ref[...] = jnp.where(tile_lane == i * ROUTE_TILES + k, pc, pc_ref[...])
        off_ref[...] = jnp.where(tile_lane == i * ROUTE_TILES + k, off, off_ref[...])


def _route_call(logits_t, bias):
    n = logits_t.shape[1]
    nt = n // TM_MOE
    tr = ROUTE_TILES * TM_MOE
    return pl.pallas_call(
        _route_kernel,
        grid=(nt // ROUTE_TILES,),
        in_specs=[pl.BlockSpec((N_EXP, tr), lambda i: (0, i)), pl.BlockSpec((N_EXP, 1), lambda i: (0, 0))],
        out_specs=[
            pl.BlockSpec((4 * N_EXP, tr), lambda i: (0, i)),
            pl.BlockSpec((tr, 4 * N_EXP), lambda i: (i, 0)),
            pl.BlockSpec((ROUTE_TILES, 2 * N_EXP, GATE_LANES), lambda i: (i, 0, 0)),
            pl.BlockSpec((ROUTE_TILES, SUBLANES, GATE_LANES), lambda i: (i, 0, 0)),
            pl.BlockSpec((N_EXP, nt), lambda i: (0, 0)),
            pl.BlockSpec((N_EXP, nt), lambda i: (0, 0)),
        ],
        out_shape=[
            jax.ShapeDtypeStruct((4 * N_EXP, n), BF16),
            jax.ShapeDtypeStruct((n, 4 * N_EXP), BF16),
            jax.ShapeDtypeStruct((nt, 2 * N_EXP, GATE_LANES), F32),
            jax.ShapeDtypeStruct((nt, SUBLANES, GATE_LANES), F32),
            jax.ShapeDtypeStruct((N_EXP, nt), F32),
            jax.ShapeDtypeStruct((N_EXP, nt), F32),
        ],
        compiler_params=_cparams("arbitrary"),
        name="route",
    )(logits_t, bias.reshape(N_EXP, 1))


def _dispatch_tables(pc_t, off_t, nb_max):
    nt = pc_t.shape[1]
    ids = jnp.arange(N_EXP, dtype=jnp.int32)
    psum = lambda a, m: jnp.dot(a, m, precision=_HI, preferred_element_type=F32)
    before_tile = (jnp.arange(nt)[:, None] < jnp.arange(nt)[None, :]).astype(F32)
    upto_expert = (ids[:, None] <= ids[None, :]).astype(F32)
    used_f = pc_t.sum(1)
    region_f = jnp.ceil(used_f * (1.0 / BM)) * BM
    ends_f = psum(region_f[None, :], upto_expert)[0]
    pc = pc_t.astype(jnp.int32)
    used, region, ends = used_f.astype(jnp.int32), region_f.astype(jnp.int32), ends_f.astype(jnp.int32)
    starts = ends - region
    dst = starts[:, None] + psum(pc_t, before_tile).astype(jnp.int32)
    nblk = ends[-1] // BM
    blk = jnp.minimum(jnp.arange(nb_max, dtype=jnp.int32), nblk - 1)
    blk_e = jnp.minimum(jnp.sum(ends[None, :] <= blk[:, None] * BM, axis=1), N_EXP - 1).astype(jnp.int32)
    has = region > 0
    slot_e = (psum(has.astype(F32)[None, :], upto_expert)[0].astype(jnp.int32) - 1) % 2
    later = has[None, :] & (ids[None, :] > ids[:, None])
    next_e = jnp.where(later.any(1), jnp.argmax(later, axis=1), -1)
    of_blk = lambda per_e: jnp.sum(jnp.where(blk_e[:, None] == ids[None, :], per_e[None, :], 0), axis=1)
    blocks = (blk_e, nblk.reshape(1).astype(jnp.int32), of_blk(slot_e).astype(jnp.int32),
              of_blk(next_e).astype(jnp.int32))
    flat = lambda a: a.reshape(-1).astype(jnp.int32)
    tail = ((starts + used).astype(jnp.int32), (region - used).astype(jnp.int32))
    off = off_t.astype(jnp.int32)
    e_idx = jnp.arange(N_EXP, dtype=jnp.int32)[:, None]
    col = lambda v: jnp.broadcast_to(jnp.asarray(v, jnp.int32), (N_EXP, 1))
    table = lambda so, do, ln: (flat(so), flat(do), flat(ln), ln.sum(0).astype(jnp.int32))
    pc_c = jnp.concatenate([pc, col(0)], 1)
    spare_c = _sorted_rows(TM_MOE) + SEG_ALIGN * e_idx
    seg_comb = table(jnp.where(pc_c > 0, jnp.concatenate([off, col(0)], 1), spare_c),
                     jnp.where(pc_c > 0, jnp.concatenate([dst, col(0)], 1), 0), jnp.maximum(pc_c, SEG_ALIGN))
    pc_d = jnp.concatenate([col(0), pc], 1)
    entry = jnp.arange(nt + 1, dtype=jnp.int32)[None, :]
    spare_d = nb_max * BM + ((entry % DISPATCH_SLOTS) * N_EXP + e_idx) * SEG_ALIGN
    seg_disp = table(jnp.where(pc_d > 0, jnp.concatenate([col(0), off], 1), 0),
                     jnp.where(pc_d > 0, jnp.concatenate([col(0), dst], 1), spare_d), jnp.maximum(pc_d, SEG_ALIGN))
    return seg_disp, seg_comb, tail, blocks


def _swiglu(u, w1, w3, w2):
    return _dot((_silu(_dot(u, w1)) * _dot(u, w3)).astype(BF16), w2)


def _round_bf16(x):
    return x.astype(BF16).astype(F32)


def _pack_pair(lo, hi):
    lo_b = lax.bitcast_convert_type(lo, jnp.uint32)
    hi_b = lax.bitcast_convert_type(hi, jnp.uint32)
    return (hi_b & jnp.uint32(0xFFFF0000)) | (lo_b >> 16)


def _unpack_pair(w):
    lo = lax.bitcast_convert_type(w << 16, F32).astype(BF16)
    hi = lax.bitcast_convert_type(w & jnp.uint32(0xFFFF0000), F32).astype(BF16)
    return lo, hi


def _segment_copies_dense(so_ref, do_ref, ln_ref, entry, entries, make_copy):
    for e in range(N_EXP):
        j = e * entries + entry
        make_copy(pl.multiple_of(so_ref[j], SEG_ALIGN), pl.multiple_of(do_ref[j], SEG_ALIGN),
                  pl.multiple_of(ln_ref[j], SEG_ALIGN)).start()


def _dispatch_kernel(nt, so_ref, do_ref, ln_ref, tt_ref, td_ref, tl_ref, nb_ref, u_ref, code_ref, segr_ref,
                     digits_ref, xs_out, buf, zbuf, sem):
    i = pl.program_id(0)
    r = buf.shape[1]
    nb_max = (xs_out.shape[0] - DISPATCH_SLOTS * N_EXP * SEG_ALIGN) // BM
    fill_sem = DISPATCH_SLOTS
    slot = i % DISPATCH_SLOTS
    prev_slot = (i + DISPATCH_SLOTS - 1) % DISPATCH_SLOTS

    def strip(slot_, so, do, ln):
        return pltpu.make_async_copy(buf.at[slot_, pl.ds(so, ln)], xs_out.at[pl.ds(do, ln)], sem.at[slot_])

    def wait_entry(slot_, entry):
        strip(slot_, 0, 0, pl.multiple_of(tt_ref[entry], SEG_ALIGN)).wait()

    def zero_fills(act):
        def tail(e, carry):
            ln = pl.multiple_of(tl_ref[e], SEG_ALIGN)

            @pl.when(ln > 0)
            def _():
                act(pltpu.make_async_copy(zbuf.at[pl.ds(0, ln)],
                                          xs_out.at[pl.ds(pl.multiple_of(td_ref[e], SEG_ALIGN), ln)], sem.at[fill_sem]))

            return carry

        lax.fori_loop(0, N_EXP, tail, 0)

        def blk(j, carry):
            act(pltpu.make_async_copy(zbuf.at[pl.ds(0, BM)], xs_out.at[pl.ds(pl.multiple_of(j * BM, BM), BM)],
                                      sem.at[fill_sem]))
            return carry

        lax.fori_loop(nb_ref[0], nb_max, blk, 0)

    @pl.when(i == 0)
    def _():
        zbuf[...] = jnp.zeros(zbuf.shape, jnp.uint32)
        buf[...] = jnp.zeros(buf.shape, jnp.uint32)
        spare = pltpu.make_async_copy(zbuf.at[pl.ds(0, DISPATCH_SLOTS * N_EXP * SEG_ALIGN)],
                                      xs_out.at[pl.ds(nb_max * BM, DISPATCH_SLOTS * N_EXP * SEG_ALIGN)],
                                      sem.at[fill_sem])
        spare.start()
        spare.wait()
        zero_fills(lambda cp: cp.start())

    @pl.when(i >= DISPATCH_SLOTS - 1)
    def _():
        wait_entry(slot, i - (DISPATCH_SLOTS - 1))

    row = lax.broadcasted_iota(jnp.int32, (r, GATE_LANES), 0).astype(F32)
    own = jnp.where(row >= segr_ref[0:1, :], jnp.where(row < segr_ref[1:2, :], 1.0, 0.0), 0.0).astype(BF16)
    z = _dot(jnp.concatenate([own, digits_ref[...]], axis=1), code_ref[...])
    perm = jnp.where(z == 0.0, 1.0, 0.0).astype(BF16)
    step = 256
    for c in range(0, PK, step):
        buf[slot, :, c:c + step] = _pack_pair(_dot(perm, u_ref[:, c:c + step]),
                                              _dot(perm, u_ref[:, PK + c:PK + c + step]))
    _segment_copies_dense(so_ref, do_ref, ln_ref, i, nt + 1, functools.partial(strip, prev_slot))

    @pl.when(i == nt - 1)
    def _():
        _segment_copies_dense(so_ref, do_ref, ln_ref, nt, nt + 1, functools.partial(strip, slot))
        for entry in range(max(nt - DISPATCH_SLOTS + 1, 0), nt + 1):
            wait_entry((entry + DISPATCH_SLOTS - 1) % DISPATCH_SLOTS, entry)
        zero_fills(lambda cp: cp.wait())


def _expert_kernel(layer, be_ref, nb_ref, ws_ref, nx_ref, x_ref, w1_hbm, w3_hbm, w2_hbm, y_ref,
                   w1f, w3f, w2f, w1b, w3b, w2b, sem):
    i = pl.program_id(0)
    live = i < nb_ref[0]

    def fetch(e, slot):
        return [pltpu.make_async_copy(src.at[layer, e], dst.at[slot], sem.at[slot, j])
                for j, (src, dst) in enumerate(((w1_hbm, w1f), (w3_hbm, w3f), (w2_hbm, w2f)))]

    @pl.when(live & ((i == 0) | (be_ref[i] != be_ref[jnp.maximum(i - 1, 0)])))
    def _():
        slot = ws_ref[i]

        @pl.when(i == 0)
        def _():
            for cp in fetch(be_ref[0], slot):
                cp.start()

        for cp in fetch(be_ref[i], slot):
            cp.wait()
        w1b[...] = w1f[slot].astype(BF16)
        w3b[...] = w3f[slot].astype(BF16)
        w2b[...] = w2f[slot].astype(BF16)

        @pl.when(nx_ref[i] >= 0)
        def _():
            for cp in fetch(nx_ref[i], 1 - slot):
                cp.start()

    @pl.when(live)
    def _():
        up = []
        for r0 in range(0, BM, EXP_SUB):
            lo, hi = _unpack_pair(x_ref[r0:r0 + EXP_SUB, :])
            up.append((_dot(lo, w1b[:PK, :]) + _dot(hi, w1b[PK:, :]), _dot(lo, w3b[:PK, :]) + _dot(hi, w3b[PK:, :])))
        for j, (a, g) in enumerate(up):
            hid = (_silu(a) * g).astype(BF16)
            y_ref[j * EXP_SUB:(j + 1) * EXP_SUB, :] = _pack_pair(_round_bf16(_dot(hid, w2b[:, :PK])),
                                                                 _round_bf16(_dot(hid, w2b[:, PK:])))


def _combine_kernel(nt, so_ref, do_ref, ln_ref, tt_ref, code_t_ref, segc_ref, digits_ref, u_ref, ys_ref,
                    s1_ref, s3_ref, s2_ref, h_ref, n_ref, mod_ref, o_ref, buf, sem):
    i = pl.program_id(0)
    slot = i % 2
    r = digits_ref.shape[1]

    def strip(slot_, so, do, ln):
        return pltpu.make_async_copy(ys_ref.at[pl.ds(do, ln)], buf.at[slot_, pl.ds(so, ln)], sem.at[slot_])

    @pl.when(i == 0)
    def _():
        buf[...] = jnp.zeros(buf.shape, jnp.uint32)
        _segment_copies_dense(so_ref, do_ref, ln_ref, 0, nt + 1, functools.partial(strip, 0))

    col = lax.broadcasted_iota(jnp.int32, (2 * N_EXP, r), 1).astype(F32)
    own = jnp.where(col >= segc_ref[:, 0:1], jnp.where(col < segc_ref[:, 1:2], 1.0, 0.0), 0.0).astype(BF16)
    z = _dot(code_t_ref[...], jnp.concatenate([own, own[:N_EXP], digits_ref[...]], axis=0))
    q = jnp.where(jnp.abs(z) < 0.5 * POS_SCALE, z, 0.0).astype(BF16)
    shared = _swiglu(u_ref[...], s1_ref[...], s3_ref[...], s2_ref[...])
    _segment_copies_dense(so_ref, do_ref, ln_ref, i + 1, nt + 1, functools.partial(strip, 1 - slot))

    strip(slot, 0, 0, pl.multiple_of(tt_ref[i], SEG_ALIGN)).wait()
    lo, hi = _unpack_pair(buf[slot, 0:r, :])
    moe = jnp.concatenate([_dot(q, lo), _dot(q, hi)], axis=1) + shared
    o_ref[...] = h_ref[...] + mod_ref[5:6, :] * _rms(moe, n_ref[3:4, :])

    @pl.when(i == nt - 1)
    def _():
        strip(1 - slot, 0, 0, pl.multiple_of(tt_ref[nt], SEG_ALIGN)).wait()


def _sorted_rows(tm):
    return -(-(TOP_K * tm + N_EXP * (SEG_ALIGN - 1)) // MXU_DIM) * MXU_DIM


def _max_blocks(n):
    nt = n // TM_MOE
    rows = TOP_K * n + N_EXP * (SEG_ALIGN - 1) * nt + N_EXP * (BM - SEG_ALIGN)
    return -(-rows // BM)


def _row_digits(r, width, first):
    idx = np.arange(r)
    digits = np.zeros((r, width), np.float32)
    digits[:, first] = idx // POS_RADIX * POS_RADIX
    digits[:, first + 1] = idx % POS_RADIX
    return jnp.asarray(digits, BF16)


def _dispatch_call(seg, tail, nblk, u2, code, segr, nb_max):
    n = u2.shape[0]
    r = _sorted_rows(TM_MOE)
    spare = DISPATCH_SLOTS * N_EXP * SEG_ALIGN
    return pl.pallas_call(
        functools.partial(_dispatch_kernel, n // TM_MOE),
        grid_spec=pltpu.PrefetchScalarGridSpec(
            num_scalar_prefetch=7,
            grid=(n // TM_MOE,),
            in_specs=[
                pl.BlockSpec((TM_MOE, D), lambda i, *_: (i, 0)),
                pl.BlockSpec((4 * N_EXP, TM_MOE), lambda i, *_: (0, i)),
                pl.BlockSpec((None, SUBLANES, GATE_LANES), lambda i, *_: (i, 0, 0)),
                pl.BlockSpec((r, GATE_LANES), lambda i, *_: (0, 0)),
            ],
            out_specs=pl.BlockSpec(memory_space=pl.ANY),
            scratch_shapes=[pltpu.VMEM((DISPATCH_SLOTS, r, PK), jnp.uint32),
                            pltpu.VMEM((max(BM, spare), PK), jnp.uint32),
                            pltpu.SemaphoreType.DMA((DISPATCH_SLOTS + 1,))],
        ),
        out_shape=jax.ShapeDtypeStruct((nb_max * BM + spare, PK), jnp.uint32),
        compiler_params=_cparams("arbitrary"),
        name="dispatch",
    )(*seg, tail[0], tail[1], nblk, u2, code, segr, _row_digits(r, GATE_LANES, N_EXP))


def _expert_call(layer, blk, xs, w1, w3, w2, nb_max):
    row_blk = lambda i, be, nb, *_: (jnp.maximum(jnp.minimum(i, nb[0] - 1), 0), 0)
    hbm = pl.BlockSpec(memory_space=pl.ANY)
    return pl.pallas_call(
        functools.partial(_expert_kernel, layer),
        grid_spec=pltpu.PrefetchScalarGridSpec(
            num_scalar_prefetch=4,
            grid=(nb_max,),
            in_specs=[pl.BlockSpec((BM, PK), row_blk), hbm, hbm, hbm],
            out_specs=pl.BlockSpec((BM, PK), row_blk),
            scratch_shapes=[pltpu.VMEM((2, D, F_EXP), F32), pltpu.VMEM((2, D, F_EXP), F32),
                            pltpu.VMEM((2, F_EXP, D), F32),
                            pltpu.VMEM((D, F_EXP), BF16), pltpu.VMEM((D, F_EXP), BF16),
                            pltpu.VMEM((F_EXP, D), BF16), pltpu.SemaphoreType.DMA((2, 3))],
        ),
        out_shape=jax.ShapeDtypeStruct(xs.shape, jnp.uint32),
        input_output_aliases={4: 0},
        compiler_params=_cparams("arbitrary"),
        name="experts",
    )(*blk, xs, w1, w3, w2)


def _combine_call(tabs, code_t, segc, u2, ys, s1, s3, s2, h, norms_l, mod_l, t):
    n = u2.shape[0]
    r = _sorted_rows(TM_MOE)
    tok = lambda w: pl.BlockSpec((TM_MOE, w), lambda i, *_: (i, 0))
    const = lambda shp: pl.BlockSpec(shp, lambda i, *_: (0,) * len(shp))
    return pl.pallas_call(
        functools.partial(_combine_kernel, n // TM_MOE),
        grid_spec=pltpu.PrefetchScalarGridSpec(
            num_scalar_prefetch=4,
            grid=(n // TM_MOE,),
            in_specs=[
                tok(4 * N_EXP),
                pl.BlockSpec((None, 2 * N_EXP, GATE_LANES), lambda i, *_: (i, 0, 0)),
                const((N_EXP, r)),
                tok(D),
                pl.BlockSpec(memory_space=pl.ANY),
                const((D, F_EXP)), const((D, F_EXP)), const((F_EXP, D)),
                tok(D),
                const((4, D)),
                pl.BlockSpec((None, 6, D), lambda i, *_: (i * TM_MOE // t, 0, 0)),
            ],
            out_specs=tok(D),
            scratch_shapes=[pltpu.VMEM((2, r + N_EXP * SEG_ALIGN, PK), jnp.uint32),
                            pltpu.SemaphoreType.DMA((2,))],
        ),
        out_shape=jax.ShapeDtypeStruct((n, D), F32),
        compiler_params=_cparams("arbitrary"),
        name="combine",
    )(*tabs, code_t, segc, _row_digits(r, N_EXP, 0).T, u2, ys, s1, s3, s2, h, norms_l, mod_l)


def _glu_kernel(h_ref, g_ref, mod_ref, w_ref, b_ref, o_ref):
    u = (_rms(h_ref[...], g_ref[...]) * (1.0 + mod_ref[1:2, :]) + mod_ref[0:1, :]).astype(BF16)
    step = 256
    for c in range(0, D, step):
        a = _dot(u, w_ref[:, c:c + step]) + b_ref[:, c:c + step]
        gate = _dot(u, w_ref[:, D + c:D + c + step]) + b_ref[:, D + c:D + c + step]
        o_ref[:, c:c + step] = (a * jax.nn.sigmoid(gate)).astype(BF16)


def _glu_call(h, g, mod_l, w1, b1, tm):
    b, t, _ = h.shape
    tok = lambda n: pl.BlockSpec((None, tm, n), lambda i, j: (i, j, 0))
    return pl.pallas_call(
        _glu_kernel,
        grid=(b, t // tm),
        in_specs=[
            tok(D),
            pl.BlockSpec((1, D), lambda i, j: (0, 0)),
            pl.BlockSpec((None, 6, D), lambda i, j: (i, 0, 0)),
            pl.BlockSpec((D, 2 * D), lambda i, j: (0, 0)),
            pl.BlockSpec((1, 2 * D), lambda i, j: (0, 0)),
        ],
        out_specs=tok(D),
        out_shape=jax.ShapeDtypeStruct((b, t, D), BF16),
        compiler_params=_cparams("parallel", "parallel"),
        name="glu",
    )(h, g, mod_l, w1, b1)


def _conv_kernel(cur_ref, prev_ref, next_ref, dw_ref, cv_ref, w2_ref, h_ref, n_ref, mod_ref, wr_ref,
                 h_out, u_out, lg_out, win, shifted, conv):
    j = pl.program_id(1)
    tm = cur_ref.shape[0]
    pad = CONV_W // 2
    win[0:HALO, :] = jnp.where(j > 0, prev_ref[...].astype(F32), 0.0)
    win[HALO:HALO + tm, :] = cur_ref[...].astype(F32)
    win[HALO + tm:, :] = jnp.where(j < pl.num_programs(1) - 1, next_ref[...].astype(F32), 0.0)
    ext = shifted.shape[1]
    full = win[...]
    shifted[0] = full[:ext, :]
    for s in range(1, SUBLANES):
        shifted[s] = pltpu.roll(full, full.shape[0] - s, axis=0)[:ext, :]

    def rows(ib, carry):
        r0 = pl.multiple_of(ib * CONV_TILES * SUBLANES, CONV_TILES * SUBLANES)
        for c0 in range(0, D, CONV_LANES):
            cols = slice(c0, c0 + CONV_LANES)
            accs = [jnp.zeros((SUBLANES, CONV_LANES), F32) for _ in range(CONV_TILES)]
            for s in range(SUBLANES):
                taps = [(off // SUBLANES, off - (HALO - pad)) for off in range(HALO - pad, HALO - pad + CONV_W)
                        if off % SUBLANES == s]
                tiles = {a: shifted[s, pl.ds(r0 + a * SUBLANES, SUBLANES), cols]
                         for a in range(taps[0][0], taps[-1][0] + CONV_TILES)}
                for a, tap in taps:
                    w = dw_ref[tap * SUBLANES:(tap + 1) * SUBLANES, cols]
                    accs = [acc + tiles[a + m] * w for m, acc in enumerate(accs)]
            for m, acc in enumerate(accs):
                conv[pl.ds(r0 + m * SUBLANES, SUBLANES), cols] = acc
        return carry

    lax.fori_loop(0, tm // (CONV_TILES * SUBLANES), rows, 0)
    for rows in _row_blocks(tm):
        uf = conv[rows, :] + cv_ref[0:1, :]
        mu = jnp.mean(uf, axis=-1, keepdims=True)
        var = jnp.mean(jnp.square(uf - mu), axis=-1, keepdims=True)
        uf = (uf - mu) * lax.rsqrt(var + LN_EPS) * cv_ref[1:2, :] + cv_ref[2:3, :]
        y = _dot(_silu(uf).astype(BF16), w2_ref[...]) + cv_ref[3:4, :]
        _mixer_epilogue(y, rows, h_ref, n_ref, mod_ref, wr_ref, h_out, u_out, lg_out)


def _conv_call(glu, dw, cvec, w2, h, norms_l, mod_l, wr_t, tm):
    b, t, _ = h.shape
    tok, ep_in, wr_spec, out_specs, out_shape = _epilogue_specs(b, t, tm)
    r = tm // HALO
    last = t // HALO - 1
    return pl.pallas_call(
        _conv_kernel,
        grid=(b, t // tm),
        in_specs=[
            tok(D),
            pl.BlockSpec((None, HALO, D), lambda i, j: (i, jnp.maximum(j * r - 1, 0), 0)),
            pl.BlockSpec((None, HALO, D), lambda i, j: (i, jnp.minimum((j + 1) * r, last), 0)),
            pl.BlockSpec((CONV_W * SUBLANES, D), lambda i, j: (0, 0)),
            pl.BlockSpec((4, D), lambda i, j: (0, 0)),
            pl.BlockSpec((D, D), lambda i, j: (0, 0)),
        ] + ep_in + [wr_spec],
        out_specs=out_specs,
        out_shape=out_shape,
        scratch_shapes=[pltpu.VMEM((tm + 2 * HALO, D), F32),
                        pltpu.VMEM((SUBLANES, tm + (CONV_W // SUBLANES) * SUBLANES, D), F32),
                        pltpu.VMEM((tm, D), F32)],
        compiler_params=_cparams("parallel", "parallel"),
        name="conv",
    )(glu, glu, glu, dw, cvec, w2, h, norms_l, mod_l, wr_t)


def _tile(t, pref):
    return pref if t % pref == 0 else t


def kernel(x, c, ctx, c_ctx, mod_w, mod_b, norms, ret_w_in, ret_w_out, ret_decay, conv_w1, conv_b1, conv_dw,
           conv_b_dw, conv_ln_g, conv_ln_b, conv_w2, conv_b2, moe_router, moe_bias, moe_w1, moe_w3, moe_w2,
           shared_w1, shared_w3, shared_w2):
    b, t, _ = x.shape
    l = ctx.shape[1]
    tm = _tile(t, 512)

    rows = -(-(b + 1) // 8) * 8
    cs = jnp.concatenate([c, c_ctx[None, :], jnp.zeros((rows - b - 1, D), F32)], 0)
    mod = _mod_call(cs, mod_w, mod_b).reshape(mod_w.shape[0], rows, 6, D)

    idx = jnp.arange(t, dtype=jnp.int32)
    pos_lat = jnp.stack([jnp.full((t,), l, jnp.int32), idx // GRID_W, idx % GRID_W], -1)
    zl = jnp.zeros((l,), jnp.int32)
    pos_ctx = jnp.stack([jnp.arange(l, dtype=jnp.int32), zl, zl], -1)
    ang_lat, ang_ctx = _rope_angles(pos_lat), _rope_angles(pos_ctx)

    w_qk = ret_w_in[0][:, :2 * QK_TOT].reshape(D, 2 * HEADS, DK // 2, 2).swapaxes(-1, -2).reshape(D, 2 * QK_TOT)
    w_in = jnp.concatenate([w_qk, ret_w_in[0][:, 2 * QK_TOT:]], axis=1).astype(BF16)
    log_g = jax.nn.log_sigmoid(ret_decay[0].astype(F32))

    n = b * t
    nb_max = _max_blocks(n)

    def moe_layer(i, h1, u2, logits_t):
        code, code_t, segc, segr, pc_t, off_t = _route_call(logits_t, moe_bias[i])
        seg_disp, seg_comb, tail, blocks = _dispatch_tables(pc_t, off_t, nb_max)
        u2f = u2.reshape(n, D)
        xs = _dispatch_call(seg_disp, tail, blocks[1], u2f, code, segr, nb_max)
        ys = _expert_call(i, blocks, xs, moe_w1, moe_w3, moe_w2, nb_max)
        out = _combine_call(seg_comb, code_t, segc, u2f, ys, shared_w1[i].astype(BF16), shared_w3[i].astype(BF16),
                            shared_w2[i].astype(BF16), h1.reshape(n, D), norms[i], mod[i], t)
        return out.reshape(b, t, D)

    q, k, v, gt = _inproj_call(x, norms[0, 0:1], mod[0], w_in, jnp.cos(ang_lat), jnp.sin(ang_lat), tm)
    sf0, sb0 = _ctxstate_call(log_g, ctx, norms[0, 0:1], mod[0, b], w_in[:, QK_TOT:2 * QK_TOT + V_TOT],
                              jnp.cos(ang_ctx), jnp.sin(ang_ctx))
    yn = _retention_call(log_g, q, k, v, sf0, sb0)
    h1, u2, logits_t = _readout_call(yn, gt, x, norms[0], mod[0], ret_w_out[0].astype(BF16),
                                     moe_router[0].T, _tile(t, 2 * EPI_ROWS))
    h2 = moe_layer(0, h1, u2, logits_t)

    glu = _glu_call(h2, norms[1, 0:1], mod[1], conv_w1[0].astype(BF16), conv_b1[0][None, :], tm)
    cvec = jnp.stack([conv_b_dw[0], conv_ln_g[0], conv_ln_b[0], conv_b2[0]], 0)
    dw_rows = jnp.repeat(conv_dw[0], SUBLANES, axis=0)
    h3, u2, logits_t = _conv_call(glu, dw_rows, cvec, conv_w2[0].astype(BF16), h2, norms[1], mod[1],
                                  moe_router[1].T, tm)
    return moe_layer(1, h3, u2, logits_t)
```

```python
import functools

import jax
import jax.numpy as jnp
import numpy as np
from jax import lax
from jax.experimental import pallas as pl
from jax.experimental.pallas import tpu as pltpu

F32 = jnp.float32
BF16 = jnp.bfloat16

D = 1024
HEADS = 4
DK = D // HEADS
DV = 2 * DK
QK_TOT = HEADS * DK
V_TOT = HEADS * DV
IN_DIM = 2 * QK_TOT + 2 * V_TOT
ROPE_AXES = (64, 96, 96)
ROPE_THETA = 10000.0
GRID_W = 64
CONV_W = 31
HALO = 16
SUBLANES = 8
CONV_LANES = 256
CONV_TILES = 4
N_EXP = 64
N_GRP = 8
PER_GRP = N_EXP // N_GRP
TOP_K = 8
TOP_G = 4
F_EXP = 256
ROUTE_SCALE = 2.5
GATE_LANES = 128
MXU_DIM = 256
TM_MOE = 256
ROUTE_TILES = 4
SEG_ALIGN = 8
POS_RADIX = 256
POS_SCALE = 8
BH = 512
DISPATCH_SLOTS = 4
EXP_SUB = 256
PK = D // 2
NORM_EPS = 1e-6
LN_EPS = 1e-5
RET_CHUNK = 256
EPI_ROWS = 512
VMEM_LIMIT = 56 * 1024 * 1024

_HI = lax.Precision.HIGHEST


def _cparams(*sem):
    return pltpu.CompilerParams(dimension_semantics=sem, vmem_limit_bytes=VMEM_LIMIT)


def _dot(a, b):
    return jnp.dot(a, b, preferred_element_type=F32)


def _rms(xf, g):
    return xf * lax.rsqrt(jnp.mean(xf * xf, axis=-1, keepdims=True) + NORM_EPS) * g


def _silu(x):
    return x * jax.nn.sigmoid(x)


def _mod_kernel(cs_ref, w_ref, b_ref, o_ref):
    s = _silu(cs_ref[...])
    o_ref[...] = jnp.dot(s, w_ref[...], preferred_element_type=F32, precision=_HI) + b_ref[...]


def _mod_call(cs, mod_w, mod_b):
    depth, _, n6 = mod_w.shape
    tn = 1536
    rows = cs.shape[0]
    return pl.pallas_call(
        _mod_kernel,
        grid=(depth, n6 // tn),
        in_specs=[
            pl.BlockSpec((rows, D), lambda i, j: (0, 0)),
            pl.BlockSpec((None, D, tn), lambda i, j: (i, 0, j)),
            pl.BlockSpec((None, 1, tn), lambda i, j: (i, 0, j)),
        ],
        out_specs=pl.BlockSpec((None, rows, tn), lambda i, j: (i, 0, j)),
        out_shape=jax.ShapeDtypeStruct((depth, rows, n6), F32),
        compiler_params=_cparams("parallel", "parallel"),
        name="mod",
    )(cs, mod_w, mod_b.reshape(depth, 1, n6))


def _rope_angles(pos):
    parts = []
    for a, d in enumerate(ROPE_AXES):
        inv = ROPE_THETA ** (-jnp.arange(0, d, 2, dtype=F32) / d)
        parts.append(pos[:, a:a + 1].astype(F32) * inv[None, :])
    return jnp.concatenate(parts, -1)


def _rope_store(z, cos, sin, scale, o_ref, col):
    half = DK // 2
    x0, x1 = z[:, :half], z[:, half:]
    o_ref[:, col:col + half] = ((x0 * cos - x1 * sin) * scale).astype(o_ref.dtype)
    o_ref[:, col + half:col + DK] = ((x0 * sin + x1 * cos) * scale).astype(o_ref.dtype)


def _inproj_kernel(x_ref, g_ref, mod_ref, w_ref, cos_ref, sin_ref, q_ref, k_ref, v_ref, gt_ref):
    u = (_rms(x_ref[...], g_ref[...]) * (1.0 + mod_ref[1:2, :]) + mod_ref[0:1, :]).astype(BF16)
    cos, sin = cos_ref[...], sin_ref[...]
    step = 512
    for j in range(V_TOT // step):
        c1 = 2 * QK_TOT + V_TOT + j * step
        gt_ref[:, j * step:(j + 1) * step] = _silu(_dot(u, w_ref[:, c1:c1 + step])).astype(BF16)
    for h in range(HEADS):
        _rope_store(_dot(u, w_ref[:, h * DK:(h + 1) * DK]), cos, sin, DK ** -0.5, q_ref, h * DK)
        _rope_store(_dot(u, w_ref[:, QK_TOT + h * DK:QK_TOT + (h + 1) * DK]), cos, sin, 1.0, k_ref, h * DK)
    for j in range(V_TOT // step):
        c0 = 2 * QK_TOT + j * step
        v_ref[:, j * step:(j + 1) * step] = _dot(u, w_ref[:, c0:c0 + step]).astype(BF16)


def _inproj_call(x, g, mod_l, w_in, cos, sin, tm):
    b, t, _ = x.shape
    grid = (b, t // tm)
    tok = lambda n: pl.BlockSpec((None, tm, n), lambda i, j: (i, j, 0))
    return pl.pallas_call(
        _inproj_kernel,
        grid=grid,
        in_specs=[
            tok(D),
            pl.BlockSpec((1, D), lambda i, j: (0, 0)),
            pl.BlockSpec((None, 6, D), lambda i, j: (i, 0, 0)),
            pl.BlockSpec((D, IN_DIM), lambda i, j: (0, 0), pipeline_mode=pl.Buffered(1)),
            pl.BlockSpec((tm, DK // 2), lambda i, j: (j, 0)),
            pl.BlockSpec((tm, DK // 2), lambda i, j: (j, 0)),
        ],
        out_specs=[tok(QK_TOT), tok(QK_TOT), tok(V_TOT), tok(V_TOT)],
        out_shape=[
            jax.ShapeDtypeStruct((b, t, QK_TOT), BF16),
            jax.ShapeDtypeStruct((b, t, QK_TOT), BF16),
            jax.ShapeDtypeStruct((b, t, V_TOT), BF16),
            jax.ShapeDtypeStruct((b, t, V_TOT), BF16),
        ],
        compiler_params=_cparams("parallel", "parallel"),
        name="inproj",
    )(x, g, mod_l, w_in, cos, sin)


def _row_pow(lg, expo_fn, rows, cols):
    i = lax.broadcasted_iota(jnp.int32, (rows, cols), 0).astype(F32)
    return jnp.exp(lg * expo_fn(i))


def _ctxstate_kernel(lg_ref, c_ref, g_ref, mod_ref, w_ref, cos_ref, sin_ref, sf_ref, sb_ref, k_scr):
    l = c_ref.shape[0]
    u = (_rms(c_ref[...], g_ref[...]) * (1.0 + mod_ref[1:2, :]) + mod_ref[0:1, :]).astype(BF16)
    cos, sin = cos_ref[...], sin_ref[...]
    for h in range(HEADS):
        _rope_store(_dot(u, w_ref[:, h * DK:(h + 1) * DK]), cos, sin, 1.0, k_scr, h * DK)
    for h in range(HEADS):
        kh = k_scr[:, h * DK:(h + 1) * DK]
        vh = _dot(u, w_ref[:, QK_TOT + h * DV:QK_TOT + (h + 1) * DV]).astype(BF16)
        wf = _row_pow(lg_ref[0, h], lambda i: (l - 1.0) - i, l, DK)
        wb = _row_pow(lg_ref[1, h], lambda i: i, l, DK)
        sf_ref[h] = _dot((kh * wf).T.astype(BF16), vh)
        sb_ref[h] = _dot((kh * wb).T.astype(BF16), vh)


def _ctxstate_call(log_g, ctx, g, mod_row, w_kv, cos, sin):
    b, l, _ = ctx.shape
    st = jax.ShapeDtypeStruct((b, HEADS, DK, DV), F32)
    st_spec = pl.BlockSpec((None, HEADS, DK, DV), lambda i: (i, 0, 0, 0))
    return pl.pallas_call(
        _ctxstate_kernel,
        grid=(b,),
        in_specs=[
            pl.BlockSpec(memory_space=pltpu.SMEM),
            pl.BlockSpec((None, l, D), lambda i: (i, 0, 0)),
            pl.BlockSpec((1, D), lambda i: (0, 0)),
            pl.BlockSpec((6, D), lambda i: (0, 0)),
            pl.BlockSpec((D, QK_TOT + V_TOT), lambda i: (0, 0)),
            pl.BlockSpec((l, DK // 2), lambda i: (0, 0)),
            pl.BlockSpec((l, DK // 2), lambda i: (0, 0)),
        ],
        out_specs=[st_spec, st_spec],
        out_shape=[st, st],
        scratch_shapes=[pltpu.VMEM((l, QK_TOT), F32)],
        compiler_params=_cparams("parallel"),
        name="ctxstate",
    )(log_g, ctx, g, mod_row, w_kv, cos, sin)


def _retention_kernel(lg_ref, q_ref, k_ref, v_ref, sf0_ref, sb0_ref, o_ref, acc, sf, sb):
    h = pl.program_id(1)
    t = q_ref.shape[0]
    c = RET_CHUNK
    nc = t // c
    lgf, lgb = lg_ref[0, h], lg_ref[1, h]
    ri = lax.broadcasted_iota(jnp.int32, (c, c), 0).astype(F32)
    ci = lax.broadcasted_iota(jnp.int32, (c, c), 1).astype(F32)
    rel = ri - ci
    mask = (jnp.where(rel >= 0, jnp.exp(lgf * jnp.maximum(rel, 0.0)), 0.0)
            + jnp.where(rel <= 0, jnp.exp(lgb * jnp.maximum(-rel, 0.0)), 0.0))
    qdf = _row_pow(lgf, lambda i: i + 1.0, c, DV)
    qdb = _row_pow(lgb, lambda i: c - i, c, DV)
    kdf = _row_pow(lgf, lambda i: (c - 1.0) - i, c, DK)
    kdb = _row_pow(lgb, lambda i: i, c, DK)
    cdf = jnp.exp(jnp.full((1, DV), lgf * c, F32))
    cdb = jnp.exp(jnp.full((1, DV), lgb * c, F32))
    sf[...] = sf0_ref[...]
    sb[...] = sb0_ref[...]

    def rows(ic):
        return pl.ds(pl.multiple_of(ic * c, c), c)

    def fwd(rs):
        q, k, v = q_ref[rs, :], k_ref[rs, :], v_ref[rs, :]
        s = lax.dot_general(q, k, (((1,), (1,)), ((), ())), preferred_element_type=F32)
        o = _dot((s * mask).astype(BF16), v) + qdf * _dot(q, sf[...].astype(BF16))
        sf[...] = sf[...] * cdf + _dot((k.astype(F32) * kdf).T.astype(BF16), v)
        return o

    def bwd(rs):
        q, k, v = q_ref[rs, :], k_ref[rs, :], v_ref[rs, :]
        o = qdb * _dot(q, sb[...].astype(BF16))
        sb[...] = sb[...] * cdb + _dot((k.astype(F32) * kdb).T.astype(BF16), v)
        return o

    def finish(rs, o):
        o_ref[rs, :] = (o * lax.rsqrt(jnp.mean(o * o, axis=-1, keepdims=True) + NORM_EPS)).astype(BF16)

    def first_half(j, carry):
        of, ob = fwd(rows(j)), bwd(rows(nc - 1 - j))
        acc[rows(j), :] = of
        acc[rows(nc - 1 - j), :] = ob
        return carry

    def second_half(j, carry):
        of, ob = fwd(rows(j)), bwd(rows(nc - 1 - j))
        finish(rows(j), acc[rows(j), :] + of)
        finish(rows(nc - 1 - j), acc[rows(nc - 1 - j), :] + ob)
        return carry

    lax.fori_loop(0, nc // 2, first_half, 0, unroll=True)
    lax.fori_loop(nc // 2, nc, second_half, 0, unroll=True)


def _retention_call(log_g, q, k, v, sf0, sb0):
    b, t, _ = q.shape
    qk_spec = pl.BlockSpec((None, t, DK), lambda i, h: (i, 0, h))
    v_spec = pl.BlockSpec((None, t, DV), lambda i, h: (i, 0, h))
    s_spec = pl.BlockSpec((None, None, DK, DV), lambda i, h: (i, h, 0, 0))
    return pl.pallas_call(
        _retention_kernel,
        grid=(b, HEADS),
        in_specs=[pl.BlockSpec(memory_space=pltpu.SMEM), qk_spec, qk_spec, v_spec, s_spec, s_spec],
        out_specs=v_spec,
        out_shape=jax.ShapeDtypeStruct((b, t, V_TOT), BF16),
        scratch_shapes=[pltpu.VMEM((t, DV), F32), pltpu.VMEM((DK, DV), F32), pltpu.VMEM((DK, DV), F32)],
        compiler_params=_cparams("parallel", "parallel"),
        name="retention",
    )(log_g, q, k, v, sf0, sb0)


def _mixer_epilogue(y, rows, h_ref, n_ref, mod_ref, wr_ref, h_out, u_out, lg_out):
    h1 = h_ref[rows, :] + mod_ref[2:3, :] * _rms(y, n_ref[1:2, :])
    h_out[rows, :] = h1
    u2 = _rms(h1, n_ref[2:3, :]) * (1.0 + mod_ref[4:5, :]) + mod_ref[3:4, :]
    u_hi = u2.astype(BF16)
    u_out[rows, :] = u_hi
    u_lo = (u2 - u_hi.astype(F32)).astype(BF16)
    w = wr_ref[...]
    w_hi = w.astype(BF16)
    w_lo = (w - w_hi.astype(F32)).astype(BF16)
    nt_dot = lambda a, b: lax.dot_general(a, b, (((1,), (1,)), ((), ())), preferred_element_type=F32)
    lg_out[:, rows] = nt_dot(w_hi, u_hi) + (nt_dot(w_hi, u_lo) + nt_dot(w_lo, u_hi))


def _row_blocks(tm):
    return [slice(r0, r0 + EPI_ROWS) for r0 in range(0, tm, EPI_ROWS)]


def _readout_kernel(y_ref, gt_ref, h_ref, n_ref, mod_ref, w_ref, wr_ref, h_out, u_out, lg_out):
    blocks = _row_blocks(y_ref.shape[0])
    ys = [_dot(y_ref[rows, :] * gt_ref[rows, :], w_ref[...]) for rows in blocks]
    for rows, y in zip(blocks, ys):
        _mixer_epilogue(y, rows, h_ref, n_ref, mod_ref, wr_ref, h_out, u_out, lg_out)


def _epilogue_specs(b, t, tm):
    nt = t // tm
    tok = lambda n: pl.BlockSpec((None, tm, n), lambda i, j: (i, j, 0))
    in_specs = [
        tok(D),
        pl.BlockSpec((4, D), lambda i, j: (0, 0)),
        pl.BlockSpec((None, 6, D), lambda i, j: (i, 0, 0)),
    ]
    wr_spec = pl.BlockSpec((N_EXP, D), lambda i, j: (0, 0))
    out_specs = [tok(D), tok(D), pl.BlockSpec((N_EXP, tm), lambda i, j: (0, i * nt + j))]
    out_shape = [
        jax.ShapeDtypeStruct((b, t, D), F32),
        jax.ShapeDtypeStruct((b, t, D), BF16),
        jax.ShapeDtypeStruct((N_EXP, b * t), F32),
    ]
    return tok, in_specs, wr_spec, out_specs, out_shape


def _readout_call(yn, gt, h, norms_l, mod_l, w_out, wr_t, tm):
    b, t, _ = h.shape
    tok, ep_in, wr_spec, out_specs, out_shape = _epilogue_specs(b, t, tm)
    return pl.pallas_call(
        _readout_kernel,
        grid=(b, t // tm),
        in_specs=[tok(V_TOT), tok(V_TOT)] + ep_in
        + [pl.BlockSpec((V_TOT, D), lambda i, j: (0, 0), pipeline_mode=pl.Buffered(1)), wr_spec],
        out_specs=out_specs,
        out_shape=out_shape,
        compiler_params=_cparams("parallel", "parallel"),
        name="readout",
    )(yn, gt, h, norms_l, mod_l, w_out, wr_t)


def _route_tile(logits, bias):
    tr = logits.shape[1]
    s = jax.nn.sigmoid(logits)
    biased = s + bias
    neg = -jnp.inf
    b3 = biased.reshape(N_GRP, PER_GRP, tr)
    io3 = lax.broadcasted_iota(jnp.int32, b3.shape, 1).astype(F32)
    m1 = jnp.max(b3, axis=1, keepdims=True)
    i1 = jnp.min(jnp.where(b3 == m1, io3, float(PER_GRP)), axis=1, keepdims=True)
    m2 = jnp.max(jnp.where(io3 == i1, neg, b3), axis=1, keepdims=True)
    gs = (m1 + m2).reshape(N_GRP, tr)
    iog = lax.broadcasted_iota(jnp.int32, gs.shape, 0).astype(F32)
    gsel = jnp.zeros(gs.shape, F32)
    work = gs
    for _ in range(TOP_G):
        m = jnp.max(work, axis=0, keepdims=True)
        gi = jnp.min(jnp.where(work == m, iog, float(N_GRP)), axis=0, keepdims=True)
        hit = iog == gi
        gsel = jnp.where(hit, 1.0, gsel)
        work = jnp.where(hit, neg, work)
    emask = jnp.broadcast_to(gsel.reshape(N_GRP, 1, tr), b3.shape).reshape(N_EXP, tr)
    work = jnp.where(emask > 0.0, biased, neg)
    ioe = lax.broadcasted_iota(jnp.int32, work.shape, 0).astype(F32)
    esel = jnp.zeros(work.shape, F32)
    for _ in range(TOP_K):
        m = jnp.max(work, axis=0, keepdims=True)
        ei = jnp.min(jnp.where(work == m, ioe, float(N_EXP)), axis=0, keepdims=True)
        hit = ioe == ei
        esel = jnp.where(hit, 1.0, esel)
        work = jnp.where(hit, neg, work)
    wsel = esel * s
    gate = wsel / jnp.sum(wsel, axis=0, keepdims=True) * ROUTE_SCALE

    t_r = lax.broadcasted_iota(jnp.int32, (tr, tr), 0)
    t_c = lax.broadcasted_iota(jnp.int32, (tr, tr), 1)
    rank = _dot(esel.astype(BF16), jnp.where(t_r < t_c, 1.0, 0.0).astype(BF16))
    cnt = jnp.sum(esel, axis=1, keepdims=True)
    pc = jnp.floor((cnt + (SEG_ALIGN - 1.0)) * (1.0 / SEG_ALIGN)) * SEG_ALIGN
    e_r = lax.broadcasted_iota(jnp.int32, (N_EXP, N_EXP), 0)
    e_c = lax.broadcasted_iota(jnp.int32, (N_EXP, N_EXP), 1)
    off = jnp.dot(jnp.where(e_c < e_r, 1.0, 0.0), jnp.broadcast_to(pc, (N_EXP, 128)),
                  preferred_element_type=F32, precision=_HI)[:, 0:1]
    p = off + rank
    p_hi = jnp.where(esel > 0.0, jnp.floor(p * (1.0 / POS_RADIX)) * POS_RADIX, -float(POS_RADIX))
    p_lo = jnp.where(esel > 0.0, p - p_hi, 0.0)
    minus_r = jnp.where(lax.broadcasted_iota(jnp.int32, (N_EXP, tr), 0) < 2, -float(POS_SCALE), 0.0)
    code = jnp.concatenate([p_hi * POS_SCALE, p_lo * POS_SCALE, gate, minus_r], 0)
    code_t = jnp.concatenate([code[:2 * N_EXP].T, code[2 * N_EXP:].T], 1)
    lane = lax.broadcasted_iota(jnp.int32, (2 * N_EXP, GATE_LANES), 1)
    off2 = jnp.concatenate([off, off], 0)
    end2 = jnp.concatenate([off + pc, off + pc], 0)
    bounds = jnp.where(lane == 0, off2, jnp.where(lane == 1, end2, 0.0))
    return code.astype(BF16), code_t.astype(BF16), bounds, bounds.T[0:SUBLANES, :], pc, off


def _route_kernel(lg_ref, b_ref, code_ref, code_t_ref, segc_ref, segr_ref, pc_ref, off_ref):
    i = pl.program_id(0)

    @pl.when(i == 0)
    def _():
        pc_ref[...] = jnp.zeros(pc_ref.shape, F32)
        off_ref[...] = jnp.zeros(off_ref.shape, F32)

    tile_lane = lax.broadcasted_iota(jnp.int32, pc_ref.shape, 1)
    for k in range(ROUTE_TILES):
        cols = slice(k * TM_MOE, (k + 1) * TM_MOE)
        code, code_t, bounds, bounds_t, pc, off = _route_tile(lg_ref[:, cols], b_ref[...])
        code_ref[:, cols] = code
        code_t_ref[cols, :] = code_t
        segc_ref[k] = bounds
        segr_ref[k] = bounds_t
        pc_ref[...] = jnp.where(tile_lane == i * ROUTE_TILES + k, pc, pc_ref[...])
        off_ref[...] = jnp.where(tile_lane == i * ROUTE_TILES + k, off, off_ref[...])


def _route_call(logits_t, bias):
    n = logits_t.shape[1]
    nt = n // TM_MOE
    tr = ROUTE_TILES * TM_MOE
    return pl.pallas_call(
        _route_kernel,
        grid=(nt // ROUTE_TILES,),
        in_specs=[pl.BlockSpec((N_EXP, tr), lambda i: (0, i)), pl.BlockSpec((N_EXP, 1), lambda i: (0, 0))],
        out_specs=[
            pl.BlockSpec((4 * N_EXP, tr), lambda i: (0, i)),
            pl.BlockSpec((tr, 4 * N_EXP), lambda i: (i, 0)),
            pl.BlockSpec((ROUTE_TILES, 2 * N_EXP, GATE_LANES), lambda i: (i, 0, 0)),
            pl.BlockSpec((ROUTE_TILES, SUBLANES, GATE_LANES), lambda i: (i, 0, 0)),
            pl.BlockSpec((N_EXP, nt), lambda i: (0, 0)),
            pl.BlockSpec((N_EXP, nt), lambda i: (0, 0)),
        ],
        out_shape=[
            jax.ShapeDtypeStruct((4 * N_EXP, n), BF16),
            jax.ShapeDtypeStruct((n, 4 * N_EXP), BF16),
            jax.ShapeDtypeStruct((nt, 2 * N_EXP, GATE_LANES), F32),
            jax.ShapeDtypeStruct((nt, SUBLANES, GATE_LANES), F32),
            jax.ShapeDtypeStruct((N_EXP, nt), F32),
            jax.ShapeDtypeStruct((N_EXP, nt), F32),
        ],
        compiler_params=_cparams("arbitrary"),
        name="route",
    )(logits_t, bias.reshape(N_EXP, 1))


def _dispatch_tables(pc_t, off_t, nb_max):
    nt = pc_t.shape[1]
    ids = jnp.arange(N_EXP, dtype=jnp.int32)
    psum = lambda a, m: jnp.dot(a, m, precision=_HI, preferred_element_type=F32)
    before_tile = (jnp.arange(nt)[:, None] < jnp.arange(nt)[None, :]).astype(F32)
    upto_expert = (ids[:, None] <= ids[None, :]).astype(F32)
    used_f = pc_t.sum(1)
    region_f = jnp.ceil(used_f * (1.0 / BH)) * BH
    ends_f = psum(region_f[None, :], upto_expert)[0]
    pc = pc_t.astype(jnp.int32)
    used, region, ends = used_f.astype(jnp.int32), region_f.astype(jnp.int32), ends_f.astype(jnp.int32)
    starts = ends - region
    dst = starts[:, None] + psum(pc_t, before_tile).astype(jnp.int32)
    nblk = ends[-1] // BH
    blk = jnp.minimum(jnp.arange(nb_max, dtype=jnp.int32), nblk - 1)
    blk_e = jnp.minimum(jnp.sum(ends[None, :] <= blk[:, None] * BH, axis=1), N_EXP - 1).astype(jnp.int32)
    has = region > 0
    slot_e = (psum(has.astype(F32)[None, :], upto_expert)[0].astype(jnp.int32) - 1) % 2
    later = has[None, :] & (ids[None, :] > ids[:, None])
    next_e = jnp.where(later.any(1), jnp.argmax(later, axis=1), -1)
    of_blk = lambda per_e: jnp.sum(jnp.where(blk_e[:, None] == ids[None, :], per_e[None, :], 0), axis=1)
    blocks = (blk_e, nblk.reshape(1).astype(jnp.int32), of_blk(slot_e).astype(jnp.int32),
              of_blk(next_e).astype(jnp.int32))
    flat = lambda a: a.reshape(-1).astype(jnp.int32)
    tail = ((starts + used).astype(jnp.int32), (region - used).astype(jnp.int32))
    off = off_t.astype(jnp.int32)
    e_idx = jnp.arange(N_EXP, dtype=jnp.int32)[:, None]
    col = lambda v: jnp.broadcast_to(jnp.asarray(v, jnp.int32), (N_EXP, 1))
    table = lambda so, do, ln: (flat(so), flat(do), flat(ln), ln.sum(0).astype(jnp.int32))
    pc_c = jnp.concatenate([pc, col(0)], 1)
    spare_c = _sorted_rows(TM_MOE) + SEG_ALIGN * e_idx
    seg_comb = table(jnp.where(pc_c > 0, jnp.concatenate([off, col(0)], 1), spare_c),
                     jnp.where(pc_c > 0, jnp.concatenate([dst, col(0)], 1), 0), jnp.maximum(pc_c, SEG_ALIGN))
    pc_d = jnp.concatenate([col(0), pc], 1)
    entry = jnp.arange(nt + 1, dtype=jnp.int32)[None, :]
    spare_d = nb_max * BH + ((entry % DISPATCH_SLOTS) * N_EXP + e_idx) * SEG_ALIGN
    seg_disp = table(jnp.where(pc_d > 0, jnp.concatenate([col(0), off], 1), 0),
                     jnp.where(pc_d > 0, jnp.concatenate([col(0), dst], 1), spare_d), jnp.maximum(pc_d, SEG_ALIGN))
    return seg_disp, seg_comb, tail, blocks


def _swiglu(u, w1, w3, w2):
    return _dot((_silu(_dot(u, w1)) * _dot(u, w3)).astype(BF16), w2)


def _round_bf16(x):
    return x.astype(BF16).astype(F32)


def _pack_pair(lo, hi):
    lo_b = lax.bitcast_convert_type(lo, jnp.uint32)
    hi_b = lax.bitcast_convert_type(hi, jnp.uint32)
    return (hi_b & jnp.uint32(0xFFFF0000)) | (lo_b >> 16)


def _unpack_pair(w):
    lo = lax.bitcast_convert_type(w << 16, F32).astype(BF16)
    hi = lax.bitcast_convert_type(w & jnp.uint32(0xFFFF0000), F32).astype(BF16)
    return lo, hi


def _segment_copies_dense(so_ref, do_ref, ln_ref, entry, entries, make_copy):
    for e in range(N_EXP):
        j = e * entries + entry
        make_copy(pl.multiple_of(so_ref[j], SEG_ALIGN), pl.multiple_of(do_ref[j], SEG_ALIGN),
                  pl.multiple_of(ln_ref[j], SEG_ALIGN)).start()


def _dispatch_kernel(nt, so_ref, do_ref, ln_ref, tt_ref, td_ref, tl_ref, nb_ref, u_ref, code_ref, segr_ref,
                     digits_ref, xs_out, buf, zbuf, sem):
    i = pl.program_id(0)
    r = buf.shape[1]
    nb_max = (xs_out.shape[0] - DISPATCH_SLOTS * N_EXP * SEG_ALIGN) // BH
    fill_sem = DISPATCH_SLOTS
    slot = i % DISPATCH_SLOTS
    prev_slot = (i + DISPATCH_SLOTS - 1) % DISPATCH_SLOTS

    def strip(slot_, so, do, ln):
        return pltpu.make_async_copy(buf.at[slot_, pl.ds(so, ln)], xs_out.at[pl.ds(do, ln)], sem.at[slot_])

    def wait_entry(slot_, entry):
        strip(slot_, 0, 0, pl.multiple_of(tt_ref[entry], SEG_ALIGN)).wait()

    def zero_fills(act):
        def tail(e, carry):
            ln = pl.multiple_of(tl_ref[e], SEG_ALIGN)

            @pl.when(ln > 0)
            def _():
                act(pltpu.make_async_copy(zbuf.at[pl.ds(0, ln)],
                                          xs_out.at[pl.ds(pl.multiple_of(td_ref[e], SEG_ALIGN), ln)], sem.at[fill_sem]))

            return carry

        lax.fori_loop(0, N_EXP, tail, 0)

        def blk(j, carry):
            act(pltpu.make_async_copy(zbuf.at[pl.ds(0, BH)], xs_out.at[pl.ds(pl.multiple_of(j * BH, BH), BH)],
                                      sem.at[fill_sem]))
            return carry

        lax.fori_loop(nb_ref[0], nb_max, blk, 0)

    @pl.when(i == 0)
    def _():
        zbuf[...] = jnp.zeros(zbuf.shape, jnp.uint32)
        buf[...] = jnp.zeros(buf.shape, jnp.uint32)
        spare = pltpu.make_async_copy(zbuf.at[pl.ds(0, DISPATCH_SLOTS * N_EXP * SEG_ALIGN)],
                                      xs_out.at[pl.ds(nb_max * BH, DISPATCH_SLOTS * N_EXP * SEG_ALIGN)],
                                      sem.at[fill_sem])
        spare.start()
        spare.wait()
        zero_fills(lambda cp: cp.start())

    @pl.when(i >= DISPATCH_SLOTS - 1)
    def _():
        wait_entry(slot, i - (DISPATCH_SLOTS - 1))

    row = lax.broadcasted_iota(jnp.int32, (r, GATE_LANES), 0).astype(F32)
    own = jnp.where(row >= segr_ref[0:1, :], jnp.where(row < segr_ref[1:2, :], 1.0, 0.0), 0.0).astype(BF16)
    z = _dot(jnp.concatenate([own, digits_ref[...]], axis=1), code_ref[...])
    perm = jnp.where(z == 0.0, 1.0, 0.0).astype(BF16)
    step = 256
    for c in range(0, PK, step):
        buf[slot, :, c:c + step] = _pack_pair(_dot(perm, u_ref[:, c:c + step]),
                                              _dot(perm, u_ref[:, PK + c:PK + c + step]))
    _segment_copies_dense(so_ref, do_ref, ln_ref, i, nt + 1, functools.partial(strip, prev_slot))

    @pl.when(i == nt - 1)
    def _():
        _segment_copies_dense(so_ref, do_ref, ln_ref, nt, nt + 1, functools.partial(strip, slot))
        for entry in range(max(nt - DISPATCH_SLOTS + 1, 0), nt + 1):
            wait_entry((entry + DISPATCH_SLOTS - 1) % DISPATCH_SLOTS, entry)
        zero_fills(lambda cp: cp.wait())


def _expert_kernel(layer, be_ref, nb_ref, ws_ref, nx_ref, x_ref, w1_hbm, w3_hbm, w2_hbm, y_ref,
                   w1f, w3f, w2f, w1b, w3b, w2b, sem):
    i = pl.program_id(0)
    first, second = 2 * i, 2 * i + 1
    live = first < nb_ref[0]
    split = (second < nb_ref[0]) & (be_ref[second] != be_ref[first])

    def fetch(e, slot):
        return [pltpu.make_async_copy(src.at[layer, e], dst.at[slot], sem.at[slot, j])
                for j, (src, dst) in enumerate(((w1_hbm, w1f), (w3_hbm, w3f), (w2_hbm, w2f)))]

    def begin_expert(blk):
        slot = ws_ref[blk]

        @pl.when(blk == 0)
        def _():
            for cp in fetch(be_ref[0], slot):
                cp.start()

        for cp in fetch(be_ref[blk], slot):
            cp.wait()
        w1b[...] = w1f[slot].astype(BF16)
        w3b[...] = w3f[slot].astype(BF16)
        w2b[...] = w2f[slot].astype(BF16)

        @pl.when(nx_ref[blk] >= 0)
        def _():
            for cp in fetch(nx_ref[blk], 1 - slot):
                cp.start()

    def compute(row0, nrows):
        up = []
        for r0 in range(row0, row0 + nrows, EXP_SUB):
            lo, hi = _unpack_pair(x_ref[r0:r0 + EXP_SUB, :])
            up.append((_dot(lo, w1b[:PK, :]) + _dot(hi, w1b[PK:, :]), _dot(lo, w3b[:PK, :]) + _dot(hi, w3b[PK:, :])))
        for j, (a, g) in enumerate(up):
            r0 = row0 + j * EXP_SUB
            hid = (_silu(a) * g).astype(BF16)
            y_ref[r0:r0 + EXP_SUB, :] = _pack_pair(_round_bf16(_dot(hid, w2b[:, :PK])),
                                                   _round_bf16(_dot(hid, w2b[:, PK:])))

    @pl.when(live & ((i == 0) | (be_ref[first] != be_ref[jnp.maximum(first - 1, 0)])))
    def _():
        begin_expert(first)

    @pl.when(live & jnp.logical_not(split))
    def _():
        compute(0, 2 * BH)

    @pl.when(split)
    def _():
        compute(0, BH)
        begin_expert(second)
        compute(BH, BH)


def _combine_kernel(nt, so_ref, do_ref, ln_ref, tt_ref, code_t_ref, segc_ref, digits_ref, u_ref, ys_ref,
                    s1_ref, s3_ref, s2_ref, h_ref, n_ref, mod_ref, o_ref, buf, sem):
    i = pl.program_id(0)
    slot = i % 2
    r = digits_ref.shape[1]

    def strip(slot_, so, do, ln):
        return pltpu.make_async_copy(ys_ref.at[pl.ds(do, ln)], buf.at[slot_, pl.ds(so, ln)], sem.at[slot_])

    @pl.when(i == 0)
    def _():
        buf[...] = jnp.zeros(buf.shape, jnp.uint32)
        _segment_copies_dense(so_ref, do_ref, ln_ref, 0, nt + 1, functools.partial(strip, 0))

    col = lax.broadcasted_iota(jnp.int32, (2 * N_EXP, r), 1).astype(F32)
    own = jnp.where(col >= segc_ref[:, 0:1], jnp.where(col < segc_ref[:, 1:2], 1.0, 0.0), 0.0).astype(BF16)
    z = _dot(code_t_ref[...], jnp.concatenate([own, own[:N_EXP], digits_ref[...]], axis=0))
    q = jnp.where(jnp.abs(z) < 0.5 * POS_SCALE, z, 0.0).astype(BF16)
    shared = _swiglu(u_ref[...], s1_ref[...], s3_ref[...], s2_ref[...])
    _segment_copies_dense(so_ref, do_ref, ln_ref, i + 1, nt + 1, functools.partial(strip, 1 - slot))

    strip(slot, 0, 0, pl.multiple_of(tt_ref[i], SEG_ALIGN)).wait()
    lo, hi = _unpack_pair(buf[slot, 0:r, :])
    moe = jnp.concatenate([_dot(q, lo), _dot(q, hi)], axis=1) + shared
    o_ref[...] = h_ref[...] + mod_ref[5:6, :] * _rms(moe, n_ref[3:4, :])

    @pl.when(i == nt - 1)
    def _():
        strip(1 - slot, 0, 0, pl.multiple_of(tt_ref[nt], SEG_ALIGN)).wait()


def _sorted_rows(tm):
    return -(-(TOP_K * tm + N_EXP * (SEG_ALIGN - 1)) // MXU_DIM) * MXU_DIM


def _max_blocks(n):
    nt = n // TM_MOE
    rows = TOP_K * n + N_EXP * (SEG_ALIGN - 1) * nt + N_EXP * (BH - SEG_ALIGN)
    return -(-rows // (2 * BH)) * 2


def _row_digits(r, width, first):
    idx = np.arange(r)
    digits = np.zeros((r, width), np.float32)
    digits[:, first] = idx // POS_RADIX * POS_RADIX
    digits[:, first + 1] = idx % POS_RADIX
    return jnp.asarray(digits, BF16)


def _dispatch_call(seg, tail, nblk, u2, code, segr, nb_max):
    n = u2.shape[0]
    r = _sorted_rows(TM_MOE)
    spare = DISPATCH_SLOTS * N_EXP * SEG_ALIGN
    return pl.pallas_call(
        functools.partial(_dispatch_kernel, n // TM_MOE),
        grid_spec=pltpu.PrefetchScalarGridSpec(
            num_scalar_prefetch=7,
            grid=(n // TM_MOE,),
            in_specs=[
                pl.BlockSpec((TM_MOE, D), lambda i, *_: (i, 0)),
                pl.BlockSpec((4 * N_EXP, TM_MOE), lambda i, *_: (0, i)),
                pl.BlockSpec((None, SUBLANES, GATE_LANES), lambda i, *_: (i, 0, 0)),
                pl.BlockSpec((r, GATE_LANES), lambda i, *_: (0, 0)),
            ],
            out_specs=pl.BlockSpec(memory_space=pl.ANY),
            scratch_shapes=[pltpu.VMEM((DISPATCH_SLOTS, r, PK), jnp.uint32),
                            pltpu.VMEM((max(BH, spare), PK), jnp.uint32),
                            pltpu.SemaphoreType.DMA((DISPATCH_SLOTS + 1,))],
        ),
        out_shape=jax.ShapeDtypeStruct((nb_max * BH + spare, PK), jnp.uint32),
        compiler_params=_cparams("arbitrary"),
        name="dispatch",
    )(*seg, tail[0], tail[1], nblk, u2, code, segr, _row_digits(r, GATE_LANES, N_EXP))


def _expert_call(layer, blk, xs, w1, w3, w2, nb_max):
    row_blk = lambda i, be, nb, *_: (jnp.maximum(jnp.minimum(i, (nb[0] - 1) // 2), 0), 0)
    hbm = pl.BlockSpec(memory_space=pl.ANY)
    return pl.pallas_call(
        functools.partial(_expert_kernel, layer),
        grid_spec=pltpu.PrefetchScalarGridSpec(
            num_scalar_prefetch=4,
            grid=(nb_max // 2,),
            in_specs=[pl.BlockSpec((2 * BH, PK), row_blk), hbm, hbm, hbm],
            out_specs=pl.BlockSpec((2 * BH, PK), row_blk),
            scratch_shapes=[pltpu.VMEM((2, D, F_EXP), F32), pltpu.VMEM((2, D, F_EXP), F32),
                            pltpu.VMEM((2, F_EXP, D), F32),
                            pltpu.VMEM((D, F_EXP), BF16), pltpu.VMEM((D, F_EXP), BF16),
                            pltpu.VMEM((F_EXP, D), BF16), pltpu.SemaphoreType.DMA((2, 3))],
        ),
        out_shape=jax.ShapeDtypeStruct(xs.shape, jnp.uint32),
        input_output_aliases={4: 0},
        compiler_params=_cparams("arbitrary"),
        name="experts",
    )(*blk, xs, w1, w3, w2)


def _combine_call(tabs, code_t, segc, u2, ys, s1, s3, s2, h, norms_l, mod_l, t):
    n = u2.shape[0]
    r = _sorted_rows(TM_MOE)
    tok = lambda w: pl.BlockSpec((TM_MOE, w), lambda i, *_: (i, 0))
    const = lambda shp: pl.BlockSpec(shp, lambda i, *_: (0,) * len(shp))
    return pl.pallas_call(
        functools.partial(_combine_kernel, n // TM_MOE),
        grid_spec=pltpu.PrefetchScalarGridSpec(
            num_scalar_prefetch=4,
            grid=(n // TM_MOE,),
            in_specs=[
                tok(4 * N_EXP),
                pl.BlockSpec((None, 2 * N_EXP, GATE_LANES), lambda i, *_: (i, 0, 0)),
                const((N_EXP, r)),
                tok(D),
                pl.BlockSpec(memory_space=pl.ANY),
                const((D, F_EXP)), const((D, F_EXP)), const((F_EXP, D)),
                tok(D),
                const((4, D)),
                pl.BlockSpec((None, 6, D), lambda i, *_: (i * TM_MOE // t, 0, 0)),
            ],
            out_specs=tok(D),
            scratch_shapes=[pltpu.VMEM((2, r + N_EXP * SEG_ALIGN, PK), jnp.uint32),
                            pltpu.SemaphoreType.DMA((2,))],
        ),
        out_shape=jax.ShapeDtypeStruct((n, D), F32),
        compiler_params=_cparams("arbitrary"),
        name="combine",
    )(*tabs, code_t, segc, _row_digits(r, N_EXP, 0).T, u2, ys, s1, s3, s2, h, norms_l, mod_l)


def _glu_kernel(h_ref, g_ref, mod_ref, w_ref, b_ref, o_ref):
    u = (_rms(h_ref[...], g_ref[...]) * (1.0 + mod_ref[1:2, :]) + mod_ref[0:1, :]).astype(BF16)
    step = 256
    for c in range(0, D, step):
        a = _dot(u, w_ref[:, c:c + step]) + b_ref[:, c:c + step]
        gate = _dot(u, w_ref[:, D + c:D + c + step]) + b_ref[:, D + c:D + c + step]
        o_ref[:, c:c + step] = (a * jax.nn.sigmoid(gate)).astype(BF16)


def _glu_call(h, g, mod_l, w1, b1, tm):
    b, t, _ = h.shape
    tok = lambda n: pl.BlockSpec((None, tm, n), lambda i, j: (i, j, 0))
    return pl.pallas_call(
        _glu_kernel,
        grid=(b, t // tm),
        in_specs=[
            tok(D),
            pl.BlockSpec((1, D), lambda i, j: (0, 0)),
            pl.BlockSpec((None, 6, D), lambda i, j: (i, 0, 0)),
            pl.BlockSpec((D, 2 * D), lambda i, j: (0, 0)),
            pl.BlockSpec((1, 2 * D), lambda i, j: (0, 0)),
        ],
        out_specs=tok(D),
        out_shape=jax.ShapeDtypeStruct((b, t, D), BF16),
        compiler_params=_cparams("parallel", "parallel"),
        name="glu",
    )(h, g, mod_l, w1, b1)


def _conv_kernel(cur_ref, prev_ref, next_ref, dw_ref, cv_ref, w2_ref, h_ref, n_ref, mod_ref, wr_ref,
                 h_out, u_out, lg_out, win, shifted, conv):
    j = pl.program_id(1)
    tm = cur_ref.shape[0]
    pad = CONV_W // 2
    win[0:HALO, :] = jnp.where(j > 0, prev_ref[...].astype(F32), 0.0)
    win[HALO:HALO + tm, :] = cur_ref[...].astype(F32)
    win[HALO + tm:, :] = jnp.where(j < pl.num_programs(1) - 1, next_ref[...].astype(F32), 0.0)
    ext = shifted.shape[1]
    full = win[...]
    shifted[0] = full[:ext, :]
    for s in range(1, SUBLANES):
        shifted[s] = pltpu.roll(full, full.shape[0] - s, axis=0)[:ext, :]

    def rows(ib, carry):
        r0 = pl.multiple_of(ib * CONV_TILES * SUBLANES, CONV_TILES * SUBLANES)
        for c0 in range(0, D, CONV_LANES):
            cols = slice(c0, c0 + CONV_LANES)
            accs = [jnp.zeros((SUBLANES, CONV_LANES), F32) for _ in range(CONV_TILES)]
            for s in range(SUBLANES):
                taps = [(off // SUBLANES, off - (HALO - pad)) for off in range(HALO - pad, HALO - pad + CONV_W)
                        if off % SUBLANES == s]
                tiles = {a: shifted[s, pl.ds(r0 + a * SUBLANES, SUBLANES), cols]
                         for a in range(taps[0][0], taps[-1][0] + CONV_TILES)}
                for a, tap in taps:
                    w = dw_ref[tap * SUBLANES:(tap + 1) * SUBLANES, cols]
                    accs = [acc + tiles[a + m] * w for m, acc in enumerate(accs)]
            for m, acc in enumerate(accs):
                conv[pl.ds(r0 + m * SUBLANES, SUBLANES), cols] = acc
        return carry

    lax.fori_loop(0, tm // (CONV_TILES * SUBLANES), rows, 0)
    for rows in _row_blocks(tm):
        uf = conv[rows, :] + cv_ref[0:1, :]
        mu = jnp.mean(uf, axis=-1, keepdims=True)
        var = jnp.mean(jnp.square(uf - mu), axis=-1, keepdims=True)
        uf = (uf - mu) * lax.rsqrt(var + LN_EPS) * cv_ref[1:2, :] + cv_ref[2:3, :]
        y = _dot(_silu(uf).astype(BF16), w2_ref[...]) + cv_ref[3:4, :]
        _mixer_epilogue(y, rows, h_ref, n_ref, mod_ref, wr_ref, h_out, u_out, lg_out)


def _conv_call(glu, dw, cvec, w2, h, norms_l, mod_l, wr_t, tm):
    b, t, _ = h.shape
    tok, ep_in, wr_spec, out_specs, out_shape = _epilogue_specs(b, t, tm)
    r = tm // HALO
    last = t // HALO - 1
    return pl.pallas_call(
        _conv_kernel,
        grid=(b, t // tm),
        in_specs=[
            tok(D),
            pl.BlockSpec((None, HALO, D), lambda i, j: (i, jnp.maximum(j * r - 1, 0), 0)),
            pl.BlockSpec((None, HALO, D), lambda i, j: (i, jnp.minimum((j + 1) * r, last), 0)),
            pl.BlockSpec((CONV_W * SUBLANES, D), lambda i, j: (0, 0)),
            pl.BlockSpec((4, D), lambda i, j: (0, 0)),
            pl.BlockSpec((D, D), lambda i, j: (0, 0)),
        ] + ep_in + [wr_spec],
        out_specs=out_specs,
        out_shape=out_shape,
        scratch_shapes=[pltpu.VMEM((tm + 2 * HALO, D), F32),
                        pltpu.VMEM((SUBLANES, tm + (CONV_W // SUBLANES) * SUBLANES, D), F32),
                        pltpu.VMEM((tm, D), F32)],
        compiler_params=_cparams("parallel", "parallel"),
        name="conv",
    )(glu, glu, glu, dw, cvec, w2, h, norms_l, mod_l, wr_t)


def _tile(t, pref):
    return pref if t % pref == 0 else t


def kernel(x, c, ctx, c_ctx, mod_w, mod_b, norms, ret_w_in, ret_w_out, ret_decay, conv_w1, conv_b1, conv_dw,
           conv_b_dw, conv_ln_g, conv_ln_b, conv_w2, conv_b2, moe_router, moe_bias, moe_w1, moe_w3, moe_w2,
           shared_w1, shared_w3, shared_w2):
    b, t, _ = x.shape
    l = ctx.shape[1]
    tm = _tile(t, 512)

    rows = -(-(b + 1) // 8) * 8
    cs = jnp.concatenate([c, c_ctx[None, :], jnp.zeros((rows - b - 1, D), F32)], 0)
    mod = _mod_call(cs, mod_w, mod_b).reshape(mod_w.shape[0], rows, 6, D)

    idx = jnp.arange(t, dtype=jnp.int32)
    pos_lat = jnp.stack([jnp.full((t,), l, jnp.int32), idx // GRID_W, idx % GRID_W], -1)
    zl = jnp.zeros((l,), jnp.int32)
    pos_ctx = jnp.stack([jnp.arange(l, dtype=jnp.int32), zl, zl], -1)
    ang_lat, ang_ctx = _rope_angles(pos_lat), _rope_angles(pos_ctx)

    w_qk = ret_w_in[0][:, :2 * QK_TOT].reshape(D, 2 * HEADS, DK // 2, 2).swapaxes(-1, -2).reshape(D, 2 * QK_TOT)
    w_in = jnp.concatenate([w_qk, ret_w_in[0][:, 2 * QK_TOT:]], axis=1).astype(BF16)
    log_g = jax.nn.log_sigmoid(ret_decay[0].astype(F32))

    n = b * t
    nb_max = _max_blocks(n)

    def moe_layer(i, h1, u2, logits_t):
        code, code_t, segc, segr, pc_t, off_t = _route_call(logits_t, moe_bias[i])
        seg_disp, seg_comb, tail, blocks = _dispatch_tables(pc_t, off_t, nb_max)
        u2f = u2.reshape(n, D)
        xs = _dispatch_call(seg_disp, tail, blocks[1], u2f, code, segr, nb_max)
        ys = _expert_call(i, blocks, xs, moe_w1, moe_w3, moe_w2, nb_max)
        out = _combine_call(seg_comb, code_t, segc, u2f, ys, shared_w1[i].astype(BF16), shared_w3[i].astype(BF16),
                            shared_w2[i].astype(BF16), h1.reshape(n, D), norms[i], mod[i], t)
        return out.reshape(b, t, D)

    q, k, v, gt = _inproj_call(x, norms[0, 0:1], mod[0], w_in, jnp.cos(ang_lat), jnp.sin(ang_lat), tm)
    sf0, sb0 = _ctxstate_call(log_g, ctx, norms[0, 0:1], mod[0, b], w_in[:, QK_TOT:2 * QK_TOT + V_TOT],
                              jnp.cos(ang_ctx), jnp.sin(ang_ctx))
    yn = _retention_call(log_g, q, k, v, sf0, sb0)
    h1, u2, logits_t = _readout_call(yn, gt, x, norms[0], mod[0], ret_w_out[0].astype(BF16),
                                     moe_router[0].T, _tile(t, 2 * EPI_ROWS))
    h2 = moe_layer(0, h1, u2, logits_t)

    glu = _glu_call(h2, norms[1, 0:1], mod[1], conv_w1[0].astype(BF16), conv_b1[0][None, :], tm)
    cvec = jnp.stack([conv_b_dw[0], conv_ln_g[0], conv_ln_b[0], conv_b2[0]], 0)
    dw_rows = jnp.repeat(conv_dw[0], SUBLANES, axis=0)
    h3, u2, logits_t = _conv_call(glu, dw_rows, cvec, conv_w2[0].astype(BF16), h2, norms[1], mod[1],
                                  moe_router[1].T, tm)
    return moe_layer(1, h3, u2, logits_t)
```

```python
import functools

import jax
import jax.numpy as jnp
import numpy as np
from jax import lax
from jax.experimental import pallas as pl
from jax.experimental.pallas import tpu as pltpu

F32 = jnp.float32
BF16 = jnp.bfloat16

D = 1024
HEADS = 4
DK = D // HEADS
DV = 2 * DK
QK_TOT = HEADS * DK
V_TOT = HEADS * DV
IN_DIM = 2 * QK_TOT + 2 * V_TOT
ROPE_AXES = (64, 96, 96)
ROPE_THETA = 10000.0
GRID_W = 64
CONV_W = 31
HALO = 16
SUBLANES = 8
CONV_LANES = 256
CONV_TILES = 4
N_EXP = 64
N_GRP = 8
PER_GRP = N_EXP // N_GRP
TOP_K = 8
TOP_G = 4
F_EXP = 256
ROUTE_SCALE = 2.5
GATE_LANES = 128
MXU_DIM = 256
TM_MOE = 256
ROUTE_TILES = 4
SEG_ALIGN = 8
POS_RADIX = 256
POS_SCALE = 8
BH = 512
DISPATCH_SLOTS = 4
EXP_SUB = 256
PK = D // 2
NORM_EPS = 1e-6
LN_EPS = 1e-5
RET_CHUNK = 256
EPI_ROWS = 512
VMEM_LIMIT = 56 * 1024 * 1024

_HI = lax.Precision.HIGHEST


def _cparams(*sem):
    return pltpu.CompilerParams(dimension_semantics=sem, vmem_limit_bytes=VMEM_LIMIT)


def _dot(a, b):
    return jnp.dot(a, b, preferred_element_type=F32)


def _rms(xf, g):
    return xf * lax.rsqrt(jnp.mean(xf * xf, axis=-1, keepdims=True) + NORM_EPS) * g


def _silu(x):
    return x * jax.nn.sigmoid(x)


def _mod_kernel(cs_ref, w_ref, b_ref, o_ref):
    s = _silu(cs_ref[...])
    o_ref[...] = jnp.dot(s, w_ref[...], preferred_element_type=F32, precision=_HI) + b_ref[...]


def _mod_call(cs, mod_w, mod_b):
    depth, _, n6 = mod_w.shape
    tn = 1536
    rows = cs.shape[0]
    return pl.pallas_call(
        _mod_kernel,
        grid=(depth, n6 // tn),
        in_specs=[
            pl.BlockSpec((rows, D), lambda i, j: (0, 0)),
            pl.BlockSpec((None, D, tn), lambda i, j: (i, 0, j)),
            pl.BlockSpec((None, 1, tn), lambda i, j: (i, 0, j)),
        ],
        out_specs=pl.BlockSpec((None, rows, tn), lambda i, j: (i, 0, j)),
        out_shape=jax.ShapeDtypeStruct((depth, rows, n6), F32),
        compiler_params=_cparams("parallel", "parallel"),
        name="mod",
    )(cs, mod_w, mod_b.reshape(depth, 1, n6))


def _rope_angles(pos):
    parts = []
    for a, d in enumerate(ROPE_AXES):
        inv = ROPE_THETA ** (-jnp.arange(0, d, 2, dtype=F32) / d)
        parts.append(pos[:, a:a + 1].astype(F32) * inv[None, :])
    return jnp.concatenate(parts, -1)


def _rope_store(z, cos, sin, scale, o_ref, col):
    half = DK // 2
    x0, x1 = z[:, :half], z[:, half:]
    o_ref[:, col:col + half] = ((x0 * cos - x1 * sin) * scale).astype(o_ref.dtype)
    o_ref[:, col + half:col + DK] = ((x0 * sin + x1 * cos) * scale).astype(o_ref.dtype)


def _inproj_kernel(x_ref, g_ref, mod_ref, w_ref, cos_ref, sin_ref, q_ref, k_ref, v_ref, gt_ref):
    u = (_rms(x_ref[...], g_ref[...]) * (1.0 + mod_ref[1:2, :]) + mod_ref[0:1, :]).astype(BF16)
    cos, sin = cos_ref[...], sin_ref[...]
    step = 512
    for j in range(V_TOT // step):
        c1 = 2 * QK_TOT + V_TOT + j * step
        gt_ref[:, j * step:(j + 1) * step] = _silu(_dot(u, w_ref[:, c1:c1 + step])).astype(BF16)
    for h in range(HEADS):
        _rope_store(_dot(u, w_ref[:, h * DK:(h + 1) * DK]), cos, sin, DK ** -0.5, q_ref, h * DK)
        _rope_store(_dot(u, w_ref[:, QK_TOT + h * DK:QK_TOT + (h + 1) * DK]), cos, sin, 1.0, k_ref, h * DK)
    for j in range(V_TOT // step):
        c0 = 2 * QK_TOT + j * step
        v_ref[:, j * step:(j + 1) * step] = _dot(u, w_ref[:, c0:c0 + step]).astype(BF16)


def _inproj_call(x, g, mod_l, w_in, cos, sin, tm):
    b, t, _ = x.shape
    grid = (b, t // tm)
    tok = lambda n: pl.BlockSpec((None, tm, n), lambda i, j: (i, j, 0))
    return pl.pallas_call(
        _inproj_kernel,
        grid=grid,
        in_specs=[
            tok(D),
            pl.BlockSpec((1, D), lambda i, j: (0, 0)),
            pl.BlockSpec((None, 6, D), lambda i, j: (i, 0, 0)),
            pl.BlockSpec((D, IN_DIM), lambda i, j: (0, 0), pipeline_mode=pl.Buffered(1)),
            pl.BlockSpec((tm, DK // 2), lambda i, j: (j, 0)),
            pl.BlockSpec((tm, DK // 2), lambda i, j: (j, 0)),
        ],
        out_specs=[tok(QK_TOT), tok(QK_TOT), tok(V_TOT), tok(V_TOT)],
        out_shape=[
            jax.ShapeDtypeStruct((b, t, QK_TOT), BF16),
            jax.ShapeDtypeStruct((b, t, QK_TOT), BF16),
            jax.ShapeDtypeStruct((b, t, V_TOT), BF16),
            jax.ShapeDtypeStruct((b, t, V_TOT), BF16),
        ],
        compiler_params=_cparams("parallel", "parallel"),
        name="inproj",
    )(x, g, mod_l, w_in, cos, sin)


def _row_pow(lg, expo_fn, rows, cols):
    i = lax.broadcasted_iota(jnp.int32, (rows, cols), 0).astype(F32)
    return jnp.exp(lg * expo_fn(i))


def _ctxstate_kernel(lg_ref, c_ref, g_ref, mod_ref, w_ref, cos_ref, sin_ref, sf_ref, sb_ref, k_scr):
    l = c_ref.shape[0]
    u = (_rms(c_ref[...], g_ref[...]) * (1.0 + mod_ref[1:2, :]) + mod_ref[0:1, :]).astype(BF16)
    cos, sin = cos_ref[...], sin_ref[...]
    for h in range(HEADS):
        _rope_store(_dot(u, w_ref[:, h * DK:(h + 1) * DK]), cos, sin, 1.0, k_scr, h * DK)
    for h in range(HEADS):
        kh = k_scr[:, h * DK:(h + 1) * DK]
        vh = _dot(u, w_ref[:, QK_TOT + h * DV:QK_TOT + (h + 1) * DV]).astype(BF16)
        wf = _row_pow(lg_ref[0, h], lambda i: (l - 1.0) - i, l, DK)
        wb = _row_pow(lg_ref[1, h], lambda i: i, l, DK)
        sf_ref[h] = _dot((kh * wf).T.astype(BF16), vh)
        sb_ref[h] = _dot((kh * wb).T.astype(BF16), vh)


def _ctxstate_call(log_g, ctx, g, mod_row, w_kv, cos, sin):
    b, l, _ = ctx.shape
    st = jax.ShapeDtypeStruct((b, HEADS, DK, DV), F32)
    st_spec = pl.BlockSpec((None, HEADS, DK, DV), lambda i: (i, 0, 0, 0))
    return pl.pallas_call(
        _ctxstate_kernel,
        grid=(b,),
        in_specs=[
            pl.BlockSpec(memory_space=pltpu.SMEM),
            pl.BlockSpec((None, l, D), lambda i: (i, 0, 0)),
            pl.BlockSpec((1, D), lambda i: (0, 0)),
            pl.BlockSpec((6, D), lambda i: (0, 0)),
            pl.BlockSpec((D, QK_TOT + V_TOT), lambda i: (0, 0)),
            pl.BlockSpec((l, DK // 2), lambda i: (0, 0)),
            pl.BlockSpec((l, DK // 2), lambda i: (0, 0)),
        ],
        out_specs=[st_spec, st_spec],
        out_shape=[st, st],
        scratch_shapes=[pltpu.VMEM((l, QK_TOT), F32)],
        compiler_params=_cparams("parallel"),
        name="ctxstate",
    )(log_g, ctx, g, mod_row, w_kv, cos, sin)


def _retention_kernel(lg_ref, q_ref, k_ref, v_ref, gt_ref, sf0_ref, sb0_ref, o_ref, acc, sf, sb):
    h = pl.program_id(1)
    t = q_ref.shape[0]
    c = RET_CHUNK
    nc = t // c
    lgf, lgb = lg_ref[0, h], lg_ref[1, h]
    ri = lax.broadcasted_iota(jnp.int32, (c, c), 0).astype(F32)
    ci = lax.broadcasted_iota(jnp.int32, (c, c), 1).astype(F32)
    rel = ri - ci
    mask = (jnp.where(rel >= 0, jnp.exp(lgf * jnp.maximum(rel, 0.0)), 0.0)
            + jnp.where(rel <= 0, jnp.exp(lgb * jnp.maximum(-rel, 0.0)), 0.0))
    qdf = _row_pow(lgf, lambda i: i + 1.0, c, DV)
    qdb = _row_pow(lgb, lambda i: c - i, c, DV)
    kdf = _row_pow(lgf, lambda i: (c - 1.0) - i, c, DK)
    kdb = _row_pow(lgb, lambda i: i, c, DK)
    cdf = jnp.exp(jnp.full((1, DV), lgf * c, F32))
    cdb = jnp.exp(jnp.full((1, DV), lgb * c, F32))
    sf[...] = sf0_ref[...]
    sb[...] = sb0_ref[...]

    def rows(ic):
        return pl.ds(pl.multiple_of(ic * c, c), c)

    def fwd(rs):
        q, k, v = q_ref[rs, :], k_ref[rs, :], v_ref[rs, :]
        s = lax.dot_general(q, k, (((1,), (1,)), ((), ())), preferred_element_type=F32)
        o = _dot((s * mask).astype(BF16), v) + qdf * _dot(q, sf[...].astype(BF16))
        sf[...] = sf[...] * cdf + _dot((k.astype(F32) * kdf).T.astype(BF16), v)
        return o

    def bwd(rs):
        q, k, v = q_ref[rs, :], k_ref[rs, :], v_ref[rs, :]
        o = qdb * _dot(q, sb[...].astype(BF16))
        sb[...] = sb[...] * cdb + _dot((k.astype(F32) * kdb).T.astype(BF16), v)
        return o

    def finish(rs, o):
        o = o * lax.rsqrt(jnp.mean(o * o, axis=-1, keepdims=True) + NORM_EPS)
        o_ref[rs, :] = (o * gt_ref[rs, :].astype(F32)).astype(BF16)

    def first_half(j, carry):
        of, ob = fwd(rows(j)), bwd(rows(nc - 1 - j))
        acc[rows(j), :] = of
        acc[rows(nc - 1 - j), :] = ob
        return carry

    def second_half(j, carry):
        of, ob = fwd(rows(j)), bwd(rows(nc - 1 - j))
        finish(rows(j), acc[rows(j), :] + of)
        finish(rows(nc - 1 - j), acc[rows(nc - 1 - j), :] + ob)
        return carry

    lax.fori_loop(0, nc // 2, first_half, 0, unroll=True)
    lax.fori_loop(nc // 2, nc, second_half, 0, unroll=True)


def _retention_call(log_g, q, k, v, gt, sf0, sb0):
    b, t, _ = q.shape
    qk_spec = pl.BlockSpec((None, t, DK), lambda i, h: (i, 0, h))
    v_spec = pl.BlockSpec((None, t, DV), lambda i, h: (i, 0, h))
    s_spec = pl.BlockSpec((None, None, DK, DV), lambda i, h: (i, h, 0, 0))
    return pl.pallas_call(
        _retention_kernel,
        grid=(b, HEADS),
        in_specs=[pl.BlockSpec(memory_space=pltpu.SMEM), qk_spec, qk_spec, v_spec, v_spec, s_spec, s_spec],
        out_specs=v_spec,
        out_shape=jax.ShapeDtypeStruct((b, t, V_TOT), BF16),
        scratch_shapes=[pltpu.VMEM((t, DV), F32), pltpu.VMEM((DK, DV), F32), pltpu.VMEM((DK, DV), F32)],
        compiler_params=_cparams("parallel", "parallel"),
        name="retention",
    )(log_g, q, k, v, gt, sf0, sb0)


def _mixer_epilogue(y, rows, h_ref, n_ref, mod_ref, wr_ref, h_out, u_out, lg_out):
    h1 = h_ref[rows, :] + mod_ref[2:3, :] * _rms(y, n_ref[1:2, :])
    h_out[rows, :] = h1
    u2 = _rms(h1, n_ref[2:3, :]) * (1.0 + mod_ref[4:5, :]) + mod_ref[3:4, :]
    u_hi = u2.astype(BF16)
    u_out[rows, :] = u_hi
    u_lo = (u2 - u_hi.astype(F32)).astype(BF16)
    w = wr_ref[...]
    w_hi = w.astype(BF16)
    w_lo = (w - w_hi.astype(F32)).astype(BF16)
    nt_dot = lambda a, b: lax.dot_general(a, b, (((1,), (1,)), ((), ())), preferred_element_type=F32)
    lg_out[:, rows] = nt_dot(w_hi, u_hi) + (nt_dot(w_hi, u_lo) + nt_dot(w_lo, u_hi))


def _row_blocks(tm):
    return [slice(r0, r0 + EPI_ROWS) for r0 in range(0, tm, EPI_ROWS)]


def _readout_kernel(y_ref, h_ref, n_ref, mod_ref, w_ref, wr_ref, h_out, u_out, lg_out):
    blocks = _row_blocks(y_ref.shape[0])
    ys = [_dot(y_ref[rows, :], w_ref[...]) for rows in blocks]
    for rows, y in zip(blocks, ys):
        _mixer_epilogue(y, rows, h_ref, n_ref, mod_ref, wr_ref, h_out, u_out, lg_out)


def _epilogue_specs(b, t, tm):
    nt = t // tm
    tok = lambda n: pl.BlockSpec((None, tm, n), lambda i, j: (i, j, 0))
    in_specs = [
        tok(D),
        pl.BlockSpec((4, D), lambda i, j: (0, 0)),
        pl.BlockSpec((None, 6, D), lambda i, j: (i, 0, 0)),
    ]
    wr_spec = pl.BlockSpec((N_EXP, D), lambda i, j: (0, 0))
    out_specs = [tok(D), tok(D), pl.BlockSpec((N_EXP, tm), lambda i, j: (0, i * nt + j))]
    out_shape = [
        jax.ShapeDtypeStruct((b, t, D), F32),
        jax.ShapeDtypeStruct((b, t, D), BF16),
        jax.ShapeDtypeStruct((N_EXP, b * t), F32),
    ]
    return tok, in_specs, wr_spec, out_specs, out_shape


def _readout_call(yg, h, norms_l, mod_l, w_out, wr_t, tm):
    b, t, _ = h.shape
    tok, ep_in, wr_spec, out_specs, out_shape = _epilogue_specs(b, t, tm)
    return pl.pallas_call(
        _readout_kernel,
        grid=(b, t // tm),
        in_specs=[tok(V_TOT)] + ep_in
        + [pl.BlockSpec((V_TOT, D), lambda i, j: (0, 0), pipeline_mode=pl.Buffered(1)), wr_spec],
        out_specs=out_specs,
        out_shape=out_shape,
        compiler_params=_cparams("parallel", "parallel"),
        name="readout",
    )(yg, h, norms_l, mod_l, w_out, wr_t)


def _route_tile(logits, bias):
    tr = logits.shape[1]
    s = jax.nn.sigmoid(logits)
    biased = s + bias
    neg = -jnp.inf
    b3 = biased.reshape(N_GRP, PER_GRP, tr)
    io3 = lax.broadcasted_iota(jnp.int32, b3.shape, 1).astype(F32)
    m1 = jnp.max(b3, axis=1, keepdims=True)
    i1 = jnp.min(jnp.where(b3 == m1, io3, float(PER_GRP)), axis=1, keepdims=True)
    m2 = jnp.max(jnp.where(io3 == i1, neg, b3), axis=1, keepdims=True)
    gs = (m1 + m2).reshape(N_GRP, tr)
    iog = lax.broadcasted_iota(jnp.int32, gs.shape, 0).astype(F32)
    gsel = jnp.zeros(gs.shape, F32)
    work = gs
    for _ in range(TOP_G):
        m = jnp.max(work, axis=0, keepdims=True)
        gi = jnp.min(jnp.where(work == m, iog, float(N_GRP)), axis=0, keepdims=True)
        hit = iog == gi
        gsel = jnp.where(hit, 1.0, gsel)
        work = jnp.where(hit, neg, work)
    emask = jnp.broadcast_to(gsel.reshape(N_GRP, 1, tr), b3.shape).reshape(N_EXP, tr)
    work = jnp.where(emask > 0.0, biased, neg)
    ioe = lax.broadcasted_iota(jnp.int32, work.shape, 0).astype(F32)
    esel = jnp.zeros(work.shape, F32)
    for _ in range(TOP_K):
        m = jnp.max(work, axis=0, keepdims=True)
        ei = jnp.min(jnp.where(work == m, ioe, float(N_EXP)), axis=0, keepdims=True)
        hit = ioe == ei
        esel = jnp.where(hit, 1.0, esel)
        work = jnp.where(hit, neg, work)
    wsel = esel * s
    gate = wsel / jnp.sum(wsel, axis=0, keepdims=True) * ROUTE_SCALE

    t_r = lax.broadcasted_iota(jnp.int32, (tr, tr), 0)
    t_c = lax.broadcasted_iota(jnp.int32, (tr, tr), 1)
    rank = _dot(esel.astype(BF16), jnp.where(t_r < t_c, 1.0, 0.0).astype(BF16))
    cnt = jnp.sum(esel, axis=1, keepdims=True)
    pc = jnp.floor((cnt + (SEG_ALIGN - 1.0)) * (1.0 / SEG_ALIGN)) * SEG_ALIGN
    e_r = lax.broadcasted_iota(jnp.int32, (N_EXP, N_EXP), 0)
    e_c = lax.broadcasted_iota(jnp.int32, (N_EXP, N_EXP), 1)
    off = jnp.dot(jnp.where(e_c < e_r, 1.0, 0.0), jnp.broadcast_to(pc, (N_EXP, 128)),
                  preferred_element_type=F32, precision=_HI)[:, 0:1]
    p = off + rank
    p_hi = jnp.where(esel > 0.0, jnp.floor(p * (1.0 / POS_RADIX)) * POS_RADIX, -float(POS_RADIX))
    p_lo = jnp.where(esel > 0.0, p - p_hi, 0.0)
    minus_r = jnp.where(lax.broadcasted_iota(jnp.int32, (N_EXP, tr), 0) < 2, -float(POS_SCALE), 0.0)
    code = jnp.concatenate([p_hi * POS_SCALE, p_lo * POS_SCALE, gate, minus_r], 0)
    code_t = jnp.concatenate([code[:2 * N_EXP].T, code[2 * N_EXP:].T], 1)
    lane = lax.broadcasted_iota(jnp.int32, (2 * N_EXP, GATE_LANES), 1)
    off2 = jnp.concatenate([off, off], 0)
    end2 = jnp.concatenate([off + pc, off + pc], 0)
    bounds = jnp.where(lane == 0, off2, jnp.where(lane == 1, end2, 0.0))
    return code.astype(BF16), code_t.astype(BF16), bounds, bounds.T[0:SUBLANES, :], pc, off


def _route_kernel(lg_ref, b_ref, code_ref, code_t_ref, segc_ref, segr_ref, pc_ref, off_ref):
    i = pl.program_id(0)

    @pl.when(i == 0)
    def _():
        pc_ref[...] = jnp.zeros(pc_ref.shape, F32)
        off_ref[...] = jnp.zeros(off_ref.shape, F32)

    tile_lane = lax.broadcasted_iota(jnp.int32, pc_ref.shape, 1)
    for k in range(ROUTE_TILES):
        cols = slice(k * TM_MOE, (k + 1) * TM_MOE)
        code, code_t, bounds, bounds_t, pc, off = _route_tile(lg_ref[:, cols], b_ref[...])
        code_ref[:, cols] = code
        code_t_ref[cols, :] = code_t
        segc_ref[k] = bounds
        segr_ref[k] = bounds_t
        pc_ref[...] = jnp.where(tile_lane == i * ROUTE_TILES + k, pc, pc_ref[...])
        off_ref[...] = jnp.where(tile_lane == i * ROUTE_TILES + k, off, off_ref[...])


def _route_call(logits_t, bias):
    n = logits_t.shape[1]
    nt = n // TM_MOE
    tr = ROUTE_TILES * TM_MOE
    return pl.pallas_call(
        _route_kernel,
        grid=(nt // ROUTE_TILES,),
        in_specs=[pl.BlockSpec((N_EXP, tr), lambda i: (0, i)), pl.BlockSpec((N_EXP, 1), lambda i: (0, 0))],
        out_specs=[
            pl.BlockSpec((4 * N_EXP, tr), lambda i: (0, i)),
            pl.BlockSpec((tr, 4 * N_EXP), lambda i: (i, 0)),
            pl.BlockSpec((ROUTE_TILES, 2 * N_EXP, GATE_LANES), lambda i: (i, 0, 0)),
            pl.BlockSpec((ROUTE_TILES, SUBLANES, GATE_LANES), lambda i: (i, 0, 0)),
            pl.BlockSpec((N_EXP, nt), lambda i: (0, 0)),
            pl.BlockSpec((N_EXP, nt), lambda i: (0, 0)),
        ],
        out_shape=[
            jax.ShapeDtypeStruct((4 * N_EXP, n), BF16),
            jax.ShapeDtypeStruct((n, 4 * N_EXP), BF16),
            jax.ShapeDtypeStruct((nt, 2 * N_EXP, GATE_LANES), F32),
            jax.ShapeDtypeStruct((nt, SUBLANES, GATE_LANES), F32),
            jax.ShapeDtypeStruct((N_EXP, nt), F32),
            jax.ShapeDtypeStruct((N_EXP, nt), F32),
        ],
        compiler_params=_cparams("arbitrary"),
        name="route",
    )(logits_t, bias.reshape(N_EXP, 1))


def _dispatch_tables(pc_t, off_t, nb_max):
    nt = pc_t.shape[1]
    ids = jnp.arange(N_EXP, dtype=jnp.int32)
    psum = lambda a, m: jnp.dot(a, m, precision=_HI, preferred_element_type=F32)
    before_tile = (jnp.arange(nt)[:, None] < jnp.arange(nt)[None, :]).astype(F32)
    upto_expert = (ids[:, None] <= ids[None, :]).astype(F32)
    used_f = pc_t.sum(1)
    region_f = jnp.ceil(used_f * (1.0 / BH)) * BH
    ends_f = psum(region_f[None, :], upto_expert)[0]
    pc = pc_t.astype(jnp.int32)
    used, region, ends = used_f.astype(jnp.int32), region_f.astype(jnp.int32), ends_f.astype(jnp.int32)
    starts = ends - region
    dst = starts[:, None] + psum(pc_t, before_tile).astype(jnp.int32)
    nblk = ends[-1] // BH
    blk = jnp.minimum(jnp.arange(nb_max, dtype=jnp.int32), nblk - 1)
    blk_e = jnp.minimum(jnp.sum(ends[None, :] <= blk[:, None] * BH, axis=1), N_EXP - 1).astype(jnp.int32)
    has = region > 0
    slot_e = (psum(has.astype(F32)[None, :], upto_expert)[0].astype(jnp.int32) - 1) % 2
    later = has[None, :] & (ids[None, :] > ids[:, None])
    next_e = jnp.where(later.any(1), jnp.argmax(later, axis=1), -1)
    of_blk = lambda per_e: jnp.sum(jnp.where(blk_e[:, None] == ids[None, :], per_e[None, :], 0), axis=1)
    blocks = (blk_e, nblk.reshape(1).astype(jnp.int32), of_blk(slot_e).astype(jnp.int32),
              of_blk(next_e).astype(jnp.int32))
    flat = lambda a: a.reshape(-1).astype(jnp.int32)
    tail = ((starts + used).astype(jnp.int32), (region - used).astype(jnp.int32))
    off = off_t.astype(jnp.int32)
    e_idx = jnp.arange(N_EXP, dtype=jnp.int32)[:, None]
    col = lambda v: jnp.broadcast_to(jnp.asarray(v, jnp.int32), (N_EXP, 1))
    table = lambda so, do, ln: (flat(so), flat(do), flat(ln), ln.sum(0).astype(jnp.int32))
    pc_c = jnp.concatenate([pc, col(0)], 1)
    spare_c = _sorted_rows(TM_MOE) + SEG_ALIGN * e_idx
    seg_comb = table(jnp.where(pc_c > 0, jnp.concatenate([off, col(0)], 1), spare_c),
                     jnp.where(pc_c > 0, jnp.concatenate([dst, col(0)], 1), 0), jnp.maximum(pc_c, SEG_ALIGN))
    pc_d = jnp.concatenate([col(0), pc], 1)
    entry = jnp.arange(nt + 1, dtype=jnp.int32)[None, :]
    spare_d = nb_max * BH + ((entry % DISPATCH_SLOTS) * N_EXP + e_idx) * SEG_ALIGN
    seg_disp = table(jnp.where(pc_d > 0, jnp.concatenate([col(0), off], 1), 0),
                     jnp.where(pc_d > 0, jnp.concatenate([col(0), dst], 1), spare_d), jnp.maximum(pc_d, SEG_ALIGN))
    return seg_disp, seg_comb, tail, blocks


def _swiglu(u, w1, w3, w2):
    return _dot((_silu(_dot(u, w1)) * _dot(u, w3)).astype(BF16), w2)


def _round_bf16(x):
    return x.astype(BF16).astype(F32)


def _pack_pair(lo, hi):
    lo_b = lax.bitcast_convert_type(lo, jnp.uint32)
    hi_b = lax.bitcast_convert_type(hi, jnp.uint32)
    return (hi_b & jnp.uint32(0xFFFF0000)) | (lo_b >> 16)


def _unpack_pair(w):
    lo = lax.bitcast_convert_type(w << 16, F32).astype(BF16)
    hi = lax.bitcast_convert_type(w & jnp.uint32(0xFFFF0000), F32).astype(BF16)
    return lo, hi


def _segment_copies_dense(so_ref, do_ref, ln_ref, entry, entries, make_copy):
    for e in range(N_EXP):
        j = e * entries + entry
        make_copy(pl.multiple_of(so_ref[j], SEG_ALIGN), pl.multiple_of(do_ref[j], SEG_ALIGN),
                  pl.multiple_of(ln_ref[j], SEG_ALIGN)).start()


def _dispatch_kernel(nt, so_ref, do_ref, ln_ref, tt_ref, td_ref, tl_ref, nb_ref, u_ref, code_ref, segr_ref,
                     digits_ref, xs_out, buf, zbuf, sem):
    i = pl.program_id(0)
    r = buf.shape[1]
    nb_max = (xs_out.shape[0] - DISPATCH_SLOTS * N_EXP * SEG_ALIGN) // BH
    fill_sem = DISPATCH_SLOTS
    slot = i % DISPATCH_SLOTS
    prev_slot = (i + DISPATCH_SLOTS - 1) % DISPATCH_SLOTS

    def strip(slot_, so, do, ln):
        return pltpu.make_async_copy(buf.at[slot_, pl.ds(so, ln)], xs_out.at[pl.ds(do, ln)], sem.at[slot_])

    def wait_entry(slot_, entry):
        strip(slot_, 0, 0, pl.multiple_of(tt_ref[entry], SEG_ALIGN)).wait()

    def zero_fills(act):
        def tail(e, carry):
            ln = pl.multiple_of(tl_ref[e], SEG_ALIGN)

            @pl.when(ln > 0)
            def _():
                act(pltpu.make_async_copy(zbuf.at[pl.ds(0, ln)],
                                          xs_out.at[pl.ds(pl.multiple_of(td_ref[e], SEG_ALIGN), ln)], sem.at[fill_sem]))

            return carry

        lax.fori_loop(0, N_EXP, tail, 0)

        def blk(j, carry):
            act(pltpu.make_async_copy(zbuf.at[pl.ds(0, BH)], xs_out.at[pl.ds(pl.multiple_of(j * BH, BH), BH)],
                                      sem.at[fill_sem]))
            return carry

        lax.fori_loop(nb_ref[0], nb_max, blk, 0)

    @pl.when(i == 0)
    def _():
        zbuf[...] = jnp.zeros(zbuf.shape, jnp.uint32)
        buf[...] = jnp.zeros(buf.shape, jnp.uint32)
        spare = pltpu.make_async_copy(zbuf.at[pl.ds(0, DISPATCH_SLOTS * N_EXP * SEG_ALIGN)],
                                      xs_out.at[pl.ds(nb_max * BH, DISPATCH_SLOTS * N_EXP * SEG_ALIGN)],
                                      sem.at[fill_sem])
        spare.start()
        spare.wait()
        zero_fills(lambda cp: cp.start())

    @pl.when(i >= DISPATCH_SLOTS - 1)
    def _():
        wait_entry(slot, i - (DISPATCH_SLOTS - 1))

    row = lax.broadcasted_iota(jnp.int32, (r, GATE_LANES), 0).astype(F32)
    own = jnp.where(row >= segr_ref[0:1, :], jnp.where(row < segr_ref[1:2, :], 1.0, 0.0), 0.0).astype(BF16)
    z = _dot(jnp.concatenate([own, digits_ref[...]], axis=1), code_ref[...])
    perm = jnp.where(z == 0.0, 1.0, 0.0).astype(BF16)
    step = 256
    for c in range(0, PK, step):
        buf[slot, :, c:c + step] = _pack_pair(_dot(perm, u_ref[:, c:c + step]),
                                              _dot(perm, u_ref[:, PK + c:PK + c + step]))
    _segment_copies_dense(so_ref, do_ref, ln_ref, i, nt + 1, functools.partial(strip, prev_slot))

    @pl.when(i == nt - 1)
    def _():
        _segment_copies_dense(so_ref, do_ref, ln_ref, nt, nt + 1, functools.partial(strip, slot))
        for entry in range(max(nt - DISPATCH_SLOTS + 1, 0), nt + 1):
            wait_entry((entry + DISPATCH_SLOTS - 1) % DISPATCH_SLOTS, entry)
        zero_fills(lambda cp: cp.wait())


def _expert_kernel(layer, be_ref, nb_ref, ws_ref, nx_ref, x_ref, w1_hbm, w3_hbm, w2_hbm, y_ref,
                   w1f, w3f, w2f, w1b, w3b, w2b, sem):
    i = pl.program_id(0)
    first, second = 2 * i, 2 * i + 1
    live = first < nb_ref[0]
    split = (second < nb_ref[0]) & (be_ref[second] != be_ref[first])

    def fetch(e, slot):
        return [pltpu.make_async_copy(src.at[layer, e], dst.at[slot], sem.at[slot, j])
                for j, (src, dst) in enumerate(((w1_hbm, w1f), (w3_hbm, w3f), (w2_hbm, w2f)))]

    def begin_expert(blk):
        slot = ws_ref[blk]

        @pl.when(blk == 0)
        def _():
            for cp in fetch(be_ref[0], slot):
                cp.start()

        for cp in fetch(be_ref[blk], slot):
            cp.wait()
        w1b[...] = w1f[slot].astype(BF16)
        w3b[...] = w3f[slot].astype(BF16)
        w2b[...] = w2f[slot].astype(BF16)

        @pl.when(nx_ref[blk] >= 0)
        def _():
            for cp in fetch(nx_ref[blk], 1 - slot):
                cp.start()

    def compute(row0, nrows):
        up = []
        for r0 in range(row0, row0 + nrows, EXP_SUB):
            lo, hi = _unpack_pair(x_ref[r0:r0 + EXP_SUB, :])
            up.append((_dot(lo, w1b[:PK, :]) + _dot(hi, w1b[PK:, :]), _dot(lo, w3b[:PK, :]) + _dot(hi, w3b[PK:, :])))
        for j, (a, g) in enumerate(up):
            r0 = row0 + j * EXP_SUB
            hid = (_silu(a) * g).astype(BF16)
            y_ref[r0:r0 + EXP_SUB, :] = _pack_pair(_round_bf16(_dot(hid, w2b[:, :PK])),
                                                   _round_bf16(_dot(hid, w2b[:, PK:])))

    @pl.when(live & ((i == 0) | (be_ref[first] != be_ref[jnp.maximum(first - 1, 0)])))
    def _():
        begin_expert(first)

    @pl.when(live & jnp.logical_not(split))
    def _():
        compute(0, 2 * BH)

    @pl.when(split)
    def _():
        compute(0, BH)
        begin_expert(second)
        compute(BH, BH)


def _combine_kernel(nt, so_ref, do_ref, ln_ref, tt_ref, code_t_ref, segc_ref, digits_ref, u_ref, ys_ref,
                    s1_ref, s3_ref, s2_ref, h_ref, n_ref, mod_ref, o_ref, buf, sem):
    i = pl.program_id(0)
    slot = i % 2
    r = digits_ref.shape[1]

    def strip(slot_, so, do, ln):
        return pltpu.make_async_copy(ys_ref.at[pl.ds(do, ln)], buf.at[slot_, pl.ds(so, ln)], sem.at[slot_])

    @pl.when(i == 0)
    def _():
        buf[...] = jnp.zeros(buf.shape, jnp.uint32)
        _segment_copies_dense(so_ref, do_ref, ln_ref, 0, nt + 1, functools.partial(strip, 0))

    col = lax.broadcasted_iota(jnp.int32, (2 * N_EXP, r), 1).astype(F32)
    own = jnp.where(col >= segc_ref[:, 0:1], jnp.where(col < segc_ref[:, 1:2], 1.0, 0.0), 0.0).astype(BF16)
    z = _dot(code_t_ref[...], jnp.concatenate([own, own[:N_EXP], digits_ref[...]], axis=0))
    q = jnp.where(jnp.abs(z) < 0.5 * POS_SCALE, z, 0.0).astype(BF16)
    shared = _swiglu(u_ref[...], s1_ref[...], s3_ref[...], s2_ref[...])
    _segment_copies_dense(so_ref, do_ref, ln_ref, i + 1, nt + 1, functools.partial(strip, 1 - slot))

    strip(slot, 0, 0, pl.multiple_of(tt_ref[i], SEG_ALIGN)).wait()
    lo, hi = _unpack_pair(buf[slot, 0:r, :])
    moe = jnp.concatenate([_dot(q, lo), _dot(q, hi)], axis=1) + shared
    o_ref[...] = h_ref[...] + mod_ref[5:6, :] * _rms(moe, n_ref[3:4, :])

    @pl.when(i == nt - 1)
    def _():
        strip(1 - slot, 0, 0, pl.multiple_of(tt_ref[nt], SEG_ALIGN)).wait()


def _sorted_rows(tm):
    return -(-(TOP_K * tm + N_EXP * (SEG_ALIGN - 1)) // MXU_DIM) * MXU_DIM


def _max_blocks(n):
    nt = n // TM_MOE
    rows = TOP_K * n + N_EXP * (SEG_ALIGN - 1) * nt + N_EXP * (BH - SEG_ALIGN)
    return -(-rows // (2 * BH)) * 2


def _row_digits(r, width, first):
    idx = np.arange(r)
    digits = np.zeros((r, width), np.float32)
    digits[:, first] = idx // POS_RADIX * POS_RADIX
    digits[:, first + 1] = idx % POS_RADIX
    return jnp.asarray(digits, BF16)


def _dispatch_call(seg, tail, nblk, u2, code, segr, nb_max):
    n = u2.shape[0]
    r = _sorted_rows(TM_MOE)
    spare = DISPATCH_SLOTS * N_EXP * SEG_ALIGN
    return pl.pallas_call(
        functools.partial(_dispatch_kernel, n // TM_MOE),
        grid_spec=pltpu.PrefetchScalarGridSpec(
            num_scalar_prefetch=7,
            grid=(n // TM_MOE,),
            in_specs=[
                pl.BlockSpec((TM_MOE, D), lambda i, *_: (i, 0)),
                pl.BlockSpec((4 * N_EXP, TM_MOE), lambda i, *_: (0, i)),
                pl.BlockSpec((None, SUBLANES, GATE_LANES), lambda i, *_: (i, 0, 0)),
                pl.BlockSpec((r, GATE_LANES), lambda i, *_: (0, 0)),
            ],
            out_specs=pl.BlockSpec(memory_space=pl.ANY),
            scratch_shapes=[pltpu.VMEM((DISPATCH_SLOTS, r, PK), jnp.uint32),
                            pltpu.VMEM((max(BH, spare), PK), jnp.uint32),
                            pltpu.SemaphoreType.DMA((DISPATCH_SLOTS + 1,))],
        ),
        out_shape=jax.ShapeDtypeStruct((nb_max * BH + spare, PK), jnp.uint32),
        compiler_params=_cparams("arbitrary"),
        name="dispatch",
    )(*seg, tail[0], tail[1], nblk, u2, code, segr, _row_digits(r, GATE_LANES, N_EXP))


def _expert_call(layer, blk, xs, w1, w3, w2, nb_max):
    row_blk = lambda i, be, nb, *_: (jnp.maximum(jnp.minimum(i, (nb[0] - 1) // 2), 0), 0)
    hbm = pl.BlockSpec(memory_space=pl.ANY)
    return pl.pallas_call(
        functools.partial(_expert_kernel, layer),
        grid_spec=pltpu.PrefetchScalarGridSpec(
            num_scalar_prefetch=4,
            grid=(nb_max // 2,),
            in_specs=[pl.BlockSpec((2 * BH, PK), row_blk), hbm, hbm, hbm],
            out_specs=pl.BlockSpec((2 * BH, PK), row_blk),
            scratch_shapes=[pltpu.VMEM((2, D, F_EXP), F32), pltpu.VMEM((2, D, F_EXP), F32),
                            pltpu.VMEM((2, F_EXP, D), F32),
                            pltpu.VMEM((D, F_EXP), BF16), pltpu.VMEM((D, F_EXP), BF16),
                            pltpu.VMEM((F_EXP, D), BF16), pltpu.SemaphoreType.DMA((2, 3))],
        ),
        out_shape=jax.ShapeDtypeStruct(xs.shape, jnp.uint32),
        input_output_aliases={4: 0},
        compiler_params=_cparams("arbitrary"),
        name="experts",
    )(*blk, xs, w1, w3, w2)


def _combine_call(tabs, code_t, segc, u2, ys, s1, s3, s2, h, norms_l, mod_l, t):
    n = u2.shape[0]
    r = _sorted_rows(TM_MOE)
    tok = lambda w: pl.BlockSpec((TM_MOE, w), lambda i, *_: (i, 0))
    const = lambda shp: pl.BlockSpec(shp, lambda i, *_: (0,) * len(shp))
    return pl.pallas_call(
        functools.partial(_combine_kernel, n // TM_MOE),
        grid_spec=pltpu.PrefetchScalarGridSpec(
            num_scalar_prefetch=4,
            grid=(n // TM_MOE,),
            in_specs=[
                tok(4 * N_EXP),
                pl.BlockSpec((None, 2 * N_EXP, GATE_LANES), lambda i, *_: (i, 0, 0)),
                const((N_EXP, r)),
                tok(D),
                pl.BlockSpec(memory_space=pl.ANY),
                const((D, F_EXP)), const((D, F_EXP)), const((F_EXP, D)),
                tok(D),
                const((4, D)),
                pl.BlockSpec((None, 6, D), lambda i, *_: (i * TM_MOE // t, 0, 0)),
            ],
            out_specs=tok(D),
            scratch_shapes=[pltpu.VMEM((2, r + N_EXP * SEG_ALIGN, PK), jnp.uint32),
                            pltpu.SemaphoreType.DMA((2,))],
        ),
        out_shape=jax.ShapeDtypeStruct((n, D), F32),
        compiler_params=_cparams("arbitrary"),
        name="combine",
    )(*tabs, code_t, segc, _row_digits(r, N_EXP, 0).T, u2, ys, s1, s3, s2, h, norms_l, mod_l)


def _glu_kernel(h_ref, g_ref, mod_ref, w_ref, b_ref, o_ref):
    u = (_rms(h_ref[...], g_ref[...]) * (1.0 + mod_ref[1:2, :]) + mod_ref[0:1, :]).astype(BF16)
    step = 256
    for c in range(0, D, step):
        a = _dot(u, w_ref[:, c:c + step]) + b_ref[:, c:c + step]
        gate = _dot(u, w_ref[:, D + c:D + c + step]) + b_ref[:, D + c:D + c + step]
        o_ref[:, c:c + step] = (a * jax.nn.sigmoid(gate)).astype(BF16)


def _glu_call(h, g, mod_l, w1, b1, tm):
    b, t, _ = h.shape
    tok = lambda n: pl.BlockSpec((None, tm, n), lambda i, j: (i, j, 0))
    return pl.pallas_call(
        _glu_kernel,
        grid=(b, t // tm),
        in_specs=[
            tok(D),
            pl.BlockSpec((1, D), lambda i, j: (0, 0)),
            pl.BlockSpec((None, 6, D), lambda i, j: (i, 0, 0)),
            pl.BlockSpec((D, 2 * D), lambda i, j: (0, 0)),
            pl.BlockSpec((1, 2 * D), lambda i, j: (0, 0)),
        ],
        out_specs=tok(D),
        out_shape=jax.ShapeDtypeStruct((b, t, D), BF16),
        compiler_params=_cparams("parallel", "parallel"),
        name="glu",
    )(h, g, mod_l, w1, b1)


def _conv_kernel(cur_ref, prev_ref, next_ref, dw_ref, cv_ref, w2_ref, h_ref, n_ref, mod_ref, wr_ref,
                 h_out, u_out, lg_out, win, shifted, conv):
    j = pl.program_id(1)
    tm = cur_ref.shape[0]
    pad = CONV_W // 2
    win[0:HALO, :] = jnp.where(j > 0, prev_ref[...].astype(F32), 0.0)
    win[HALO:HALO + tm, :] = cur_ref[...].astype(F32)
    win[HALO + tm:, :] = jnp.where(j < pl.num_programs(1) - 1, next_ref[...].astype(F32), 0.0)
    ext = shifted.shape[1]
    full = win[...]
    shifted[0] = full[:ext, :]
    for s in range(1, SUBLANES):
        shifted[s] = pltpu.roll(full, full.shape[0] - s, axis=0)[:ext, :]

    def rows(ib, carry):
        r0 = pl.multiple_of(ib * CONV_TILES * SUBLANES, CONV_TILES * SUBLANES)
        for c0 in range(0, D, CONV_LANES):
            cols = slice(c0, c0 + CONV_LANES)
            accs = [jnp.zeros((SUBLANES, CONV_LANES), F32) for _ in range(CONV_TILES)]
            for s in range(SUBLANES):
                taps = [(off // SUBLANES, off - (HALO - pad)) for off in range(HALO - pad, HALO - pad + CONV_W)
                        if off % SUBLANES == s]
                tiles = {a: shifted[s, pl.ds(r0 + a * SUBLANES, SUBLANES), cols]
                         for a in range(taps[0][0], taps[-1][0] + CONV_TILES)}
                for a, tap in taps:
                    w = dw_ref[tap * SUBLANES:(tap + 1) * SUBLANES, cols]
                    accs = [acc + tiles[a + m] * w for m, acc in enumerate(accs)]
            for m, acc in enumerate(accs):
                conv[pl.ds(r0 + m * SUBLANES, SUBLANES), cols] = acc
        return carry

    lax.fori_loop(0, tm // (CONV_TILES * SUBLANES), rows, 0)
    for rows in _row_blocks(tm):
        uf = conv[rows, :] + cv_ref[0:1, :]
        mu = jnp.mean(uf, axis=-1, keepdims=True)
        var = jnp.mean(jnp.square(uf - mu), axis=-1, keepdims=True)
        uf = (uf - mu) * lax.rsqrt(var + LN_EPS) * cv_ref[1:2, :] + cv_ref[2:3, :]
        y = _dot(_silu(uf).astype(BF16), w2_ref[...]) + cv_ref[3:4, :]
        _mixer_epilogue(y, rows, h_ref, n_ref, mod_ref, wr_ref, h_out, u_out, lg_out)


def _conv_call(glu, dw, cvec, w2, h, norms_l, mod_l, wr_t, tm):
    b, t, _ = h.shape
    tok, ep_in, wr_spec, out_specs, out_shape = _epilogue_specs(b, t, tm)
    r = tm // HALO
    last = t // HALO - 1
    return pl.pallas_call(
        _conv_kernel,
        grid=(b, t // tm),
        in_specs=[
            tok(D),
            pl.BlockSpec((None, HALO, D), lambda i, j: (i, jnp.maximum(j * r - 1, 0), 0)),
            pl.BlockSpec((None, HALO, D), lambda i, j: (i, jnp.minimum((j + 1) * r, last), 0)),
            pl.BlockSpec((CONV_W * SUBLANES, D), lambda i, j: (0, 0)),
            pl.BlockSpec((4, D), lambda i, j: (0, 0)),
            pl.BlockSpec((D, D), lambda i, j: (0, 0)),
        ] + ep_in + [wr_spec],
        out_specs=out_specs,
        out_shape=out_shape,
        scratch_shapes=[pltpu.VMEM((tm + 2 * HALO, D), F32),
                        pltpu.VMEM((SUBLANES, tm + (CONV_W // SUBLANES) * SUBLANES, D), F32),
                        pltpu.VMEM((tm, D), F32)],
        compiler_params=_cparams("parallel", "parallel"),
        name="conv",
    )(glu, glu, glu, dw, cvec, w2, h, norms_l, mod_l, wr_t)


def _tile(t, pref):
    return pref if t % pref == 0 else t


def kernel(x, c, ctx, c_ctx, mod_w, mod_b, norms, ret_w_in, ret_w_out, ret_decay, conv_w1, conv_b1, conv_dw,
           conv_b_dw, conv_ln_g, conv_ln_b, conv_w2, conv_b2, moe_router, moe_bias, moe_w1, moe_w3, moe_w2,
           shared_w1, shared_w3, shared_w2):
    b, t, _ = x.shape
    l = ctx.shape[1]
    tm = _tile(t, 512)

    rows = -(-(b + 1) // 8) * 8
    cs = jnp.concatenate([c, c_ctx[None, :], jnp.zeros((rows - b - 1, D), F32)], 0)
    mod = _mod_call(cs, mod_w, mod_b).reshape(mod_w.shape[0], rows, 6, D)

    idx = jnp.arange(t, dtype=jnp.int32)
    pos_lat = jnp.stack([jnp.full((t,), l, jnp.int32), idx // GRID_W, idx % GRID_W], -1)
    zl = jnp.zeros((l,), jnp.int32)
    pos_ctx = jnp.stack([jnp.arange(l, dtype=jnp.int32), zl, zl], -1)
    ang_lat, ang_ctx = _rope_angles(pos_lat), _rope_angles(pos_ctx)

    w_qk = ret_w_in[0][:, :2 * QK_TOT].reshape(D, 2 * HEADS, DK // 2, 2).swapaxes(-1, -2).reshape(D, 2 * QK_TOT)
    w_in = jnp.concatenate([w_qk, ret_w_in[0][:, 2 * QK_TOT:]], axis=1).astype(BF16)
    log_g = jax.nn.log_sigmoid(ret_decay[0].astype(F32))

    n = b * t
    nb_max = _max_blocks(n)

    def moe_layer(i, h1, u2, logits_t):
        code, code_t, segc, segr, pc_t, off_t = _route_call(logits_t, moe_bias[i])
        seg_disp, seg_comb, tail, blocks = _dispatch_tables(pc_t, off_t, nb_max)
        u2f = u2.reshape(n, D)
        xs = _dispatch_call(seg_disp, tail, blocks[1], u2f, code, segr, nb_max)
        ys = _expert_call(i, blocks, xs, moe_w1, moe_w3, moe_w2, nb_max)
        out = _combine_call(seg_comb, code_t, segc, u2f, ys, shared_w1[i].astype(BF16), shared_w3[i].astype(BF16),
                            shared_w2[i].astype(BF16), h1.reshape(n, D), norms[i], mod[i], t)
        return out.reshape(b, t, D)

    q, k, v, gt = _inproj_call(x, norms[0, 0:1], mod[0], w_in, jnp.cos(ang_lat), jnp.sin(ang_lat), tm)
    sf0, sb0 = _ctxstate_call(log_g, ctx, norms[0, 0:1], mod[0, b], w_in[:, QK_TOT:2 * QK_TOT + V_TOT],
                              jnp.cos(ang_ctx), jnp.sin(ang_ctx))
    yg = _retention_call(log_g, q, k, v, gt, sf0, sb0)
    h1, u2, logits_t = _readout_call(yg, x, norms[0], mod[0], ret_w_out[0].astype(BF16),
                                     moe_router[0].T, _tile(t, 2 * EPI_ROWS))
    h2 = moe_layer(0, h1, u2, logits_t)

    glu = _glu_call(h2, norms[1, 0:1], mod[1], conv_w1[0].astype(BF16), conv_b1[0][None, :], tm)
    cvec = jnp.stack([conv_b_dw[0], conv_ln_g[0], conv_ln_b[0], conv_b2[0]], 0)
    dw_rows = jnp.repeat(conv_dw[0], SUBLANES, axis=0)
    h3, u2, logits_t = _conv_call(glu, dw_rows, cvec, conv_w2[0].astype(BF16), h2, norms[1], mod[1],
                                  moe_router[1].T, tm)
    return moe_layer(1, h3, u2, logits_t)
```

```python
import functools

import jax
import jax.numpy as jnp
import numpy as np
from jax import lax
from jax.experimental import pallas as pl
from jax.experimental.pallas import tpu as pltpu

F32 = jnp.float32
BF16 = jnp.bfloat16

D = 1024
HEADS = 4
DK = D // HEADS
DV = 2 * DK
QK_TOT = HEADS * DK
V_TOT = HEADS * DV
IN_DIM = 2 * QK_TOT + 2 * V_TOT
ROPE_AXES = (64, 96, 96)
ROPE_THETA = 10000.0
GRID_W = 64
CONV_W = 31
HALO = 16
SUBLANES = 8
CONV_LANES = 256
CONV_TILES = 4
N_EXP = 64
N_GRP = 8
PER_GRP = N_EXP // N_GRP
TOP_K = 8
TOP_G = 4
F_EXP = 256
ROUTE_SCALE = 2.5
GATE_LANES = 128
MXU_DIM = 256
TM_MOE = 256
ROUTE_TILES = 4
SEG_ALIGN = 8
POS_RADIX = 256
POS_SCALE = 8
BH = 512
DISPATCH_SLOTS = 4
EXP_SUB = 256
PK = D // 2
NORM_EPS = 1e-6
LN_EPS = 1e-5
RET_CHUNK = 256
EPI_ROWS = 512
VMEM_LIMIT = 56 * 1024 * 1024

_HI = lax.Precision.HIGHEST


def _cparams(*sem):
    return pltpu.CompilerParams(dimension_semantics=sem, vmem_limit_bytes=VMEM_LIMIT)


def _dot(a, b):
    return jnp.dot(a, b, preferred_element_type=F32)


def _rms(xf, g):
    return xf * lax.rsqrt(jnp.mean(xf * xf, axis=-1, keepdims=True) + NORM_EPS) * g


def _silu(x):
    return x * jax.nn.sigmoid(x)


def _mod_kernel(cs_ref, w_ref, b_ref, o_ref):
    s = _silu(cs_ref[...])
    o_ref[...] = jnp.dot(s, w_ref[...], preferred_element_type=F32, precision=_HI) + b_ref[...]


def _mod_call(cs, mod_w, mod_b):
    depth, _, n6 = mod_w.shape
    tn = 1536
    rows = cs.shape[0]
    return pl.pallas_call(
        _mod_kernel,
        grid=(depth, n6 // tn),
        in_specs=[
            pl.BlockSpec((rows, D), lambda i, j: (0, 0)),
            pl.BlockSpec((None, D, tn), lambda i, j: (i, 0, j)),
            pl.BlockSpec((None, 1, tn), lambda i, j: (i, 0, j)),
        ],
        out_specs=pl.BlockSpec((None, rows, tn), lambda i, j: (i, 0, j)),
        out_shape=jax.ShapeDtypeStruct((depth, rows, n6), F32),
        compiler_params=_cparams("parallel", "parallel"),
        name="mod",
    )(cs, mod_w, mod_b.reshape(depth, 1, n6))


def _rope_angles(pos):
    parts = []
    for a, d in enumerate(ROPE_AXES):
        inv = ROPE_THETA ** (-jnp.arange(0, d, 2, dtype=F32) / d)
        parts.append(pos[:, a:a + 1].astype(F32) * inv[None, :])
    return jnp.concatenate(parts, -1)


def _rope_store(z, cos, sin, scale, o_ref, col):
    half = DK // 2
    x0, x1 = z[:, :half], z[:, half:]
    o_ref[:, col:col + half] = ((x0 * cos - x1 * sin) * scale).astype(o_ref.dtype)
    o_ref[:, col + half:col + DK] = ((x0 * sin + x1 * cos) * scale).astype(o_ref.dtype)


def _inproj_kernel(x_ref, g_ref, mod_ref, w_ref, cos_ref, sin_ref, q_ref, k_ref, v_ref, gt_ref):
    u = (_rms(x_ref[...], g_ref[...] * (1.0 + mod_ref[1:2, :])) + mod_ref[0:1, :]).astype(BF16)
    cos, sin = cos_ref[...], sin_ref[...]
    step = 512
    for j in range(V_TOT // step):
        c1 = 2 * QK_TOT + V_TOT + j * step
        gt_ref[:, j * step:(j + 1) * step] = _silu(_dot(u, w_ref[:, c1:c1 + step])).astype(BF16)
    for h in range(HEADS):
        _rope_store(_dot(u, w_ref[:, h * DK:(h + 1) * DK]), cos, sin, DK ** -0.5, q_ref, h * DK)
        _rope_store(_dot(u, w_ref[:, QK_TOT + h * DK:QK_TOT + (h + 1) * DK]), cos, sin, 1.0, k_ref, h * DK)
    for j in range(V_TOT // step):
        c0 = 2 * QK_TOT + j * step
        v_ref[:, j * step:(j + 1) * step] = _dot(u, w_ref[:, c0:c0 + step]).astype(BF16)


def _inproj_call(x, g, mod_l, w_in, cos, sin, tm):
    b, t, _ = x.shape
    grid = (b, t // tm)
    tok = lambda n: pl.BlockSpec((None, tm, n), lambda i, j: (i, j, 0))
    return pl.pallas_call(
        _inproj_kernel,
        grid=grid,
        in_specs=[
            tok(D),
            pl.BlockSpec((1, D), lambda i, j: (0, 0)),
            pl.BlockSpec((None, 6, D), lambda i, j: (i, 0, 0)),
            pl.BlockSpec((D, IN_DIM), lambda i, j: (0, 0), pipeline_mode=pl.Buffered(1)),
            pl.BlockSpec((tm, DK // 2), lambda i, j: (j, 0)),
            pl.BlockSpec((tm, DK // 2), lambda i, j: (j, 0)),
        ],
        out_specs=[tok(QK_TOT), tok(QK_TOT), tok(V_TOT), tok(V_TOT)],
        out_shape=[
            jax.ShapeDtypeStruct((b, t, QK_TOT), BF16),
            jax.ShapeDtypeStruct((b, t, QK_TOT), BF16),
            jax.ShapeDtypeStruct((b, t, V_TOT), BF16),
            jax.ShapeDtypeStruct((b, t, V_TOT), BF16),
        ],
        compiler_params=_cparams("parallel", "parallel"),
        name="inproj",
    )(x, g, mod_l, w_in, cos, sin)


def _row_pow(lg, expo_fn, rows, cols):
    i = lax.broadcasted_iota(jnp.int32, (rows, cols), 0).astype(F32)
    return jnp.exp(lg * expo_fn(i))


def _ctxstate_kernel(lg_ref, c_ref, g_ref, mod_ref, w_ref, cos_ref, sin_ref, sf_ref, sb_ref, k_scr):
    l = c_ref.shape[0]
    u = (_rms(c_ref[...], g_ref[...] * (1.0 + mod_ref[1:2, :])) + mod_ref[0:1, :]).astype(BF16)
    cos, sin = cos_ref[...], sin_ref[...]
    for h in range(HEADS):
        _rope_store(_dot(u, w_ref[:, h * DK:(h + 1) * DK]), cos, sin, 1.0, k_scr, h * DK)
    for h in range(HEADS):
        kh = k_scr[:, h * DK:(h + 1) * DK]
        vh = _dot(u, w_ref[:, QK_TOT + h * DV:QK_TOT + (h + 1) * DV]).astype(BF16)
        wf = _row_pow(lg_ref[0, h], lambda i: (l - 1.0) - i, l, DK)
        wb = _row_pow(lg_ref[1, h], lambda i: i, l, DK)
        sf_ref[h] = _dot((kh * wf).T.astype(BF16), vh)
        sb_ref[h] = _dot((kh * wb).T.astype(BF16), vh)


def _ctxstate_call(log_g, ctx, g, mod_row, w_kv, cos, sin):
    b, l, _ = ctx.shape
    st = jax.ShapeDtypeStruct((b, HEADS, DK, DV), F32)
    st_spec = pl.BlockSpec((None, HEADS, DK, DV), lambda i: (i, 0, 0, 0))
    return pl.pallas_call(
        _ctxstate_kernel,
        grid=(b,),
        in_specs=[
            pl.BlockSpec(memory_space=pltpu.SMEM),
            pl.BlockSpec((None, l, D), lambda i: (i, 0, 0)),
            pl.BlockSpec((1, D), lambda i: (0, 0)),
            pl.BlockSpec((6, D), lambda i: (0, 0)),
            pl.BlockSpec((D, QK_TOT + V_TOT), lambda i: (0, 0)),
            pl.BlockSpec((l, DK // 2), lambda i: (0, 0)),
            pl.BlockSpec((l, DK // 2), lambda i: (0, 0)),
        ],
        out_specs=[st_spec, st_spec],
        out_shape=[st, st],
        scratch_shapes=[pltpu.VMEM((l, QK_TOT), F32)],
        compiler_params=_cparams("parallel"),
        name="ctxstate",
    )(log_g, ctx, g, mod_row, w_kv, cos, sin)


def _retention_kernel(lg_ref, q_ref, k_ref, v_ref, gt_ref, sf0_ref, sb0_ref, o_ref, acc, sf, sb):
    h = pl.program_id(1)
    t = q_ref.shape[0]
    c = RET_CHUNK
    nc = t // c
    lgf, lgb = lg_ref[0, h], lg_ref[1, h]
    ri = lax.broadcasted_iota(jnp.int32, (c, c), 0).astype(F32)
    ci = lax.broadcasted_iota(jnp.int32, (c, c), 1).astype(F32)
    rel = ri - ci
    mask = (jnp.where(rel >= 0, jnp.exp(lgf * jnp.maximum(rel, 0.0)), 0.0)
            + jnp.where(rel <= 0, jnp.exp(lgb * jnp.maximum(-rel, 0.0)), 0.0))
    qdf = _row_pow(lgf, lambda i: i + 1.0, c, DV)
    qdb = _row_pow(lgb, lambda i: c - i, c, DV)
    kdf = _row_pow(lgf, lambda i: (c - 1.0) - i, c, DK)
    kdb = _row_pow(lgb, lambda i: i, c, DK)
    cdf = jnp.exp(jnp.full((1, DV), lgf * c, F32))
    cdb = jnp.exp(jnp.full((1, DV), lgb * c, F32))
    sf[...] = sf0_ref[...]
    sb[...] = sb0_ref[...]

    def rows(ic):
        return pl.ds(pl.multiple_of(ic * c, c), c)

    def fwd(rs):
        q, k, v = q_ref[rs, :], k_ref[rs, :], v_ref[rs, :]
        s = lax.dot_general(q, k, (((1,), (1,)), ((), ())), preferred_element_type=F32)
        o = _dot((s * mask).astype(BF16), v) + qdf * _dot(q, sf[...].astype(BF16))
        sf[...] = sf[...] * cdf + _dot((k.astype(F32) * kdf).T.astype(BF16), v)
        return o

    def bwd(rs):
        q, k, v = q_ref[rs, :], k_ref[rs, :], v_ref[rs, :]
        o = qdb * _dot(q, sb[...].astype(BF16))
        sb[...] = sb[...] * cdb + _dot((k.astype(F32) * kdb).T.astype(BF16), v)
        return o

    def finish(rs, o):
        o = o * lax.rsqrt(jnp.mean(o * o, axis=-1, keepdims=True) + NORM_EPS)
        o_ref[rs, :] = (o * gt_ref[rs, :].astype(F32)).astype(BF16)

    def first_half(j, carry):
        of, ob = fwd(rows(j)), bwd(rows(nc - 1 - j))
        acc[rows(j), :] = of
        acc[rows(nc - 1 - j), :] = ob
        return carry

    def second_half(j, carry):
        of, ob = fwd(rows(j)), bwd(rows(nc - 1 - j))
        finish(rows(j), acc[rows(j), :] + of)
        finish(rows(nc - 1 - j), acc[rows(nc - 1 - j), :] + ob)
        return carry

    lax.fori_loop(0, nc // 2, first_half, 0, unroll=True)
    lax.fori_loop(nc // 2, nc, second_half, 0, unroll=True)


def _retention_call(log_g, q, k, v, gt, sf0, sb0):
    b, t, _ = q.shape
    qk_spec = pl.BlockSpec((None, t, DK), lambda i, h: (i, 0, h))
    v_spec = pl.BlockSpec((None, t, DV), lambda i, h: (i, 0, h))
    s_spec = pl.BlockSpec((None, None, DK, DV), lambda i, h: (i, h, 0, 0))
    return pl.pallas_call(
        _retention_kernel,
        grid=(b, HEADS),
        in_specs=[pl.BlockSpec(memory_space=pltpu.SMEM), qk_spec, qk_spec, v_spec, v_spec, s_spec, s_spec],
        out_specs=v_spec,
        out_shape=jax.ShapeDtypeStruct((b, t, V_TOT), BF16),
        scratch_shapes=[pltpu.VMEM((t, DV), F32), pltpu.VMEM((DK, DV), F32), pltpu.VMEM((DK, DV), F32)],
        compiler_params=_cparams("parallel", "parallel"),
        name="retention",
    )(log_g, q, k, v, gt, sf0, sb0)


def _mixer_epilogue(y, rows, h_ref, n_ref, mod_ref, wr_ref, h_out, u_out, lg_out):
    h1 = h_ref[rows, :] + _rms(y, mod_ref[2:3, :] * n_ref[1:2, :])
    h_out[rows, :] = h1
    u2 = _rms(h1, n_ref[2:3, :] * (1.0 + mod_ref[4:5, :])) + mod_ref[3:4, :]
    u_hi = u2.astype(BF16)
    u_out[rows, :] = u_hi
    u_lo = (u2 - u_hi.astype(F32)).astype(BF16)
    w = wr_ref[...]
    w_hi = w.astype(BF16)
    w_lo = (w - w_hi.astype(F32)).astype(BF16)
    nt_dot = lambda a, b: lax.dot_general(a, b, (((1,), (1,)), ((), ())), preferred_element_type=F32)
    lg_out[:, rows] = nt_dot(w_hi, u_hi) + (nt_dot(w_hi, u_lo) + nt_dot(w_lo, u_hi))


def _row_blocks(tm):
    return [slice(r0, r0 + EPI_ROWS) for r0 in range(0, tm, EPI_ROWS)]


def _readout_kernel(y_ref, h_ref, n_ref, mod_ref, w_ref, wr_ref, h_out, u_out, lg_out):
    blocks = _row_blocks(y_ref.shape[0])
    ys = [_dot(y_ref[rows, :], w_ref[...]) for rows in blocks]
    for rows, y in zip(blocks, ys):
        _mixer_epilogue(y, rows, h_ref, n_ref, mod_ref, wr_ref, h_out, u_out, lg_out)


def _epilogue_specs(b, t, tm):
    nt = t // tm
    tok = lambda n: pl.BlockSpec((None, tm, n), lambda i, j: (i, j, 0))
    in_specs = [
        tok(D),
        pl.BlockSpec((4, D), lambda i, j: (0, 0)),
        pl.BlockSpec((None, 6, D), lambda i, j: (i, 0, 0)),
    ]
    wr_spec = pl.BlockSpec((N_EXP, D), lambda i, j: (0, 0))
    out_specs = [tok(D), tok(D), pl.BlockSpec((N_EXP, tm), lambda i, j: (0, i * nt + j))]
    out_shape = [
        jax.ShapeDtypeStruct((b, t, D), F32),
        jax.ShapeDtypeStruct((b, t, D), BF16),
        jax.ShapeDtypeStruct((N_EXP, b * t), F32),
    ]
    return tok, in_specs, wr_spec, out_specs, out_shape


def _readout_call(yg, h, norms_l, mod_l, w_out, wr_t, tm):
    b, t, _ = h.shape
    tok, ep_in, wr_spec, out_specs, out_shape = _epilogue_specs(b, t, tm)
    return pl.pallas_call(
        _readout_kernel,
        grid=(b, t // tm),
        in_specs=[tok(V_TOT)] + ep_in
        + [pl.BlockSpec((V_TOT, D), lambda i, j: (0, 0), pipeline_mode=pl.Buffered(1)), wr_spec],
        out_specs=out_specs,
        out_shape=out_shape,
        compiler_params=_cparams("parallel", "parallel"),
        name="readout",
    )(yg, h, norms_l, mod_l, w_out, wr_t)


def _route_tile(logits, bias):
    tr = logits.shape[1]
    s = jax.nn.sigmoid(logits)
    biased = s + bias
    neg = -jnp.inf
    b3 = biased.reshape(N_GRP, PER_GRP, tr)
    io3 = lax.broadcasted_iota(jnp.int32, b3.shape, 1).astype(F32)
    m1 = jnp.max(b3, axis=1, keepdims=True)
    i1 = jnp.min(jnp.where(b3 == m1, io3, float(PER_GRP)), axis=1, keepdims=True)
    m2 = jnp.max(jnp.where(io3 == i1, neg, b3), axis=1, keepdims=True)
    gs = (m1 + m2).reshape(N_GRP, tr)
    iog = lax.broadcasted_iota(jnp.int32, gs.shape, 0).astype(F32)
    gsel = jnp.zeros(gs.shape, F32)
    work = gs
    for _ in range(TOP_G):
        m = jnp.max(work, axis=0, keepdims=True)
        gi = jnp.min(jnp.where(work == m, iog, float(N_GRP)), axis=0, keepdims=True)
        hit = iog == gi
        gsel = jnp.where(hit, 1.0, gsel)
        work = jnp.where(hit, neg, work)
    emask = jnp.broadcast_to(gsel.reshape(N_GRP, 1, tr), b3.shape).reshape(N_EXP, tr)
    work = jnp.where(emask > 0.0, biased, neg)
    ioe = lax.broadcasted_iota(jnp.int32, work.shape, 0).astype(F32)
    esel = jnp.zeros(work.shape, F32)
    for _ in range(TOP_K):
        m = jnp.max(work, axis=0, keepdims=True)
        ei = jnp.min(jnp.where(work == m, ioe, float(N_EXP)), axis=0, keepdims=True)
        hit = ioe == ei
        esel = jnp.where(hit, 1.0, esel)
        work = jnp.where(hit, neg, work)
    wsel = esel * s
    gate = wsel / jnp.sum(wsel, axis=0, keepdims=True) * ROUTE_SCALE

    t_r = lax.broadcasted_iota(jnp.int32, (tr, tr), 0)
    t_c = lax.broadcasted_iota(jnp.int32, (tr, tr), 1)
    rank = _dot(esel.astype(BF16), jnp.where(t_r < t_c, 1.0, 0.0).astype(BF16))
    cnt = jnp.sum(esel, axis=1, keepdims=True)
    pc = jnp.floor((cnt + (SEG_ALIGN - 1.0)) * (1.0 / SEG_ALIGN)) * SEG_ALIGN
    e_r = lax.broadcasted_iota(jnp.int32, (N_EXP, N_EXP), 0)
    e_c = lax.broadcasted_iota(jnp.int32, (N_EXP, N_EXP), 1)
    off = jnp.dot(jnp.where(e_c < e_r, 1.0, 0.0), jnp.broadcast_to(pc, (N_EXP, 128)),
                  preferred_element_type=F32, precision=_HI)[:, 0:1]
    p = off + rank
    p_hi = jnp.where(esel > 0.0, jnp.floor(p * (1.0 / POS_RADIX)) * POS_RADIX, -float(POS_RADIX))
    p_lo = jnp.where(esel > 0.0, p - p_hi, 0.0)
    minus_r = jnp.where(lax.broadcasted_iota(jnp.int32, (N_EXP, tr), 0) < 2, -float(POS_SCALE), 0.0)
    code = jnp.concatenate([p_hi * POS_SCALE, p_lo * POS_SCALE, gate, minus_r], 0)
    code_t = jnp.concatenate([code[:2 * N_EXP].T, code[2 * N_EXP:].T], 1)
    lane = lax.broadcasted_iota(jnp.int32, (2 * N_EXP, GATE_LANES), 1)
    off2 = jnp.concatenate([off, off], 0)
    end2 = jnp.concatenate([off + pc, off + pc], 0)
    bounds = jnp.where(lane == 0, off2, jnp.where(lane == 1, end2, 0.0))
    return code.astype(BF16), code_t.astype(BF16), bounds, bounds.T[0:SUBLANES, :], pc, off


def _route_kernel(lg_ref, b_ref, code_ref, code_t_ref, segc_ref, segr_ref, pc_ref, off_ref):
    i = pl.program_id(0)

    @pl.when(i == 0)
    def _():
        pc_ref[...] = jnp.zeros(pc_ref.shape, F32)
        off_ref[...] = jnp.zeros(off_ref.shape, F32)

    tile_lane = lax.broadcasted_iota(jnp.int32, pc_ref.shape, 1)
    for k in range(ROUTE_TILES):
        cols = slice(k * TM_MOE, (k + 1) * TM_MOE)
        code, code_t, bounds, bounds_t, pc, off = _route_tile(lg_ref[:, cols], b_ref[...])
        code_ref[:, cols] = code
        code_t_ref[cols, :] = code_t
        segc_ref[k] = bounds
        segr_ref[k] = bounds_t
        pc_ref[...] = jnp.where(tile_lane == i * ROUTE_TILES + k, pc, pc_ref[...])
        off_ref[...] = jnp.where(tile_lane == i * ROUTE_TILES + k, off, off_ref[...])


def _route_call(logits_t, bias):
    n = logits_t.shape[1]
    nt = n // TM_MOE
    tr = ROUTE_TILES * TM_MOE
    return pl.pallas_call(
        _route_kernel,
        grid=(nt // ROUTE_TILES,),
        in_specs=[pl.BlockSpec((N_EXP, tr), lambda i: (0, i)), pl.BlockSpec((N_EXP, 1), lambda i: (0, 0))],
        out_specs=[
            pl.BlockSpec((4 * N_EXP, tr), lambda i: (0, i)),
            pl.BlockSpec((tr, 4 * N_EXP), lambda i: (i, 0)),
            pl.BlockSpec((ROUTE_TILES, 2 * N_EXP, GATE_LANES), lambda i: (i, 0, 0)),
            pl.BlockSpec((ROUTE_TILES, SUBLANES, GATE_LANES), lambda i: (i, 0, 0)),
            pl.BlockSpec((N_EXP, nt), lambda i: (0, 0)),
            pl.BlockSpec((N_EXP, nt), lambda i: (0, 0)),
        ],
        out_shape=[
            jax.ShapeDtypeStruct((4 * N_EXP, n), BF16),
            jax.ShapeDtypeStruct((n, 4 * N_EXP), BF16),
            jax.ShapeDtypeStruct((nt, 2 * N_EXP, GATE_LANES), F32),
            jax.ShapeDtypeStruct((nt, SUBLANES, GATE_LANES), F32),
            jax.ShapeDtypeStruct((N_EXP, nt), F32),
            jax.ShapeDtypeStruct((N_EXP, nt), F32),
        ],
        compiler_params=_cparams("arbitrary"),
        name="route",
    )(logits_t, bias.reshape(N_EXP, 1))


def _dispatch_tables(pc_t, off_t, nb_max):
    nt = pc_t.shape[1]
    ids = jnp.arange(N_EXP, dtype=jnp.int32)
    psum = lambda a, m: jnp.dot(a, m, precision=_HI, preferred_element_type=F32)
    before_tile = (jnp.arange(nt)[:, None] < jnp.arange(nt)[None, :]).astype(F32)
    upto_expert = (ids[:, None] <= ids[None, :]).astype(F32)
    used_f = pc_t.sum(1)
    region_f = jnp.ceil(used_f * (1.0 / BH)) * BH
    ends_f = psum(region_f[None, :], upto_expert)[0]
    pc = pc_t.astype(jnp.int32)
    used, region, ends = used_f.astype(jnp.int32), region_f.astype(jnp.int32), ends_f.astype(jnp.int32)
    starts = ends - region
    dst = starts[:, None] + psum(pc_t, before_tile).astype(jnp.int32)
    nblk = ends[-1] // BH
    blk = jnp.minimum(jnp.arange(nb_max, dtype=jnp.int32), nblk - 1)
    blk_e = jnp.minimum(jnp.sum(ends[None, :] <= blk[:, None] * BH, axis=1), N_EXP - 1).astype(jnp.int32)
    has = region > 0
    slot_e = (psum(has.astype(F32)[None, :], upto_expert)[0].astype(jnp.int32) - 1) % 2
    later = has[None, :] & (ids[None, :] > ids[:, None])
    next_e = jnp.where(later.any(1), jnp.argmax(later, axis=1), -1)
    of_blk = lambda per_e: jnp.sum(jnp.where(blk_e[:, None] == ids[None, :], per_e[None, :], 0), axis=1)
    blocks = (blk_e, nblk.reshape(1).astype(jnp.int32), of_blk(slot_e).astype(jnp.int32),
              of_blk(next_e).astype(jnp.int32))
    flat = lambda a: a.reshape(-1).astype(jnp.int32)
    tail = ((starts + used).astype(jnp.int32), (region - used).astype(jnp.int32))
    off = off_t.astype(jnp.int32)
    e_idx = jnp.arange(N_EXP, dtype=jnp.int32)[:, None]
    col = lambda v: jnp.broadcast_to(jnp.asarray(v, jnp.int32), (N_EXP, 1))
    table = lambda so, do, ln: (flat(so), flat(do), flat(ln), ln.sum(0).astype(jnp.int32))
    pc_c = jnp.concatenate([pc, col(0)], 1)
    spare_c = _sorted_rows(TM_MOE) + SEG_ALIGN * e_idx
    seg_comb = table(jnp.where(pc_c > 0, jnp.concatenate([off, col(0)], 1), spare_c),
                     jnp.where(pc_c > 0, jnp.concatenate([dst, col(0)], 1), 0), jnp.maximum(pc_c, SEG_ALIGN))
    pc_d = jnp.concatenate([col(0), pc], 1)
    entry = jnp.arange(nt + 1, dtype=jnp.int32)[None, :]
    spare_d = nb_max * BH + ((entry % DISPATCH_SLOTS) * N_EXP + e_idx) * SEG_ALIGN
    seg_disp = table(jnp.where(pc_d > 0, jnp.concatenate([col(0), off], 1), 0),
                     jnp.where(pc_d > 0, jnp.concatenate([col(0), dst], 1), spare_d), jnp.maximum(pc_d, SEG_ALIGN))
    return seg_disp, seg_comb, tail, blocks


def _swiglu(u, w1, w3, w2):
    return _dot((_silu(_dot(u, w1)) * _dot(u, w3)).astype(BF16), w2)


def _round_bf16(x):
    return x.astype(BF16).astype(F32)


def _pack_pair(lo, hi):
    lo_b = lax.bitcast_convert_type(lo, jnp.uint32)
    hi_b = lax.bitcast_convert_type(hi, jnp.uint32)
    return (hi_b & jnp.uint32(0xFFFF0000)) | (lo_b >> 16)


def _unpack_pair(w):
    lo = lax.bitcast_convert_type(w << 16, F32).astype(BF16)
    hi = lax.bitcast_convert_type(w & jnp.uint32(0xFFFF0000), F32).astype(BF16)
    return lo, hi


def _segment_copies_dense(so_ref, do_ref, ln_ref, entry, entries, make_copy):
    for e in range(N_EXP):
        j = e * entries + entry
        make_copy(pl.multiple_of(so_ref[j], SEG_ALIGN), pl.multiple_of(do_ref[j], SEG_ALIGN),
                  pl.multiple_of(ln_ref[j], SEG_ALIGN)).start()


def _dispatch_kernel(nt, so_ref, do_ref, ln_ref, tt_ref, td_ref, tl_ref, nb_ref, u_ref, code_ref, segr_ref,
                     digits_ref, xs_out, buf, zbuf, sem):
    i = pl.program_id(0)
    r = buf.shape[1]
    nb_max = (xs_out.shape[0] - DISPATCH_SLOTS * N_EXP * SEG_ALIGN) // BH
    fill_sem = DISPATCH_SLOTS
    slot = i % DISPATCH_SLOTS
    prev_slot = (i + DISPATCH_SLOTS - 1) % DISPATCH_SLOTS

    def strip(slot_, so, do, ln):
        return pltpu.make_async_copy(buf.at[slot_, pl.ds(so, ln)], xs_out.at[pl.ds(do, ln)], sem.at[slot_])

    def wait_entry(slot_, entry):
        strip(slot_, 0, 0, pl.multiple_of(tt_ref[entry], SEG_ALIGN)).wait()

    def zero_fills(act):
        def tail(e, carry):
            ln = pl.multiple_of(tl_ref[e], SEG_ALIGN)

            @pl.when(ln > 0)
            def _():
                act(pltpu.make_async_copy(zbuf.at[pl.ds(0, ln)],
                                          xs_out.at[pl.ds(pl.multiple_of(td_ref[e], SEG_ALIGN), ln)], sem.at[fill_sem]))

            return carry

        lax.fori_loop(0, N_EXP, tail, 0)

        def blk(j, carry):
            act(pltpu.make_async_copy(zbuf.at[pl.ds(0, BH)], xs_out.at[pl.ds(pl.multiple_of(j * BH, BH), BH)],
                                      sem.at[fill_sem]))
            return carry

        lax.fori_loop(nb_ref[0], nb_max, blk, 0)

    @pl.when(i == 0)
    def _():
        zbuf[...] = jnp.zeros(zbuf.shape, jnp.uint32)
        buf[...] = jnp.zeros(buf.shape, jnp.uint32)
        spare = pltpu.make_async_copy(zbuf.at[pl.ds(0, DISPATCH_SLOTS * N_EXP * SEG_ALIGN)],
                                      xs_out.at[pl.ds(nb_max * BH, DISPATCH_SLOTS * N_EXP * SEG_ALIGN)],
                                      sem.at[fill_sem])
        spare.start()
        spare.wait()
        zero_fills(lambda cp: cp.start())

    @pl.when(i >= DISPATCH_SLOTS - 1)
    def _():
        wait_entry(slot, i - (DISPATCH_SLOTS - 1))

    row = lax.broadcasted_iota(jnp.int32, (r, GATE_LANES), 0).astype(F32)
    own = jnp.where(row >= segr_ref[0:1, :], jnp.where(row < segr_ref[1:2, :], 1.0, 0.0), 0.0).astype(BF16)
    z = _dot(jnp.concatenate([own, digits_ref[...]], axis=1), code_ref[...])
    perm = jnp.where(z == 0.0, 1.0, 0.0).astype(BF16)
    step = 256
    for c in range(0, PK, step):
        buf[slot, :, c:c + step] = _pack_pair(_dot(perm, u_ref[:, c:c + step]),
                                              _dot(perm, u_ref[:, PK + c:PK + c + step]))
    _segment_copies_dense(so_ref, do_ref, ln_ref, i, nt + 1, functools.partial(strip, prev_slot))

    @pl.when(i == nt - 1)
    def _():
        _segment_copies_dense(so_ref, do_ref, ln_ref, nt, nt + 1, functools.partial(strip, slot))
        for entry in range(max(nt - DISPATCH_SLOTS + 1, 0), nt + 1):
            wait_entry((entry + DISPATCH_SLOTS - 1) % DISPATCH_SLOTS, entry)
        zero_fills(lambda cp: cp.wait())


def _expert_kernel(layer, be_ref, nb_ref, ws_ref, nx_ref, x_ref, w1_hbm, w3_hbm, w2_hbm, y_ref,
                   w1f, w3f, w2f, w1b, w3b, w2b, sem):
    i = pl.program_id(0)
    first, second = 2 * i, 2 * i + 1
    live = first < nb_ref[0]
    split = (second < nb_ref[0]) & (be_ref[second] != be_ref[first])

    def fetch(e, slot):
        return [pltpu.make_async_copy(src.at[layer, e], dst.at[slot], sem.at[slot, j])
                for j, (src, dst) in enumerate(((w1_hbm, w1f), (w3_hbm, w3f), (w2_hbm, w2f)))]

    def begin_expert(blk):
        slot = ws_ref[blk]

        @pl.when(blk == 0)
        def _():
            for cp in fetch(be_ref[0], slot):
                cp.start()

        for cp in fetch(be_ref[blk], slot):
            cp.wait()
        w1b[...] = w1f[slot].astype(BF16)
        w3b[...] = w3f[slot].astype(BF16)
        w2b[...] = w2f[slot].astype(BF16)

        @pl.when(nx_ref[blk] >= 0)
        def _():
            for cp in fetch(nx_ref[blk], 1 - slot):
                cp.start()

    def compute(row0, nrows):
        up = []
        for r0 in range(row0, row0 + nrows, EXP_SUB):
            lo, hi = _unpack_pair(x_ref[r0:r0 + EXP_SUB, :])
            up.append((_dot(lo, w1b[:PK, :]) + _dot(hi, w1b[PK:, :]), _dot(lo, w3b[:PK, :]) + _dot(hi, w3b[PK:, :])))
        for j, (a, g) in enumerate(up):
            r0 = row0 + j * EXP_SUB
            hid = (_silu(a) * g).astype(BF16)
            y_ref[r0:r0 + EXP_SUB, :] = _pack_pair(_round_bf16(_dot(hid, w2b[:, :PK])),
                                                   _round_bf16(_dot(hid, w2b[:, PK:])))

    @pl.when(live & ((i == 0) | (be_ref[first] != be_ref[jnp.maximum(first - 1, 0)])))
    def _():
        begin_expert(first)

    @pl.when(live & jnp.logical_not(split))
    def _():
        compute(0, 2 * BH)

    @pl.when(split)
    def _():
        compute(0, BH)
        begin_expert(second)
        compute(BH, BH)


def _combine_kernel(nt, so_ref, do_ref, ln_ref, tt_ref, code_t_ref, segc_ref, digits_ref, u_ref, ys_ref,
                    s1_ref, s3_ref, s2_ref, h_ref, n_ref, mod_ref, o_ref, buf, sem):
    i = pl.program_id(0)
    slot = i % 2
    r = digits_ref.shape[1]

    def strip(slot_, so, do, ln):
        return pltpu.make_async_copy(ys_ref.at[pl.ds(do, ln)], buf.at[slot_, pl.ds(so, ln)], sem.at[slot_])

    @pl.when(i == 0)
    def _():
        buf[...] = jnp.zeros(buf.shape, jnp.uint32)
        _segment_copies_dense(so_ref, do_ref, ln_ref, 0, nt + 1, functools.partial(strip, 0))

    col = lax.broadcasted_iota(jnp.int32, (2 * N_EXP, r), 1).astype(F32)
    own = jnp.where(col >= segc_ref[:, 0:1], jnp.where(col < segc_ref[:, 1:2], 1.0, 0.0), 0.0).astype(BF16)
    z = _dot(code_t_ref[...], jnp.concatenate([own, own[:N_EXP], digits_ref[...]], axis=0))
    q = jnp.where(jnp.abs(z) < 0.5 * POS_SCALE, z, 0.0).astype(BF16)
    shared = _swiglu(u_ref[...], s1_ref[...], s3_ref[...], s2_ref[...])
    _segment_copies_dense(so_ref, do_ref, ln_ref, i + 1, nt + 1, functools.partial(strip, 1 - slot))

    strip(slot, 0, 0, pl.multiple_of(tt_ref[i], SEG_ALIGN)).wait()
    lo, hi = _unpack_pair(buf[slot, 0:r, :])
    moe = jnp.concatenate([_dot(q, lo), _dot(q, hi)], axis=1) + shared
    o_ref[...] = h_ref[...] + mod_ref[5:6, :] * _rms(moe, n_ref[3:4, :])

    @pl.when(i == nt - 1)
    def _():
        strip(1 - slot, 0, 0, pl.multiple_of(tt_ref[nt], SEG_ALIGN)).wait()


def _sorted_rows(tm):
    return -(-(TOP_K * tm + N_EXP * (SEG_ALIGN - 1)) // MXU_DIM) * MXU_DIM


def _max_blocks(n):
    nt = n // TM_MOE
    rows = TOP_K * n + N_EXP * (SEG_ALIGN - 1) * nt + N_EXP * (BH - SEG_ALIGN)
    return -(-rows // (2 * BH)) * 2


def _row_digits(r, width, first):
    idx = np.arange(r)
    digits = np.zeros((r, width), np.float32)
    digits[:, first] = idx // POS_RADIX * POS_RADIX
    digits[:, first + 1] = idx % POS_RADIX
    return jnp.asarray(digits, BF16)


def _dispatch_call(seg, tail, nblk, u2, code, segr, nb_max):
    n = u2.shape[0]
    r = _sorted_rows(TM_MOE)
    spare = DISPATCH_SLOTS * N_EXP * SEG_ALIGN
    return pl.pallas_call(
        functools.partial(_dispatch_kernel, n // TM_MOE),
        grid_spec=pltpu.PrefetchScalarGridSpec(
            num_scalar_prefetch=7,
            grid=(n // TM_MOE,),
            in_specs=[
                pl.BlockSpec((TM_MOE, D), lambda i, *_: (i, 0)),
                pl.BlockSpec((4 * N_EXP, TM_MOE), lambda i, *_: (0, i)),
                pl.BlockSpec((None, SUBLANES, GATE_LANES), lambda i, *_: (i, 0, 0)),
                pl.BlockSpec((r, GATE_LANES), lambda i, *_: (0, 0)),
            ],
            out_specs=pl.BlockSpec(memory_space=pl.ANY),
            scratch_shapes=[pltpu.VMEM((DISPATCH_SLOTS, r, PK), jnp.uint32),
                            pltpu.VMEM((max(BH, spare), PK), jnp.uint32),
                            pltpu.SemaphoreType.DMA((DISPATCH_SLOTS + 1,))],
        ),
        out_shape=jax.ShapeDtypeStruct((nb_max * BH + spare, PK), jnp.uint32),
        compiler_params=_cparams("arbitrary"),
        name="dispatch",
    )(*seg, tail[0], tail[1], nblk, u2, code, segr, _row_digits(r, GATE_LANES, N_EXP))


def _expert_call(layer, blk, xs, w1, w3, w2, nb_max):
    row_blk = lambda i, be, nb, *_: (jnp.maximum(jnp.minimum(i, (nb[0] - 1) // 2), 0), 0)
    hbm = pl.BlockSpec(memory_space=pl.ANY)
    return pl.pallas_call(
        functools.partial(_expert_kernel, layer),
        grid_spec=pltpu.PrefetchScalarGridSpec(
            num_scalar_prefetch=4,
            grid=(nb_max // 2,),
            in_specs=[pl.BlockSpec((2 * BH, PK), row_blk), hbm, hbm, hbm],
            out_specs=pl.BlockSpec((2 * BH, PK), row_blk),
            scratch_shapes=[pltpu.VMEM((2, D, F_EXP), F32), pltpu.VMEM((2, D, F_EXP), F32),
                            pltpu.VMEM((2, F_EXP, D), F32),
                            pltpu.VMEM((D, F_EXP), BF16), pltpu.VMEM((D, F_EXP), BF16),
                            pltpu.VMEM((F_EXP, D), BF16), pltpu.SemaphoreType.DMA((2, 3))],
        ),
        out_shape=jax.ShapeDtypeStruct(xs.shape, jnp.uint32),
        input_output_aliases={4: 0},
        compiler_params=_cparams("arbitrary"),
        name="experts",
    )(*blk, xs, w1, w3, w2)


def _combine_call(tabs, code_t, segc, u2, ys, s1, s3, s2, h, norms_l, mod_l, t):
    n = u2.shape[0]
    r = _sorted_rows(TM_MOE)
    tok = lambda w: pl.BlockSpec((TM_MOE, w), lambda i, *_: (i, 0))
    const = lambda shp: pl.BlockSpec(shp, lambda i, *_: (0,) * len(shp))
    return pl.pallas_call(
        functools.partial(_combine_kernel, n // TM_MOE),
        grid_spec=pltpu.PrefetchScalarGridSpec(
            num_scalar_prefetch=4,
            grid=(n // TM_MOE,),
            in_specs=[
                tok(4 * N_EXP),
                pl.BlockSpec((None, 2 * N_EXP, GATE_LANES), lambda i, *_: (i, 0, 0)),
                const((N_EXP, r)),
                tok(D),
                pl.BlockSpec(memory_space=pl.ANY),
                const((D, F_EXP)), const((D, F_EXP)), const((F_EXP, D)),
                tok(D),
                const((4, D)),
                pl.BlockSpec((None, 6, D), lambda i, *_: (i * TM_MOE // t, 0, 0)),
            ],
            out_specs=tok(D),
            scratch_shapes=[pltpu.VMEM((2, r + N_EXP * SEG_ALIGN, PK), jnp.uint32),
                            pltpu.SemaphoreType.DMA((2,))],
        ),
        out_shape=jax.ShapeDtypeStruct((n, D), F32),
        compiler_params=_cparams("arbitrary"),
        name="combine",
    )(*tabs, code_t, segc, _row_digits(r, N_EXP, 0).T, u2, ys, s1, s3, s2, h, norms_l, mod_l)


def _glu_kernel(h_ref, g_ref, mod_ref, w_ref, b_ref, o_ref):
    u = (_rms(h_ref[...], g_ref[...] * (1.0 + mod_ref[1:2, :])) + mod_ref[0:1, :]).astype(BF16)
    step = 256
    for c in range(0, D, step):
        a = _dot(u, w_ref[:, c:c + step]) + b_ref[:, c:c + step]
        gate = _dot(u, w_ref[:, D + c:D + c + step]) + b_ref[:, D + c:D + c + step]
        o_ref[:, c:c + step] = (a * jax.nn.sigmoid(gate)).astype(BF16)


def _glu_call(h, g, mod_l, w1, b1, tm):
    b, t, _ = h.shape
    tok = lambda n: pl.BlockSpec((None, tm, n), lambda i, j: (i, j, 0))
    return pl.pallas_call(
        _glu_kernel,
        grid=(b, t // tm),
        in_specs=[
            tok(D),
            pl.BlockSpec((1, D), lambda i, j: (0, 0)),
            pl.BlockSpec((None, 6, D), lambda i, j: (i, 0, 0)),
            pl.BlockSpec((D, 2 * D), lambda i, j: (0, 0)),
            pl.BlockSpec((1, 2 * D), lambda i, j: (0, 0)),
        ],
        out_specs=tok(D),
        out_shape=jax.ShapeDtypeStruct((b, t, D), BF16),
        compiler_params=_cparams("parallel", "parallel"),
        name="glu",
    )(h, g, mod_l, w1, b1)


def _conv_kernel(cur_ref, prev_ref, next_ref, dw_ref, cv_ref, w2_ref, h_ref, n_ref, mod_ref, wr_ref,
                 h_out, u_out, lg_out, win, shifted, conv):
    j = pl.program_id(1)
    tm = cur_ref.shape[0]
    pad = CONV_W // 2
    win[0:HALO, :] = jnp.where(j > 0, prev_ref[...].astype(F32), 0.0)
    win[HALO:HALO + tm, :] = cur_ref[...].astype(F32)
    win[HALO + tm:, :] = jnp.where(j < pl.num_programs(1) - 1, next_ref[...].astype(F32), 0.0)
    ext = shifted.shape[1]
    full = win[...]
    shifted[0] = full[:ext, :]
    for s in range(1, SUBLANES):
        shifted[s] = pltpu.roll(full, full.shape[0] - s, axis=0)[:ext, :]

    def rows(ib, carry):
        r0 = pl.multiple_of(ib * CONV_TILES * SUBLANES, CONV_TILES * SUBLANES)
        for c0 in range(0, D, CONV_LANES):
            cols = slice(c0, c0 + CONV_LANES)
            accs = [jnp.zeros((SUBLANES, CONV_LANES), F32) for _ in range(CONV_TILES)]
            for s in range(SUBLANES):
                taps = [(off // SUBLANES, off - (HALO - pad)) for off in range(HALO - pad, HALO - pad + CONV_W)
                        if off % SUBLANES == s]
                tiles = {a: shifted[s, pl.ds(r0 + a * SUBLANES, SUBLANES), cols]
                         for a in range(taps[0][0], taps[-1][0] + CONV_TILES)}
                for a, tap in taps:
                    w = dw_ref[tap * SUBLANES:(tap + 1) * SUBLANES, cols]
                    accs = [acc + tiles[a + m] * w for m, acc in enumerate(accs)]
            for m, acc in enumerate(accs):
                conv[pl.ds(r0 + m * SUBLANES, SUBLANES), cols] = acc
        return carry

    lax.fori_loop(0, tm // (CONV_TILES * SUBLANES), rows, 0)
    for rows in _row_blocks(tm):
        uf = conv[rows, :] + cv_ref[0:1, :]
        mu = jnp.mean(uf, axis=-1, keepdims=True)
        var = jnp.mean(jnp.square(uf - mu), axis=-1, keepdims=True)
        uf = (uf - mu) * lax.rsqrt(var + LN_EPS) * cv_ref[1:2, :] + cv_ref[2:3, :]
        y = _dot(_silu(uf).astype(BF16), w2_ref[...]) + cv_ref[3:4, :]
        _mixer_epilogue(y, rows, h_ref, n_ref, mod_ref, wr_ref, h_out, u_out, lg_out)


def _conv_call(glu, dw, cvec, w2, h, norms_l, mod_l, wr_t, tm):
    b, t, _ = h.shape
    tok, ep_in, wr_spec, out_specs, out_shape = _epilogue_specs(b, t, tm)
    r = tm // HALO
    last = t // HALO - 1
    return pl.pallas_call(
        _conv_kernel,
        grid=(b, t // tm),
        in_specs=[
            tok(D),
            pl.BlockSpec((None, HALO, D), lambda i, j: (i, jnp.maximum(j * r - 1, 0), 0)),
            pl.BlockSpec((None, HALO, D), lambda i, j: (i, jnp.minimum((j + 1) * r, last), 0)),
            pl.BlockSpec((CONV_W * SUBLANES, D), lambda i, j: (0, 0)),
            pl.BlockSpec((4, D), lambda i, j: (0, 0)),
            pl.BlockSpec((D, D), lambda i, j: (0, 0)),
        ] + ep_in + [wr_spec],
        out_specs=out_specs,
        out_shape=out_shape,
        scratch_shapes=[pltpu.VMEM((tm + 2 * HALO, D), F32),
                        pltpu.VMEM((SUBLANES, tm + (CONV_W // SUBLANES) * SUBLANES, D), F32),
                        pltpu.VMEM((tm, D), F32)],
        compiler_params=_cparams("parallel", "parallel"),
        name="conv",
    )(glu, glu, glu, dw, cvec, w2, h, norms_l, mod_l, wr_t)


def _tile(t, pref):
    return pref if t % pref == 0 else t


def kernel(x, c, ctx, c_ctx, mod_w, mod_b, norms, ret_w_in, ret_w_out, ret_decay, conv_w1, conv_b1, conv_dw,
           conv_b_dw, conv_ln_g, conv_ln_b, conv_w2, conv_b2, moe_router, moe_bias, moe_w1, moe_w3, moe_w2,
           shared_w1, shared_w3, shared_w2):
    b, t, _ = x.shape
    l = ctx.shape[1]
    tm = _tile(t, 512)

    rows = -(-(b + 1) // 8) * 8
    cs = jnp.concatenate([c, c_ctx[None, :], jnp.zeros((rows - b - 1, D), F32)], 0)
    mod = _mod_call(cs, mod_w, mod_b).reshape(mod_w.shape[0], rows, 6, D)

    idx = jnp.arange(t, dtype=jnp.int32)
    pos_lat = jnp.stack([jnp.full((t,), l, jnp.int32), idx // GRID_W, idx % GRID_W], -1)
    zl = jnp.zeros((l,), jnp.int32)
    pos_ctx = jnp.stack([jnp.arange(l, dtype=jnp.int32), zl, zl], -1)
    ang_lat, ang_ctx = _rope_angles(pos_lat), _rope_angles(pos_ctx)

    w_qk = ret_w_in[0][:, :2 * QK_TOT].reshape(D, 2 * HEADS, DK // 2, 2).swapaxes(-1, -2).reshape(D, 2 * QK_TOT)
    w_in = jnp.concatenate([w_qk, ret_w_in[0][:, 2 * QK_TOT:]], axis=1).astype(BF16)
    log_g = jax.nn.log_sigmoid(ret_decay[0].astype(F32))

    n = b * t
    nb_max = _max_blocks(n)

    def moe_layer(i, h1, u2, logits_t):
        code, code_t, segc, segr, pc_t, off_t = _route_call(logits_t, moe_bias[i])
        seg_disp, seg_comb, tail, blocks = _dispatch_tables(pc_t, off_t, nb_max)
        u2f = u2.reshape(n, D)
        xs = _dispatch_call(seg_disp, tail, blocks[1], u2f, code, segr, nb_max)
        ys = _expert_call(i, blocks, xs, moe_w1, moe_w3, moe_w2, nb_max)
        out = _combine_call(seg_comb, code_t, segc, u2f, ys, shared_w1[i].astype(BF16), shared_w3[i].astype(BF16),
                            shared_w2[i].astype(BF16), h1.reshape(n, D), norms[i], mod[i], t)
        return out.reshape(b, t, D)

    q, k, v, gt = _inproj_call(x, norms[0, 0:1], mod[0], w_in, jnp.cos(ang_lat), jnp.sin(ang_lat), tm)
    sf0, sb0 = _ctxstate_call(log_g, ctx, norms[0, 0:1], mod[0, b], w_in[:, QK_TOT:2 * QK_TOT + V_TOT],
                              jnp.cos(ang_ctx), jnp.sin(ang_ctx))
    yg = _retention_call(log_g, q, k, v, gt, sf0, sb0)
    h1, u2, logits_t = _readout_call(yg, x, norms[0], mod[0], ret_w_out[0].astype(BF16),
                                     moe_router[0].T, _tile(t, 2 * EPI_ROWS))
    h2 = moe_layer(0, h1, u2, logits_t)

    glu = _glu_call(h2, norms[1, 0:1], mod[1], conv_w1[0].astype(BF16), conv_b1[0][None, :], tm)
    cvec = jnp.stack([conv_b_dw[0], conv_ln_g[0], conv_ln_b[0], conv_b2[0]], 0)
    dw_rows = jnp.repeat(conv_dw[0], SUBLANES, axis=0)
    h3, u2, logits_t = _conv_call(glu, dw_rows, cvec, conv_w2[0].astype(BF16), h2, norms[1], mod[1],
                                  moe_router[1].T, tm)
    return moe_layer(1, h3, u2, logits_t)
```

```python
import functools

import jax
import jax.numpy as jnp
import numpy as np
from jax import lax
from jax.experimental import pallas as pl
from jax.experimental.pallas import tpu as pltpu

F32 = jnp.float32
BF16 = jnp.bfloat16

D = 1024
HEADS = 4
DK = D // HEADS
DV = 2 * DK
QK_TOT = HEADS * DK
V_TOT = HEADS * DV
IN_DIM = 2 * QK_TOT + 2 * V_TOT
ROPE_AXES = (64, 96, 96)
ROPE_THETA = 10000.0
GRID_W = 64
CONV_W = 31
HALO = 16
SUBLANES = 8
CONV_LANES = 256
CONV_TILES = 8
N_EXP = 64
N_GRP = 8
PER_GRP = N_EXP // N_GRP
TOP_K = 8
TOP_G = 4
F_EXP = 256
ROUTE_SCALE = 2.5
GATE_LANES = 128
MXU_DIM = 256
TM_MOE = 256
ROUTE_TILES = 4
SEG_ALIGN = 8
POS_RADIX = 256
POS_SCALE = 8
BH = 512
DISPATCH_SLOTS = 4
EXP_SUB = 256
PK = D // 2
NORM_EPS = 1e-6
LN_EPS = 1e-5
RET_CHUNK = 256
EPI_ROWS = 512
VMEM_LIMIT = 56 * 1024 * 1024

_HI = lax.Precision.HIGHEST


def _cparams(*sem):
    return pltpu.CompilerParams(dimension_semantics=sem, vmem_limit_bytes=VMEM_LIMIT)


def _dot(a, b):
    return jnp.dot(a, b, preferred_element_type=F32)


def _rms(xf, g):
    return xf * lax.rsqrt(jnp.mean(xf * xf, axis=-1, keepdims=True) + NORM_EPS) * g


def _silu(x):
    return x * jax.nn.sigmoid(x)


def _mod_kernel(cs_ref, w_ref, b_ref, o_ref):
    s = _silu(cs_ref[...])
    o_ref[...] = jnp.dot(s, w_ref[...], preferred_element_type=F32, precision=_HI) + b_ref[...]


def _mod_call(cs, mod_w, mod_b):
    depth, _, n6 = mod_w.shape
    tn = 1536
    rows = cs.shape[0]
    return pl.pallas_call(
        _mod_kernel,
        grid=(depth, n6 // tn),
        in_specs=[
            pl.BlockSpec((rows, D), lambda i, j: (0, 0)),
            pl.BlockSpec((None, D, tn), lambda i, j: (i, 0, j)),
            pl.BlockSpec((None, 1, tn), lambda i, j: (i, 0, j)),
        ],
        out_specs=pl.BlockSpec((None, rows, tn), lambda i, j: (i, 0, j)),
        out_shape=jax.ShapeDtypeStruct((depth, rows, n6), F32),
        compiler_params=_cparams("parallel", "parallel"),
        name="mod",
    )(cs, mod_w, mod_b.reshape(depth, 1, n6))


def _rope_angles(pos):
    parts = []
    for a, d in enumerate(ROPE_AXES):
        inv = ROPE_THETA ** (-jnp.arange(0, d, 2, dtype=F32) / d)
        parts.append(pos[:, a:a + 1].astype(F32) * inv[None, :])
    return jnp.concatenate(parts, -1)


def _rope_store(z, cos, sin, scale, o_ref, col):
    half = DK // 2
    x0, x1 = z[:, :half], z[:, half:]
    o_ref[:, col:col + half] = ((x0 * cos - x1 * sin) * scale).astype(o_ref.dtype)
    o_ref[:, col + half:col + DK] = ((x0 * sin + x1 * cos) * scale).astype(o_ref.dtype)


def _inproj_kernel(x_ref, g_ref, mod_ref, w_ref, cos_ref, sin_ref, q_ref, k_ref, v_ref, gt_ref):
    u = (_rms(x_ref[...], g_ref[...] * (1.0 + mod_ref[1:2, :])) + mod_ref[0:1, :]).astype(BF16)
    cos, sin = cos_ref[...], sin_ref[...]
    step = 512
    for j in range(V_TOT // step):
        c1 = 2 * QK_TOT + V_TOT + j * step
        gt_ref[:, j * step:(j + 1) * step] = _silu(_dot(u, w_ref[:, c1:c1 + step])).astype(BF16)
    for h in range(HEADS):
        _rope_store(_dot(u, w_ref[:, h * DK:(h + 1) * DK]), cos, sin, DK ** -0.5, q_ref, h * DK)
        _rope_store(_dot(u, w_ref[:, QK_TOT + h * DK:QK_TOT + (h + 1) * DK]), cos, sin, 1.0, k_ref, h * DK)
    for j in range(V_TOT // step):
        c0 = 2 * QK_TOT + j * step
        v_ref[:, j * step:(j + 1) * step] = _dot(u, w_ref[:, c0:c0 + step]).astype(BF16)


def _inproj_call(x, g, mod_l, w_in, cos, sin, tm):
    b, t, _ = x.shape
    grid = (b, t // tm)
    tok = lambda n: pl.BlockSpec((None, tm, n), lambda i, j: (i, j, 0))
    return pl.pallas_call(
        _inproj_kernel,
        grid=grid,
        in_specs=[
            tok(D),
            pl.BlockSpec((1, D), lambda i, j: (0, 0)),
            pl.BlockSpec((None, 6, D), lambda i, j: (i, 0, 0)),
            pl.BlockSpec((D, IN_DIM), lambda i, j: (0, 0), pipeline_mode=pl.Buffered(1)),
            pl.BlockSpec((tm, DK // 2), lambda i, j: (j, 0)),
            pl.BlockSpec((tm, DK // 2), lambda i, j: (j, 0)),
        ],
        out_specs=[tok(QK_TOT), tok(QK_TOT), tok(V_TOT), tok(V_TOT)],
        out_shape=[
            jax.ShapeDtypeStruct((b, t, QK_TOT), BF16),
            jax.ShapeDtypeStruct((b, t, QK_TOT), BF16),
            jax.ShapeDtypeStruct((b, t, V_TOT), BF16),
            jax.ShapeDtypeStruct((b, t, V_TOT), BF16),
        ],
        compiler_params=_cparams("parallel", "parallel"),
        name="inproj",
    )(x, g, mod_l, w_in, cos, sin)


def _row_pow(lg, expo_fn, rows, cols):
    i = lax.broadcasted_iota(jnp.int32, (rows, cols), 0).astype(F32)
    return jnp.exp(lg * expo_fn(i))


def _ctxstate_kernel(lg_ref, c_ref, g_ref, mod_ref, w_ref, cos_ref, sin_ref, sf_ref, sb_ref, k_scr):
    l = c_ref.shape[0]
    u = (_rms(c_ref[...], g_ref[...] * (1.0 + mod_ref[1:2, :])) + mod_ref[0:1, :]).astype(BF16)
    cos, sin = cos_ref[...], sin_ref[...]
    for h in range(HEADS):
        _rope_store(_dot(u, w_ref[:, h * DK:(h + 1) * DK]), cos, sin, 1.0, k_scr, h * DK)
    for h in range(HEADS):
        kh = k_scr[:, h * DK:(h + 1) * DK]
        vh = _dot(u, w_ref[:, QK_TOT + h * DV:QK_TOT + (h + 1) * DV]).astype(BF16)
        wf = _row_pow(lg_ref[0, h], lambda i: (l - 1.0) - i, l, DK)
        wb = _row_pow(lg_ref[1, h], lambda i: i, l, DK)
        sf_ref[h] = _dot((kh * wf).T.astype(BF16), vh)
        sb_ref[h] = _dot((kh * wb).T.astype(BF16), vh)


def _ctxstate_call(log_g, ctx, g, mod_row, w_kv, cos, sin):
    b, l, _ = ctx.shape
    st = jax.ShapeDtypeStruct((b, HEADS, DK, DV), F32)
    st_spec = pl.BlockSpec((None, HEADS, DK, DV), lambda i: (i, 0, 0, 0))
    return pl.pallas_call(
        _ctxstate_kernel,
        grid=(b,),
        in_specs=[
            pl.BlockSpec(memory_space=pltpu.SMEM),
            pl.BlockSpec((None, l, D), lambda i: (i, 0, 0)),
            pl.BlockSpec((1, D), lambda i: (0, 0)),
            pl.BlockSpec((6, D), lambda i: (0, 0)),
            pl.BlockSpec((D, QK_TOT + V_TOT), lambda i: (0, 0)),
            pl.BlockSpec((l, DK // 2), lambda i: (0, 0)),
            pl.BlockSpec((l, DK // 2), lambda i: (0, 0)),
        ],
        out_specs=[st_spec, st_spec],
        out_shape=[st, st],
        scratch_shapes=[pltpu.VMEM((l, QK_TOT), F32)],
        compiler_params=_cparams("parallel"),
        name="ctxstate",
    )(log_g, ctx, g, mod_row, w_kv, cos, sin)


def _retention_kernel(lg_ref, q_ref, k_ref, v_ref, gt_ref, sf0_ref, sb0_ref, o_ref, acc, sf, sb):
    h = pl.program_id(1)
    t = q_ref.shape[0]
    c = RET_CHUNK
    nc = t // c
    lgf, lgb = lg_ref[0, h], lg_ref[1, h]
    ri = lax.broadcasted_iota(jnp.int32, (c, c), 0).astype(F32)
    ci = lax.broadcasted_iota(jnp.int32, (c, c), 1).astype(F32)
    rel = ri - ci
    mask = (jnp.where(rel >= 0, jnp.exp(lgf * jnp.maximum(rel, 0.0)), 0.0)
            + jnp.where(rel <= 0, jnp.exp(lgb * jnp.maximum(-rel, 0.0)), 0.0))
    qdf = _row_pow(lgf, lambda i: i + 1.0, c, DV)
    qdb = _row_pow(lgb, lambda i: c - i, c, DV)
    kdf = _row_pow(lgf, lambda i: (c - 1.0) - i, c, DK)
    kdb = _row_pow(lgb, lambda i: i, c, DK)
    cdf = jnp.exp(jnp.full((1, DV), lgf * c, F32))
    cdb = jnp.exp(jnp.full((1, DV), lgb * c, F32))
    sf[...] = sf0_ref[...]
    sb[...] = sb0_ref[...]

    def rows(ic):
        return pl.ds(pl.multiple_of(ic * c, c), c)

    def fwd(rs):
        q, k, v = q_ref[rs, :], k_ref[rs, :], v_ref[rs, :]
        s = lax.dot_general(q, k, (((1,), (1,)), ((), ())), preferred_element_type=F32)
        o = _dot((s * mask).astype(BF16), v) + qdf * _dot(q, sf[...].astype(BF16))
        sf[...] = sf[...] * cdf + _dot((k.astype(F32) * kdf).T.astype(BF16), v)
        return o

    def bwd(rs):
        q, k, v = q_ref[rs, :], k_ref[rs, :], v_ref[rs, :]
        o = qdb * _dot(q, sb[...].astype(BF16))
        sb[...] = sb[...] * cdb + _dot((k.astype(F32) * kdb).T.astype(BF16), v)
        return o

    def finish(rs, o):
        o = o * lax.rsqrt(jnp.mean(o * o, axis=-1, keepdims=True) + NORM_EPS)
        o_ref[rs, :] = (o * gt_ref[rs, :].astype(F32)).astype(BF16)

    def first_half(j, carry):
        of, ob = fwd(rows(j)), bwd(rows(nc - 1 - j))
        acc[rows(j), :] = of
        acc[rows(nc - 1 - j), :] = ob
        return carry

    def second_half(j, carry):
        of, ob = fwd(rows(j)), bwd(rows(nc - 1 - j))
        finish(rows(j), acc[rows(j), :] + of)
        finish(rows(nc - 1 - j), acc[rows(nc - 1 - j), :] + ob)
        return carry

    lax.fori_loop(0, nc // 2, first_half, 0, unroll=True)
    lax.fori_loop(nc // 2, nc, second_half, 0, unroll=True)


def _retention_call(log_g, q, k, v, gt, sf0, sb0):
    b, t, _ = q.shape
    qk_spec = pl.BlockSpec((None, t, DK), lambda i, h: (i, 0, h))
    v_spec = pl.BlockSpec((None, t, DV), lambda i, h: (i, 0, h))
    s_spec = pl.BlockSpec((None, None, DK, DV), lambda i, h: (i, h, 0, 0))
    return pl.pallas_call(
        _retention_kernel,
        grid=(b, HEADS),
        in_specs=[pl.BlockSpec(memory_space=pltpu.SMEM), qk_spec, qk_spec, v_spec, v_spec, s_spec, s_spec],
        out_specs=v_spec,
        out_shape=jax.ShapeDtypeStruct((b, t, V_TOT), BF16),
        scratch_shapes=[pltpu.VMEM((t, DV), F32), pltpu.VMEM((DK, DV), F32), pltpu.VMEM((DK, DV), F32)],
        compiler_params=_cparams("parallel", "parallel"),
        name="retention",
    )(log_g, q, k, v, gt, sf0, sb0)


def _mixer_epilogue(y, rows, h_ref, n_ref, mod_ref, wr_ref, h_out, u_out, lg_out):
    h1 = h_ref[rows, :] + _rms(y, mod_ref[2:3, :] * n_ref[1:2, :])
    h_out[rows, :] = h1
    u2 = _rms(h1, n_ref[2:3, :] * (1.0 + mod_ref[4:5, :])) + mod_ref[3:4, :]
    u_hi = u2.astype(BF16)
    u_out[rows, :] = u_hi
    u_lo = (u2 - u_hi.astype(F32)).astype(BF16)
    w = wr_ref[...]
    w_hi = w.astype(BF16)
    w_lo = (w - w_hi.astype(F32)).astype(BF16)
    nt_dot = lambda a, b: lax.dot_general(a, b, (((1,), (1,)), ((), ())), preferred_element_type=F32)
    lg_out[:, rows] = nt_dot(w_hi, u_hi) + (nt_dot(w_hi, u_lo) + nt_dot(w_lo, u_hi))


def _row_blocks(tm):
    return [slice(r0, r0 + EPI_ROWS) for r0 in range(0, tm, EPI_ROWS)]


def _readout_kernel(y_ref, h_ref, n_ref, mod_ref, w_ref, wr_ref, h_out, u_out, lg_out):
    blocks = _row_blocks(y_ref.shape[0])
    ys = [_dot(y_ref[rows, :], w_ref[...]) for rows in blocks]
    for rows, y in zip(blocks, ys):
        _mixer_epilogue(y, rows, h_ref, n_ref, mod_ref, wr_ref, h_out, u_out, lg_out)


def _epilogue_specs(b, t, tm):
    nt = t // tm
    tok = lambda n: pl.BlockSpec((None, tm, n), lambda i, j: (i, j, 0))
    in_specs = [
        tok(D),
        pl.BlockSpec((4, D), lambda i, j: (0, 0)),
        pl.BlockSpec((None, 6, D), lambda i, j: (i, 0, 0)),
    ]
    wr_spec = pl.BlockSpec((N_EXP, D), lambda i, j: (0, 0))
    out_specs = [tok(D), tok(D), pl.BlockSpec((N_EXP, tm), lambda i, j: (0, i * nt + j))]
    out_shape = [
        jax.ShapeDtypeStruct((b, t, D), F32),
        jax.ShapeDtypeStruct((b, t, D), BF16),
        jax.ShapeDtypeStruct((N_EXP, b * t), F32),
    ]
    return tok, in_specs, wr_spec, out_specs, out_shape


def _readout_call(yg, h, norms_l, mod_l, w_out, wr_t, tm):
    b, t, _ = h.shape
    tok, ep_in, wr_spec, out_specs, out_shape = _epilogue_specs(b, t, tm)
    return pl.pallas_call(
        _readout_kernel,
        grid=(b, t // tm),
        in_specs=[tok(V_TOT)] + ep_in
        + [pl.BlockSpec((V_TOT, D), lambda i, j: (0, 0), pipeline_mode=pl.Buffered(1)), wr_spec],
        out_specs=out_specs,
        out_shape=out_shape,
        compiler_params=_cparams("parallel", "parallel"),
        name="readout",
    )(yg, h, norms_l, mod_l, w_out, wr_t)


def _route_tile(logits, bias):
    tr = logits.shape[1]
    s = jax.nn.sigmoid(logits)
    biased = s + bias
    neg = -jnp.inf
    b3 = biased.reshape(N_GRP, PER_GRP, tr)
    io3 = lax.broadcasted_iota(jnp.int32, b3.shape, 1).astype(F32)
    m1 = jnp.max(b3, axis=1, keepdims=True)
    i1 = jnp.min(jnp.where(b3 == m1, io3, float(PER_GRP)), axis=1, keepdims=True)
    m2 = jnp.max(jnp.where(io3 == i1, neg, b3), axis=1, keepdims=True)
    gs = (m1 + m2).reshape(N_GRP, tr)
    iog = lax.broadcasted_iota(jnp.int32, gs.shape, 0).astype(F32)
    gsel = jnp.zeros(gs.shape, F32)
    work = gs
    for _ in range(TOP_G):
        m = jnp.max(work, axis=0, keepdims=True)
        gi = jnp.min(jnp.where(work == m, iog, float(N_GRP)), axis=0, keepdims=True)
        hit = iog == gi
        gsel = jnp.where(hit, 1.0, gsel)
        work = jnp.where(hit, neg, work)
    emask = jnp.broadcast_to(gsel.reshape(N_GRP, 1, tr), b3.shape).reshape(N_EXP, tr)
    work = jnp.where(emask > 0.0, biased, neg)
    ioe = lax.broadcasted_iota(jnp.int32, work.shape, 0).astype(F32)
    esel = jnp.zeros(work.shape, F32)
    for _ in range(TOP_K):
        m = jnp.max(work, axis=0, keepdims=True)
        ei = jnp.min(jnp.where(work == m, ioe, float(N_EXP)), axis=0, keepdims=True)
        hit = ioe == ei
        esel = jnp.where(hit, 1.0, esel)
        work = jnp.where(hit, neg, work)
    wsel = esel * s
    gate = wsel / jnp.sum(wsel, axis=0, keepdims=True) * ROUTE_SCALE

    t_r = lax.broadcasted_iota(jnp.int32, (tr, tr), 0)
    t_c = lax.broadcasted_iota(jnp.int32, (tr, tr), 1)
    rank = _dot(esel.astype(BF16), jnp.where(t_r < t_c, 1.0, 0.0).astype(BF16))
    cnt = jnp.sum(esel, axis=1, keepdims=True)
    pc = jnp.floor((cnt + (SEG_ALIGN - 1.0)) * (1.0 / SEG_ALIGN)) * SEG_ALIGN
    e_r = lax.broadcasted_iota(jnp.int32, (N_EXP, N_EXP), 0)
    e_c = lax.broadcasted_iota(jnp.int32, (N_EXP, N_EXP), 1)
    off = jnp.dot(jnp.where(e_c < e_r, 1.0, 0.0), jnp.broadcast_to(pc, (N_EXP, 128)),
                  preferred_element_type=F32, precision=_HI)[:, 0:1]
    p = off + rank
    p_hi = jnp.where(esel > 0.0, jnp.floor(p * (1.0 / POS_RADIX)) * POS_RADIX, -float(POS_RADIX))
    p_lo = jnp.where(esel > 0.0, p - p_hi, 0.0)
    minus_r = jnp.where(lax.broadcasted_iota(jnp.int32, (N_EXP, tr), 0) < 2, -float(POS_SCALE), 0.0)
    code = jnp.concatenate([p_hi * POS_SCALE, p_lo * POS_SCALE, gate, minus_r], 0)
    code_t = jnp.concatenate([code[:2 * N_EXP].T, code[2 * N_EXP:].T], 1)
    lane = lax.broadcasted_iota(jnp.int32, (2 * N_EXP, GATE_LANES), 1)
    off2 = jnp.concatenate([off, off], 0)
    end2 = jnp.concatenate([off + pc, off + pc], 0)
    bounds = jnp.where(lane == 0, off2, jnp.where(lane == 1, end2, 0.0))
    return code.astype(BF16), code_t.astype(BF16), bounds, bounds.T[0:SUBLANES, :], pc, off


def _route_kernel(lg_ref, b_ref, code_ref, code_t_ref, segc_ref, segr_ref, pc_ref, off_ref):
    i = pl.program_id(0)

    @pl.when(i == 0)
    def _():
        pc_ref[...] = jnp.zeros(pc_ref.shape, F32)
        off_ref[...] = jnp.zeros(off_ref.shape, F32)

    tile_lane = lax.broadcasted_iota(jnp.int32, pc_ref.shape, 1)
    for k in range(ROUTE_TILES):
        cols = slice(k * TM_MOE, (k + 1) * TM_MOE)
        code, code_t, bounds, bounds_t, pc, off = _route_tile(lg_ref[:, cols], b_ref[...])
        code_ref[:, cols] = code
        code_t_ref[cols, :] = code_t
        segc_ref[k] = bounds
        segr_ref[k] = bounds_t
        pc_ref[...] = jnp.where(tile_lane == i * ROUTE_TILES + k, pc, pc_ref[...])
        off_ref[...] = jnp.where(tile_lane == i * ROUTE_TILES + k, off, off_ref[...])


def _route_call(logits_t, bias):
    n = logits_t.shape[1]
    nt = n // TM_MOE
    tr = ROUTE_TILES * TM_MOE
    return pl.pallas_call(
        _route_kernel,
        grid=(nt // ROUTE_TILES,),
        in_specs=[pl.BlockSpec((N_EXP, tr), lambda i: (0, i)), pl.BlockSpec((N_EXP, 1), lambda i: (0, 0))],
        out_specs=[
            pl.BlockSpec((4 * N_EXP, tr), lambda i: (0, i)),
            pl.BlockSpec((tr, 4 * N_EXP), lambda i: (i, 0)),
            pl.BlockSpec((ROUTE_TILES, 2 * N_EXP, GATE_LANES), lambda i: (i, 0, 0)),
            pl.BlockSpec((ROUTE_TILES, SUBLANES, GATE_LANES), lambda i: (i, 0, 0)),
            pl.BlockSpec((N_EXP, nt), lambda i: (0, 0)),
            pl.BlockSpec((N_EXP, nt), lambda i: (0, 0)),
        ],
        out_shape=[
            jax.ShapeDtypeStruct((4 * N_EXP, n), BF16),
            jax.ShapeDtypeStruct((n, 4 * N_EXP), BF16),
            jax.ShapeDtypeStruct((nt, 2 * N_EXP, GATE_LANES), F32),
            jax.ShapeDtypeStruct((nt, SUBLANES, GATE_LANES), F32),
            jax.ShapeDtypeStruct((N_EXP, nt), F32),
            jax.ShapeDtypeStruct((N_EXP, nt), F32),
        ],
        compiler_params=_cparams("arbitrary"),
        name="route",
    )(logits_t, bias.reshape(N_EXP, 1))


def _dispatch_tables(pc_t, off_t, nb_max):
    nt = pc_t.shape[1]
    ids = jnp.arange(N_EXP, dtype=jnp.int32)
    psum = lambda a, m: jnp.dot(a, m, precision=_HI, preferred_element_type=F32)
    before_tile = (jnp.arange(nt)[:, None] < jnp.arange(nt)[None, :]).astype(F32)
    upto_expert = (ids[:, None] <= ids[None, :]).astype(F32)
    used_f = pc_t.sum(1)
    region_f = jnp.ceil(used_f * (1.0 / BH)) * BH
    ends_f = psum(region_f[None, :], upto_expert)[0]
    pc = pc_t.astype(jnp.int32)
    used, region, ends = used_f.astype(jnp.int32), region_f.astype(jnp.int32), ends_f.astype(jnp.int32)
    starts = ends - region
    dst = starts[:, None] + psum(pc_t, before_tile).astype(jnp.int32)
    nblk = ends[-1] // BH
    blk = jnp.minimum(jnp.arange(nb_max, dtype=jnp.int32), nblk - 1)
    blk_e = jnp.minimum(jnp.sum(ends[None, :] <= blk[:, None] * BH, axis=1), N_EXP - 1).astype(jnp.int32)
    has = region > 0
    slot_e = (psum(has.astype(F32)[None, :], upto_expert)[0].astype(jnp.int32) - 1) % 2
    later = has[None, :] & (ids[None, :] > ids[:, None])
    next_e = jnp.where(later.any(1), jnp.argmax(later, axis=1), -1)
    of_blk = lambda per_e: jnp.sum(jnp.where(blk_e[:, None] == ids[None, :], per_e[None, :], 0), axis=1)
    blocks = (blk_e, nblk.reshape(1).astype(jnp.int32), of_blk(slot_e).astype(jnp.int32),
              of_blk(next_e).astype(jnp.int32))
    flat = lambda a: a.reshape(-1).astype(jnp.int32)
    tail = ((starts + used).astype(jnp.int32), (region - used).astype(jnp.int32))
    off = off_t.astype(jnp.int32)
    e_idx = jnp.arange(N_EXP, dtype=jnp.int32)[:, None]
    col = lambda v: jnp.broadcast_to(jnp.asarray(v, jnp.int32), (N_EXP, 1))
    table = lambda so, do, ln: (flat(so), flat(do), flat(ln), ln.sum(0).astype(jnp.int32))
    pc_c = jnp.concatenate([pc, col(0)], 1)
    spare_c = _sorted_rows(TM_MOE) + SEG_ALIGN * e_idx
    seg_comb = table(jnp.where(pc_c > 0, jnp.concatenate([off, col(0)], 1), spare_c),
                     jnp.where(pc_c > 0, jnp.concatenate([dst, col(0)], 1), 0), jnp.maximum(pc_c, SEG_ALIGN))
    pc_d = jnp.concatenate([col(0), pc], 1)
    entry = jnp.arange(nt + 1, dtype=jnp.int32)[None, :]
    spare_d = nb_max * BH + ((entry % DISPATCH_SLOTS) * N_EXP + e_idx) * SEG_ALIGN
    seg_disp = table(jnp.where(pc_d > 0, jnp.concatenate([col(0), off], 1), 0),
                     jnp.where(pc_d > 0, jnp.concatenate([col(0), dst], 1), spare_d), jnp.maximum(pc_d, SEG_ALIGN))
    return seg_disp, seg_comb, tail, blocks


def _swiglu(u, w1, w3, w2):
    return _dot((_silu(_dot(u, w1)) * _dot(u, w3)).astype(BF16), w2)


def _round_bf16(x):
    return x.astype(BF16).astype(F32)


def _pack_pair(lo, hi):
    lo_b = lax.bitcast_convert_type(lo, jnp.uint32)
    hi_b = lax.bitcast_convert_type(hi, jnp.uint32)
    return (hi_b & jnp.uint32(0xFFFF0000)) | (lo_b >> 16)


def _unpack_pair(w):
    lo = lax.bitcast_convert_type(w << 16, F32).astype(BF16)
    hi = lax.bitcast_convert_type(w & jnp.uint32(0xFFFF0000), F32).astype(BF16)
    return lo, hi


def _segment_copies_dense(so_ref, do_ref, ln_ref, entry, entries, make_copy):
    for e in range(N_EXP):
        j = e * entries + entry
        make_copy(pl.multiple_of(so_ref[j], SEG_ALIGN), pl.multiple_of(do_ref[j], SEG_ALIGN),
                  pl.multiple_of(ln_ref[j], SEG_ALIGN)).start()


def _dispatch_kernel(nt, so_ref, do_ref, ln_ref, tt_ref, td_ref, tl_ref, nb_ref, u_ref, code_ref, segr_ref,
                     digits_ref, xs_out, buf, zbuf, sem):
    i = pl.program_id(0)
    r = buf.shape[1]
    nb_max = (xs_out.shape[0] - DISPATCH_SLOTS * N_EXP * SEG_ALIGN) // BH
    fill_sem = DISPATCH_SLOTS
    slot = i % DISPATCH_SLOTS
    prev_slot = (i + DISPATCH_SLOTS - 1) % DISPATCH_SLOTS

    def strip(slot_, so, do, ln):
        return pltpu.make_async_copy(buf.at[slot_, pl.ds(so, ln)], xs_out.at[pl.ds(do, ln)], sem.at[slot_])

    def wait_entry(slot_, entry):
        strip(slot_, 0, 0, pl.multiple_of(tt_ref[entry], SEG_ALIGN)).wait()

    def zero_fills(act):
        def tail(e, carry):
            ln = pl.multiple_of(tl_ref[e], SEG_ALIGN)

            @pl.when(ln > 0)
            def _():
                act(pltpu.make_async_copy(zbuf.at[pl.ds(0, ln)],
                                          xs_out.at[pl.ds(pl.multiple_of(td_ref[e], SEG_ALIGN), ln)], sem.at[fill_sem]))

            return carry

        lax.fori_loop(0, N_EXP, tail, 0)

        def blk(j, carry):
            act(pltpu.make_async_copy(zbuf.at[pl.ds(0, BH)], xs_out.at[pl.ds(pl.multiple_of(j * BH, BH), BH)],
                                      sem.at[fill_sem]))
            return carry

        lax.fori_loop(nb_ref[0], nb_max, blk, 0)

    @pl.when(i == 0)
    def _():
        zbuf[...] = jnp.zeros(zbuf.shape, jnp.uint32)
        buf[...] = jnp.zeros(buf.shape, jnp.uint32)
        spare = pltpu.make_async_copy(zbuf.at[pl.ds(0, DISPATCH_SLOTS * N_EXP * SEG_ALIGN)],
                                      xs_out.at[pl.ds(nb_max * BH, DISPATCH_SLOTS * N_EXP * SEG_ALIGN)],
                                      sem.at[fill_sem])
        spare.start()
        spare.wait()
        zero_fills(lambda cp: cp.start())

    @pl.when(i >= DISPATCH_SLOTS - 1)
    def _():
        wait_entry(slot, i - (DISPATCH_SLOTS - 1))

    row = lax.broadcasted_iota(jnp.int32, (r, GATE_LANES), 0).astype(F32)
    own = jnp.where(row >= segr_ref[0:1, :], jnp.where(row < segr_ref[1:2, :], 1.0, 0.0), 0.0).astype(BF16)
    z = _dot(jnp.concatenate([own, digits_ref[...]], axis=1), code_ref[...])
    perm = jnp.where(z == 0.0, 1.0, 0.0).astype(BF16)
    step = 256
    for c in range(0, PK, step):
        buf[slot, :, c:c + step] = _pack_pair(_dot(perm, u_ref[:, c:c + step]),
                                              _dot(perm, u_ref[:, PK + c:PK + c + step]))
    _segment_copies_dense(so_ref, do_ref, ln_ref, i, nt + 1, functools.partial(strip, prev_slot))

    @pl.when(i == nt - 1)
    def _():
        _segment_copies_dense(so_ref, do_ref, ln_ref, nt, nt + 1, functools.partial(strip, slot))
        for entry in range(max(nt - DISPATCH_SLOTS + 1, 0), nt + 1):
            wait_entry((entry + DISPATCH_SLOTS - 1) % DISPATCH_SLOTS, entry)
        zero_fills(lambda cp: cp.wait())


def _expert_kernel(layer, be_ref, nb_ref, ws_ref, nx_ref, x_ref, w1_hbm, w3_hbm, w2_hbm, y_ref,
                   w1f, w3f, w2f, w1b, w3b, w2b, sem):
    i = pl.program_id(0)
    first, second = 2 * i, 2 * i + 1
    live = first < nb_ref[0]
    split = (second < nb_ref[0]) & (be_ref[second] != be_ref[first])

    def fetch(e, slot):
        return [pltpu.make_async_copy(src.at[layer, e], dst.at[slot], sem.at[slot, j])
                for j, (src, dst) in enumerate(((w1_hbm, w1f), (w3_hbm, w3f), (w2_hbm, w2f)))]

    def begin_expert(blk):
        slot = ws_ref[blk]

        @pl.when(blk == 0)
        def _():
            for cp in fetch(be_ref[0], slot):
                cp.start()

        for cp in fetch(be_ref[blk], slot):
            cp.wait()
        w1b[...] = w1f[slot].astype(BF16)
        w3b[...] = w3f[slot].astype(BF16)
        w2b[...] = w2f[slot].astype(BF16)

        @pl.when(nx_ref[blk] >= 0)
        def _():
            for cp in fetch(nx_ref[blk], 1 - slot):
                cp.start()

    def compute(row0, nrows):
        up = []
        for r0 in range(row0, row0 + nrows, EXP_SUB):
            lo, hi = _unpack_pair(x_ref[r0:r0 + EXP_SUB, :])
            up.append((_dot(lo, w1b[:PK, :]) + _dot(hi, w1b[PK:, :]), _dot(lo, w3b[:PK, :]) + _dot(hi, w3b[PK:, :])))
        for j, (a, g) in enumerate(up):
            r0 = row0 + j * EXP_SUB
            hid = (_silu(a) * g).astype(BF16)
            y_ref[r0:r0 + EXP_SUB, :] = _pack_pair(_round_bf16(_dot(hid, w2b[:, :PK])),
                                                   _round_bf16(_dot(hid, w2b[:, PK:])))

    @pl.when(live & ((i == 0) | (be_ref[first] != be_ref[jnp.maximum(first - 1, 0)])))
    def _():
        begin_expert(first)

    @pl.when(live & jnp.logical_not(split))
    def _():
        compute(0, 2 * BH)

    @pl.when(split)
    def _():
        compute(0, BH)
        begin_expert(second)
        compute(BH, BH)


def _combine_kernel(nt, so_ref, do_ref, ln_ref, tt_ref, code_t_ref, segc_ref, digits_ref, u_ref, ys_ref,
                    s1_ref, s3_ref, s2_ref, h_ref, n_ref, mod_ref, o_ref, buf, sem):
    i = pl.program_id(0)
    slot = i % 2
    r = digits_ref.shape[1]

    def strip(slot_, so, do, ln):
        return pltpu.make_async_copy(ys_ref.at[pl.ds(do, ln)], buf.at[slot_, pl.ds(so, ln)], sem.at[slot_])

    @pl.when(i == 0)
    def _():
        buf[...] = jnp.zeros(buf.shape, jnp.uint32)
        _segment_copies_dense(so_ref, do_ref, ln_ref, 0, nt + 1, functools.partial(strip, 0))

    col = lax.broadcasted_iota(jnp.int32, (2 * N_EXP, r), 1).astype(F32)
    own = jnp.where(col >= segc_ref[:, 0:1], jnp.where(col < segc_ref[:, 1:2], 1.0, 0.0), 0.0).astype(BF16)
    z = _dot(code_t_ref[...], jnp.concatenate([own, own[:N_EXP], digits_ref[...]], axis=0))
    q = jnp.where(jnp.abs(z) < 0.5 * POS_SCALE, z, 0.0).astype(BF16)
    shared = _swiglu(u_ref[...], s1_ref[...], s3_ref[...], s2_ref[...])
    _segment_copies_dense(so_ref, do_ref, ln_ref, i + 1, nt + 1, functools.partial(strip, 1 - slot))

    strip(slot, 0, 0, pl.multiple_of(tt_ref[i], SEG_ALIGN)).wait()
    lo, hi = _unpack_pair(buf[slot, 0:r, :])
    moe = jnp.concatenate([_dot(q, lo), _dot(q, hi)], axis=1) + shared
    o_ref[...] = h_ref[...] + mod_ref[5:6, :] * _rms(moe, n_ref[3:4, :])

    @pl.when(i == nt - 1)
    def _():
        strip(1 - slot, 0, 0, pl.multiple_of(tt_ref[nt], SEG_ALIGN)).wait()


def _sorted_rows(tm):
    return -(-(TOP_K * tm + N_EXP * (SEG_ALIGN - 1)) // MXU_DIM) * MXU_DIM


def _max_blocks(n):
    nt = n // TM_MOE
    rows = TOP_K * n + N_EXP * (SEG_ALIGN - 1) * nt + N_EXP * (BH - SEG_ALIGN)
    return -(-rows // (2 * BH)) * 2


def _row_digits(r, width, first):
    idx = np.arange(r)
    digits = np.zeros((r, width), np.float32)
    digits[:, first] = idx // POS_RADIX * POS_RADIX
    digits[:, first + 1] = idx % POS_RADIX
    return jnp.asarray(digits, BF16)


def _dispatch_call(seg, tail, nblk, u2, code, segr, nb_max):
    n = u2.shape[0]
    r = _sorted_rows(TM_MOE)
    spare = DISPATCH_SLOTS * N_EXP * SEG_ALIGN
    return pl.pallas_call(
        functools.partial(_dispatch_kernel, n // TM_MOE),
        grid_spec=pltpu.PrefetchScalarGridSpec(
            num_scalar_prefetch=7,
            grid=(n // TM_MOE,),
            in_specs=[
                pl.BlockSpec((TM_MOE, D), lambda i, *_: (i, 0)),
                pl.BlockSpec((4 * N_EXP, TM_MOE), lambda i, *_: (0, i)),
                pl.BlockSpec((None, SUBLANES, GATE_LANES), lambda i, *_: (i, 0, 0)),
                pl.BlockSpec((r, GATE_LANES), lambda i, *_: (0, 0)),
            ],
            out_specs=pl.BlockSpec(memory_space=pl.ANY),
            scratch_shapes=[pltpu.VMEM((DISPATCH_SLOTS, r, PK), jnp.uint32),
                            pltpu.VMEM((max(BH, spare), PK), jnp.uint32),
                            pltpu.SemaphoreType.DMA((DISPATCH_SLOTS + 1,))],
        ),
        out_shape=jax.ShapeDtypeStruct((nb_max * BH + spare, PK), jnp.uint32),
        compiler_params=_cparams("arbitrary"),
        name="dispatch",
    )(*seg, tail[0], tail[1], nblk, u2, code, segr, _row_digits(r, GATE_LANES, N_EXP))


def _expert_call(layer, blk, xs, w1, w3, w2, nb_max):
    row_blk = lambda i, be, nb, *_: (jnp.maximum(jnp.minimum(i, (nb[0] - 1) // 2), 0), 0)
    hbm = pl.BlockSpec(memory_space=pl.ANY)
    return pl.pallas_call(
        functools.partial(_expert_kernel, layer),
        grid_spec=pltpu.PrefetchScalarGridSpec(
            num_scalar_prefetch=4,
            grid=(nb_max // 2,),
            in_specs=[pl.BlockSpec((2 * BH, PK), row_blk), hbm, hbm, hbm],
            out_specs=pl.BlockSpec((2 * BH, PK), row_blk),
            scratch_shapes=[pltpu.VMEM((2, D, F_EXP), F32), pltpu.VMEM((2, D, F_EXP), F32),
                            pltpu.VMEM((2, F_EXP, D), F32),
                            pltpu.VMEM((D, F_EXP), BF16), pltpu.VMEM((D, F_EXP), BF16),
                            pltpu.VMEM((F_EXP, D), BF16), pltpu.SemaphoreType.DMA((2, 3))],
        ),
        out_shape=jax.ShapeDtypeStruct(xs.shape, jnp.uint32),
        input_output_aliases={4: 0},
        compiler_params=_cparams("arbitrary"),
        name="experts",
    )(*blk, xs, w1, w3, w2)


def _combine_call(tabs, code_t, segc, u2, ys, s1, s3, s2, h, norms_l, mod_l, t):
    n = u2.shape[0]
    r = _sorted_rows(TM_MOE)
    tok = lambda w: pl.BlockSpec((TM_MOE, w), lambda i, *_: (i, 0))
    const = lambda shp: pl.BlockSpec(shp, lambda i, *_: (0,) * len(shp))
    return pl.pallas_call(
        functools.partial(_combine_kernel, n // TM_MOE),
        grid_spec=pltpu.PrefetchScalarGridSpec(
            num_scalar_prefetch=4,
            grid=(n // TM_MOE,),
            in_specs=[
                tok(4 * N_EXP),
                pl.BlockSpec((None, 2 * N_EXP, GATE_LANES), lambda i, *_: (i, 0, 0)),
                const((N_EXP, r)),
                tok(D),
                pl.BlockSpec(memory_space=pl.ANY),
                const((D, F_EXP)), const((D, F_EXP)), const((F_EXP, D)),
                tok(D),
                const((4, D)),
                pl.BlockSpec((None, 6, D), lambda i, *_: (i * TM_MOE // t, 0, 0)),
            ],
            out_specs=tok(D),
            scratch_shapes=[pltpu.VMEM((2, r + N_EXP * SEG_ALIGN, PK), jnp.uint32),
                            pltpu.SemaphoreType.DMA((2,))],
        ),
        out_shape=jax.ShapeDtypeStruct((n, D), F32),
        compiler_params=_cparams("arbitrary"),
        name="combine",
    )(*tabs, code_t, segc, _row_digits(r, N_EXP, 0).T, u2, ys, s1, s3, s2, h, norms_l, mod_l)


def _glu_kernel(h_ref, g_ref, mod_ref, w_ref, b_ref, o_ref):
    u = (_rms(h_ref[...], g_ref[...] * (1.0 + mod_ref[1:2, :])) + mod_ref[0:1, :]).astype(BF16)
    step = 256
    for c in range(0, D, step):
        a = _dot(u, w_ref[:, c:c + step]) + b_ref[:, c:c + step]
        gate = _dot(u, w_ref[:, D + c:D + c + step]) + b_ref[:, D + c:D + c + step]
        o_ref[:, c:c + step] = (a * jax.nn.sigmoid(gate)).astype(BF16)


def _glu_call(h, g, mod_l, w1, b1, tm):
    b, t, _ = h.shape
    tok = lambda n: pl.BlockSpec((None, tm, n), lambda i, j: (i, j, 0))
    return pl.pallas_call(
        _glu_kernel,
        grid=(b, t // tm),
        in_specs=[
            tok(D),
            pl.BlockSpec((1, D), lambda i, j: (0, 0)),
            pl.BlockSpec((None, 6, D), lambda i, j: (i, 0, 0)),
            pl.BlockSpec((D, 2 * D), lambda i, j: (0, 0)),
            pl.BlockSpec((1, 2 * D), lambda i, j: (0, 0)),
        ],
        out_specs=tok(D),
        out_shape=jax.ShapeDtypeStruct((b, t, D), BF16),
        compiler_params=_cparams("parallel", "parallel"),
        name="glu",
    )(h, g, mod_l, w1, b1)


def _conv_kernel(cur_ref, prev_ref, next_ref, dw_ref, cv_ref, w2_ref, h_ref, n_ref, mod_ref, wr_ref,
                 h_out, u_out, lg_out, win, shifted, conv):
    j = pl.program_id(1)
    tm = cur_ref.shape[0]
    pad = CONV_W // 2
    win[0:HALO, :] = jnp.where(j > 0, prev_ref[...].astype(F32), 0.0)
    win[HALO:HALO + tm, :] = cur_ref[...].astype(F32)
    win[HALO + tm:, :] = jnp.where(j < pl.num_programs(1) - 1, next_ref[...].astype(F32), 0.0)
    ext = shifted.shape[1]
    full = win[...]
    shifted[0] = full[:ext, :]
    for s in range(1, SUBLANES):
        shifted[s] = pltpu.roll(full, full.shape[0] - s, axis=0)[:ext, :]

    def rows(ib, carry):
        r0 = pl.multiple_of(ib * CONV_TILES * SUBLANES, CONV_TILES * SUBLANES)
        for c0 in range(0, D, CONV_LANES):
            cols = slice(c0, c0 + CONV_LANES)
            accs = [jnp.zeros((SUBLANES, CONV_LANES), F32) for _ in range(CONV_TILES)]
            for s in range(SUBLANES):
                taps = [(off // SUBLANES, off - (HALO - pad)) for off in range(HALO - pad, HALO - pad + CONV_W)
                        if off % SUBLANES == s]
                tiles = {a: shifted[s, pl.ds(r0 + a * SUBLANES, SUBLANES), cols]
                         for a in range(taps[0][0], taps[-1][0] + CONV_TILES)}
                for a, tap in taps:
                    w = dw_ref[tap * SUBLANES:(tap + 1) * SUBLANES, cols]
                    accs = [acc + tiles[a + m] * w for m, acc in enumerate(accs)]
            for m, acc in enumerate(accs):
                conv[pl.ds(r0 + m * SUBLANES, SUBLANES), cols] = acc
        return carry

    lax.fori_loop(0, tm // (CONV_TILES * SUBLANES), rows, 0)
    for rows in _row_blocks(tm):
        uf = conv[rows, :] + cv_ref[0:1, :]
        mu = jnp.mean(uf, axis=-1, keepdims=True)
        var = jnp.mean(jnp.square(uf - mu), axis=-1, keepdims=True)
        uf = (uf - mu) * lax.rsqrt(var + LN_EPS) * cv_ref[1:2, :] + cv_ref[2:3, :]
        y = _dot(_silu(uf).astype(BF16), w2_ref[...]) + cv_ref[3:4, :]
        _mixer_epilogue(y, rows, h_ref, n_ref, mod_ref, wr_ref, h_out, u_out, lg_out)


def _conv_call(glu, dw, cvec, w2, h, norms_l, mod_l, wr_t, tm):
    b, t, _ = h.shape
    tok, ep_in, wr_spec, out_specs, out_shape = _epilogue_specs(b, t, tm)
    r = tm // HALO
    last = t // HALO - 1
    return pl.pallas_call(
        _conv_kernel,
        grid=(b, t // tm),
        in_specs=[
            tok(D),
            pl.BlockSpec((None, HALO, D), lambda i, j: (i, jnp.maximum(j * r - 1, 0), 0)),
            pl.BlockSpec((None, HALO, D), lambda i, j: (i, jnp.minimum((j + 1) * r, last), 0)),
            pl.BlockSpec((CONV_W * SUBLANES, D), lambda i, j: (0, 0)),
            pl.BlockSpec((4, D), lambda i, j: (0, 0)),
            pl.BlockSpec((D, D), lambda i, j: (0, 0)),
        ] + ep_in + [wr_spec],
        out_specs=out_specs,
        out_shape=out_shape,
        scratch_shapes=[pltpu.VMEM((tm + 2 * HALO, D), F32),
                        pltpu.VMEM((SUBLANES, tm + (CONV_W // SUBLANES) * SUBLANES, D), F32),
                        pltpu.VMEM((tm, D), F32)],
        compiler_params=_cparams("parallel", "parallel"),
        name="conv",
    )(glu, glu, glu, dw, cvec, w2, h, norms_l, mod_l, wr_t)


def _tile(t, pref):
    return pref if t % pref == 0 else t


def kernel(x, c, ctx, c_ctx, mod_w, mod_b, norms, ret_w_in, ret_w_out, ret_decay, conv_w1, conv_b1, conv_dw,
           conv_b_dw, conv_ln_g, conv_ln_b, conv_w2, conv_b2, moe_router, moe_bias, moe_w1, moe_w3, moe_w2,
           shared_w1, shared_w3, shared_w2):
    b, t, _ = x.shape
    l = ctx.shape[1]
    tm = _tile(t, 512)

    rows = -(-(b + 1) // 8) * 8
    cs = jnp.concatenate([c, c_ctx[None, :], jnp.zeros((rows - b - 1, D), F32)], 0)
    mod = _mod_call(cs, mod_w, mod_b).reshape(mod_w.shape[0], rows, 6, D)

    idx = jnp.arange(t, dtype=jnp.int32)
    pos_lat = jnp.stack([jnp.full((t,), l, jnp.int32), idx // GRID_W, idx % GRID_W], -1)
    zl = jnp.zeros((l,), jnp.int32)
    pos_ctx = jnp.stack([jnp.arange(l, dtype=jnp.int32), zl, zl], -1)
    ang_lat, ang_ctx = _rope_angles(pos_lat), _rope_angles(pos_ctx)

    w_qk = ret_w_in[0][:, :2 * QK_TOT].reshape(D, 2 * HEADS, DK // 2, 2).swapaxes(-1, -2).reshape(D, 2 * QK_TOT)
    w_in = jnp.concatenate([w_qk, ret_w_in[0][:, 2 * QK_TOT:]], axis=1).astype(BF16)
    log_g = jax.nn.log_sigmoid(ret_decay[0].astype(F32))

    n = b * t
    nb_max = _max_blocks(n)

    def moe_layer(i, h1, u2, logits_t):
        code, code_t, segc, segr, pc_t, off_t = _route_call(logits_t, moe_bias[i])
        seg_disp, seg_comb, tail, blocks = _dispatch_tables(pc_t, off_t, nb_max)
        u2f = u2.reshape(n, D)
        xs = _dispatch_call(seg_disp, tail, blocks[1], u2f, code, segr, nb_max)
        ys = _expert_call(i, blocks, xs, moe_w1, moe_w3, moe_w2, nb_max)
        out = _combine_call(seg_comb, code_t, segc, u2f, ys, shared_w1[i].astype(BF16), shared_w3[i].astype(BF16),
                            shared_w2[i].astype(BF16), h1.reshape(n, D), norms[i], mod[i], t)
        return out.reshape(b, t, D)

    q, k, v, gt = _inproj_call(x, norms[0, 0:1], mod[0], w_in, jnp.cos(ang_lat), jnp.sin(ang_lat), tm)
    sf0, sb0 = _ctxstate_call(log_g, ctx, norms[0, 0:1], mod[0, b], w_in[:, QK_TOT:2 * QK_TOT + V_TOT],
                              jnp.cos(ang_ctx), jnp.sin(ang_ctx))
    yg = _retention_call(log_g, q, k, v, gt, sf0, sb0)
    h1, u2, logits_t = _readout_call(yg, x, norms[0], mod[0], ret_w_out[0].astype(BF16),
                                     moe_router[0].T, _tile(t, 2 * EPI_ROWS))
    h2 = moe_layer(0, h1, u2, logits_t)

    glu = _glu_call(h2, norms[1, 0:1], mod[1], conv_w1[0].astype(BF16), conv_b1[0][None, :], tm)
    cvec = jnp.stack([conv_b_dw[0], conv_ln_g[0], conv_ln_b[0], conv_b2[0]], 0)
    dw_rows = jnp.repeat(conv_dw[0], SUBLANES, axis=0)
    h3, u2, logits_t = _conv_call(glu, dw_rows, cvec, conv_w2[0].astype(BF16), h2, norms[1], mod[1],
                                  moe_router[1].T, tm)
    return moe_layer(1, h3, u2, logits_t)
```
